```python
import math
import jax
import jax.numpy as jnp
from jax import lax
import numpy as np

D_MODEL = 2048
BATCH = 2
SEQ = 8192
DEPTH = 4

CTX_LEN = 256
GRID_W = 64
HEAD_DIM = 128
N_MIXERS = 4
GROUP_W = D_MODEL // N_MIXERS
GROUP_HEADS = GROUP_W // HEAD_DIM
N_MOD = 6
EPS = 1e-6

LRU_W = GROUP_W
LRU_BLOCKS = GROUP_HEADS
LRU_BW = LRU_W // LRU_BLOCKS
LRU_CONV = 4
LRU_C = 8.0

MLA_HEADS = GROUP_HEADS
MLA_Q_LORA = 3 * D_MODEL // 16
MLA_KV_LORA = D_MODEL // 8
MLA_NOPE = 128
MLA_ROPE = 64
MLA_V = GROUP_W // MLA_HEADS
MLA_QK = MLA_NOPE + MLA_ROPE
ROPE_BASE = 10000.0
Q_BLOCK = 128

GDN_HEADS = GROUP_HEADS
GDN_DK = 128
GDN_DV = GROUP_W // GDN_HEADS
GDN_CONV = 4
GDN_CHUNK = 64

NA_HEADS = GROUP_HEADS
NA_HD = GROUP_W // NA_HEADS
NA_KH = 8
NA_KW = 16

D_FF = 5 * D_MODEL // 2
FFN_CONV = 3

IN_LAYOUT = (
    ('lru_x', LRU_W), ('lru_gate', LRU_W),
    ('mla_cq', MLA_Q_LORA), ('mla_ckv', MLA_KV_LORA), ('mla_kr', MLA_ROPE),
    ('gdn_q', GDN_HEADS * GDN_DK), ('gdn_k', GDN_HEADS * GDN_DK),
    ('gdn_v', GDN_HEADS * GDN_DV), ('gdn_gate', GDN_HEADS * GDN_DV),
    ('gdn_beta', 2 * GDN_HEADS), ('gdn_alpha', 2 * GDN_HEADS),
    ('na_q', NA_HEADS * NA_HD), ('na_k', NA_HEADS * NA_HD), ('na_v', NA_HEADS * NA_HD),
)
IN_COLS = sum(n for _, n in IN_LAYOUT)

kernel_name = 'hybrid_parallel_group_diffusion_block'


def rms_norm(x, g):
    xf = x.astype(jnp.float32)
    y = xf * lax.rsqrt(jnp.mean(xf * xf, axis=-1, keepdims=True) + EPS)
    return (y * g.astype(jnp.float32)).astype(x.dtype)


def l2_norm(x):
    xf = x.astype(jnp.float32)
    return xf * lax.rsqrt(jnp.sum(xf * xf, axis=-1, keepdims=True) + EPS)


def modulate(x, shift, scale):
    return x * (1.0 + scale) + shift


def dw_conv(x, w, b=None):
    width = w.shape[0]
    left = (width - 1) // 2
    right = width - 1 - left
    T = x.shape[1]
    xp = jnp.pad(x, ((0, 0), (left, right), (0, 0)))
    y = xp[:, 0:T] * w[0]
    for i in range(1, width):
        y = y + xp[:, i:i + T] * w[i]
    if b is not None:
        y = y + b
    return y


def split_in(p):
    out, off = {}, 0
    for name, n in IN_LAYOUT:
        out[name] = p[..., off:off + n]
        off += n
    return out


def axial_rope_tables(n_tokens):
    t = jnp.arange(n_tokens)
    row = (t // GRID_W).astype(jnp.float32)
    col = (t % GRID_W).astype(jnp.float32)
    n_freq = MLA_ROPE // 4
    inv = ROPE_BASE ** (-jnp.arange(n_freq, dtype=jnp.float32) / n_freq)
    ang = jnp.concatenate([row[:, None] * inv, col[:, None] * inv], axis=-1)
    return jnp.cos(ang), jnp.sin(ang)


def apply_rope(x, cos, sin):
    extra = x.ndim - 3
    cos = cos.reshape(cos.shape[:1] + (1,) * extra + cos.shape[1:])
    sin = sin.reshape(sin.shape[:1] + (1,) * extra + sin.shape[1:])
    half = x.shape[-1] // 2
    x1, x2 = x[..., :half], x[..., half:]
    return jnp.concatenate([x1 * cos - x2 * sin, x1 * sin + x2 * cos], axis=-1).astype(x.dtype)


def context_attention(q, k, v, scale):
    s = jnp.einsum('bqhd,bkhd->bhqk', q, k).astype(jnp.float32) * scale
    p = jax.nn.softmax(s, axis=-1).astype(v.dtype)
    o = jnp.einsum('bhqk,bkhd->bqhd', p, v)
    return o.reshape(o.shape[0], o.shape[1], -1)


def latent_block_attention(q, k, v, k_ctx, v_ctx, scale):
    B, T, H, Dq = q.shape
    Dv = v.shape[-1]
    n_blk = T // Q_BLOCK
    q_blocks = jnp.moveaxis(q.reshape(B, n_blk, Q_BLOCK, H, Dq), 1, 0)

    def block(qb):
        s_lat = jnp.einsum('bqhd,bkhd->bhqk', qb, k).astype(jnp.float32) * scale
        s_ctx = jnp.einsum('bqhd,bchd->bhqc', qb, k_ctx).astype(jnp.float32) * scale
        p = jax.nn.softmax(jnp.concatenate([s_lat, s_ctx], axis=-1), axis=-1).astype(v.dtype)
        return (jnp.einsum('bhqk,bkhd->bqhd', p[..., :T], v)
                + jnp.einsum('bhqc,bchd->bqhd', p[..., T:], v_ctx))

    o = lax.map(block, q_blocks)
    return jnp.moveaxis(o, 0, 1).reshape(B, T, H * Dv)


def rglru_coeffs(u, wa, ba, wx, bx, lam):
    B, T, W = u.shape
    ub = u.reshape(B, T, LRU_BLOCKS, LRU_BW)
    r = jax.nn.sigmoid((jnp.einsum('btnd,nde->btne', ub, wa).reshape(B, T, W) + ba).astype(jnp.float32))
    i = jax.nn.sigmoid((jnp.einsum('btnd,nde->btne', ub, wx).reshape(B, T, W) + bx).astype(jnp.float32))
    log_a = -LRU_C * r * jax.nn.softplus(-lam.astype(jnp.float32))
    a = jnp.exp(log_a)
    b = jnp.sqrt(-jnp.expm1(2.0 * log_a)) * i * u.astype(jnp.float32)
    return a, b


def linear_scan(a, b, h0, reverse):
    def combine(lhs, rhs):
        a_l, b_l = lhs
        a_r, b_r = rhs
        return a_l * a_r, a_r * b_l + b_r
    a_cum, b_cum = lax.associative_scan(combine, (a, b), axis=1, reverse=reverse)
    if h0 is None:
        return b_cum
    return a_cum * h0[:, None, :] + b_cum


def mixer_rglru(xc, gc, xl, gl, conv_w, conv_b, wa, ba, wx, bx, lam, want_ctx):
    uc = dw_conv(xc, conv_w, conv_b)
    ul = dw_conv(xl, conv_w, conv_b)
    hc_dirs, hl_dirs = [], []
    for d in range(2):
        rev = d == 1
        a, b = rglru_coeffs(uc, wa[d], ba[d], wx[d], bx[d], lam[d])
        hc = linear_scan(a, b, None, rev)
        hc_last = hc[:, 0] if rev else hc[:, -1]
        a, b = rglru_coeffs(ul, wa[d], ba[d], wx[d], bx[d], lam[d])
        hl_dirs.append(linear_scan(a, b, hc_last, rev))
        hc_dirs.append(hc)
    yl = ((hl_dirs[0] + hl_dirs[1]) * jax.nn.gelu(gl.astype(jnp.float32))).astype(xl.dtype)
    yc = None
    if want_ctx:
        yc = ((hc_dirs[0] + hc_dirs[1]) * jax.nn.gelu(gc.astype(jnp.float32))).astype(xc.dtype)
    return yc, yl


def mla_heads(cq, ckv, kr, qa_g, w_uq, kva_g, w_ukv, qn_g, kn_g, rope):
    B, T, _ = cq.shape
    q = (rms_norm(cq, qa_g) @ w_uq).reshape(B, T, MLA_HEADS, MLA_QK)
    kv = (rms_norm(ckv, kva_g) @ w_ukv).reshape(B, T, MLA_HEADS, MLA_NOPE + MLA_V)
    k_nope, v = kv[..., :MLA_NOPE], kv[..., MLA_NOPE:]
    k_rope = jnp.broadcast_to(kr[:, :, None, :], (B, T, MLA_HEADS, MLA_ROPE))
    k = jnp.concatenate([k_nope, k_rope], axis=-1)
    q = rms_norm(q, qn_g)
    k = rms_norm(k, kn_g)
    if rope is not None:
        cos, sin = rope
        q = jnp.concatenate([q[..., :MLA_NOPE], apply_rope(q[..., MLA_NOPE:], cos, sin)], axis=-1)
        k = jnp.concatenate([k[..., :MLA_NOPE], apply_rope(k[..., MLA_NOPE:], cos, sin)], axis=-1)
    return q, k, v


def mixer_mla(pc, pl, qa_g, w_uq, kva_g, w_ukv, qn_g, kn_g, cos, sin, want_ctx):
    scale = MLA_QK ** -0.5
    qc, kc, vc = mla_heads(pc['mla_cq'], pc['mla_ckv'], pc['mla_kr'], qa_g, w_uq, kva_g, w_ukv, qn_g, kn_g, None)
    ql, kl, vl = mla_heads(pl['mla_cq'], pl['mla_ckv'], pl['mla_kr'], qa_g, w_uq, kva_g, w_ukv, qn_g, kn_g, (cos, sin))
    yl = latent_block_attention(ql, kl, vl, kc, vc, scale)
    yc = context_attention(qc, kc, vc, scale) if want_ctx else None
    return yc, yl


def chunked_gated_delta(q, k, v, g, beta, S0):
    B, T, H, DK = q.shape
    DV = v.shape[-1]
    C = GDN_CHUNK
    N = T // C

    def to_chunks(t):
        return jnp.moveaxis(t.reshape((B, N, C, H) + t.shape[3:]), (1, 3), (0, 2))

    q, k, v, g, beta = (to_chunks(t) for t in (q, k, v, g, beta))
    gc = jnp.cumsum(g, axis=-1)
    tril = jnp.tril(jnp.ones((C, C), dtype=bool))
    strict = jnp.tril(jnp.ones((C, C), dtype=bool), -1)
    diff = gc[..., :, None] - gc[..., None, :]
    decay = jnp.where(tril, jnp.exp(jnp.where(tril, diff, 0.0)), 0.0)
    kb = k * beta[..., None]
    a_strict = jnp.where(strict, jnp.einsum('nbhid,nbhjd->nbhij', kb, k) * decay, 0.0)
    eye = jnp.eye(C, dtype=jnp.float32)
    t_inv = lax.linalg.triangular_solve(eye + a_strict, jnp.broadcast_to(eye, a_strict.shape),
                                        left_side=True, lower=True, unit_diagonal=True)
    u = t_inv @ (v * beta[..., None])
    w = t_inv @ (kb * jnp.exp(gc)[..., None])
    qk = jnp.einsum('nbhid,nbhjd->nbhij', q, k) * decay
    q_dec = q * jnp.exp(gc)[..., None]
    k_dec = k * jnp.exp(gc[..., -1:] - gc)[..., None]
    g_tot = jnp.exp(gc[..., -1])

    def step(S, inp):
        qd, kd, ui, wi, qki, gt = inp
        v_new = ui - jnp.einsum('bhck,bhkv->bhcv', wi, S)
        o = jnp.einsum('bhck,bhkv->bhcv', qd, S) + jnp.einsum('bhij,bhjv->bhiv', qki, v_new)
        S = S * gt[..., None, None] + jnp.einsum('bhck,bhcv->bhkv', kd, v_new)
        return S, o

    S, o = lax.scan(step, S0, (q_dec, k_dec, u, w, qk, g_tot))
    o = jnp.moveaxis(o, (0, 2), (1, 3)).reshape(B, T, H, DV)
    return o, S


def gated_delta(q, k, v, g, beta, S0, reverse):
    if reverse:
        q, k, v, g, beta = (jnp.flip(t, axis=1) for t in (q, k, v, g, beta))
    o, S = chunked_gated_delta(q, k, v, g, beta, S0)
    if reverse:
        o = jnp.flip(o, axis=1)
    return o, S


def gdn_prepare(q, k, v, conv_w):
    B, T, _ = q.shape
    nq = GDN_HEADS * GDN_DK
    qkv = jax.nn.silu(dw_conv(jnp.concatenate([q, k, v], axis=-1), conv_w))
    qh = l2_norm(qkv[..., :nq].reshape(B, T, GDN_HEADS, GDN_DK)) * (GDN_DK ** -0.5)
    kh = l2_norm(qkv[..., nq:2 * nq].reshape(B, T, GDN_HEADS, GDN_DK))
    vh = qkv[..., 2 * nq:].reshape(B, T, GDN_HEADS, GDN_DV).astype(jnp.float32)
    return qh, kh, vh


def gdn_gates(beta_in, alpha_in, a_log, dt_bias, d):
    sl = slice(d * GDN_HEADS, (d + 1) * GDN_HEADS)
    beta = jax.nn.sigmoid(beta_in[..., sl].astype(jnp.float32))
    g = -jnp.exp(a_log[d].astype(jnp.float32)) * jax.nn.softplus(
        alpha_in[..., sl].astype(jnp.float32) + dt_bias[d].astype(jnp.float32))
    return g, beta


def gdn_finish(o, gate, norm_g):
    B, T, _ = gate.shape
    o = rms_norm(o, norm_g) * jax.nn.silu(gate.astype(jnp.float32).reshape(B, T, GDN_HEADS, GDN_DV))
    return o.reshape(B, T, GDN_HEADS * GDN_DV).astype(gate.dtype)


def mixer_gdn(pc, pl, conv_w, a_log, dt_bias, norm_g, want_ctx):
    qc, kc, vc = gdn_prepare(pc['gdn_q'], pc['gdn_k'], pc['gdn_v'], conv_w)
    ql, kl, vl = gdn_prepare(pl['gdn_q'], pl['gdn_k'], pl['gdn_v'], conv_w)
    B = qc.shape[0]
    oc_dirs, ol_dirs = [], []
    for d in range(2):
        rev = d == 1
        g_c, b_c = gdn_gates(pc['gdn_beta'], pc['gdn_alpha'], a_log, dt_bias, d)
        S0 = jnp.zeros((B, GDN_HEADS, GDN_DK, GDN_DV), jnp.float32)
        o_c, S_c = gated_delta(qc, kc, vc, g_c, b_c, S0, rev)
        g_l, b_l = gdn_gates(pl['gdn_beta'], pl['gdn_alpha'], a_log, dt_bias, d)
        o_l, _ = gated_delta(ql, kl, vl, g_l, b_l, S_c, rev)
        oc_dirs.append(o_c)
        ol_dirs.append(o_l)
    yl = gdn_finish(ol_dirs[0] + ol_dirs[1], pl['gdn_gate'], norm_g)
    yc = gdn_finish(oc_dirs[0] + oc_dirs[1], pc['gdn_gate'], norm_g) if want_ctx else None
    return yc, yl


def na_heads(q, k, v, qn_g, kn_g):
    B, T, _ = q.shape
    shp = (B, T, NA_HEADS, NA_HD)
    return rms_norm(q.reshape(shp), qn_g), rms_norm(k.reshape(shp), kn_g), v.reshape(shp)


def mixer_na(pc, pl, qn_g, kn_g, rpb, rows, want_ctx):
    scale = NA_HD ** -0.5
    qc, kc, vc = na_heads(pc['na_q'], pc['na_k'], pc['na_v'], qn_g, kn_g)
    ql, kl, vl = na_heads(pl['na_q'], pl['na_k'], pl['na_v'], qn_g, kn_g)
    B, T = ql.shape[:2]
    kh = min(NA_KH, rows)
    r = jnp.arange(rows)
    row_idx = jnp.clip(r - kh // 2, 0, rows - kh)[:, None] + jnp.arange(kh)[None, :]
    cq = jnp.arange(GRID_W)
    c0 = jnp.clip(cq - NA_KW // 2, 0, GRID_W - NA_KW)
    col_ok = (cq[None, :] >= c0[:, None]) & (cq[None, :] < c0[:, None] + NA_KW)
    dr = row_idx - r[:, None] + (NA_KH - 1)
    dc = jnp.clip(cq[None, :] - cq[:, None] + (NA_KW - 1), 0, 2 * NA_KW - 2)
    bias = rpb[:, dr[:, None, :, None], dc[None, :, None, :]]
    qg = ql.reshape(B, rows, GRID_W, NA_HEADS, NA_HD)
    kg = kl.reshape(B, rows, GRID_W, NA_HEADS, NA_HD)[:, row_idx]
    vg = vl.reshape(B, rows, GRID_W, NA_HEADS, NA_HD)[:, row_idx]
    s_loc = jnp.einsum('brqhd,brkwhd->bhrqkw', qg, kg).astype(jnp.float32) * scale + bias
    s_loc = jnp.where(col_ok[:, None, :], s_loc, -jnp.inf)
    s_ctx = jnp.einsum('brqhd,bchd->bhrqc', qg, kc).astype(jnp.float32) * scale
    n_loc = kh * GRID_W
    s = jnp.concatenate([s_loc.reshape(B, NA_HEADS, rows, GRID_W, n_loc), s_ctx], axis=-1)
    p = jax.nn.softmax(s, axis=-1).astype(vl.dtype)
    p_loc = p[..., :n_loc].reshape(B, NA_HEADS, rows, GRID_W, kh, GRID_W)
    o = (jnp.einsum('bhrqkw,brkwhd->brqhd', p_loc, vg)
         + jnp.einsum('bhrqc,bchd->brqhd', p[..., n_loc:], vc))
    yl = o.reshape(B, T, NA_HEADS * NA_HD)
    yc = context_attention(qc, kc, vc, scale) if want_ctx else None
    return yc, yl


def conv_ffn(h, w_up, conv_w, conv_b, w_down):
    u = dw_conv(h @ w_up, conv_w, conv_b)
    return (u[..., :D_FF] * jax.nn.silu(u[..., D_FF:])) @ w_down


def setup_inputs(seed: int = 0) -> dict:
    key = jax.random.key(seed)
    keys = iter(jax.random.split(key, 40))
    f32 = jnp.float32

    def nrm(shape, std):
        return jax.random.normal(next(keys), shape, f32) * std

    def gain(shape):
        return 1.0 + nrm(shape, 0.02)

    def unif(shape, lo, hi):
        return jax.random.uniform(next(keys), shape, f32, lo, hi)

    L, D = DEPTH, D_MODEL
    lru_a = unif((L, 2, LRU_W), 0.9, 0.999)
    lru_s = lru_a ** (1.0 / LRU_C)
    gdn_dt = jnp.exp(unif((L, 2, GDN_HEADS), math.log(1e-3), math.log(1e-1)))
    return {
        'x': nrm((BATCH, SEQ, D), 1.0),
        'c': nrm((BATCH, D), 1.0),
        'ctx': nrm((BATCH, CTX_LEN, D), 1.0),
        'c_ctx': nrm((D,), 1.0),
        'ada_w': nrm((L, D, N_MOD * D), 0.5 * D ** -0.5),
        'ada_b': nrm((L, N_MOD * D), 0.01),
        'norm_mix_g': gain((L, D)),
        'norm_ffn_g': gain((L, D)),
        'w_in': nrm((L, D, IN_COLS), D ** -0.5),
        'w_out': nrm((L, D, D), D ** -0.5),
        'lru_conv_w': nrm((L, LRU_CONV, LRU_W), LRU_CONV ** -0.5),
        'lru_conv_b': nrm((L, LRU_W), 0.01),
        'lru_wa': nrm((L, 2, LRU_BLOCKS, LRU_BW, LRU_BW), LRU_BW ** -0.5),
        'lru_ba': nrm((L, 2, LRU_W), 0.01),
        'lru_wx': nrm((L, 2, LRU_BLOCKS, LRU_BW, LRU_BW), LRU_BW ** -0.5),
        'lru_bx': nrm((L, 2, LRU_W), 0.01),
        'lru_lam': jnp.log(lru_s) - jnp.log1p(-lru_s),
        'mla_qa_g': gain((L, MLA_Q_LORA)),
        'mla_w_uq': nrm((L, MLA_Q_LORA, MLA_HEADS * MLA_QK), MLA_Q_LORA ** -0.5),
        'mla_kva_g': gain((L, MLA_KV_LORA)),
        'mla_w_ukv': nrm((L, MLA_KV_LORA, MLA_HEADS * (MLA_NOPE + MLA_V)), MLA_KV_LORA ** -0.5),
        'mla_qn_g': gain((L, MLA_QK)),
        'mla_kn_g': gain((L, MLA_QK)),
        'gdn_conv_w': nrm((L, GDN_CONV, GDN_HEADS * (2 * GDN_DK + GDN_DV)), GDN_CONV ** -0.5),
        'gdn_a_log': jnp.log(unif((L, 2, GDN_HEADS), 1.0, 16.0)),
        'gdn_dt_bias': gdn_dt + jnp.log(-jnp.expm1(-gdn_dt)),
        'gdn_norm_g': gain((L, GDN_DV)),
        'na_qn_g': gain((L, NA_HD)),
        'na_kn_g': gain((L, NA_HD)),
        'na_rpb': nrm((L, NA_HEADS, 2 * NA_KH - 1, 2 * NA_KW - 1), 0.05),
        'ffn_w_up': nrm((L, D, 2 * D_FF), D ** -0.5),
        'ffn_conv_w': nrm((L, FFN_CONV, 2 * D_FF), FFN_CONV ** -0.5),
        'ffn_conv_b': nrm((L, 2 * D_FF), 0.01),
        'ffn_w_down': nrm((L, D_FF, D), D_FF ** -0.5),
    }


def reference(x, c, ctx, c_ctx, ada_w, ada_b, norm_mix_g, norm_ffn_g, w_in, w_out,
              lru_conv_w, lru_conv_b, lru_wa, lru_ba, lru_wx, lru_bx, lru_lam,
              mla_qa_g, mla_w_uq, mla_kva_g, mla_w_ukv, mla_qn_g, mla_kn_g,
              gdn_conv_w, gdn_a_log, gdn_dt_bias, gdn_norm_g,
              na_qn_g, na_kn_g, na_rpb,
              ffn_w_up, ffn_conv_w, ffn_conv_b, ffn_w_down):
    T = x.shape[1]
    rows = T // GRID_W
    cos, sin = axial_rope_tables(T)
    silu_c = jax.nn.silu(c)
    silu_cc = jax.nn.silu(c_ctx)
    h_lat, h_ctx = x, ctx
    for l in range(DEPTH):
        want_ctx = l < DEPTH - 1
        mod_l = jnp.split((silu_c @ ada_w[l] + ada_b[l])[:, None, :], N_MOD, axis=-1)
        mod_c = jnp.split((silu_cc @ ada_w[l] + ada_b[l])[None, None, :], N_MOD, axis=-1)

        u_l = modulate(rms_norm(h_lat, norm_mix_g[l]), mod_l[0], mod_l[1])
        u_c = modulate(rms_norm(h_ctx, norm_mix_g[l]), mod_c[0], mod_c[1])
        pl = split_in(u_l @ w_in[l])
        pc = split_in(u_c @ w_in[l])
        ya_c, ya_l = mixer_rglru(pc['lru_x'], pc['lru_gate'], pl['lru_x'], pl['lru_gate'],
                                 lru_conv_w[l], lru_conv_b[l], lru_wa[l], lru_ba[l],
                                 lru_wx[l], lru_bx[l], lru_lam[l], want_ctx)
        yb_c, yb_l = mixer_mla(pc, pl, mla_qa_g[l], mla_w_uq[l], mla_kva_g[l], mla_w_ukv[l],
                               mla_qn_g[l], mla_kn_g[l], cos, sin, want_ctx)
        yc_c, yc_l = mixer_gdn(pc, pl, gdn_conv_w[l], gdn_a_log[l], gdn_dt_bias[l],
                               gdn_norm_g[l], want_ctx)
        yd_c, yd_l = mixer_na(pc, pl, na_qn_g[l], na_kn_g[l], na_rpb[l], rows, want_ctx)
        h_lat = h_lat + mod_l[2] * (jnp.concatenate([ya_l, yb_l, yc_l, yd_l], axis=-1) @ w_out[l])
        if want_ctx:
            h_ctx = h_ctx + mod_c[2] * (jnp.concatenate([ya_c, yb_c, yc_c, yd_c], axis=-1) @ w_out[l])

        v_l = modulate(rms_norm(h_lat, norm_ffn_g[l]), mod_l[3], mod_l[4])
        h_lat = h_lat + mod_l[5] * conv_ffn(v_l, ffn_w_up[l], ffn_conv_w[l], ffn_conv_b[l], ffn_w_down[l])
        if want_ctx:
            v_c = modulate(rms_norm(h_ctx, norm_ffn_g[l]), mod_c[3], mod_c[4])
            h_ctx = h_ctx + mod_c[5] * conv_ffn(v_c, ffn_w_up[l], ffn_conv_w[l], ffn_conv_b[l], ffn_w_down[l])
    return h_lat
```

```python
import functools
import math

import jax
import jax.numpy as jnp
from jax import lax
from jax.experimental import pallas as pl
from jax.experimental.pallas import tpu as pltpu

F32 = jnp.float32
BF16 = jnp.bfloat16

GRID_W = 64
HEADS = 4
HEAD_DIM = 128
GROUP_W = HEADS * HEAD_DIM
N_MOD = 6
EPS = 1e-6
LRU_C = 8.0
MLA_Q_LORA = 384
MLA_KV_LORA = 256
MLA_NOPE = 128
MLA_ROPE = 64
MLA_QK = MLA_NOPE + MLA_ROPE
MLA_QPAD = 256
ROPE_BASE = 10000.0
GDN_CHUNK = 64
NA_KH = 8
NA_KW = 16
NA_BAND = 8
NA_WIN = 16
NEG_BIG = -1e30

VMEM_LIMIT = 56 * 1024 * 1024
SUBLANES = 8
LANES = 128

P_GDN_QKV = 0
P_GDN_GATE = 1536
P_LRU_X = 2048
P_LRU_GATE = 2560
P_NA_QKV = 3072
P_MLA = 4608
P_COLS = 5376
P_TAIL = 5248
TAIL_BETA = 64
TAIL_ALPHA = 72


def _cparams(sem):
    return pltpu.CompilerParams(dimension_semantics=sem, vmem_limit_bytes=VMEM_LIMIT)


def _dot(a, b):
    return jnp.dot(a.astype(BF16), b.astype(BF16), preferred_element_type=F32)


def _dot_nt(a, b):
    return lax.dot_general(a.astype(BF16), b.astype(BF16), (((1,), (1,)), ((), ())),
                           preferred_element_type=F32)


def _dot_tn(a, b):
    return lax.dot_general(a.astype(BF16), b.astype(BF16), (((0,), (0,)), ((), ())),
                           preferred_element_type=F32)


def _sigmoid(x):
    return 1.0 / (1.0 + jnp.exp(-x))


def _silu(x):
    return x * _sigmoid(x)


def _softplus(x):
    return jnp.maximum(x, 0.0) + jnp.log(1.0 + jnp.exp(-jnp.abs(x)))


def _gelu_tanh(x):
    return 0.5 * x * (1.0 + jnp.tanh(math.sqrt(2.0 / math.pi) * (x + 0.044715 * x * x * x)))


def _mod_kernel(c_ref, w_ref, b_ref, o_ref):
    c = c_ref[...]
    o_ref[...] = _dot(_silu(c), w_ref[...]) + b_ref[...]


def _modulation(cvec, ada_w, ada_b):
    L, D, N = ada_w.shape
    tn = 1024
    return pl.pallas_call(
        _mod_kernel,
        grid=(L, N // tn),
        in_specs=[pl.BlockSpec((SUBLANES, D), lambda l, j: (0, 0)),
                  pl.BlockSpec((None, D, tn), lambda l, j: (l, 0, j)),
                  pl.BlockSpec((None, 1, tn), lambda l, j: (l, 0, j))],
        out_specs=pl.BlockSpec((None, SUBLANES, tn), lambda l, j: (l, 0, j)),
        out_shape=jax.ShapeDtypeStruct((L, SUBLANES, N), F32),
        compiler_params=_cparams(("parallel", "parallel")),
        name="modulation",
    )(cvec, ada_w, ada_b.reshape(L, 1, N))


def _mod_spec(which, rows_per_mod, D, ngrid):
    if rows_per_mod is None:
        row = lambda i: 2
    else:
        row = lambda i: i // rows_per_mod
    if ngrid == 1:
        return pl.BlockSpec((None, None, 1, D), lambda i: (row(i), which, 0, 0))
    return pl.BlockSpec((None, None, 1, D), lambda i, j: (row(i), which, 0, 0))


def _inproj_kernel(h_ref, g_ref, shift_ref, scale_ref, w_ref, o_ref, xn_ref):
    @pl.when(pl.program_id(1) == 0)
    def _():
        x = h_ref[...]
        y = x * lax.rsqrt(jnp.mean(x * x, axis=-1, keepdims=True) + EPS) * g_ref[...]
        xn_ref[...] = (y * (1.0 + scale_ref[...]) + shift_ref[...]).astype(BF16)

    o_ref[...] = jnp.dot(xn_ref[...], w_ref[...], preferred_element_type=F32)


def _inproj(h, gain, mod, w, seq_len, is_ctx):
    M, D = h.shape
    N = w.shape[1]
    tm = min(512, M)
    tn = 768
    rpm = None if is_ctx else seq_len // tm
    return pl.pallas_call(
        _inproj_kernel,
        grid=(M // tm, N // tn),
        in_specs=[pl.BlockSpec((tm, D), lambda i, j: (i, 0)),
                  pl.BlockSpec((1, D), lambda i, j: (0, 0)),
                  _mod_spec(0, rpm, D, 2),
                  _mod_spec(1, rpm, D, 2),
                  pl.BlockSpec((D, tn), lambda i, j: (0, j))],
        out_specs=pl.BlockSpec((tm, tn), lambda i, j: (i, j)),
        out_shape=jax.ShapeDtypeStruct((M, N), F32),
        scratch_shapes=[pltpu.VMEM((tm, D), BF16)],
        compiler_params=_cparams(("parallel", "arbitrary")),
        name="inproj",
    )(h, gain, mod, mod, w)


def _outproj_kernel(ya_ref, yb_ref, yc_ref, yd_ref, w_ref, h_ref, gate_ref, o_ref):
    acc = jnp.dot(ya_ref[...], w_ref[0], preferred_element_type=F32)
    acc += jnp.dot(yb_ref[...], w_ref[1], preferred_element_type=F32)
    acc += jnp.dot(yc_ref[...], w_ref[2], preferred_element_type=F32)
    acc += jnp.dot(yd_ref[...], w_ref[3], preferred_element_type=F32)
    o_ref[...] = h_ref[...] + gate_ref[...] * acc


def _outproj(ys, w4, h, mod, seq_len, is_ctx):
    M, D = h.shape
    tm = min(512, M)
    rpm = None if is_ctx else seq_len // tm
    yspec = pl.BlockSpec((tm, GROUP_W), lambda i: (i, 0))
    return pl.pallas_call(
        _outproj_kernel,
        grid=(M // tm,),
        in_specs=[yspec, yspec, yspec, yspec,
                  pl.BlockSpec((4, GROUP_W, D), lambda i: (0, 0, 0)),
                  pl.BlockSpec((tm, D), lambda i: (i, 0)),
                  _mod_spec(2, rpm, D, 1)],
        out_specs=pl.BlockSpec((tm, D), lambda i: (i, 0)),
        out_shape=jax.ShapeDtypeStruct((M, D), F32),
        compiler_params=_cparams(("parallel",)),
        name="outproj",
    )(*ys, w4, h, mod)


def _halo_specs(tm, width, col_block, nrows, ngrid):
    r = tm // SUBLANES
    last = nrows // SUBLANES - 1
    if ngrid == 1:
        prev = pl.BlockSpec((SUBLANES, width), lambda i: (jnp.maximum(i * r - 1, 0), col_block))
        nxt = pl.BlockSpec((SUBLANES, width), lambda i: (jnp.minimum((i + 1) * r, last), col_block))
    else:
        prev = pl.BlockSpec((SUBLANES, width), lambda i, j: (jnp.maximum(i * r - 1, 0), col_block))
        nxt = pl.BlockSpec((SUBLANES, width), lambda i, j: (jnp.minimum((i + 1) * r, last), col_block))
    return prev, nxt


def _ffn_kernel(seq_len, h_ref, hp_ref, hn_ref, g_ref, shift_ref, scale_ref, gate_ref,
                wa_ref, wg_ref, cwa_ref, cwg_ref, cba_ref, cbg_ref, wd_ref, o_ref, xn_ref):
    tm = h_ref.shape[0]
    i = pl.program_id(0)
    j = pl.program_id(1)

    @pl.when(j == 0)
    def _():
        def norm(x):
            y = x * lax.rsqrt(jnp.mean(x * x, axis=-1, keepdims=True) + EPS) * g_ref[...]
            return (y * (1.0 + scale_ref[...]) + shift_ref[...]).astype(BF16)
        xn_ref[0:SUBLANES, :] = norm(hp_ref[...])
        xn_ref[SUBLANES:SUBLANES + tm, :] = norm(h_ref[...])
        xn_ref[SUBLANES + tm:, :] = norm(hn_ref[...])

    pos = (i * tm + lax.broadcasted_iota(jnp.int32, (tm, 1), 0)) % seq_len
    has_prev = pos > 0
    has_next = pos < seq_len - 1

    def branch(w_ref, cw_ref, cb_ref):
        u = jnp.dot(xn_ref[...], w_ref[...], preferred_element_type=F32)
        lo = jnp.where(has_prev, u[SUBLANES - 1:SUBLANES - 1 + tm], 0.0)
        mid = u[SUBLANES:SUBLANES + tm]
        hi = jnp.where(has_next, u[SUBLANES + 1:SUBLANES + 1 + tm], 0.0)
        return lo * cw_ref[0:1, :] + mid * cw_ref[1:2, :] + hi * cw_ref[2:3, :] + cb_ref[...]

    a = branch(wa_ref, cwa_ref, cba_ref)
    g = branch(wg_ref, cwg_ref, cbg_ref)
    part = jnp.dot((a * _silu(g)).astype(BF16), wd_ref[...], preferred_element_type=F32)

    @pl.when(j == 0)
    def _():
        o_ref[...] = part

    @pl.when(j > 0)
    def _():
        o_ref[...] += part

    @pl.when(j == pl.num_programs(1) - 1)
    def _():
        o_ref[...] = h_ref[...] + gate_ref[...] * o_ref[...]


def _ffn(h, gain, mod, w_up, conv_w, conv_b, w_down, seq_len, is_ctx):
    M, D = h.shape
    FF = w_down.shape[0]
    tm = min(512, M)
    tf = 512
    nf = FF // tf
    rpm = None if is_ctx else seq_len // tm
    prev, nxt = _halo_specs(tm, D, 0, M, 2)
    return pl.pallas_call(
        functools.partial(_ffn_kernel, seq_len),
        grid=(M // tm, nf),
        in_specs=[pl.BlockSpec((tm, D), lambda i, j: (i, 0)), prev, nxt,
                  pl.BlockSpec((1, D), lambda i, j: (0, 0)),
                  _mod_spec(3, rpm, D, 2), _mod_spec(4, rpm, D, 2), _mod_spec(5, rpm, D, 2),
                  pl.BlockSpec((D, tf), lambda i, j: (0, j)),
                  pl.BlockSpec((D, tf), lambda i, j: (0, j + nf)),
                  pl.BlockSpec((3, tf), lambda i, j: (0, j)),
                  pl.BlockSpec((3, tf), lambda i, j: (0, j + nf)),
                  pl.BlockSpec((1, tf), lambda i, j: (0, j)),
                  pl.BlockSpec((1, tf), lambda i, j: (0, j + nf)),
                  pl.BlockSpec((tf, D), lambda i, j: (j, 0))],
        out_specs=pl.BlockSpec((tm, D), lambda i, j: (i, 0)),
        out_shape=jax.ShapeDtypeStruct((M, D), F32),
        scratch_shapes=[pltpu.VMEM((tm + 2 * SUBLANES, D), BF16)],
        compiler_params=_cparams(("parallel", "arbitrary")),
        name="conv_ffn",
    )(h, h, h, gain, mod, mod, mod, w_up, w_up, conv_w, conv_w, conv_b, conv_b, w_down)


def _conv4(xp_ref, x_ref, xn_ref, w_ref, pos, seq_len):
    tm = x_ref.shape[0]
    xe = jnp.concatenate([xp_ref[...], x_ref[...], xn_ref[...]], axis=0)
    t0 = jnp.where(pos >= 1, xe[SUBLANES - 1:SUBLANES - 1 + tm], 0.0)
    t1 = xe[SUBLANES:SUBLANES + tm]
    t2 = jnp.where(pos + 1 < seq_len, xe[SUBLANES + 1:SUBLANES + 1 + tm], 0.0)
    t3 = jnp.where(pos + 2 < seq_len, xe[SUBLANES + 2:SUBLANES + 2 + tm], 0.0)
    return t0 * w_ref[0:1, :] + t1 * w_ref[1:2, :] + t2 * w_ref[2:3, :] + t3 * w_ref[3:4, :]


def _shift_rows(x, k, fill, rev):
    n = x.shape[0]
    if k % SUBLANES == 0:
        pad = jnp.full((k, x.shape[1]), fill, x.dtype)
        return jnp.concatenate([x[k:], pad], 0) if rev else jnp.concatenate([pad, x[:n - k]], 0)
    row = lax.broadcasted_iota(jnp.int32, x.shape, 0)
    if rev:
        return jnp.where(row >= n - k, fill, pltpu.roll(x, n - k, 0))
    return jnp.where(row < k, fill, pltpu.roll(x, k, 0))


def _lru_kernel(rev, finish, seq_len, *refs):
    if finish:
        (x_ref, xp_ref, xn_ref, gate_ref, hf_ref, cw_ref, cb_ref, wcat_ref, ba_ref, bx_ref,
         lam_ref, h0_ref, out_ref, st_ref, carry_ref) = refs
    else:
        (x_ref, xp_ref, xn_ref, cw_ref, cb_ref, wcat_ref, ba_ref, bx_ref,
         lam_ref, h0_ref, out_ref, st_ref, carry_ref) = refs
    tc = x_ref.shape[0]
    j = pl.program_id(1)
    nch = pl.num_programs(1)
    c = (nch - 1 - j) if rev else j

    @pl.when(j == 0)
    def _():
        carry_ref[...] = h0_ref[...]

    pos = c * tc + lax.broadcasted_iota(jnp.int32, (tc, 1), 0)
    u = _conv4(xp_ref, x_ref, xn_ref, cw_ref, pos, seq_len) + cb_ref[...]
    rs, is_ = [], []
    for n in range(HEADS):
        z = _dot(u[:, n * HEAD_DIM:(n + 1) * HEAD_DIM], wcat_ref[n])
        rs.append(z[:, :HEAD_DIM])
        is_.append(z[:, HEAD_DIM:])
    r = _sigmoid(jnp.concatenate(rs, axis=1) + ba_ref[...])
    ig = _sigmoid(jnp.concatenate(is_, axis=1) + bx_ref[...])
    log_a = -LRU_C * r * _softplus(-lam_ref[...])
    a = jnp.exp(log_a)
    b = jnp.sqrt(1.0 - jnp.exp(2.0 * log_a)) * ig * u
    k = 1
    while k < tc:
        b = a * _shift_rows(b, k, 0.0, rev) + b
        a = a * _shift_rows(a, k, 1.0, rev)
        k *= 2
    h = b + a * carry_ref[0:1, :]
    last = h[0:1, :] if rev else h[tc - 1:tc, :]
    carry_ref[...] = jnp.broadcast_to(last, carry_ref.shape)
    st_ref[...] = jnp.broadcast_to(last, st_ref.shape)
    if finish:
        out_ref[...] = ((hf_ref[...] + h) * _gelu_tanh(gate_ref[...])).astype(out_ref.dtype)
    else:
        out_ref[...] = h


def _lru_pass(P, seq_len, rev, h0, wts, hf=None):
    cw, cb, wcat, ba, bx, lam = wts
    M = P.shape[0]
    B = M // seq_len
    tc = 256
    nch = seq_len // tc
    finish = hf is not None
    W = GROUP_W
    xb = P_LRU_X // W
    gb = P_LRU_GATE // W
    r = tc // SUBLANES
    last = M // SUBLANES - 1

    def chunk(j):
        return (nch - 1 - j) if rev else j

    row = lambda b, j: (b * nch + chunk(j), xb)
    specs = [pl.BlockSpec((tc, W), row),
             pl.BlockSpec((SUBLANES, W), lambda b, j: (jnp.maximum((b * nch + chunk(j)) * r - 1, 0), xb)),
             pl.BlockSpec((SUBLANES, W), lambda b, j: (jnp.minimum((b * nch + chunk(j) + 1) * r, last), xb))]
    args = [P, P, P]
    if finish:
        specs += [pl.BlockSpec((tc, W), lambda b, j: (b * nch + chunk(j), gb)),
                  pl.BlockSpec((tc, W), lambda b, j: (b * nch + chunk(j), 0))]
        args += [P, hf]
    full2 = lambda b, j: (0, 0)
    specs += [pl.BlockSpec((4, W), full2), pl.BlockSpec((1, W), full2),
              pl.BlockSpec((HEADS, HEAD_DIM, 2 * HEAD_DIM), lambda b, j: (0, 0, 0)),
              pl.BlockSpec((1, W), full2), pl.BlockSpec((1, W), full2), pl.BlockSpec((1, W), full2),
              pl.BlockSpec((None, SUBLANES, W), lambda b, j: (b, 0, 0))]
    args += [cw, cb, wcat, ba, bx, lam, h0]
    out_dtype = BF16 if finish else F32
    return pl.pallas_call(
        functools.partial(_lru_kernel, rev, finish, seq_len),
        grid=(B, nch),
        in_specs=specs,
        out_specs=[pl.BlockSpec((tc, W), lambda b, j: (b * nch + chunk(j), 0)),
                   pl.BlockSpec((None, SUBLANES, W), lambda b, j: (b, 0, 0))],
        out_shape=[jax.ShapeDtypeStruct((M, W), out_dtype),
                   jax.ShapeDtypeStruct((B, SUBLANES, W), F32)],
        scratch_shapes=[pltpu.VMEM((SUBLANES, W), F32)],
        compiler_params=_cparams(("parallel", "arbitrary")),
        name="rglru_bwd" if rev else "rglru_fwd",
    )(*args)


def _mixer_lru(Pc, Pl, ctx_len, seq_len, wts_f, wts_b):
    B = Pc.shape[0] // ctx_len
    zero = jnp.zeros((B, SUBLANES, GROUP_W), F32)
    hc_f, sc_f = _lru_pass(Pc, ctx_len, False, zero, wts_f)
    hl_f, _ = _lru_pass(Pl, seq_len, False, sc_f, wts_f)
    yc, sc_b = _lru_pass(Pc, ctx_len, True, zero, wts_b, hf=hc_f)
    yl, _ = _lru_pass(Pl, seq_len, True, sc_b, wts_b, hf=hl_f)
    return yc, yl


def _rope_mix(x, cos, sin):
    lane = lax.broadcasted_iota(jnp.int32, x.shape, 1)
    half = MLA_ROPE // 2
    swapped = jnp.where(lane < half, pltpu.roll(x, LANES - half, 1), pltpu.roll(x, half, 1))
    return x * cos + swapped * sin


def _mla_prep_kernel(use_rope, x_ref, qag_ref, wuq_ref, kvg_ref, wukv_ref, qg_ref, kg_ref,
                     cos_ref, sin_ref, q_out, k_out, v_out):
    x = x_ref[...]

    def rms(v, g):
        return v * lax.rsqrt(jnp.mean(v * v, axis=-1, keepdims=True) + EPS) * g

    q = _dot(rms(x[:, :MLA_Q_LORA], qag_ref[...]), wuq_ref[...])
    kv = _dot(rms(x[:, MLA_Q_LORA:MLA_Q_LORA + MLA_KV_LORA], kvg_ref[...]), wukv_ref[...])
    tail = x[:, MLA_Q_LORA + MLA_KV_LORA:]
    lane = lax.broadcasted_iota(jnp.int32, tail.shape, 1)
    kr = jnp.where(lane < MLA_ROPE, tail, 0.0)
    kr_ss = jnp.sum(kr * kr, axis=-1, keepdims=True)
    scale = MLA_QK ** -0.5
    for h in range(HEADS):
        qh = q[:, h * MLA_QPAD:(h + 1) * MLA_QPAD]
        inv = lax.rsqrt(jnp.sum(qh * qh, axis=-1, keepdims=True) / MLA_QK + EPS) * scale
        qh = qh * inv * qg_ref[...]
        qn, qr = qh[:, :LANES], qh[:, LANES:]
        kn = kv[:, h * 2 * HEAD_DIM:h * 2 * HEAD_DIM + MLA_NOPE]
        vh = kv[:, h * 2 * HEAD_DIM + MLA_NOPE:(h + 1) * 2 * HEAD_DIM]
        kinv = lax.rsqrt((jnp.sum(kn * kn, axis=-1, keepdims=True) + kr_ss) / MLA_QK + EPS)
        kn = kn * kinv * kg_ref[:, :LANES]
        krh = kr * kinv * kg_ref[:, LANES:]
        if use_rope:
            qr = _rope_mix(qr, cos_ref[...], sin_ref[...])
            krh = _rope_mix(krh, cos_ref[...], sin_ref[...])
        q_out[h, :, :LANES] = qn.astype(BF16)
        q_out[h, :, LANES:] = qr.astype(BF16)
        k_out[h, :, :LANES] = kn.astype(BF16)
        k_out[h, :, LANES:] = krh.astype(BF16)
        v_out[h] = vh.astype(BF16)


def _mla_prep(P, seq_len, use_rope, wts, cosf, sinf):
    qag, wuq, kvg, wukv, qg, kg = wts
    M = P.shape[0]
    tm = 256
    nt = seq_len // tm
    full = lambda i: (0, 0)
    wmla = P_COLS - P_MLA
    tab = pl.BlockSpec((tm, LANES), (lambda i: (i % nt, 0)) if use_rope else (lambda i: (0, 0)))
    hspec = lambda w: pl.BlockSpec((HEADS, tm, w), lambda i: (0, i, 0))
    return pl.pallas_call(
        functools.partial(_mla_prep_kernel, use_rope),
        grid=(M // tm,),
        in_specs=[pl.BlockSpec((tm, wmla), lambda i: (i, P_MLA // wmla)),
                  pl.BlockSpec((1, MLA_Q_LORA), full), pl.BlockSpec(wuq.shape, full),
                  pl.BlockSpec((1, MLA_KV_LORA), full), pl.BlockSpec(wukv.shape, full),
                  pl.BlockSpec((1, MLA_QPAD), full), pl.BlockSpec((1, MLA_QPAD), full),
                  tab, tab],
        out_specs=[hspec(MLA_QPAD), hspec(MLA_QPAD), hspec(HEAD_DIM)],
        out_shape=[jax.ShapeDtypeStruct((HEADS, M, MLA_QPAD), BF16),
                   jax.ShapeDtypeStruct((HEADS, M, MLA_QPAD), BF16),
                   jax.ShapeDtypeStruct((HEADS, M, HEAD_DIM), BF16)],
        compiler_params=_cparams(("parallel",)),
        name="mla_prep",
    )(P, qag, wuq, kvg, wukv, qg, kg, cosf, sinf)


def _flash_kernel(tk, q_ref, kc_ref, vc_ref, kl_ref, vl_ref, o_ref, m_ref, l_ref, acc_ref):
    q = q_ref[...]
    s = _dot_nt(q, kc_ref[...])
    m = jnp.max(s, axis=-1, keepdims=True)
    p = jnp.exp(s - m)
    m_ref[...] = m
    l_ref[...] = jnp.sum(p, axis=-1, keepdims=True)
    acc_ref[...] = _dot(p, vc_ref[...])

    def body(j, carry):
        off = pl.multiple_of(j * tk, tk)
        s = _dot_nt(q, kl_ref[pl.ds(off, tk), :])
        m_old = m_ref[...]
        m_new = jnp.maximum(m_old, jnp.max(s, axis=-1, keepdims=True))
        alpha = jnp.exp(m_old - m_new)
        p = jnp.exp(s - m_new)
        l_ref[...] = alpha * l_ref[...] + jnp.sum(p, axis=-1, keepdims=True)
        acc_ref[...] = alpha * acc_ref[...] + _dot(p, vl_ref[pl.ds(off, tk), :])
        m_ref[...] = m_new
        return carry

    lax.fori_loop(0, kl_ref.shape[0] // tk, body, 0)
    o_ref[...] = (acc_ref[...] / l_ref[...]).astype(o_ref.dtype)


def _flash_attention(Ql, Kl, Vl, Kc, Vc, seq_len, ctx_len):
    H, M, dq = Ql.shape
    B = M // seq_len
    tq = 512
    tk = 512
    nq = seq_len // tq
    return pl.pallas_call(
        functools.partial(_flash_kernel, tk),
        grid=(B, H, nq),
        in_specs=[pl.BlockSpec((None, tq, dq), lambda b, h, i: (h, b * nq + i, 0)),
                  pl.BlockSpec((None, ctx_len, dq), lambda b, h, i: (h, b, 0)),
                  pl.BlockSpec((None, ctx_len, HEAD_DIM), lambda b, h, i: (h, b, 0)),
                  pl.BlockSpec((None, seq_len, dq), lambda b, h, i: (h, b, 0)),
                  pl.BlockSpec((None, seq_len, HEAD_DIM), lambda b, h, i: (h, b, 0))],
        out_specs=pl.BlockSpec((tq, HEAD_DIM), lambda b, h, i: (b * nq + i, h)),
        out_shape=jax.ShapeDtypeStruct((M, GROUP_W), BF16),
        scratch_shapes=[pltpu.VMEM((tq, 1), F32), pltpu.VMEM((tq, 1), F32),
                        pltpu.VMEM((tq, HEAD_DIM), F32)],
        compiler_params=_cparams(("parallel", "parallel", "arbitrary")),
        name="mla_attention",
    )(Ql, Kc, Vc, Kl, Vl)


def _ctx_attn_kernel(q_ref, k_ref, v_ref, o_ref):
    s = _dot_nt(q_ref[...], k_ref[...])
    p = jnp.exp(s - jnp.max(s, axis=-1, keepdims=True))
    o = _dot(p, v_ref[...]) / jnp.sum(p, axis=-1, keepdims=True)
    o_ref[...] = o.astype(o_ref.dtype)


def _ctx_attention(Qc, Kc, Vc, ctx_len):
    H, M, dq = Qc.shape
    B = M // ctx_len
    return pl.pallas_call(
        _ctx_attn_kernel,
        grid=(B, H),
        in_specs=[pl.BlockSpec((None, ctx_len, dq), lambda b, h: (h, b, 0)),
                  pl.BlockSpec((None, ctx_len, dq), lambda b, h: (h, b, 0)),
                  pl.BlockSpec((None, ctx_len, HEAD_DIM), lambda b, h: (h, b, 0))],
        out_specs=pl.BlockSpec((ctx_len, HEAD_DIM), lambda b, h: (b, h)),
        out_shape=jax.ShapeDtypeStruct((M, GROUP_W), BF16),
        compiler_params=_cparams(("parallel", "parallel")),
        name="ctx_attention",
    )(Qc, Kc, Vc)


def _na_prep_kernel(x_ref, qg_ref, kg_ref, q_out, k_out, v_out):
    x = x_ref[...]
    scale = HEAD_DIM ** -0.5
    for h in range(HEADS):
        qh = x[:, h * HEAD_DIM:(h + 1) * HEAD_DIM]
        kh = x[:, GROUP_W + h * HEAD_DIM:GROUP_W + (h + 1) * HEAD_DIM]
        vh = x[:, 2 * GROUP_W + h * HEAD_DIM:2 * GROUP_W + (h + 1) * HEAD_DIM]
        qh = qh * (lax.rsqrt(jnp.mean(qh * qh, axis=-1, keepdims=True) + EPS) * scale) * qg_ref[...]
        kh = kh * lax.rsqrt(jnp.mean(kh * kh, axis=-1, keepdims=True) + EPS) * kg_ref[...]
        q_out[h] = qh.astype(BF16)
        k_out[h] = kh.astype(BF16)
        v_out[h] = vh.astype(BF16)


def _na_prep(P, qg, kg):
    M = P.shape[0]
    tm = 256
    w = 3 * GROUP_W
    hspec = pl.BlockSpec((HEADS, tm, HEAD_DIM), lambda i: (0, i, 0))
    sds = jax.ShapeDtypeStruct((HEADS, M, HEAD_DIM), BF16)
    return pl.pallas_call(
        _na_prep_kernel,
        grid=(M // tm,),
        in_specs=[pl.BlockSpec((tm, w), lambda i: (i, P_NA_QKV // w)),
                  pl.BlockSpec((1, HEAD_DIM), lambda i: (0, 0)),
                  pl.BlockSpec((1, HEAD_DIM), lambda i: (0, 0))],
        out_specs=[hspec, hspec, hspec],
        out_shape=[sds, sds, sds],
        compiler_params=_cparams(("parallel",)),
        name="na_prep",
    )(P, qg, kg)


def _na_bias_kernel(rows, rpb_ref, o_ref):
    h = pl.program_id(0)
    qc = lax.broadcasted_iota(jnp.int32, (GRID_W, GRID_W), 0)
    kc = lax.broadcasted_iota(jnp.int32, (GRID_W, GRID_W), 1)
    dc = jnp.clip(kc - qc + (NA_KW - 1), 0, 2 * NA_KW - 2)
    c0 = jnp.clip(qc - NA_KW // 2, 0, GRID_W - NA_KW)
    col_ok = (kc >= c0) & (kc < c0 + NA_KW)
    neg = jnp.full((GRID_W, GRID_W), NEG_BIG, F32)
    tiles = []
    for dr in range(2 * NA_KH - 1):
        t = jnp.zeros((GRID_W, GRID_W), F32)
        for d in range(2 * NA_KW - 1):
            t = t + jnp.where(dc == d, rpb_ref[h, dr, d], 0.0)
        tiles.append(jnp.where(col_ok, t, NEG_BIG))
    for kind in range(3):
        q_base = (0, NA_KH // 2, rows - NA_BAND)[kind]
        w_base = (0, 0, rows - NA_WIN)[kind]
        for qr in range(NA_BAND):
            r = q_base + qr
            if kind == 1:
                r0 = r - NA_KH // 2
            else:
                r0 = min(max(r - NA_KH // 2, 0), rows - NA_KH)
            for kr in range(NA_WIN):
                ka = w_base + kr
                ok = r0 <= ka < r0 + NA_KH
                blk = tiles[ka - r + NA_KH - 1] if ok else neg
                o_ref[kind, qr * GRID_W:(qr + 1) * GRID_W, kr * GRID_W:(kr + 1) * GRID_W] = blk


def _na_bias(rpb, rows):
    H = rpb.shape[0]
    nq, nk = NA_BAND * GRID_W, NA_WIN * GRID_W
    return pl.pallas_call(
        functools.partial(_na_bias_kernel, rows),
        grid=(H,),
        in_specs=[pl.BlockSpec(memory_space=pltpu.SMEM)],
        out_specs=pl.BlockSpec((None, 3, nq, nk), lambda h: (h, 0, 0, 0)),
        out_shape=jax.ShapeDtypeStruct((H, 3, nq, nk), F32),
        compiler_params=_cparams(("parallel",)),
        name="na_bias",
    )(rpb)


def _na_kernel(seq_len, q_ref, k_ref, v_ref, kc_ref, vc_ref, bias_ref, o_ref):
    nq = q_ref.shape[0]
    nk = bias_ref.shape[1]
    j = pl.program_id(2)
    base = jnp.clip(j * nq - (NA_KH // 2) * GRID_W, 0, seq_len - nk)
    base = pl.multiple_of(base, (NA_KH // 2) * GRID_W)
    q = q_ref[...]
    s_loc = _dot_nt(q, k_ref[pl.ds(base, nk), :]) + bias_ref[...]
    s_ctx = _dot_nt(q, kc_ref[...])
    m = jnp.maximum(jnp.max(s_loc, axis=-1, keepdims=True), jnp.max(s_ctx, axis=-1, keepdims=True))
    p_loc = jnp.exp(s_loc - m)
    p_ctx = jnp.exp(s_ctx - m)
    l = jnp.sum(p_loc, axis=-1, keepdims=True) + jnp.sum(p_ctx, axis=-1, keepdims=True)
    o = _dot(p_loc, v_ref[pl.ds(base, nk), :]) + _dot(p_ctx, vc_ref[...])
    o_ref[...] = (o / l).astype(o_ref.dtype)


def _na_attention(Ql, Kl, Vl, Kc, Vc, bias, seq_len, ctx_len):
    H, M, d = Ql.shape
    B = M // seq_len
    nq, nk = NA_BAND * GRID_W, NA_WIN * GRID_W
    nb = seq_len // nq

    def kind(j):
        return jnp.where(j == 0, 0, jnp.where(j == nb - 1, 2, 1))

    return pl.pallas_call(
        functools.partial(_na_kernel, seq_len),
        grid=(B, H, nb),
        in_specs=[pl.BlockSpec((None, nq, d), lambda b, h, j: (h, b * nb + j, 0)),
                  pl.BlockSpec((None, seq_len, d), lambda b, h, j: (h, b, 0)),
                  pl.BlockSpec((None, seq_len, d), lambda b, h, j: (h, b, 0)),
                  pl.BlockSpec((None, ctx_len, d), lambda b, h, j: (h, b, 0)),
                  pl.BlockSpec((None, ctx_len, d), lambda b, h, j: (h, b, 0)),
                  pl.BlockSpec((None, None, nq, nk), lambda b, h, j: (h, kind(j), 0, 0))],
        out_specs=pl.BlockSpec((nq, d), lambda b, h, j: (b * nb + j, h)),
        out_shape=jax.ShapeDtypeStruct((M, GROUP_W), BF16),
        compiler_params=_cparams(("parallel", "parallel", "arbitrary")),
        name="na_attention",
    )(Ql, Kl, Vl, Kc, Vc, bias)


def _gdn_prep_kernel(seq_len, x_ref, xp_ref, xn_ref, tail_ref, cw_ref, alog_ref, dtb_ref,
                     qkv_out, gates_out):
    tm = x_ref.shape[0]
    pos = (pl.program_id(0) * tm + lax.broadcasted_iota(jnp.int32, (tm, 1), 0)) % seq_len
    y = _silu(_conv4(xp_ref, x_ref, xn_ref, cw_ref, pos, seq_len))
    for h in range(HEADS):
        sl = slice(h * HEAD_DIM, (h + 1) * HEAD_DIM)
        qh = y[:, sl]
        qkv_out[:, sl] = qh * (lax.rsqrt(jnp.sum(qh * qh, axis=-1, keepdims=True) + EPS) * HEAD_DIM ** -0.5)
        sl = slice(GROUP_W + h * HEAD_DIM, GROUP_W + (h + 1) * HEAD_DIM)
        kh = y[:, sl]
        qkv_out[:, sl] = kh * lax.rsqrt(jnp.sum(kh * kh, axis=-1, keepdims=True) + EPS)
    qkv_out[:, 2 * GROUP_W:] = y[:, 2 * GROUP_W:]
    t = tail_ref[...]
    lane = lax.broadcasted_iota(jnp.int32, t.shape, 1)
    alpha = pltpu.roll(t, LANES - TAIL_ALPHA, 1)
    beta = pltpu.roll(t, LANES - TAIL_BETA + 2 * HEADS, 1)
    g = -jnp.exp(alog_ref[...]) * _softplus(alpha + dtb_ref[...])
    gates_out[...] = jnp.where(lane < 2 * HEADS, g, jnp.where(lane < 4 * HEADS, _sigmoid(beta), 0.0))


def _gdn_prep(P, seq_len, cw, alog, dtb):
    M = P.shape[0]
    tm = 256
    w = 3 * GROUP_W
    prev, nxt = _halo_specs(tm, w, 0, M, 1)
    full = lambda i: (0, 0)
    return pl.pallas_call(
        functools.partial(_gdn_prep_kernel, seq_len),
        grid=(M // tm,),
        in_specs=[pl.BlockSpec((tm, w), lambda i: (i, 0)), prev, nxt,
                  pl.BlockSpec((tm, LANES), lambda i: (i, P_TAIL // LANES)),
                  pl.BlockSpec((4, w), full), pl.BlockSpec((1, LANES), full), pl.BlockSpec((1, LANES), full)],
        out_specs=[pl.BlockSpec((tm, w), lambda i: (i, 0)), pl.BlockSpec((tm, LANES), lambda i: (i, 0))],
        out_shape=[jax.ShapeDtypeStruct((M, w), F32), jax.ShapeDtypeStruct((M, LANES), F32)],
        compiler_params=_cparams(("parallel",)),
        name="gdn_prep",
    )(P, P, P, P, cw, alog, dtb)


def _unit_tri_inverse(a, rev):
    C = a.shape[0]
    ns = C // SUBLANES
    row = lax.broadcasted_iota(jnp.int32, (SUBLANES, C), 0)
    col = lax.broadcasted_iota(jnp.int32, (SUBLANES, C), 1)
    slabs = [jnp.where(col == row + s * SUBLANES, 1.0, 0.0).astype(F32) for s in range(ns)]
    a_slabs = [a[s * SUBLANES:(s + 1) * SUBLANES, :] for s in range(ns)]
    order = range(C - 1, -1, -1) if rev else range(C)
    for j in order:
        sj, rj = divmod(j, SUBLANES)
        mrow = slabs[sj][rj:rj + 1, :]
        targets = range(0, sj + 1) if rev else range(sj, ns)
        for s in targets:
            slabs[s] = slabs[s] - a_slabs[s][:, j:j + 1] * mrow
    return jnp.concatenate(slabs, axis=0)


def _gdn_kernel(rev, finish, d, *refs):
    if finish:
        (qkv_ref, gates_ref, s0_ref, of_ref, gate_ref, ng_ref, out_ref, sout_ref, s_ref, o_ref) = refs
    else:
        (qkv_ref, gates_ref, s0_ref, out_ref, sout_ref, s_ref) = refs
        o_ref = out_ref
    C = GDN_CHUNK
    nchunk = qkv_ref.shape[0] // C
    j = pl.program_id(1)

    @pl.when(j == 0)
    def _():
        s_ref[...] = s0_ref[...]

    ri = lax.broadcasted_iota(jnp.int32, (C, C), 0)
    ci = lax.broadcasted_iota(jnp.int32, (C, C), 1)
    incl = (ci >= ri) if rev else (ci <= ri)
    strict = (ci > ri) if rev else (ci < ri)
    last_row = 0 if rev else C - 1

    def chunk_body(step, carry):
        cidx = (nchunk - 1 - step) if rev else step
        r0 = pl.multiple_of(cidx * C, C)
        gt = gates_ref[pl.ds(r0, C), :]
        gcum = gt
        k = 1
        while k < C:
            gcum = gcum + _shift_rows(gcum, k, 0.0, rev)
            k *= 2
        grow = gcum.T
        for h in range(HEADS):
            lg = d * HEADS + h
            gc_c = gcum[:, lg:lg + 1]
            gc_r = grow[lg:lg + 1, :]
            beta = gt[:, 2 * HEADS + lg:2 * HEADS + lg + 1]
            gl = gcum[last_row:last_row + 1, lg:lg + 1]
            q = qkv_ref[pl.ds(r0, C), h * HEAD_DIM:(h + 1) * HEAD_DIM]
            kk = qkv_ref[pl.ds(r0, C), GROUP_W + h * HEAD_DIM:GROUP_W + (h + 1) * HEAD_DIM]
            v = qkv_ref[pl.ds(r0, C), 2 * GROUP_W + h * HEAD_DIM:2 * GROUP_W + (h + 1) * HEAD_DIM]
            decay = jnp.where(incl, jnp.exp(jnp.where(incl, gc_c - gc_r, 0.0)), 0.0)
            kb = kk * beta
            a = jnp.where(strict, _dot_nt(kb, kk) * decay, 0.0)
            qk = _dot_nt(q, kk) * decay
            tinv = _unit_tri_inverse(a, rev)
            eg = jnp.exp(gc_c)
            uw = _dot(tinv, jnp.concatenate([v * beta, kb * eg], axis=1))
            u, w = uw[:, :HEAD_DIM], uw[:, HEAD_DIM:]
            s = s_ref[h]
            ws = _dot(jnp.concatenate([w, q * eg], axis=0), s)
            v_new = u - ws[:C]
            o = ws[C:] + _dot(qk, v_new)
            s_ref[h] = s * jnp.exp(gl) + _dot_tn(kk * jnp.exp(gl - gc_c), v_new)
            o_ref[pl.ds(r0, C), h * HEAD_DIM:(h + 1) * HEAD_DIM] = o
        return carry

    lax.fori_loop(0, nchunk, chunk_body, 0)
    sout_ref[...] = s_ref[...]
    if finish:
        tot = of_ref[...] + o_ref[...]
        gate = gate_ref[...]
        for h in range(HEADS):
            sl = slice(h * HEAD_DIM, (h + 1) * HEAD_DIM)
            x = tot[:, sl]
            y = x * lax.rsqrt(jnp.mean(x * x, axis=-1, keepdims=True) + EPS) * ng_ref[...]
            out_ref[:, sl] = (y * _silu(gate[:, sl])).astype(out_ref.dtype)


def _gdn_pass(qkv, gates, seq_len, rev, d, s0, of=None, P=None, ng=None):
    M = qkv.shape[0]
    B = M // seq_len
    rb = 256
    nblk = seq_len // rb
    finish = of is not None
    w = 3 * GROUP_W

    def blk(j):
        return (nblk - 1 - j) if rev else j

    sspec = pl.BlockSpec((None, HEADS, HEAD_DIM, HEAD_DIM), lambda b, j: (b, 0, 0, 0))
    specs = [pl.BlockSpec((rb, w), lambda b, j: (b * nblk + blk(j), 0)),
             pl.BlockSpec((rb, LANES), lambda b, j: (b * nblk + blk(j), 0)), sspec]
    args = [qkv, gates, s0]
    scratch = [pltpu.VMEM((HEADS, HEAD_DIM, HEAD_DIM), F32)]
    if finish:
        specs += [pl.BlockSpec((rb, GROUP_W), lambda b, j: (b * nblk + blk(j), 0)),
                  pl.BlockSpec((rb, GROUP_W), lambda b, j: (b * nblk + blk(j), P_GDN_GATE // GROUP_W)),
                  pl.BlockSpec((1, HEAD_DIM), lambda b, j: (0, 0))]
        args += [of, P, ng]
        scratch.append(pltpu.VMEM((rb, GROUP_W), F32))
    return pl.pallas_call(
        functools.partial(_gdn_kernel, rev, finish, d),
        grid=(B, nblk),
        in_specs=specs,
        out_specs=[pl.BlockSpec((rb, GROUP_W), lambda b, j: (b * nblk + blk(j), 0)), sspec],
        out_shape=[jax.ShapeDtypeStruct((M, GROUP_W), BF16 if finish else F32),
                   jax.ShapeDtypeStruct((B, HEADS, HEAD_DIM, HEAD_DIM), F32)],
        scratch_shapes=scratch,
        compiler_params=_cparams(("parallel", "arbitrary")),
        name="gdn_bwd" if rev else "gdn_fwd",
    )(*args)


def _mixer_gdn(Pc, Pl, ctx_len, seq_len, cw, alog, dtb, ng):
    B = Pc.shape[0] // ctx_len
    qkv_c, g_c = _gdn_prep(Pc, ctx_len, cw, alog, dtb)
    qkv_l, g_l = _gdn_prep(Pl, seq_len, cw, alog, dtb)
    zero = jnp.zeros((B, HEADS, HEAD_DIM, HEAD_DIM), F32)
    oc_f, sc_f = _gdn_pass(qkv_c, g_c, ctx_len, False, 0, zero)
    ol_f, _ = _gdn_pass(qkv_l, g_l, seq_len, False, 0, sc_f)
    yc, sc_b = _gdn_pass(qkv_c, g_c, ctx_len, True, 1, zero, of=oc_f, P=Pc, ng=ng)
    yl, _ = _gdn_pass(qkv_l, g_l, seq_len, True, 1, sc_b, of=ol_f, P=Pl, ng=ng)
    return yc, yl


def _pack_w_in(w_in):
    off = {}
    o = 0
    for name, n in (('lru_x', 512), ('lru_gate', 512), ('mla_cq', 384), ('mla_ckv', 256), ('mla_kr', 64),
                    ('gdn_q', 512), ('gdn_k', 512), ('gdn_v', 512), ('gdn_gate', 512),
                    ('gdn_beta', 8), ('gdn_alpha', 8), ('na_q', 512), ('na_k', 512), ('na_v', 512)):
        off[name] = (o, n)
        o += n
    order = ['gdn_q', 'gdn_k', 'gdn_v', 'gdn_gate', 'lru_x', 'lru_gate', 'na_q', 'na_k', 'na_v',
             'mla_cq', 'mla_ckv', 'mla_kr', 'gdn_beta', 'gdn_alpha']
    parts = [w_in[..., off[n][0]:off[n][0] + off[n][1]] for n in order]
    used = sum(off[n][1] for n in order)
    parts.append(jnp.zeros(w_in.shape[:-1] + (P_COLS - used,), w_in.dtype))
    return jnp.concatenate(parts, axis=-1).astype(BF16)


def _pack_w_uq(w_uq):
    L, K, _ = w_uq.shape
    w = w_uq.reshape(L, K, HEADS, MLA_QK)
    w = jnp.pad(w, ((0, 0), (0, 0), (0, 0), (0, MLA_QPAD - MLA_QK)))
    return w.reshape(L, K, HEADS * MLA_QPAD).astype(BF16)


def _pad_gain(g):
    return jnp.pad(g, ((0, 0), (0, MLA_QPAD - MLA_QK)))[:, None, :]


def _rope_tables(T):
    t = jnp.arange(T)
    rowp = (t // GRID_W).astype(F32)
    colp = (t % GRID_W).astype(F32)
    n_freq = MLA_ROPE // 4
    inv = ROPE_BASE ** (-jnp.arange(n_freq, dtype=F32) / n_freq)
    ang = jnp.concatenate([rowp[:, None] * inv, colp[:, None] * inv], axis=-1)
    cos, sin = jnp.cos(ang), jnp.sin(ang)
    z = jnp.zeros((T, LANES - MLA_ROPE), F32)
    return (jnp.concatenate([cos, cos, z], axis=-1), jnp.concatenate([-sin, sin, z], axis=-1))


def _lane_vec(x):
    L = x.shape[0]
    return jnp.pad(x.reshape(L, 2 * HEADS), ((0, 0), (0, LANES - 2 * HEADS)))[:, None, :]


def kernel(x, c, ctx, c_ctx, ada_w, ada_b, norm_mix_g, norm_ffn_g, w_in, w_out, lru_conv_w, lru_conv_b, lru_wa, lru_ba, lru_wx, lru_bx, lru_lam, mla_qa_g, mla_w_uq, mla_kva_g, mla_w_ukv, mla_qn_g, mla_kn_g, gdn_conv_w, gdn_a_log, gdn_dt_bias, gdn_norm_g, na_qn_g, na_kn_g, na_rpb, ffn_w_up, ffn_conv_w, ffn_conv_b, ffn_w_down):
    B, T, D = x.shape
    TC = ctx.shape[1]
    L = ada_w.shape[0]
    rows = T // GRID_W
    assert B <= 2 and T % 512 == 0 and TC == 256 and rows >= NA_WIN

    w_in_p = _pack_w_in(w_in)
    w_out4 = w_out.reshape(L, 4, GROUP_W, D).astype(BF16)
    w_up = ffn_w_up.astype(BF16)
    w_down = ffn_w_down.astype(BF16)
    w_uq = _pack_w_uq(mla_w_uq)
    w_ukv = mla_w_ukv.astype(BF16)
    qn_g = _pad_gain(mla_qn_g)
    kn_g = _pad_gain(mla_kn_g)
    lru_wcat = jnp.concatenate([lru_wa, lru_wx], axis=-1).astype(BF16)
    alog_v = _lane_vec(gdn_a_log)
    dtb_v = _lane_vec(gdn_dt_bias)
    cosf, sinf = _rope_tables(T)

    cvec = jnp.zeros((SUBLANES, D), F32).at[:B].set(c).at[2].set(c_ctx)
    mod_all = _modulation(cvec, ada_w, ada_b).reshape(L, SUBLANES, N_MOD, 1, D)

    h_lat = x.reshape(B * T, D)
    h_ctx = ctx.reshape(B * TC, D)
    for l in range(L):
        want_ctx = l < L - 1
        mod = mod_all[l]
        gmix = norm_mix_g[l][None, :]
        Pl = _inproj(h_lat, gmix, mod, w_in_p[l], T, False)
        Pc = _inproj(h_ctx, gmix, mod, w_in_p[l], TC, True)

        lw = lambda dd: (lru_conv_w[l], lru_conv_b[l][None, :], lru_wcat[l, dd], lru_ba[l, dd][None, :],
                         lru_bx[l, dd][None, :], lru_lam[l, dd][None, :])
        ya_c, ya_l = _mixer_lru(Pc, Pl, TC, T, lw(0), lw(1))

        mw = (mla_qa_g[l][None, :], w_uq[l], mla_kva_g[l][None, :], w_ukv[l], qn_g[l], kn_g[l])
        Qc, Kc, Vc = _mla_prep(Pc, TC, False, mw, cosf, sinf)
        Ql, Kl, Vl = _mla_prep(Pl, T, True, mw, cosf, sinf)
        yb_l = _flash_attention(Ql, Kl, Vl, Kc, Vc, T, TC)

        yc_c, yc_l = _mixer_gdn(Pc, Pl, TC, T, gdn_conv_w[l], alog_v[l], dtb_v[l], gdn_norm_g[l][None, :])

        nqg, nkg = na_qn_g[l][None, :], na_kn_g[l][None, :]
        NQc, NKc, NVc = _na_prep(Pc, nqg, nkg)
        NQl, NKl, NVl = _na_prep(Pl, nqg, nkg)
        bias = _na_bias(na_rpb[l], rows)
        yd_l = _na_attention(NQl, NKl, NVl, NKc, NVc, bias, T, TC)

        h_lat = _outproj((ya_l, yb_l, yc_l, yd_l), w_out4[l], h_lat, mod, T, False)
        gffn = norm_ffn_g[l][None, :]
        h_lat = _ffn(h_lat, gffn, mod, w_up[l], ffn_conv_w[l], ffn_conv_b[l][None, :], w_down[l], T, False)
        if want_ctx:
            yb_c = _ctx_attention(Qc, Kc, Vc, TC)
            yd_c = _ctx_attention(NQc, NKc, NVc, TC)
            h_ctx = _outproj((ya_c, yb_c, yc_c, yd_c), w_out4[l], h_ctx, mod, TC, True)
            h_ctx = _ffn(h_ctx, gffn, mod, w_up[l], ffn_conv_w[l], ffn_conv_b[l][None, :], w_down[l], TC, True)
    return h_lat.reshape(B, T, D)
```

```python
import functools
import math

import jax
import jax.numpy as jnp
from jax import lax
from jax.experimental import pallas as pl
from jax.experimental.pallas import tpu as pltpu

F32 = jnp.float32
BF16 = jnp.bfloat16

GRID_W = 64
HEADS = 4
HEAD_DIM = 128
GROUP_W = HEADS * HEAD_DIM
N_MOD = 6
EPS = 1e-6
LRU_C = 8.0
MLA_Q_LORA = 384
MLA_KV_LORA = 256
MLA_NOPE = 128
MLA_ROPE = 64
MLA_QK = MLA_NOPE + MLA_ROPE
MLA_QPAD = 256
ROPE_BASE = 10000.0
GDN_CHUNK = 64
NA_KH = 8
NA_KW = 16
NA_BAND = 8
NA_WIN = 16
NEG_BIG = -1e30
LOG2E = math.log2(math.e)
FLASH_UNROLL = 4

VMEM_LIMIT = 56 * 1024 * 1024
SUBLANES = 8
LANES = 128

P_GDN_QKV = 0
P_GDN_GATE = 1536
P_LRU_X = 2048
P_LRU_GATE = 2560
P_NA_QKV = 3072
P_MLA = 4608
P_COLS = 5376
P_TAIL = 5248
TAIL_BETA = 64
TAIL_ALPHA = 72


def _cparams(sem):
    return pltpu.CompilerParams(dimension_semantics=sem, vmem_limit_bytes=VMEM_LIMIT)


def _dot(a, b):
    return jnp.dot(a.astype(BF16), b.astype(BF16), preferred_element_type=F32)


def _dot_nt(a, b):
    return lax.dot_general(a.astype(BF16), b.astype(BF16), (((1,), (1,)), ((), ())),
                           preferred_element_type=F32)


def _dot_tn(a, b):
    return lax.dot_general(a.astype(BF16), b.astype(BF16), (((0,), (0,)), ((), ())),
                           preferred_element_type=F32)


def _sigmoid(x):
    return 1.0 / (1.0 + jnp.exp(-x))


def _silu(x):
    return x * _sigmoid(x)


def _softplus(x):
    return jnp.maximum(x, 0.0) + jnp.log(1.0 + jnp.exp(-jnp.abs(x)))


def _gelu_tanh(x):
    return 0.5 * x * (1.0 + jnp.tanh(math.sqrt(2.0 / math.pi) * (x + 0.044715 * x * x * x)))


def _mod_kernel(c_ref, w_ref, b_ref, o_ref):
    c = c_ref[...]
    o_ref[...] = _dot(_silu(c), w_ref[...]) + b_ref[...]


def _modulation(cvec, ada_w, ada_b):
    L, D, N = ada_w.shape
    tn = 1024
    return pl.pallas_call(
        _mod_kernel,
        grid=(L, N // tn),
        in_specs=[pl.BlockSpec((SUBLANES, D), lambda l, j: (0, 0)),
                  pl.BlockSpec((None, D, tn), lambda l, j: (l, 0, j)),
                  pl.BlockSpec((None, 1, tn), lambda l, j: (l, 0, j))],
        out_specs=pl.BlockSpec((None, SUBLANES, tn), lambda l, j: (l, 0, j)),
        out_shape=jax.ShapeDtypeStruct((L, SUBLANES, N), F32),
        compiler_params=_cparams(("parallel", "parallel")),
        name="modulation",
    )(cvec, ada_w, ada_b.reshape(L, 1, N))


def _mod_spec(which, rows_per_mod, D, ngrid):
    if rows_per_mod is None:
        row = lambda i: 2
    else:
        row = lambda i: i // rows_per_mod
    if ngrid == 1:
        return pl.BlockSpec((None, None, 1, D), lambda i: (row(i), which, 0, 0))
    return pl.BlockSpec((None, None, 1, D), lambda i, j: (row(i), which, 0, 0))


def _inproj_kernel(h_ref, g_ref, shift_ref, scale_ref, w_ref, o_ref, xn_ref):
    @pl.when(pl.program_id(1) == 0)
    def _():
        x = h_ref[...]
        y = x * lax.rsqrt(jnp.mean(x * x, axis=-1, keepdims=True) + EPS) * g_ref[...]
        xn_ref[...] = (y * (1.0 + scale_ref[...]) + shift_ref[...]).astype(BF16)

    o_ref[...] = jnp.dot(xn_ref[...], w_ref[...], preferred_element_type=F32)


def _inproj(h, gain, mod, w, seq_len, is_ctx):
    M, D = h.shape
    N = w.shape[1]
    tm = min(512, M)
    tn = 768
    rpm = None if is_ctx else seq_len // tm
    return pl.pallas_call(
        _inproj_kernel,
        grid=(M // tm, N // tn),
        in_specs=[pl.BlockSpec((tm, D), lambda i, j: (i, 0)),
                  pl.BlockSpec((1, D), lambda i, j: (0, 0)),
                  _mod_spec(0, rpm, D, 2),
                  _mod_spec(1, rpm, D, 2),
                  pl.BlockSpec((D, tn), lambda i, j: (0, j))],
        out_specs=pl.BlockSpec((tm, tn), lambda i, j: (i, j)),
        out_shape=jax.ShapeDtypeStruct((M, N), F32),
        scratch_shapes=[pltpu.VMEM((tm, D), BF16)],
        compiler_params=_cparams(("parallel", "arbitrary")),
        name="inproj",
    )(h, gain, mod, mod, w)


def _outproj_kernel(ya_ref, yb_ref, yc_ref, yd_ref, w_ref, h_ref, gate_ref, o_ref):
    acc = jnp.dot(ya_ref[...], w_ref[0], preferred_element_type=F32)
    acc += jnp.dot(yb_ref[...], w_ref[1], preferred_element_type=F32)
    acc += jnp.dot(yc_ref[...], w_ref[2], preferred_element_type=F32)
    acc += jnp.dot(yd_ref[...], w_ref[3], preferred_element_type=F32)
    o_ref[...] = h_ref[...] + gate_ref[...] * acc


def _outproj(ys, w4, h, mod, seq_len, is_ctx):
    M, D = h.shape
    tm = min(512, M)
    rpm = None if is_ctx else seq_len // tm
    yspec = pl.BlockSpec((tm, GROUP_W), lambda i: (i, 0))
    return pl.pallas_call(
        _outproj_kernel,
        grid=(M // tm,),
        in_specs=[yspec, yspec, yspec, yspec,
                  pl.BlockSpec((4, GROUP_W, D), lambda i: (0, 0, 0)),
                  pl.BlockSpec((tm, D), lambda i: (i, 0)),
                  _mod_spec(2, rpm, D, 1)],
        out_specs=pl.BlockSpec((tm, D), lambda i: (i, 0)),
        out_shape=jax.ShapeDtypeStruct((M, D), F32),
        compiler_params=_cparams(("parallel",)),
        name="outproj",
    )(*ys, w4, h, mod)


def _halo_specs(tm, width, col_block, nrows, ngrid):
    r = tm // SUBLANES
    last = nrows // SUBLANES - 1
    if ngrid == 1:
        prev = pl.BlockSpec((SUBLANES, width), lambda i: (jnp.maximum(i * r - 1, 0), col_block))
        nxt = pl.BlockSpec((SUBLANES, width), lambda i: (jnp.minimum((i + 1) * r, last), col_block))
    else:
        prev = pl.BlockSpec((SUBLANES, width), lambda i, j: (jnp.maximum(i * r - 1, 0), col_block))
        nxt = pl.BlockSpec((SUBLANES, width), lambda i, j: (jnp.minimum((i + 1) * r, last), col_block))
    return prev, nxt


def _ffn_kernel(seq_len, h_ref, hp_ref, hn_ref, g_ref, shift_ref, scale_ref, gate_ref,
                wa_ref, wg_ref, cwa_ref, cwg_ref, cba_ref, cbg_ref, wd_ref, o_ref, xn_ref):
    tm = h_ref.shape[0]
    i = pl.program_id(0)
    j = pl.program_id(1)

    @pl.when(j == 0)
    def _():
        def norm(x):
            y = x * lax.rsqrt(jnp.mean(x * x, axis=-1, keepdims=True) + EPS) * g_ref[...]
            return (y * (1.0 + scale_ref[...]) + shift_ref[...]).astype(BF16)
        xn_ref[0:SUBLANES, :] = norm(hp_ref[...])
        xn_ref[SUBLANES:SUBLANES + tm, :] = norm(h_ref[...])
        xn_ref[SUBLANES + tm:, :] = norm(hn_ref[...])

    pos = (i * tm + lax.broadcasted_iota(jnp.int32, (tm, 1), 0)) % seq_len
    has_prev = pos > 0
    has_next = pos < seq_len - 1

    def branch(w_ref, cw_ref, cb_ref):
        u = jnp.dot(xn_ref[...], w_ref[...], preferred_element_type=F32)
        lo = jnp.where(has_prev, u[SUBLANES - 1:SUBLANES - 1 + tm], 0.0)
        mid = u[SUBLANES:SUBLANES + tm]
        hi = jnp.where(has_next, u[SUBLANES + 1:SUBLANES + 1 + tm], 0.0)
        return lo * cw_ref[0:1, :] + mid * cw_ref[1:2, :] + hi * cw_ref[2:3, :] + cb_ref[...]

    a = branch(wa_ref, cwa_ref, cba_ref)
    g = branch(wg_ref, cwg_ref, cbg_ref)
    part = jnp.dot((a * _silu(g)).astype(BF16), wd_ref[...], preferred_element_type=F32)

    @pl.when(j == 0)
    def _():
        o_ref[...] = part

    @pl.when(j > 0)
    def _():
        o_ref[...] += part

    @pl.when(j == pl.num_programs(1) - 1)
    def _():
        o_ref[...] = h_ref[...] + gate_ref[...] * o_ref[...]


def _ffn(h, gain, mod, w_up, conv_w, conv_b, w_down, seq_len, is_ctx):
    M, D = h.shape
    FF = w_down.shape[0]
    tm = min(512, M)
    tf = 512
    nf = FF // tf
    rpm = None if is_ctx else seq_len // tm
    prev, nxt = _halo_specs(tm, D, 0, M, 2)
    return pl.pallas_call(
        functools.partial(_ffn_kernel, seq_len),
        grid=(M // tm, nf),
        in_specs=[pl.BlockSpec((tm, D), lambda i, j: (i, 0)), prev, nxt,
                  pl.BlockSpec((1, D), lambda i, j: (0, 0)),
                  _mod_spec(3, rpm, D, 2), _mod_spec(4, rpm, D, 2), _mod_spec(5, rpm, D, 2),
                  pl.BlockSpec((D, tf), lambda i, j: (0, j)),
                  pl.BlockSpec((D, tf), lambda i, j: (0, j + nf)),
                  pl.BlockSpec((3, tf), lambda i, j: (0, j)),
                  pl.BlockSpec((3, tf), lambda i, j: (0, j + nf)),
                  pl.BlockSpec((1, tf), lambda i, j: (0, j)),
                  pl.BlockSpec((1, tf), lambda i, j: (0, j + nf)),
                  pl.BlockSpec((tf, D), lambda i, j: (j, 0))],
        out_specs=pl.BlockSpec((tm, D), lambda i, j: (i, 0)),
        out_shape=jax.ShapeDtypeStruct((M, D), F32),
        scratch_shapes=[pltpu.VMEM((tm + 2 * SUBLANES, D), BF16)],
        compiler_params=_cparams(("parallel", "arbitrary")),
        name="conv_ffn",
    )(h, h, h, gain, mod, mod, mod, w_up, w_up, conv_w, conv_w, conv_b, conv_b, w_down)


def _conv4(xp_ref, x_ref, xn_ref, w_ref, pos, seq_len):
    tm = x_ref.shape[0]
    xe = jnp.concatenate([xp_ref[...], x_ref[...], xn_ref[...]], axis=0)
    t0 = jnp.where(pos >= 1, xe[SUBLANES - 1:SUBLANES - 1 + tm], 0.0)
    t1 = xe[SUBLANES:SUBLANES + tm]
    t2 = jnp.where(pos + 1 < seq_len, xe[SUBLANES + 1:SUBLANES + 1 + tm], 0.0)
    t3 = jnp.where(pos + 2 < seq_len, xe[SUBLANES + 2:SUBLANES + 2 + tm], 0.0)
    return t0 * w_ref[0:1, :] + t1 * w_ref[1:2, :] + t2 * w_ref[2:3, :] + t3 * w_ref[3:4, :]


def _shift_rows(x, k, fill, rev):
    n = x.shape[0]
    if k % SUBLANES == 0:
        pad = jnp.full((k, x.shape[1]), fill, x.dtype)
        return jnp.concatenate([x[k:], pad], 0) if rev else jnp.concatenate([pad, x[:n - k]], 0)
    row = lax.broadcasted_iota(jnp.int32, x.shape, 0)
    if rev:
        return jnp.where(row >= n - k, fill, pltpu.roll(x, n - k, 0))
    return jnp.where(row < k, fill, pltpu.roll(x, k, 0))


def _lru_kernel(rev, finish, seq_len, *refs):
    if finish:
        (x_ref, xp_ref, xn_ref, gate_ref, hf_ref, cw_ref, cb_ref, wcat_ref, ba_ref, bx_ref,
         lam_ref, h0_ref, out_ref, st_ref, carry_ref) = refs
    else:
        (x_ref, xp_ref, xn_ref, cw_ref, cb_ref, wcat_ref, ba_ref, bx_ref,
         lam_ref, h0_ref, out_ref, st_ref, carry_ref) = refs
    tc = x_ref.shape[0]
    j = pl.program_id(1)
    nch = pl.num_programs(1)
    c = (nch - 1 - j) if rev else j

    @pl.when(j == 0)
    def _():
        carry_ref[...] = h0_ref[...]

    pos = c * tc + lax.broadcasted_iota(jnp.int32, (tc, 1), 0)
    u = _conv4(xp_ref, x_ref, xn_ref, cw_ref, pos, seq_len) + cb_ref[...]
    rs, is_ = [], []
    for n in range(HEADS):
        z = _dot(u[:, n * HEAD_DIM:(n + 1) * HEAD_DIM], wcat_ref[n])
        rs.append(z[:, :HEAD_DIM])
        is_.append(z[:, HEAD_DIM:])
    r = _sigmoid(jnp.concatenate(rs, axis=1) + ba_ref[...])
    ig = _sigmoid(jnp.concatenate(is_, axis=1) + bx_ref[...])
    log_a = -LRU_C * r * _softplus(-lam_ref[...])
    a = jnp.exp(log_a)
    b = jnp.sqrt(1.0 - jnp.exp(2.0 * log_a)) * ig * u
    k = 1
    while k < tc:
        b = a * _shift_rows(b, k, 0.0, rev) + b
        a = a * _shift_rows(a, k, 1.0, rev)
        k *= 2
    h = b + a * carry_ref[0:1, :]
    last = h[0:1, :] if rev else h[tc - 1:tc, :]
    carry_ref[...] = jnp.broadcast_to(last, carry_ref.shape)
    st_ref[...] = jnp.broadcast_to(last, st_ref.shape)
    if finish:
        out_ref[...] = ((hf_ref[...] + h) * _gelu_tanh(gate_ref[...])).astype(out_ref.dtype)
    else:
        out_ref[...] = h


def _lru_pass(P, seq_len, rev, h0, wts, hf=None):
    cw, cb, wcat, ba, bx, lam = wts
    M = P.shape[0]
    B = M // seq_len
    tc = 256
    nch = seq_len // tc
    finish = hf is not None
    W = GROUP_W
    xb = P_LRU_X // W
    gb = P_LRU_GATE // W
    r = tc // SUBLANES
    last = M // SUBLANES - 1

    def chunk(j):
        return (nch - 1 - j) if rev else j

    row = lambda b, j: (b * nch + chunk(j), xb)
    specs = [pl.BlockSpec((tc, W), row),
             pl.BlockSpec((SUBLANES, W), lambda b, j: (jnp.maximum((b * nch + chunk(j)) * r - 1, 0), xb)),
             pl.BlockSpec((SUBLANES, W), lambda b, j: (jnp.minimum((b * nch + chunk(j) + 1) * r, last), xb))]
    args = [P, P, P]
    if finish:
        specs += [pl.BlockSpec((tc, W), lambda b, j: (b * nch + chunk(j), gb)),
                  pl.BlockSpec((tc, W), lambda b, j: (b * nch + chunk(j), 0))]
        args += [P, hf]
    full2 = lambda b, j: (0, 0)
    specs += [pl.BlockSpec((4, W), full2), pl.BlockSpec((1, W), full2),
              pl.BlockSpec((HEADS, HEAD_DIM, 2 * HEAD_DIM), lambda b, j: (0, 0, 0)),
              pl.BlockSpec((1, W), full2), pl.BlockSpec((1, W), full2), pl.BlockSpec((1, W), full2),
              pl.BlockSpec((None, SUBLANES, W), lambda b, j: (b, 0, 0))]
    args += [cw, cb, wcat, ba, bx, lam, h0]
    out_dtype = BF16 if finish else F32
    return pl.pallas_call(
        functools.partial(_lru_kernel, rev, finish, seq_len),
        grid=(B, nch),
        in_specs=specs,
        out_specs=[pl.BlockSpec((tc, W), lambda b, j: (b * nch + chunk(j), 0)),
                   pl.BlockSpec((None, SUBLANES, W), lambda b, j: (b, 0, 0))],
        out_shape=[jax.ShapeDtypeStruct((M, W), out_dtype),
                   jax.ShapeDtypeStruct((B, SUBLANES, W), F32)],
        scratch_shapes=[pltpu.VMEM((SUBLANES, W), F32)],
        compiler_params=_cparams(("parallel", "arbitrary")),
        name="rglru_bwd" if rev else "rglru_fwd",
    )(*args)


def _mixer_lru(Pc, Pl, ctx_len, seq_len, wts_f, wts_b):
    B = Pc.shape[0] // ctx_len
    zero = jnp.zeros((B, SUBLANES, GROUP_W), F32)
    hc_f, sc_f = _lru_pass(Pc, ctx_len, False, zero, wts_f)
    hl_f, _ = _lru_pass(Pl, seq_len, False, sc_f, wts_f)
    yc, sc_b = _lru_pass(Pc, ctx_len, True, zero, wts_b, hf=hc_f)
    yl, _ = _lru_pass(Pl, seq_len, True, sc_b, wts_b, hf=hl_f)
    return yc, yl


def _rope_mix(x, cos, sin):
    lane = lax.broadcasted_iota(jnp.int32, x.shape, 1)
    half = MLA_ROPE // 2
    swapped = jnp.where(lane < half, pltpu.roll(x, LANES - half, 1), pltpu.roll(x, half, 1))
    return x * cos + swapped * sin


def _mla_prep_kernel(use_rope, x_ref, qag_ref, wuq_ref, kvg_ref, wukv_ref, qg_ref, kg_ref,
                     cos_ref, sin_ref, qt_out, k_out, vt_out):
    x = x_ref[...]

    def rms(v, g):
        return v * lax.rsqrt(jnp.mean(v * v, axis=-1, keepdims=True) + EPS) * g

    q = _dot(rms(x[:, :MLA_Q_LORA], qag_ref[...]), wuq_ref[...])
    kv = _dot(rms(x[:, MLA_Q_LORA:MLA_Q_LORA + MLA_KV_LORA], kvg_ref[...]), wukv_ref[...])
    tail = x[:, MLA_Q_LORA + MLA_KV_LORA:]
    lane = lax.broadcasted_iota(jnp.int32, tail.shape, 1)
    kr = jnp.where(lane < MLA_ROPE, tail, 0.0)
    kr_ss = jnp.sum(kr * kr, axis=-1, keepdims=True)
    scale = MLA_QK ** -0.5 * LOG2E
    for h in range(HEADS):
        qh = q[:, h * MLA_QPAD:(h + 1) * MLA_QPAD]
        inv = lax.rsqrt(jnp.sum(qh * qh, axis=-1, keepdims=True) / MLA_QK + EPS) * scale
        qh = qh * inv * qg_ref[...]
        qn, qr = qh[:, :LANES], qh[:, LANES:]
        kn = kv[:, h * 2 * HEAD_DIM:h * 2 * HEAD_DIM + MLA_NOPE]
        vh = kv[:, h * 2 * HEAD_DIM + MLA_NOPE:(h + 1) * 2 * HEAD_DIM]
        kinv = lax.rsqrt((jnp.sum(kn * kn, axis=-1, keepdims=True) + kr_ss) / MLA_QK + EPS)
        kn = kn * kinv * kg_ref[:, :LANES]
        krh = kr * kinv * kg_ref[:, LANES:]
        if use_rope:
            qr = _rope_mix(qr, cos_ref[...], sin_ref[...])
            krh = _rope_mix(krh, cos_ref[...], sin_ref[...])
        qt_out[h] = jnp.concatenate([qn, qr], axis=1).T.astype(BF16)
        k_out[h, :, :LANES] = kn.astype(BF16)
        k_out[h, :, LANES:] = krh.astype(BF16)
        vt_out[h] = vh.T.astype(BF16)


def _mla_prep(P, seq_len, use_rope, wts, cosf, sinf):
    qag, wuq, kvg, wukv, qg, kg = wts
    M = P.shape[0]
    tm = 256
    nt = seq_len // tm
    full = lambda i: (0, 0)
    wmla = P_COLS - P_MLA
    tab = pl.BlockSpec((tm, LANES), (lambda i: (i % nt, 0)) if use_rope else (lambda i: (0, 0)))
    tspec = lambda w: pl.BlockSpec((HEADS, w, tm), lambda i: (0, 0, i))
    return pl.pallas_call(
        functools.partial(_mla_prep_kernel, use_rope),
        grid=(M // tm,),
        in_specs=[pl.BlockSpec((tm, wmla), lambda i: (i, P_MLA // wmla)),
                  pl.BlockSpec((1, MLA_Q_LORA), full), pl.BlockSpec(wuq.shape, full),
                  pl.BlockSpec((1, MLA_KV_LORA), full), pl.BlockSpec(wukv.shape, full),
                  pl.BlockSpec((1, MLA_QPAD), full), pl.BlockSpec((1, MLA_QPAD), full),
                  tab, tab],
        out_specs=[tspec(MLA_QPAD), pl.BlockSpec((HEADS, tm, MLA_QPAD), lambda i: (0, i, 0)),
                   tspec(HEAD_DIM)],
        out_shape=[jax.ShapeDtypeStruct((HEADS, MLA_QPAD, M), BF16),
                   jax.ShapeDtypeStruct((HEADS, M, MLA_QPAD), BF16),
                   jax.ShapeDtypeStruct((HEADS, HEAD_DIM, M), BF16)],
        compiler_params=_cparams(("parallel",)),
        name="mla_prep",
    )(P, qag, wuq, kvg, wukv, qg, kg, cosf, sinf)


def _flash_kernel(tk, has_lat, *refs):
    if has_lat:
        qt_ref, kc_ref, vct_ref, kl_ref, vlt_ref, o_ref, m_ref, l_ref, acc_ref, s_ref = refs
    else:
        qt_ref, kc_ref, vct_ref, o_ref = refs
    qt = qt_ref[...]
    s = jnp.dot(kc_ref[...], qt, preferred_element_type=F32)
    m = jnp.max(s, axis=0, keepdims=True)
    p = jnp.exp2(s - m)
    l = jnp.sum(p, axis=0, keepdims=True)
    acc = jnp.dot(vct_ref[...], p.astype(BF16), preferred_element_type=F32)
    if has_lat:
        m_ref[...] = m
        l_ref[...] = l
        acc_ref[...] = acc

        nk = kl_ref.shape[0] // tk

        def scores(c, slot):
            off = pl.multiple_of(jnp.minimum(c, nk - 1) * tk, tk)
            s_ref[slot] = jnp.dot(kl_ref[pl.ds(off, tk), :], qt, preferred_element_type=F32)

        def step(c, slot):
            scores(c + 1, 1 - slot)
            off = pl.multiple_of(c * tk, tk)
            s = s_ref[slot]
            m_old = m_ref[...]
            m_new = jnp.maximum(m_old, jnp.max(s, axis=0, keepdims=True))
            alpha = jnp.exp2(m_old - m_new)
            p = jnp.exp2(s - m_new)
            l_ref[...] = alpha * l_ref[...] + jnp.sum(p, axis=0, keepdims=True)
            acc_ref[...] = alpha * acc_ref[...] + jnp.dot(
                vlt_ref[:, pl.ds(off, tk)], p.astype(BF16), preferred_element_type=F32)
            m_ref[...] = m_new

        scores(0, 0)

        unroll = min(FLASH_UNROLL, nk)

        def body(j, carry):
            for u in range(unroll):
                step(unroll * j + u, u % 2)
            return carry

        lax.fori_loop(0, nk // unroll, body, 0)
        acc = acc_ref[...]
        l = l_ref[...]
    o_ref[...] = (acc / l).T.astype(o_ref.dtype)


def _flash_attention(Qt, Kc, Vct, seq_len, ctx_len, Kl=None, Vlt=None):
    H, dq, M = Qt.shape
    B = M // seq_len
    has_lat = Kl is not None
    tq = min(512, seq_len)
    tk = 512
    nq = seq_len // tq
    specs = [pl.BlockSpec((None, dq, tq), lambda b, h, i: (h, 0, b * nq + i)),
             pl.BlockSpec((None, ctx_len, dq), lambda b, h, i: (h, b, 0)),
             pl.BlockSpec((None, HEAD_DIM, ctx_len), lambda b, h, i: (h, 0, b))]
    args = [Qt, Kc, Vct]
    scratch = []
    if has_lat:
        specs += [pl.BlockSpec((None, seq_len, dq), lambda b, h, i: (h, b, 0)),
                  pl.BlockSpec((None, HEAD_DIM, seq_len), lambda b, h, i: (h, 0, b))]
        args += [Kl, Vlt]
        nk = seq_len // tk
        assert nk % 2 == 0 and nk % min(FLASH_UNROLL, nk) == 0
        scratch = [pltpu.VMEM((1, tq), F32), pltpu.VMEM((1, tq), F32), pltpu.VMEM((HEAD_DIM, tq), F32),
                   pltpu.VMEM((2, tk, tq), F32)]
    return pl.pallas_call(
        functools.partial(_flash_kernel, tk, has_lat),
        grid=(B, H, nq),
        in_specs=specs,
        out_specs=pl.BlockSpec((tq, HEAD_DIM), lambda b, h, i: (b * nq + i, h)),
        out_shape=jax.ShapeDtypeStruct((M, GROUP_W), BF16),
        scratch_shapes=scratch,
        compiler_params=_cparams(("parallel", "parallel", "arbitrary")),
        name="mla_attention" if has_lat else "mla_ctx_attention",
    )(*args)


def _ctx_attn_kernel(q_ref, k_ref, v_ref, o_ref):
    s = _dot_nt(q_ref[...], k_ref[...])
    p = jnp.exp(s - jnp.max(s, axis=-1, keepdims=True))
    o = _dot(p, v_ref[...]) / jnp.sum(p, axis=-1, keepdims=True)
    o_ref[...] = o.astype(o_ref.dtype)


def _ctx_attention(Qc, Kc, Vc, ctx_len):
    H, M, dq = Qc.shape
    B = M // ctx_len
    return pl.pallas_call(
        _ctx_attn_kernel,
        grid=(B, H),
        in_specs=[pl.BlockSpec((None, ctx_len, dq), lambda b, h: (h, b, 0)),
                  pl.BlockSpec((None, ctx_len, dq), lambda b, h: (h, b, 0)),
                  pl.BlockSpec((None, ctx_len, HEAD_DIM), lambda b, h: (h, b, 0))],
        out_specs=pl.BlockSpec((ctx_len, HEAD_DIM), lambda b, h: (b, h)),
        out_shape=jax.ShapeDtypeStruct((M, GROUP_W), BF16),
        compiler_params=_cparams(("parallel", "parallel")),
        name="ctx_attention",
    )(Qc, Kc, Vc)


def _na_prep_kernel(x_ref, qg_ref, kg_ref, q_out, k_out, v_out):
    x = x_ref[...]
    scale = HEAD_DIM ** -0.5
    for h in range(HEADS):
        qh = x[:, h * HEAD_DIM:(h + 1) * HEAD_DIM]
        kh = x[:, GROUP_W + h * HEAD_DIM:GROUP_W + (h + 1) * HEAD_DIM]
        vh = x[:, 2 * GROUP_W + h * HEAD_DIM:2 * GROUP_W + (h + 1) * HEAD_DIM]
        qh = qh * (lax.rsqrt(jnp.mean(qh * qh, axis=-1, keepdims=True) + EPS) * scale) * qg_ref[...]
        kh = kh * lax.rsqrt(jnp.mean(kh * kh, axis=-1, keepdims=True) + EPS) * kg_ref[...]
        q_out[h] = qh.astype(BF16)
        k_out[h] = kh.astype(BF16)
        v_out[h] = vh.astype(BF16)


def _na_prep(P, qg, kg):
    M = P.shape[0]
    tm = 256
    w = 3 * GROUP_W
    hspec = pl.BlockSpec((HEADS, tm, HEAD_DIM), lambda i: (0, i, 0))
    sds = jax.ShapeDtypeStruct((HEADS, M, HEAD_DIM), BF16)
    return pl.pallas_call(
        _na_prep_kernel,
        grid=(M // tm,),
        in_specs=[pl.BlockSpec((tm, w), lambda i: (i, P_NA_QKV // w)),
                  pl.BlockSpec((1, HEAD_DIM), lambda i: (0, 0)),
                  pl.BlockSpec((1, HEAD_DIM), lambda i: (0, 0))],
        out_specs=[hspec, hspec, hspec],
        out_shape=[sds, sds, sds],
        compiler_params=_cparams(("parallel",)),
        name="na_prep",
    )(P, qg, kg)


def _na_bias_kernel(rows, rpb_ref, o_ref):
    h = pl.program_id(0)
    qc = lax.broadcasted_iota(jnp.int32, (GRID_W, GRID_W), 0)
    kc = lax.broadcasted_iota(jnp.int32, (GRID_W, GRID_W), 1)
    dc = jnp.clip(kc - qc + (NA_KW - 1), 0, 2 * NA_KW - 2)
    c0 = jnp.clip(qc - NA_KW // 2, 0, GRID_W - NA_KW)
    col_ok = (kc >= c0) & (kc < c0 + NA_KW)
    neg = jnp.full((GRID_W, GRID_W), NEG_BIG, F32)
    tiles = []
    for dr in range(2 * NA_KH - 1):
        t = jnp.zeros((GRID_W, GRID_W), F32)
        for d in range(2 * NA_KW - 1):
            t = t + jnp.where(dc == d, rpb_ref[h, dr, d], 0.0)
        tiles.append(jnp.where(col_ok, t, NEG_BIG))
    for kind in range(3):
        q_base = (0, NA_KH // 2, rows - NA_BAND)[kind]
        w_base = (0, 0, rows - NA_WIN)[kind]
        for qr in range(NA_BAND):
            r = q_base + qr
            if kind == 1:
                r0 = r - NA_KH // 2
            else:
                r0 = min(max(r - NA_KH // 2, 0), rows - NA_KH)
            for kr in range(NA_WIN):
                ka = w_base + kr
                ok = r0 <= ka < r0 + NA_KH
                blk = tiles[ka - r + NA_KH - 1] if ok else neg
                o_ref[kind, qr * GRID_W:(qr + 1) * GRID_W, kr * GRID_W:(kr + 1) * GRID_W] = blk


def _na_bias(rpb, rows):
    H = rpb.shape[0]
    nq, nk = NA_BAND * GRID_W, NA_WIN * GRID_W
    return pl.pallas_call(
        functools.partial(_na_bias_kernel, rows),
        grid=(H,),
        in_specs=[pl.BlockSpec(memory_space=pltpu.SMEM)],
        out_specs=pl.BlockSpec((None, 3, nq, nk), lambda h: (h, 0, 0, 0)),
        out_shape=jax.ShapeDtypeStruct((H, 3, nq, nk), F32),
        compiler_params=_cparams(("parallel",)),
        name="na_bias",
    )(rpb)


def _na_kernel(seq_len, q_ref, k_ref, v_ref, kc_ref, vc_ref, bias_ref, o_ref):
    nq = q_ref.shape[0]
    nk = bias_ref.shape[1]
    j = pl.program_id(2)
    base = jnp.clip(j * nq - (NA_KH // 2) * GRID_W, 0, seq_len - nk)
    base = pl.multiple_of(base, (NA_KH // 2) * GRID_W)
    q = q_ref[...]
    s_loc = _dot_nt(q, k_ref[pl.ds(base, nk), :]) + bias_ref[...]
    s_ctx = _dot_nt(q, kc_ref[...])
    m = jnp.maximum(jnp.max(s_loc, axis=-1, keepdims=True), jnp.max(s_ctx, axis=-1, keepdims=True))
    p_loc = jnp.exp(s_loc - m)
    p_ctx = jnp.exp(s_ctx - m)
    l = jnp.sum(p_loc, axis=-1, keepdims=True) + jnp.sum(p_ctx, axis=-1, keepdims=True)
    o = _dot(p_loc, v_ref[pl.ds(base, nk), :]) + _dot(p_ctx, vc_ref[...])
    o_ref[...] = (o / l).astype(o_ref.dtype)


def _na_attention(Ql, Kl, Vl, Kc, Vc, bias, seq_len, ctx_len):
    H, M, d = Ql.shape
    B = M // seq_len
    nq, nk = NA_BAND * GRID_W, NA_WIN * GRID_W
    nb = seq_len // nq

    def kind(j):
        return jnp.where(j == 0, 0, jnp.where(j == nb - 1, 2, 1))

    return pl.pallas_call(
        functools.partial(_na_kernel, seq_len),
        grid=(B, H, nb),
        in_specs=[pl.BlockSpec((None, nq, d), lambda b, h, j: (h, b * nb + j, 0)),
                  pl.BlockSpec((None, seq_len, d), lambda b, h, j: (h, b, 0)),
                  pl.BlockSpec((None, seq_len, d), lambda b, h, j: (h, b, 0)),
                  pl.BlockSpec((None, ctx_len, d), lambda b, h, j: (h, b, 0)),
                  pl.BlockSpec((None, ctx_len, d), lambda b, h, j: (h, b, 0)),
                  pl.BlockSpec((None, None, nq, nk), lambda b, h, j: (h, kind(j), 0, 0))],
        out_specs=pl.BlockSpec((nq, d), lambda b, h, j: (b * nb + j, h)),
        out_shape=jax.ShapeDtypeStruct((M, GROUP_W), BF16),
        compiler_params=_cparams(("parallel", "parallel", "arbitrary")),
        name="na_attention",
    )(Ql, Kl, Vl, Kc, Vc, bias)


def _gdn_prep_kernel(seq_len, x_ref, xp_ref, xn_ref, tail_ref, cw_ref, alog_ref, dtb_ref,
                     qkv_out, gates_out):
    tm = x_ref.shape[0]
    pos = (pl.program_id(0) * tm + lax.broadcasted_iota(jnp.int32, (tm, 1), 0)) % seq_len
    y = _silu(_conv4(xp_ref, x_ref, xn_ref, cw_ref, pos, seq_len))
    for h in range(HEADS):
        sl = slice(h * HEAD_DIM, (h + 1) * HEAD_DIM)
        qh = y[:, sl]
        qkv_out[:, sl] = qh * (lax.rsqrt(jnp.sum(qh * qh, axis=-1, keepdims=True) + EPS) * HEAD_DIM ** -0.5)
        sl = slice(GROUP_W + h * HEAD_DIM, GROUP_W + (h + 1) * HEAD_DIM)
        kh = y[:, sl]
        qkv_out[:, sl] = kh * lax.rsqrt(jnp.sum(kh * kh, axis=-1, keepdims=True) + EPS)
    qkv_out[:, 2 * GROUP_W:] = y[:, 2 * GROUP_W:]
    t = tail_ref[...]
    lane = lax.broadcasted_iota(jnp.int32, t.shape, 1)
    alpha = pltpu.roll(t, LANES - TAIL_ALPHA, 1)
    beta = pltpu.roll(t, LANES - TAIL_BETA + 2 * HEADS, 1)
    g = -jnp.exp(alog_ref[...]) * _softplus(alpha + dtb_ref[...])
    gates_out[...] = jnp.where(lane < 2 * HEADS, g, jnp.where(lane < 4 * HEADS, _sigmoid(beta), 0.0))


def _gdn_prep(P, seq_len, cw, alog, dtb):
    M = P.shape[0]
    tm = 256
    w = 3 * GROUP_W
    prev, nxt = _halo_specs(tm, w, 0, M, 1)
    full = lambda i: (0, 0)
    return pl.pallas_call(
        functools.partial(_gdn_prep_kernel, seq_len),
        grid=(M // tm,),
        in_specs=[pl.BlockSpec((tm, w), lambda i: (i, 0)), prev, nxt,
                  pl.BlockSpec((tm, LANES), lambda i: (i, P_TAIL // LANES)),
                  pl.BlockSpec((4, w), full), pl.BlockSpec((1, LANES), full), pl.BlockSpec((1, LANES), full)],
        out_specs=[pl.BlockSpec((tm, w), lambda i: (i, 0)), pl.BlockSpec((tm, LANES), lambda i: (i, 0))],
        out_shape=[jax.ShapeDtypeStruct((M, w), F32), jax.ShapeDtypeStruct((M, LANES), F32)],
        compiler_params=_cparams(("parallel",)),
        name="gdn_prep",
    )(P, P, P, P, cw, alog, dtb)


def _unit_tri_inverse(a, rev):
    C = a.shape[0]
    ns = C // SUBLANES
    row = lax.broadcasted_iota(jnp.int32, (SUBLANES, C), 0)
    col = lax.broadcasted_iota(jnp.int32, (SUBLANES, C), 1)
    slabs = [jnp.where(col == row + s * SUBLANES, 1.0, 0.0).astype(F32) for s in range(ns)]
    a_slabs = [a[s * SUBLANES:(s + 1) * SUBLANES, :] for s in range(ns)]
    order = range(C - 1, -1, -1) if rev else range(C)
    for j in order:
        sj, rj = divmod(j, SUBLANES)
        mrow = slabs[sj][rj:rj + 1, :]
        targets = range(0, sj + 1) if rev else range(sj, ns)
        for s in targets:
            slabs[s] = slabs[s] - a_slabs[s][:, j:j + 1] * mrow
    return jnp.concatenate(slabs, axis=0)


def _gdn_kernel(rev, finish, d, *refs):
    if finish:
        (qkv_ref, gates_ref, s0_ref, of_ref, gate_ref, ng_ref, out_ref, sout_ref, s_ref, o_ref) = refs
    else:
        (qkv_ref, gates_ref, s0_ref, out_ref, sout_ref, s_ref) = refs
        o_ref = out_ref
    C = GDN_CHUNK
    nchunk = qkv_ref.shape[0] // C
    j = pl.program_id(1)

    @pl.when(j == 0)
    def _():
        s_ref[...] = s0_ref[...]

    ri = lax.broadcasted_iota(jnp.int32, (C, C), 0)
    ci = lax.broadcasted_iota(jnp.int32, (C, C), 1)
    incl = (ci >= ri) if rev else (ci <= ri)
    strict = (ci > ri) if rev else (ci < ri)
    last_row = 0 if rev else C - 1

    def chunk_body(step, carry):
        cidx = (nchunk - 1 - step) if rev else step
        r0 = pl.multiple_of(cidx * C, C)
        gt = gates_ref[pl.ds(r0, C), :]
        gcum = gt
        k = 1
        while k < C:
            gcum = gcum + _shift_rows(gcum, k, 0.0, rev)
            k *= 2
        grow = gcum.T
        for h in range(HEADS):
            lg = d * HEADS + h
            gc_c = gcum[:, lg:lg + 1]
            gc_r = grow[lg:lg + 1, :]
            beta = gt[:, 2 * HEADS + lg:2 * HEADS + lg + 1]
            gl = gcum[last_row:last_row + 1, lg:lg + 1]
            q = qkv_ref[pl.ds(r0, C), h * HEAD_DIM:(h + 1) * HEAD_DIM]
            kk = qkv_ref[pl.ds(r0, C), GROUP_W + h * HEAD_DIM:GROUP_W + (h + 1) * HEAD_DIM]
            v = qkv_ref[pl.ds(r0, C), 2 * GROUP_W + h * HEAD_DIM:2 * GROUP_W + (h + 1) * HEAD_DIM]
            decay = jnp.where(incl, jnp.exp(jnp.where(incl, gc_c - gc_r, 0.0)), 0.0)
            kb = kk * beta
            a = jnp.where(strict, _dot_nt(kb, kk) * decay, 0.0)
            qk = _dot_nt(q, kk) * decay
            tinv = _unit_tri_inverse(a, rev)
            eg = jnp.exp(gc_c)
            uw = _dot(tinv, jnp.concatenate([v * beta, kb * eg], axis=1))
            u, w = uw[:, :HEAD_DIM], uw[:, HEAD_DIM:]
            s = s_ref[h]
            ws = _dot(jnp.concatenate([w, q * eg], axis=0), s)
            v_new = u - ws[:C]
            o = ws[C:] + _dot(qk, v_new)
            s_ref[h] = s * jnp.exp(gl) + _dot_tn(kk * jnp.exp(gl - gc_c), v_new)
            o_ref[pl.ds(r0, C), h * HEAD_DIM:(h + 1) * HEAD_DIM] = o
        return carry

    lax.fori_loop(0, nchunk, chunk_body, 0)
    sout_ref[...] = s_ref[...]
    if finish:
        tot = of_ref[...] + o_ref[...]
        gate = gate_ref[...]
        for h in range(HEADS):
            sl = slice(h * HEAD_DIM, (h + 1) * HEAD_DIM)
            x = tot[:, sl]
            y = x * lax.rsqrt(jnp.mean(x * x, axis=-1, keepdims=True) + EPS) * ng_ref[...]
            out_ref[:, sl] = (y * _silu(gate[:, sl])).astype(out_ref.dtype)


def _gdn_pass(qkv, gates, seq_len, rev, d, s0, of=None, P=None, ng=None):
    M = qkv.shape[0]
    B = M // seq_len
    rb = 256
    nblk = seq_len // rb
    finish = of is not None
    w = 3 * GROUP_W

    def blk(j):
        return (nblk - 1 - j) if rev else j

    sspec = pl.BlockSpec((None, HEADS, HEAD_DIM, HEAD_DIM), lambda b, j: (b, 0, 0, 0))
    specs = [pl.BlockSpec((rb, w), lambda b, j: (b * nblk + blk(j), 0)),
             pl.BlockSpec((rb, LANES), lambda b, j: (b * nblk + blk(j), 0)), sspec]
    args = [qkv, gates, s0]
    scratch = [pltpu.VMEM((HEADS, HEAD_DIM, HEAD_DIM), F32)]
    if finish:
        specs += [pl.BlockSpec((rb, GROUP_W), lambda b, j: (b * nblk + blk(j), 0)),
                  pl.BlockSpec((rb, GROUP_W), lambda b, j: (b * nblk + blk(j), P_GDN_GATE // GROUP_W)),
                  pl.BlockSpec((1, HEAD_DIM), lambda b, j: (0, 0))]
        args += [of, P, ng]
        scratch.append(pltpu.VMEM((rb, GROUP_W), F32))
    return pl.pallas_call(
        functools.partial(_gdn_kernel, rev, finish, d),
        grid=(B, nblk),
        in_specs=specs,
        out_specs=[pl.BlockSpec((rb, GROUP_W), lambda b, j: (b * nblk + blk(j), 0)), sspec],
        out_shape=[jax.ShapeDtypeStruct((M, GROUP_W), BF16 if finish else F32),
                   jax.ShapeDtypeStruct((B, HEADS, HEAD_DIM, HEAD_DIM), F32)],
        scratch_shapes=scratch,
        compiler_params=_cparams(("parallel", "arbitrary")),
        name="gdn_bwd" if rev else "gdn_fwd",
    )(*args)


def _mixer_gdn(Pc, Pl, ctx_len, seq_len, cw, alog, dtb, ng):
    B = Pc.shape[0] // ctx_len
    qkv_c, g_c = _gdn_prep(Pc, ctx_len, cw, alog, dtb)
    qkv_l, g_l = _gdn_prep(Pl, seq_len, cw, alog, dtb)
    zero = jnp.zeros((B, HEADS, HEAD_DIM, HEAD_DIM), F32)
    oc_f, sc_f = _gdn_pass(qkv_c, g_c, ctx_len, False, 0, zero)
    ol_f, _ = _gdn_pass(qkv_l, g_l, seq_len, False, 0, sc_f)
    yc, sc_b = _gdn_pass(qkv_c, g_c, ctx_len, True, 1, zero, of=oc_f, P=Pc, ng=ng)
    yl, _ = _gdn_pass(qkv_l, g_l, seq_len, True, 1, sc_b, of=ol_f, P=Pl, ng=ng)
    return yc, yl


def _pack_w_in(w_in):
    off = {}
    o = 0
    for name, n in (('lru_x', 512), ('lru_gate', 512), ('mla_cq', 384), ('mla_ckv', 256), ('mla_kr', 64),
                    ('gdn_q', 512), ('gdn_k', 512), ('gdn_v', 512), ('gdn_gate', 512),
                    ('gdn_beta', 8), ('gdn_alpha', 8), ('na_q', 512), ('na_k', 512), ('na_v', 512)):
        off[name] = (o, n)
        o += n
    order = ['gdn_q', 'gdn_k', 'gdn_v', 'gdn_gate', 'lru_x', 'lru_gate', 'na_q', 'na_k', 'na_v',
             'mla_cq', 'mla_ckv', 'mla_kr', 'gdn_beta', 'gdn_alpha']
    parts = [w_in[..., off[n][0]:off[n][0] + off[n][1]] for n in order]
    used = sum(off[n][1] for n in order)
    parts.append(jnp.zeros(w_in.shape[:-1] + (P_COLS - used,), w_in.dtype))
    return jnp.concatenate(parts, axis=-1).astype(BF16)


def _pack_w_uq(w_uq):
    L, K, _ = w_uq.shape
    w = w_uq.reshape(L, K, HEADS, MLA_QK)
    w = jnp.pad(w, ((0, 0), (0, 0), (0, 0), (0, MLA_QPAD - MLA_QK)))
    return w.reshape(L, K, HEADS * MLA_QPAD).astype(BF16)


def _pad_gain(g):
    return jnp.pad(g, ((0, 0), (0, MLA_QPAD - MLA_QK)))[:, None, :]


def _rope_tables(T):
    t = jnp.arange(T)
    rowp = (t // GRID_W).astype(F32)
    colp = (t % GRID_W).astype(F32)
    n_freq = MLA_ROPE // 4
    inv = ROPE_BASE ** (-jnp.arange(n_freq, dtype=F32) / n_freq)
    ang = jnp.concatenate([rowp[:, None] * inv, colp[:, None] * inv], axis=-1)
    cos, sin = jnp.cos(ang), jnp.sin(ang)
    z = jnp.zeros((T, LANES - MLA_ROPE), F32)
    return (jnp.concatenate([cos, cos, z], axis=-1), jnp.concatenate([-sin, sin, z], axis=-1))


def _lane_vec(x):
    L = x.shape[0]
    return jnp.pad(x.reshape(L, 2 * HEADS), ((0, 0), (0, LANES - 2 * HEADS)))[:, None, :]


def kernel(x, c, ctx, c_ctx, ada_w, ada_b, norm_mix_g, norm_ffn_g, w_in, w_out, lru_conv_w, lru_conv_b, lru_wa, lru_ba, lru_wx, lru_bx, lru_lam, mla_qa_g, mla_w_uq, mla_kva_g, mla_w_ukv, mla_qn_g, mla_kn_g, gdn_conv_w, gdn_a_log, gdn_dt_bias, gdn_norm_g, na_qn_g, na_kn_g, na_rpb, ffn_w_up, ffn_conv_w, ffn_conv_b, ffn_w_down):
    B, T, D = x.shape
    TC = ctx.shape[1]
    L = ada_w.shape[0]
    rows = T // GRID_W
    assert B <= 2 and T % 512 == 0 and TC == 256 and rows >= NA_WIN

    w_in_p = _pack_w_in(w_in)
    w_out4 = w_out.reshape(L, 4, GROUP_W, D).astype(BF16)
    w_up = ffn_w_up.astype(BF16)
    w_down = ffn_w_down.astype(BF16)
    w_uq = _pack_w_uq(mla_w_uq)
    w_ukv = mla_w_ukv.astype(BF16)
    qn_g = _pad_gain(mla_qn_g)
    kn_g = _pad_gain(mla_kn_g)
    lru_wcat = jnp.concatenate([lru_wa, lru_wx], axis=-1).astype(BF16)
    alog_v = _lane_vec(gdn_a_log)
    dtb_v = _lane_vec(gdn_dt_bias)
    cosf, sinf = _rope_tables(T)

    cvec = jnp.zeros((SUBLANES, D), F32).at[:B].set(c).at[2].set(c_ctx)
    mod_all = _modulation(cvec, ada_w, ada_b).reshape(L, SUBLANES, N_MOD, 1, D)

    h_lat = x.reshape(B * T, D)
    h_ctx = ctx.reshape(B * TC, D)
    for l in range(L):
        want_ctx = l < L - 1
        mod = mod_all[l]
        gmix = norm_mix_g[l][None, :]
        Pl = _inproj(h_lat, gmix, mod, w_in_p[l], T, False)
        Pc = _inproj(h_ctx, gmix, mod, w_in_p[l], TC, True)

        lw = lambda dd: (lru_conv_w[l], lru_conv_b[l][None, :], lru_wcat[l, dd], lru_ba[l, dd][None, :],
                         lru_bx[l, dd][None, :], lru_lam[l, dd][None, :])
        ya_c, ya_l = _mixer_lru(Pc, Pl, TC, T, lw(0), lw(1))

        mw = (mla_qa_g[l][None, :], w_uq[l], mla_kva_g[l][None, :], w_ukv[l], qn_g[l], kn_g[l])
        Qtc, Kc, Vtc = _mla_prep(Pc, TC, False, mw, cosf, sinf)
        Qtl, Kl, Vtl = _mla_prep(Pl, T, True, mw, cosf, sinf)
        yb_l = _flash_attention(Qtl, Kc, Vtc, T, TC, Kl, Vtl)

        yc_c, yc_l = _mixer_gdn(Pc, Pl, TC, T, gdn_conv_w[l], alog_v[l], dtb_v[l], gdn_norm_g[l][None, :])

        nqg, nkg = na_qn_g[l][None, :], na_kn_g[l][None, :]
        NQc, NKc, NVc = _na_prep(Pc, nqg, nkg)
        NQl, NKl, NVl = _na_prep(Pl, nqg, nkg)
        bias = _na_bias(na_rpb[l], rows)
        yd_l = _na_attention(NQl, NKl, NVl, NKc, NVc, bias, T, TC)

        h_lat = _outproj((ya_l, yb_l, yc_l, yd_l), w_out4[l], h_lat, mod, T, False)
        gffn = norm_ffn_g[l][None, :]
        h_lat = _ffn(h_lat, gffn, mod, w_up[l], ffn_conv_w[l], ffn_conv_b[l][None, :], w_down[l], T, False)
        if want_ctx:
            yb_c = _flash_attention(Qtc, Kc, Vtc, TC, TC)
            yd_c = _ctx_attention(NQc, NKc, NVc, TC)
            h_ctx = _outproj((ya_c, yb_c, yc_c, yd_c), w_out4[l], h_ctx, mod, TC, True)
            h_ctx = _ffn(h_ctx, gffn, mod, w_up[l], ffn_conv_w[l], ffn_conv_b[l][None, :], w_down[l], TC, True)
    return h_lat.reshape(B, T, D)
```

```python
import functools
import math

import jax
import jax.numpy as jnp
from jax import lax
from jax.experimental import pallas as pl
from jax.experimental.pallas import tpu as pltpu

F32 = jnp.float32
BF16 = jnp.bfloat16

GRID_W = 64
HEADS = 4
HEAD_DIM = 128
GROUP_W = HEADS * HEAD_DIM
N_MOD = 6
EPS = 1e-6
LRU_C = 8.0
MLA_Q_LORA = 384
MLA_KV_LORA = 256
MLA_NOPE = 128
MLA_ROPE = 64
MLA_QK = MLA_NOPE + MLA_ROPE
MLA_QPAD = 256
ROPE_BASE = 10000.0
GDN_CHUNK = 64
NA_KH = 8
NA_KW = 16
NA_BAND = 8
NA_WIN = 16
NEG_BIG = -1e30
LOG2E = math.log2(math.e)
FLASH_UNROLL = 4
FFN_SUB = 256

VMEM_LIMIT = 56 * 1024 * 1024
SUBLANES = 8
LANES = 128

P_GDN_QKV = 0
P_GDN_GATE = 1536
P_LRU_X = 2048
P_LRU_GATE = 2560
P_NA_QKV = 3072
P_MLA = 4608
P_COLS = 5376
P_TAIL = 5248
TAIL_BETA = 64
TAIL_ALPHA = 72


def _cparams(sem):
    return pltpu.CompilerParams(dimension_semantics=sem, vmem_limit_bytes=VMEM_LIMIT)


def _dot(a, b):
    return jnp.dot(a.astype(BF16), b.astype(BF16), preferred_element_type=F32)


def _dot_nt(a, b):
    return lax.dot_general(a.astype(BF16), b.astype(BF16), (((1,), (1,)), ((), ())),
                           preferred_element_type=F32)


def _dot_tn(a, b):
    return lax.dot_general(a.astype(BF16), b.astype(BF16), (((0,), (0,)), ((), ())),
                           preferred_element_type=F32)


def _sigmoid(x):
    return 1.0 / (1.0 + jnp.exp(-x))


def _silu(x):
    return x * _sigmoid(x)


def _softplus(x):
    return jnp.maximum(x, 0.0) + jnp.log(1.0 + jnp.exp(-jnp.abs(x)))


def _gelu_tanh(x):
    return 0.5 * x * (1.0 + jnp.tanh(math.sqrt(2.0 / math.pi) * (x + 0.044715 * x * x * x)))


def _mod_kernel(c_ref, w_ref, b_ref, o_ref):
    c = c_ref[...]
    o_ref[...] = _dot(_silu(c), w_ref[...]) + b_ref[...]


def _modulation(cvec, ada_w, ada_b):
    L, D, N = ada_w.shape
    tn = 1024
    return pl.pallas_call(
        _mod_kernel,
        grid=(L, N // tn),
        in_specs=[pl.BlockSpec((SUBLANES, D), lambda l, j: (0, 0)),
                  pl.BlockSpec((None, D, tn), lambda l, j: (l, 0, j)),
                  pl.BlockSpec((None, 1, tn), lambda l, j: (l, 0, j))],
        out_specs=pl.BlockSpec((None, SUBLANES, tn), lambda l, j: (l, 0, j)),
        out_shape=jax.ShapeDtypeStruct((L, SUBLANES, N), F32),
        compiler_params=_cparams(("parallel", "parallel")),
        name="modulation",
    )(cvec, ada_w, ada_b.reshape(L, 1, N))


def _mod_spec(which, rows_per_mod, D, ngrid):
    if rows_per_mod is None:
        row = lambda i: 2
    else:
        row = lambda i: i // rows_per_mod
    if ngrid == 1:
        return pl.BlockSpec((None, None, 1, D), lambda i: (row(i), which, 0, 0))
    return pl.BlockSpec((None, None, 1, D), lambda i, j: (row(i), which, 0, 0))


def _inproj_kernel(h_ref, g_ref, shift_ref, scale_ref, w_ref, o_ref, xn_ref):
    @pl.when(pl.program_id(1) == 0)
    def _():
        x = h_ref[...]
        y = x * lax.rsqrt(jnp.mean(x * x, axis=-1, keepdims=True) + EPS) * g_ref[...]
        xn_ref[...] = (y * (1.0 + scale_ref[...]) + shift_ref[...]).astype(BF16)

    o_ref[...] = jnp.dot(xn_ref[...], w_ref[...], preferred_element_type=F32)


def _inproj(h, gain, mod, w, seq_len, is_ctx):
    M, D = h.shape
    N = w.shape[1]
    tm = min(1024, M)
    tn = 896
    rpm = None if is_ctx else seq_len // tm
    return pl.pallas_call(
        _inproj_kernel,
        grid=(M // tm, N // tn),
        in_specs=[pl.BlockSpec((tm, D), lambda i, j: (i, 0)),
                  pl.BlockSpec((1, D), lambda i, j: (0, 0)),
                  _mod_spec(0, rpm, D, 2),
                  _mod_spec(1, rpm, D, 2),
                  pl.BlockSpec((D, tn), lambda i, j: (0, j))],
        out_specs=pl.BlockSpec((tm, tn), lambda i, j: (i, j)),
        out_shape=jax.ShapeDtypeStruct((M, N), F32),
        scratch_shapes=[pltpu.VMEM((tm, D), BF16)],
        compiler_params=_cparams(("parallel", "arbitrary")),
        name="inproj",
    )(h, gain, mod, mod, w)


def _outproj_kernel(ya_ref, yb_ref, yc_ref, yd_ref, w_ref, h_ref, gate_ref, o_ref):
    acc = jnp.dot(ya_ref[...], w_ref[0], preferred_element_type=F32)
    acc += jnp.dot(yb_ref[...], w_ref[1], preferred_element_type=F32)
    acc += jnp.dot(yc_ref[...], w_ref[2], preferred_element_type=F32)
    acc += jnp.dot(yd_ref[...], w_ref[3], preferred_element_type=F32)
    o_ref[...] = h_ref[...] + gate_ref[...] * acc


def _outproj(ys, w4, h, mod, seq_len, is_ctx):
    M, D = h.shape
    tm = min(512, M)
    rpm = None if is_ctx else seq_len // tm
    yspec = pl.BlockSpec((tm, GROUP_W), lambda i: (i, 0))
    return pl.pallas_call(
        _outproj_kernel,
        grid=(M // tm,),
        in_specs=[yspec, yspec, yspec, yspec,
                  pl.BlockSpec((4, GROUP_W, D), lambda i: (0, 0, 0)),
                  pl.BlockSpec((tm, D), lambda i: (i, 0)),
                  _mod_spec(2, rpm, D, 1)],
        out_specs=pl.BlockSpec((tm, D), lambda i: (i, 0)),
        out_shape=jax.ShapeDtypeStruct((M, D), F32),
        compiler_params=_cparams(("parallel",)),
        name="outproj",
    )(*ys, w4, h, mod)


def _halo_specs(tm, width, col_block, nrows, ngrid):
    r = tm // SUBLANES
    last = nrows // SUBLANES - 1
    if ngrid == 1:
        prev = pl.BlockSpec((SUBLANES, width), lambda i: (jnp.maximum(i * r - 1, 0), col_block))
        nxt = pl.BlockSpec((SUBLANES, width), lambda i: (jnp.minimum((i + 1) * r, last), col_block))
    else:
        prev = pl.BlockSpec((SUBLANES, width), lambda i, j: (jnp.maximum(i * r - 1, 0), col_block))
        nxt = pl.BlockSpec((SUBLANES, width), lambda i, j: (jnp.minimum((i + 1) * r, last), col_block))
    return prev, nxt


def _ffn_kernel(seq_len, h_ref, hp_ref, hn_ref, g_ref, shift_ref, scale_ref, gate_ref,
                wa_ref, wg_ref, cwa_ref, cwg_ref, cba_ref, cbg_ref, wd_ref, o_ref, xn_ref):
    tm = h_ref.shape[0]
    tf = wa_ref.shape[1]
    i = pl.program_id(0)
    j = pl.program_id(1)

    @pl.when(j == 0)
    def _():
        def norm(x):
            y = x * lax.rsqrt(jnp.mean(x * x, axis=-1, keepdims=True) + EPS) * g_ref[...]
            return (y * (1.0 + scale_ref[...]) + shift_ref[...]).astype(BF16)
        keep_prev = (i * tm) % seq_len != 0
        keep_next = ((i + 1) * tm) % seq_len != 0
        xn_ref[0:SUBLANES, :] = jnp.where(keep_prev, norm(hp_ref[...]), jnp.zeros((), BF16))
        xn_ref[SUBLANES:SUBLANES + tm, :] = norm(h_ref[...])
        xn_ref[SUBLANES + tm:, :] = jnp.where(keep_next, norm(hn_ref[...]), jnp.zeros((), BF16))

    def conv(u, cw_ref, cb_ref, cols):
        lo = u[SUBLANES - 1:SUBLANES - 1 + tm]
        mid = u[SUBLANES:SUBLANES + tm]
        hi = u[SUBLANES + 1:SUBLANES + 1 + tm]
        return lo * cw_ref[0:1, cols] + mid * cw_ref[1:2, cols] + hi * cw_ref[2:3, cols] + cb_ref[:, cols]

    xn = xn_ref[...]
    subs = [slice(s, s + FFN_SUB) for s in range(0, tf, FFN_SUB)]
    ups = [(jnp.dot(xn, wa_ref[:, c], preferred_element_type=F32),
            jnp.dot(xn, wg_ref[:, c], preferred_element_type=F32)) for c in subs]
    part = None
    for c, (ua, ug) in zip(subs, ups):
        act = conv(ua, cwa_ref, cba_ref, c) * _silu(conv(ug, cwg_ref, cbg_ref, c))
        d = jnp.dot(act.astype(BF16), wd_ref[c, :], preferred_element_type=F32)
        part = d if part is None else part + d

    @pl.when(j == 0)
    def _():
        o_ref[...] = part

    @pl.when(j > 0)
    def _():
        o_ref[...] += part

    @pl.when(j == pl.num_programs(1) - 1)
    def _():
        o_ref[...] = h_ref[...] + gate_ref[...] * o_ref[...]


def _ffn(h, gain, mod, w_up, conv_w, conv_b, w_down, seq_len, is_ctx):
    M, D = h.shape
    FF = w_down.shape[0]
    tm = min(512, seq_len)
    tf = 1024
    nf = FF // tf
    rpm = None if is_ctx else seq_len // tm
    prev, nxt = _halo_specs(tm, D, 0, M, 2)
    return pl.pallas_call(
        functools.partial(_ffn_kernel, seq_len),
        grid=(M // tm, nf),
        in_specs=[pl.BlockSpec((tm, D), lambda i, j: (i, 0)), prev, nxt,
                  pl.BlockSpec((1, D), lambda i, j: (0, 0)),
                  _mod_spec(3, rpm, D, 2), _mod_spec(4, rpm, D, 2), _mod_spec(5, rpm, D, 2),
                  pl.BlockSpec((D, tf), lambda i, j: (0, j)),
                  pl.BlockSpec((D, tf), lambda i, j: (0, j + nf)),
                  pl.BlockSpec((3, tf), lambda i, j: (0, j)),
                  pl.BlockSpec((3, tf), lambda i, j: (0, j + nf)),
                  pl.BlockSpec((1, tf), lambda i, j: (0, j)),
                  pl.BlockSpec((1, tf), lambda i, j: (0, j + nf)),
                  pl.BlockSpec((tf, D), lambda i, j: (j, 0))],
        out_specs=pl.BlockSpec((tm, D), lambda i, j: (i, 0)),
        out_shape=jax.ShapeDtypeStruct((M, D), F32),
        scratch_shapes=[pltpu.VMEM((tm + 2 * SUBLANES, D), BF16)],
        compiler_params=_cparams(("parallel", "arbitrary")),
        name="conv_ffn",
    )(h, h, h, gain, mod, mod, mod, w_up, w_up, conv_w, conv_w, conv_b, conv_b, w_down)


def _conv4(xp_ref, x_ref, xn_ref, w_ref, pos, seq_len):
    tm = x_ref.shape[0]
    xe = jnp.concatenate([xp_ref[...], x_ref[...], xn_ref[...]], axis=0)
    t0 = jnp.where(pos >= 1, xe[SUBLANES - 1:SUBLANES - 1 + tm], 0.0)
    t1 = xe[SUBLANES:SUBLANES + tm]
    t2 = jnp.where(pos + 1 < seq_len, xe[SUBLANES + 1:SUBLANES + 1 + tm], 0.0)
    t3 = jnp.where(pos + 2 < seq_len, xe[SUBLANES + 2:SUBLANES + 2 + tm], 0.0)
    return t0 * w_ref[0:1, :] + t1 * w_ref[1:2, :] + t2 * w_ref[2:3, :] + t3 * w_ref[3:4, :]


def _shift_rows(x, k, fill, rev):
    n = x.shape[0]
    if k % SUBLANES == 0:
        pad = jnp.full((k, x.shape[1]), fill, x.dtype)
        return jnp.concatenate([x[k:], pad], 0) if rev else jnp.concatenate([pad, x[:n - k]], 0)
    row = lax.broadcasted_iota(jnp.int32, x.shape, 0)
    if rev:
        return jnp.where(row >= n - k, fill, pltpu.roll(x, n - k, 0))
    return jnp.where(row < k, fill, pltpu.roll(x, k, 0))


def _lru_kernel(rev, finish, seq_len, *refs):
    if finish:
        (x_ref, xp_ref, xn_ref, gate_ref, hf_ref, cw_ref, cb_ref, wcat_ref, ba_ref, bx_ref,
         lam_ref, h0_ref, out_ref, st_ref, carry_ref) = refs
    else:
        (x_ref, xp_ref, xn_ref, cw_ref, cb_ref, wcat_ref, ba_ref, bx_ref,
         lam_ref, h0_ref, out_ref, st_ref, carry_ref) = refs
    tc = x_ref.shape[0]
    j = pl.program_id(1)
    nch = pl.num_programs(1)
    c = (nch - 1 - j) if rev else j

    @pl.when(j == 0)
    def _():
        carry_ref[...] = h0_ref[...]

    pos = c * tc + lax.broadcasted_iota(jnp.int32, (tc, 1), 0)
    u = _conv4(xp_ref, x_ref, xn_ref, cw_ref, pos, seq_len) + cb_ref[...]
    rs, is_ = [], []
    for n in range(HEADS):
        z = _dot(u[:, n * HEAD_DIM:(n + 1) * HEAD_DIM], wcat_ref[n])
        rs.append(z[:, :HEAD_DIM])
        is_.append(z[:, HEAD_DIM:])
    r = _sigmoid(jnp.concatenate(rs, axis=1) + ba_ref[...])
    ig = _sigmoid(jnp.concatenate(is_, axis=1) + bx_ref[...])
    log_a = -LRU_C * r * _softplus(-lam_ref[...])
    a = jnp.exp(log_a)
    b = jnp.sqrt(1.0 - jnp.exp(2.0 * log_a)) * ig * u
    k = 1
    while k < tc:
        b = a * _shift_rows(b, k, 0.0, rev) + b
        a = a * _shift_rows(a, k, 1.0, rev)
        k *= 2
    h = b + a * carry_ref[0:1, :]
    last = h[0:1, :] if rev else h[tc - 1:tc, :]
    carry_ref[...] = jnp.broadcast_to(last, carry_ref.shape)
    st_ref[...] = jnp.broadcast_to(last, st_ref.shape)
    if finish:
        out_ref[...] = ((hf_ref[...] + h) * _gelu_tanh(gate_ref[...])).astype(out_ref.dtype)
    else:
        out_ref[...] = h


def _lru_pass(P, seq_len, rev, h0, wts, hf=None):
    cw, cb, wcat, ba, bx, lam = wts
    M = P.shape[0]
    B = M // seq_len
    tc = 256
    nch = seq_len // tc
    finish = hf is not None
    W = GROUP_W
    xb = P_LRU_X // W
    gb = P_LRU_GATE // W
    r = tc // SUBLANES
    last = M // SUBLANES - 1

    def chunk(j):
        return (nch - 1 - j) if rev else j

    row = lambda b, j: (b * nch + chunk(j), xb)
    specs = [pl.BlockSpec((tc, W), row),
             pl.BlockSpec((SUBLANES, W), lambda b, j: (jnp.maximum((b * nch + chunk(j)) * r - 1, 0), xb)),
             pl.BlockSpec((SUBLANES, W), lambda b, j: (jnp.minimum((b * nch + chunk(j) + 1) * r, last), xb))]
    args = [P, P, P]
    if finish:
        specs += [pl.BlockSpec((tc, W), lambda b, j: (b * nch + chunk(j), gb)),
                  pl.BlockSpec((tc, W), lambda b, j: (b * nch + chunk(j), 0))]
        args += [P, hf]
    full2 = lambda b, j: (0, 0)
    specs += [pl.BlockSpec((4, W), full2), pl.BlockSpec((1, W), full2),
              pl.BlockSpec((HEADS, HEAD_DIM, 2 * HEAD_DIM), lambda b, j: (0, 0, 0)),
              pl.BlockSpec((1, W), full2), pl.BlockSpec((1, W), full2), pl.BlockSpec((1, W), full2),
              pl.BlockSpec((None, SUBLANES, W), lambda b, j: (b, 0, 0))]
    args += [cw, cb, wcat, ba, bx, lam, h0]
    out_dtype = BF16 if finish else F32
    return pl.pallas_call(
        functools.partial(_lru_kernel, rev, finish, seq_len),
        grid=(B, nch),
        in_specs=specs,
        out_specs=[pl.BlockSpec((tc, W), lambda b, j: (b * nch + chunk(j), 0)),
                   pl.BlockSpec((None, SUBLANES, W), lambda b, j: (b, 0, 0))],
        out_shape=[jax.ShapeDtypeStruct((M, W), out_dtype),
                   jax.ShapeDtypeStruct((B, SUBLANES, W), F32)],
        scratch_shapes=[pltpu.VMEM((SUBLANES, W), F32)],
        compiler_params=_cparams(("parallel", "arbitrary")),
        name="rglru_bwd" if rev else "rglru_fwd",
    )(*args)


def _mixer_lru(Pc, Pl, ctx_len, seq_len, wts_f, wts_b):
    B = Pc.shape[0] // ctx_len
    zero = jnp.zeros((B, SUBLANES, GROUP_W), F32)
    hc_f, sc_f = _lru_pass(Pc, ctx_len, False, zero, wts_f)
    hl_f, _ = _lru_pass(Pl, seq_len, False, sc_f, wts_f)
    yc, sc_b = _lru_pass(Pc, ctx_len, True, zero, wts_b, hf=hc_f)
    yl, _ = _lru_pass(Pl, seq_len, True, sc_b, wts_b, hf=hl_f)
    return yc, yl


def _rope_mix(x, cos, sin):
    lane = lax.broadcasted_iota(jnp.int32, x.shape, 1)
    half = MLA_ROPE // 2
    swapped = jnp.where(lane < half, pltpu.roll(x, LANES - half, 1), pltpu.roll(x, half, 1))
    return x * cos + swapped * sin


def _mla_prep_kernel(use_rope, x_ref, qag_ref, wuq_ref, kvg_ref, wukv_ref, qg_ref, kg_ref,
                     cos_ref, sin_ref, qt_out, k_out, vt_out):
    x = x_ref[...]

    def rms(v, g):
        return v * lax.rsqrt(jnp.mean(v * v, axis=-1, keepdims=True) + EPS) * g

    q = _dot(rms(x[:, :MLA_Q_LORA], qag_ref[...]), wuq_ref[...])
    kv = _dot(rms(x[:, MLA_Q_LORA:MLA_Q_LORA + MLA_KV_LORA], kvg_ref[...]), wukv_ref[...])
    tail = x[:, MLA_Q_LORA + MLA_KV_LORA:]
    lane = lax.broadcasted_iota(jnp.int32, tail.shape, 1)
    kr = jnp.where(lane < MLA_ROPE, tail, 0.0)
    kr_ss = jnp.sum(kr * kr, axis=-1, keepdims=True)
    scale = MLA_QK ** -0.5 * LOG2E
    for h in range(HEADS):
        qh = q[:, h * MLA_QPAD:(h + 1) * MLA_QPAD]
        inv = lax.rsqrt(jnp.sum(qh * qh, axis=-1, keepdims=True) / MLA_QK + EPS) * scale
        qh = qh * inv * qg_ref[...]
        qn, qr = qh[:, :LANES], qh[:, LANES:]
        kn = kv[:, h * 2 * HEAD_DIM:h * 2 * HEAD_DIM + MLA_NOPE]
        vh = kv[:, h * 2 * HEAD_DIM + MLA_NOPE:(h + 1) * 2 * HEAD_DIM]
        kinv = lax.rsqrt((jnp.sum(kn * kn, axis=-1, keepdims=True) + kr_ss) / MLA_QK + EPS)
        kn = kn * kinv * kg_ref[:, :LANES]
        krh = kr * kinv * kg_ref[:, LANES:]
        if use_rope:
            qr = _rope_mix(qr, cos_ref[...], sin_ref[...])
            krh = _rope_mix(krh, cos_ref[...], sin_ref[...])
        qt_out[h] = jnp.concatenate([qn, qr], axis=1).T.astype(BF16)
        k_out[h, :, :LANES] = kn.astype(BF16)
        k_out[h, :, LANES:] = krh.astype(BF16)
        vt_out[h] = vh.T.astype(BF16)


def _mla_prep(P, seq_len, use_rope, wts, cosf, sinf):
    qag, wuq, kvg, wukv, qg, kg = wts
    M = P.shape[0]
    tm = 256
    nt = seq_len // tm
    full = lambda i: (0, 0)
    wmla = P_COLS - P_MLA
    tab = pl.BlockSpec((tm, LANES), (lambda i: (i % nt, 0)) if use_rope else (lambda i: (0, 0)))
    tspec = lambda w: pl.BlockSpec((HEADS, w, tm), lambda i: (0, 0, i))
    return pl.pallas_call(
        functools.partial(_mla_prep_kernel, use_rope),
        grid=(M // tm,),
        in_specs=[pl.BlockSpec((tm, wmla), lambda i: (i, P_MLA // wmla)),
                  pl.BlockSpec((1, MLA_Q_LORA), full), pl.BlockSpec(wuq.shape, full),
                  pl.BlockSpec((1, MLA_KV_LORA), full), pl.BlockSpec(wukv.shape, full),
                  pl.BlockSpec((1, MLA_QPAD), full), pl.BlockSpec((1, MLA_QPAD), full),
                  tab, tab],
        out_specs=[tspec(MLA_QPAD), pl.BlockSpec((HEADS, tm, MLA_QPAD), lambda i: (0, i, 0)),
                   tspec(HEAD_DIM)],
        out_shape=[jax.ShapeDtypeStruct((HEADS, MLA_QPAD, M), BF16),
                   jax.ShapeDtypeStruct((HEADS, M, MLA_QPAD), BF16),
                   jax.ShapeDtypeStruct((HEADS, HEAD_DIM, M), BF16)],
        compiler_params=_cparams(("parallel",)),
        name="mla_prep",
    )(P, qag, wuq, kvg, wukv, qg, kg, cosf, sinf)


def _flash_kernel(tk, has_lat, *refs):
    if has_lat:
        qt_ref, kc_ref, vct_ref, kl_ref, vlt_ref, o_ref, m_ref, l_ref, acc_ref, s_ref = refs
    else:
        qt_ref, kc_ref, vct_ref, o_ref = refs
    qt = qt_ref[...]
    s = jnp.dot(kc_ref[...], qt, preferred_element_type=F32)
    m = jnp.max(s, axis=0, keepdims=True)
    p = jnp.exp2(s - m)
    l = jnp.sum(p, axis=0, keepdims=True)
    acc = jnp.dot(vct_ref[...], p.astype(BF16), preferred_element_type=F32)
    if has_lat:
        m_ref[...] = m
        l_ref[...] = l
        acc_ref[...] = acc

        nk = kl_ref.shape[0] // tk

        def scores(c, slot):
            off = pl.multiple_of(jnp.minimum(c, nk - 1) * tk, tk)
            s_ref[slot] = jnp.dot(kl_ref[pl.ds(off, tk), :], qt, preferred_element_type=F32)

        def step(c, slot):
            scores(c + 1, 1 - slot)
            off = pl.multiple_of(c * tk, tk)
            s = s_ref[slot]
            m_old = m_ref[...]
            m_new = jnp.maximum(m_old, jnp.max(s, axis=0, keepdims=True))
            alpha = jnp.exp2(m_old - m_new)
            p = jnp.exp2(s - m_new)
            l_ref[...] = alpha * l_ref[...] + jnp.sum(p, axis=0, keepdims=True)
            acc_ref[...] = alpha * acc_ref[...] + jnp.dot(
                vlt_ref[:, pl.ds(off, tk)], p.astype(BF16), preferred_element_type=F32)
            m_ref[...] = m_new

        scores(0, 0)

        unroll = min(FLASH_UNROLL, nk)

        def body(j, carry):
            for u in range(unroll):
                step(unroll * j + u, u % 2)
            return carry

        lax.fori_loop(0, nk // unroll, body, 0)
        acc = acc_ref[...]
        l = l_ref[...]
    o_ref[...] = (acc / l).T.astype(o_ref.dtype)


def _flash_attention(Qt, Kc, Vct, seq_len, ctx_len, Kl=None, Vlt=None):
    H, dq, M = Qt.shape
    B = M // seq_len
    has_lat = Kl is not None
    tq = min(512, seq_len)
    tk = 512
    nq = seq_len // tq
    specs = [pl.BlockSpec((None, dq, tq), lambda b, h, i: (h, 0, b * nq + i)),
             pl.BlockSpec((None, ctx_len, dq), lambda b, h, i: (h, b, 0)),
             pl.BlockSpec((None, HEAD_DIM, ctx_len), lambda b, h, i: (h, 0, b))]
    args = [Qt, Kc, Vct]
    scratch = []
    if has_lat:
        specs += [pl.BlockSpec((None, seq_len, dq), lambda b, h, i: (h, b, 0)),
                  pl.BlockSpec((None, HEAD_DIM, seq_len), lambda b, h, i: (h, 0, b))]
        args += [Kl, Vlt]
        nk = seq_len // tk
        assert nk % 2 == 0 and nk % min(FLASH_UNROLL, nk) == 0
        scratch = [pltpu.VMEM((1, tq), F32), pltpu.VMEM((1, tq), F32), pltpu.VMEM((HEAD_DIM, tq), F32),
                   pltpu.VMEM((2, tk, tq), F32)]
    return pl.pallas_call(
        functools.partial(_flash_kernel, tk, has_lat),
        grid=(B, H, nq),
        in_specs=specs,
        out_specs=pl.BlockSpec((tq, HEAD_DIM), lambda b, h, i: (b * nq + i, h)),
        out_shape=jax.ShapeDtypeStruct((M, GROUP_W), BF16),
        scratch_shapes=scratch,
        compiler_params=_cparams(("parallel", "parallel", "arbitrary")),
        name="mla_attention" if has_lat else "mla_ctx_attention",
    )(*args)


def _ctx_attn_kernel(q_ref, k_ref, v_ref, o_ref):
    s = _dot_nt(q_ref[...], k_ref[...])
    p = jnp.exp(s - jnp.max(s, axis=-1, keepdims=True))
    o = _dot(p, v_ref[...]) / jnp.sum(p, axis=-1, keepdims=True)
    o_ref[...] = o.astype(o_ref.dtype)


def _ctx_attention(Qc, Kc, Vc, ctx_len):
    H, M, dq = Qc.shape
    B = M // ctx_len
    return pl.pallas_call(
        _ctx_attn_kernel,
        grid=(B, H),
        in_specs=[pl.BlockSpec((None, ctx_len, dq), lambda b, h: (h, b, 0)),
                  pl.BlockSpec((None, ctx_len, dq), lambda b, h: (h, b, 0)),
                  pl.BlockSpec((None, ctx_len, HEAD_DIM), lambda b, h: (h, b, 0))],
        out_specs=pl.BlockSpec((ctx_len, HEAD_DIM), lambda b, h: (b, h)),
        out_shape=jax.ShapeDtypeStruct((M, GROUP_W), BF16),
        compiler_params=_cparams(("parallel", "parallel")),
        name="ctx_attention",
    )(Qc, Kc, Vc)


def _na_prep_kernel(x_ref, qg_ref, kg_ref, q_out, k_out, v_out):
    x = x_ref[...]
    scale = HEAD_DIM ** -0.5
    for h in range(HEADS):
        qh = x[:, h * HEAD_DIM:(h + 1) * HEAD_DIM]
        kh = x[:, GROUP_W + h * HEAD_DIM:GROUP_W + (h + 1) * HEAD_DIM]
        vh = x[:, 2 * GROUP_W + h * HEAD_DIM:2 * GROUP_W + (h + 1) * HEAD_DIM]
        qh = qh * (lax.rsqrt(jnp.mean(qh * qh, axis=-1, keepdims=True) + EPS) * scale) * qg_ref[...]
        kh = kh * lax.rsqrt(jnp.mean(kh * kh, axis=-1, keepdims=True) + EPS) * kg_ref[...]
        q_out[h] = qh.astype(BF16)
        k_out[h] = kh.astype(BF16)
        v_out[h] = vh.astype(BF16)


def _na_prep(P, qg, kg):
    M = P.shape[0]
    tm = 256
    w = 3 * GROUP_W
    hspec = pl.BlockSpec((HEADS, tm, HEAD_DIM), lambda i: (0, i, 0))
    sds = jax.ShapeDtypeStruct((HEADS, M, HEAD_DIM), BF16)
    return pl.pallas_call(
        _na_prep_kernel,
        grid=(M // tm,),
        in_specs=[pl.BlockSpec((tm, w), lambda i: (i, P_NA_QKV // w)),
                  pl.BlockSpec((1, HEAD_DIM), lambda i: (0, 0)),
                  pl.BlockSpec((1, HEAD_DIM), lambda i: (0, 0))],
        out_specs=[hspec, hspec, hspec],
        out_shape=[sds, sds, sds],
        compiler_params=_cparams(("parallel",)),
        name="na_prep",
    )(P, qg, kg)


def _na_bias_kernel(rows, rpb_ref, o_ref):
    h = pl.program_id(0)
    qc = lax.broadcasted_iota(jnp.int32, (GRID_W, GRID_W), 0)
    kc = lax.broadcasted_iota(jnp.int32, (GRID_W, GRID_W), 1)
    dc = jnp.clip(kc - qc + (NA_KW - 1), 0, 2 * NA_KW - 2)
    c0 = jnp.clip(qc - NA_KW // 2, 0, GRID_W - NA_KW)
    col_ok = (kc >= c0) & (kc < c0 + NA_KW)
    neg = jnp.full((GRID_W, GRID_W), NEG_BIG, F32)
    tiles = []
    for dr in range(2 * NA_KH - 1):
        t = jnp.zeros((GRID_W, GRID_W), F32)
        for d in range(2 * NA_KW - 1):
            t = t + jnp.where(dc == d, rpb_ref[h, dr, d], 0.0)
        tiles.append(jnp.where(col_ok, t, NEG_BIG))
    for kind in range(3):
        q_base = (0, NA_KH // 2, rows - NA_BAND)[kind]
        w_base = (0, 0, rows - NA_WIN)[kind]
        for qr in range(NA_BAND):
            r = q_base + qr
            if kind == 1:
                r0 = r - NA_KH // 2
            else:
                r0 = min(max(r - NA_KH // 2, 0), rows - NA_KH)
            for kr in range(NA_WIN):
                ka = w_base + kr
                ok = r0 <= ka < r0 + NA_KH
                blk = tiles[ka - r + NA_KH - 1] if ok else neg
                o_ref[kind, qr * GRID_W:(qr + 1) * GRID_W, kr * GRID_W:(kr + 1) * GRID_W] = blk


def _na_bias(rpb, rows):
    H = rpb.shape[0]
    nq, nk = NA_BAND * GRID_W, NA_WIN * GRID_W
    return pl.pallas_call(
        functools.partial(_na_bias_kernel, rows),
        grid=(H,),
        in_specs=[pl.BlockSpec(memory_space=pltpu.SMEM)],
        out_specs=pl.BlockSpec((None, 3, nq, nk), lambda h: (h, 0, 0, 0)),
        out_shape=jax.ShapeDtypeStruct((H, 3, nq, nk), F32),
        compiler_params=_cparams(("parallel",)),
        name="na_bias",
    )(rpb)


def _na_kernel(seq_len, q_ref, k_ref, v_ref, kc_ref, vc_ref, bias_ref, o_ref):
    nq = q_ref.shape[0]
    nk = bias_ref.shape[1]
    j = pl.program_id(2)
    base = jnp.clip(j * nq - (NA_KH // 2) * GRID_W, 0, seq_len - nk)
    base = pl.multiple_of(base, (NA_KH // 2) * GRID_W)
    q = q_ref[...]
    s_loc = _dot_nt(q, k_ref[pl.ds(base, nk), :]) + bias_ref[...]
    s_ctx = _dot_nt(q, kc_ref[...])
    m = jnp.maximum(jnp.max(s_loc, axis=-1, keepdims=True), jnp.max(s_ctx, axis=-1, keepdims=True))
    p_loc = jnp.exp(s_loc - m)
    p_ctx = jnp.exp(s_ctx - m)
    l = jnp.sum(p_loc, axis=-1, keepdims=True) + jnp.sum(p_ctx, axis=-1, keepdims=True)
    o = _dot(p_loc, v_ref[pl.ds(base, nk), :]) + _dot(p_ctx, vc_ref[...])
    o_ref[...] = (o / l).astype(o_ref.dtype)


def _na_attention(Ql, Kl, Vl, Kc, Vc, bias, seq_len, ctx_len):
    H, M, d = Ql.shape
    B = M // seq_len
    nq, nk = NA_BAND * GRID_W, NA_WIN * GRID_W
    nb = seq_len // nq

    def kind(j):
        return jnp.where(j == 0, 0, jnp.where(j == nb - 1, 2, 1))

    return pl.pallas_call(
        functools.partial(_na_kernel, seq_len),
        grid=(B, H, nb),
        in_specs=[pl.BlockSpec((None, nq, d), lambda b, h, j: (h, b * nb + j, 0)),
                  pl.BlockSpec((None, seq_len, d), lambda b, h, j: (h, b, 0)),
                  pl.BlockSpec((None, seq_len, d), lambda b, h, j: (h, b, 0)),
                  pl.BlockSpec((None, ctx_len, d), lambda b, h, j: (h, b, 0)),
                  pl.BlockSpec((None, ctx_len, d), lambda b, h, j: (h, b, 0)),
                  pl.BlockSpec((None, None, nq, nk), lambda b, h, j: (h, kind(j), 0, 0))],
        out_specs=pl.BlockSpec((nq, d), lambda b, h, j: (b * nb + j, h)),
        out_shape=jax.ShapeDtypeStruct((M, GROUP_W), BF16),
        compiler_params=_cparams(("parallel", "parallel", "arbitrary")),
        name="na_attention",
    )(Ql, Kl, Vl, Kc, Vc, bias)


def _gdn_prep_kernel(seq_len, x_ref, xp_ref, xn_ref, tail_ref, cw_ref, alog_ref, dtb_ref,
                     qkv_out, gates_out):
    tm = x_ref.shape[0]
    pos = (pl.program_id(0) * tm + lax.broadcasted_iota(jnp.int32, (tm, 1), 0)) % seq_len
    y = _silu(_conv4(xp_ref, x_ref, xn_ref, cw_ref, pos, seq_len))
    for h in range(HEADS):
        sl = slice(h * HEAD_DIM, (h + 1) * HEAD_DIM)
        qh = y[:, sl]
        qkv_out[:, sl] = qh * (lax.rsqrt(jnp.sum(qh * qh, axis=-1, keepdims=True) + EPS) * HEAD_DIM ** -0.5)
        sl = slice(GROUP_W + h * HEAD_DIM, GROUP_W + (h + 1) * HEAD_DIM)
        kh = y[:, sl]
        qkv_out[:, sl] = kh * lax.rsqrt(jnp.sum(kh * kh, axis=-1, keepdims=True) + EPS)
    qkv_out[:, 2 * GROUP_W:] = y[:, 2 * GROUP_W:]
    t = tail_ref[...]
    lane = lax.broadcasted_iota(jnp.int32, t.shape, 1)
    alpha = pltpu.roll(t, LANES - TAIL_ALPHA, 1)
    beta = pltpu.roll(t, LANES - TAIL_BETA + 2 * HEADS, 1)
    g = -jnp.exp(alog_ref[...]) * _softplus(alpha + dtb_ref[...])
    gates_out[...] = jnp.where(lane < 2 * HEADS, g, jnp.where(lane < 4 * HEADS, _sigmoid(beta), 0.0))


def _gdn_prep(P, seq_len, cw, alog, dtb):
    M = P.shape[0]
    tm = 256
    w = 3 * GROUP_W
    prev, nxt = _halo_specs(tm, w, 0, M, 1)
    full = lambda i: (0, 0)
    return pl.pallas_call(
        functools.partial(_gdn_prep_kernel, seq_len),
        grid=(M // tm,),
        in_specs=[pl.BlockSpec((tm, w), lambda i: (i, 0)), prev, nxt,
                  pl.BlockSpec((tm, LANES), lambda i: (i, P_TAIL // LANES)),
                  pl.BlockSpec((4, w), full), pl.BlockSpec((1, LANES), full), pl.BlockSpec((1, LANES), full)],
        out_specs=[pl.BlockSpec((tm, w), lambda i: (i, 0)), pl.BlockSpec((tm, LANES), lambda i: (i, 0))],
        out_shape=[jax.ShapeDtypeStruct((M, w), F32), jax.ShapeDtypeStruct((M, LANES), F32)],
        compiler_params=_cparams(("parallel",)),
        name="gdn_prep",
    )(P, P, P, P, cw, alog, dtb)


def _time_cumsum(x, rev):
    k = 1
    while k < x.shape[0]:
        x = x + _shift_rows(x, k, 0.0, rev)
        k *= 2
    return x


def _gdn_local_kernel(qkv_ref, gates_ref, a_out, qk_out):
    C = GDN_CHUNK
    ri = lax.broadcasted_iota(jnp.int32, (C, C), 0)
    ci = lax.broadcasted_iota(jnp.int32, (C, C), 1)
    for c in range(qkv_ref.shape[0] // C):
        rows = slice(c * C, (c + 1) * C)
        gt = gates_ref[rows, :]
        for d in range(2):
            rev = d == 1
            incl = (ci >= ri) if rev else (ci <= ri)
            strict = (ci > ri) if rev else (ci < ri)
            gcum = _time_cumsum(gt, rev)
            grow = gcum.T
            for h in range(HEADS):
                lg = d * HEADS + h
                gc_c = gcum[:, lg:lg + 1]
                gc_r = grow[lg:lg + 1, :]
                beta = gt[:, 2 * HEADS + lg:2 * HEADS + lg + 1]
                q = qkv_ref[rows, h * HEAD_DIM:(h + 1) * HEAD_DIM]
                kk = qkv_ref[rows, GROUP_W + h * HEAD_DIM:GROUP_W + (h + 1) * HEAD_DIM]
                decay = jnp.where(incl, jnp.exp(jnp.where(incl, gc_c - gc_r, 0.0)), 0.0)
                z = _dot_nt(jnp.concatenate([kk * beta, q], axis=0), kk)
                lanes = slice((h % 2) * C, (h % 2 + 1) * C)
                a_out[d, c * 2 + h // 2, :, lanes] = jnp.where(strict, z[:C] * decay, 0.0)
                qk_out[d, c * 2 + h // 2, :, lanes] = z[C:] * decay


def _gdn_local(qkv, gates):
    M = qkv.shape[0]
    tm = 256
    npair = tm // GDN_CHUNK * (HEADS // 2)
    w = 3 * GROUP_W
    ospec = pl.BlockSpec((2, npair, GDN_CHUNK, LANES), lambda i: (0, i, 0, 0))
    sds = jax.ShapeDtypeStruct((2, M // GDN_CHUNK * (HEADS // 2), GDN_CHUNK, LANES), F32)
    return pl.pallas_call(
        _gdn_local_kernel,
        grid=(M // tm,),
        in_specs=[pl.BlockSpec((tm, w), lambda i: (i, 0)), pl.BlockSpec((tm, LANES), lambda i: (i, 0))],
        out_specs=[ospec, ospec],
        out_shape=[sds, sds],
        compiler_params=_cparams(("parallel",)),
        name="gdn_local",
    )(qkv, gates)


SOLVE_BATCH = 128


def _gdn_solve_kernel(upper, a_ref, o_ref, at_ref, m_ref):
    C = GDN_CHUNK
    ns = C // SUBLANES
    for i in range(C):
        xt = a_ref[pl.ds(i, SOLVE_BATCH, stride=C), :].T
        at_ref[i, 0] = xt[:C]
        at_ref[i, 1] = xt[C:]
    sub = lax.broadcasted_iota(jnp.int32, (SUBLANES, SOLVE_BATCH), 0)
    for t in range(C):
        i = C - 1 - t if upper else t
        si = i // SUBLANES
        slabs = list(range(si, ns)) if upper else list(range(si + 1))
        unit = jnp.where(sub == i % SUBLANES, 1.0, 0.0).astype(F32)
        zero = jnp.zeros((SUBLANES, SOLVE_BATCH), F32)
        init = tuple(unit if s == si else zero for _ in range(2) for s in slabs)

        def col_body(j, acc):
            new = []
            for hp in range(2):
                a = jnp.broadcast_to(at_ref[i, hp, pl.ds(j, 1), :], (SUBLANES, SOLVE_BATCH))
                for n, s in enumerate(slabs):
                    new.append(acc[hp * len(slabs) + n] - a * m_ref[hp, j, s * SUBLANES:(s + 1) * SUBLANES, :])
            return tuple(new)

        lo, hi = (i + 1, C) if upper else (0, i)
        if hi > lo:
            acc = lax.fori_loop(lo, hi, col_body, init, unroll=min(4, hi - lo))
        else:
            acc = init
        for hp in range(2):
            for s in range(ns):
                val = acc[hp * len(slabs) + slabs.index(s)] if s in slabs else zero
                m_ref[hp, i, s * SUBLANES:(s + 1) * SUBLANES, :] = val
    for i in range(C):
        y = jnp.concatenate([m_ref[0, i], m_ref[1, i]], axis=0)
        o_ref[pl.ds(i, SOLVE_BATCH, stride=C), :] = y.T


def _gdn_solve(a_all):
    _, NP, C, _ = a_all.shape
    rows = SOLVE_BATCH * C
    a2 = a_all.reshape(2, NP * C, LANES)
    outs = []
    for d in range(2):
        spec = pl.BlockSpec((None, rows, LANES), lambda b, d=d: (d, b, 0))
        outs.append(pl.pallas_call(
            functools.partial(_gdn_solve_kernel, d == 1),
            grid=(NP // SOLVE_BATCH,),
            in_specs=[spec],
            out_specs=pl.BlockSpec((rows, LANES), lambda b: (b, 0)),
            out_shape=jax.ShapeDtypeStruct((NP * C, LANES), F32),
            scratch_shapes=[pltpu.VMEM((C, 2, C, SOLVE_BATCH), F32), pltpu.VMEM((2, C, C, SOLVE_BATCH), F32)],
            compiler_params=_cparams(("parallel",)),
            name="gdn_solve_upper" if d else "gdn_solve_lower",
        )(a2).reshape(NP, C, LANES))
    return outs


def _gdn_kernel(rev, finish, d, *refs):
    if finish:
        (qkv_ref, gates_ref, tinv_ref, qk_ref, s0_ref, of_ref, gate_ref, ng_ref,
         out_ref, sout_ref, s_ref, o_ref) = refs
    else:
        (qkv_ref, gates_ref, tinv_ref, qk_ref, s0_ref, out_ref, sout_ref, s_ref) = refs
        o_ref = out_ref
    C = GDN_CHUNK
    nchunk = qkv_ref.shape[0] // C
    j = pl.program_id(1)

    @pl.when(j == 0)
    def _():
        s_ref[...] = s0_ref[...]

    last_row = 0 if rev else C - 1
    order = [(nchunk - 1 - t) if rev else t for t in range(nchunk)]

    def local(c):
        rows = slice(c * C, (c + 1) * C)
        gt = gates_ref[rows, :]
        gcum = _time_cumsum(gt, rev)
        out = []
        for h in range(HEADS):
            lg = d * HEADS + h
            lanes = slice((h % 2) * C, (h % 2 + 1) * C)
            gc_c = gcum[:, lg:lg + 1]
            beta = gt[:, 2 * HEADS + lg:2 * HEADS + lg + 1]
            gl = gcum[last_row:last_row + 1, lg:lg + 1]
            q = qkv_ref[rows, h * HEAD_DIM:(h + 1) * HEAD_DIM]
            kk = qkv_ref[rows, GROUP_W + h * HEAD_DIM:GROUP_W + (h + 1) * HEAD_DIM]
            v = qkv_ref[rows, 2 * GROUP_W + h * HEAD_DIM:2 * GROUP_W + (h + 1) * HEAD_DIM]
            tinv = tinv_ref[c * 2 + h // 2, :, lanes]
            qk = qk_ref[c * 2 + h // 2, :, lanes]
            kb = kk * beta
            eg = jnp.exp(gc_c)
            uw = _dot(tinv, jnp.concatenate([v * beta, kb * eg], axis=1))
            wq = jnp.concatenate([uw[:, HEAD_DIM:], q * eg], axis=0).astype(BF16)
            kdt = (kk * jnp.exp(gl - gc_c)).T
            qkk = jnp.concatenate([qk, kdt], axis=0).astype(BF16)
            out.append((uw[:, :HEAD_DIM], wq, qkk, jnp.exp(gl)))
        return out

    states = [s_ref[h] for h in range(HEADS)]
    nxt = local(order[0])
    for t, c in enumerate(order):
        cur = nxt
        ws = [jnp.dot(cur[h][1], states[h].astype(BF16), preferred_element_type=F32) for h in range(HEADS)]
        if t + 1 < nchunk:
            nxt = local(order[t + 1])
        for h in range(HEADS):
            u, _, qkk, decay = cur[h]
            v_new = u - ws[h][:C]
            res = jnp.dot(qkk, v_new.astype(BF16), preferred_element_type=F32)
            o_ref[c * C:(c + 1) * C, h * HEAD_DIM:(h + 1) * HEAD_DIM] = ws[h][C:] + res[:C]
            states[h] = states[h] * decay + res[C:]
    for h in range(HEADS):
        s_ref[h] = states[h]
    sout_ref[...] = s_ref[...]
    if finish:
        tot = of_ref[...] + o_ref[...]
        gate = gate_ref[...]
        for h in range(HEADS):
            sl = slice(h * HEAD_DIM, (h + 1) * HEAD_DIM)
            x = tot[:, sl]
            y = x * lax.rsqrt(jnp.mean(x * x, axis=-1, keepdims=True) + EPS) * ng_ref[...]
            out_ref[:, sl] = (y * _silu(gate[:, sl])).astype(out_ref.dtype)


def _gdn_pass(qkv, gates, tinv_all, tinv_off, qk, seq_len, rev, d, s0, of=None, P=None, ng=None):
    M = qkv.shape[0]
    B = M // seq_len
    rb = 256
    nblk = seq_len // rb
    finish = of is not None
    w = 3 * GROUP_W
    npair = rb // GDN_CHUNK * (HEADS // 2)
    toff = tinv_off // npair

    def blk(j):
        return (nblk - 1 - j) if rev else j

    sspec = pl.BlockSpec((None, HEADS, HEAD_DIM, HEAD_DIM), lambda b, j: (b, 0, 0, 0))
    specs = [pl.BlockSpec((rb, w), lambda b, j: (b * nblk + blk(j), 0)),
             pl.BlockSpec((rb, LANES), lambda b, j: (b * nblk + blk(j), 0)),
             pl.BlockSpec((npair, GDN_CHUNK, LANES), lambda b, j: (toff + b * nblk + blk(j), 0, 0)),
             pl.BlockSpec((None, npair, GDN_CHUNK, LANES), lambda b, j: (d, b * nblk + blk(j), 0, 0)),
             sspec]
    args = [qkv, gates, tinv_all, qk, s0]
    scratch = [pltpu.VMEM((HEADS, HEAD_DIM, HEAD_DIM), F32)]
    if finish:
        specs += [pl.BlockSpec((rb, GROUP_W), lambda b, j: (b * nblk + blk(j), 0)),
                  pl.BlockSpec((rb, GROUP_W), lambda b, j: (b * nblk + blk(j), P_GDN_GATE // GROUP_W)),
                  pl.BlockSpec((1, HEAD_DIM), lambda b, j: (0, 0))]
        args += [of, P, ng]
        scratch.append(pltpu.VMEM((rb, GROUP_W), F32))
    return pl.pallas_call(
        functools.partial(_gdn_kernel, rev, finish, d),
        grid=(B, nblk),
        in_specs=specs,
        out_specs=[pl.BlockSpec((rb, GROUP_W), lambda b, j: (b * nblk + blk(j), 0)), sspec],
        out_shape=[jax.ShapeDtypeStruct((M, GROUP_W), BF16 if finish else F32),
                   jax.ShapeDtypeStruct((B, HEADS, HEAD_DIM, HEAD_DIM), F32)],
        scratch_shapes=scratch,
        compiler_params=_cparams(("parallel", "arbitrary")),
        name="gdn_bwd" if rev else "gdn_fwd",
    )(*args)


def _mixer_gdn(Pc, Pl, ctx_len, seq_len, cw, alog, dtb, ng):
    B = Pc.shape[0] // ctx_len
    qkv_c, g_c = _gdn_prep(Pc, ctx_len, cw, alog, dtb)
    qkv_l, g_l = _gdn_prep(Pl, seq_len, cw, alog, dtb)
    a_c, qk_c = _gdn_local(qkv_c, g_c)
    a_l, qk_l = _gdn_local(qkv_l, g_l)
    n_c, n_l = a_c.shape[1], a_l.shape[1]
    n_pad = -(n_c + n_l) % SOLVE_BATCH
    pad = jnp.zeros((2, n_pad, GDN_CHUNK, LANES), F32)
    tinv = _gdn_solve(jnp.concatenate([a_c, a_l, pad], axis=1))
    zero = jnp.zeros((B, HEADS, HEAD_DIM, HEAD_DIM), F32)
    oc_f, sc_f = _gdn_pass(qkv_c, g_c, tinv[0], 0, qk_c, ctx_len, False, 0, zero)
    ol_f, _ = _gdn_pass(qkv_l, g_l, tinv[0], n_c, qk_l, seq_len, False, 0, sc_f)
    yc, sc_b = _gdn_pass(qkv_c, g_c, tinv[1], 0, qk_c, ctx_len, True, 1, zero, of=oc_f, P=Pc, ng=ng)
    yl, _ = _gdn_pass(qkv_l, g_l, tinv[1], n_c, qk_l, seq_len, True, 1, sc_b, of=ol_f, P=Pl, ng=ng)
    return yc, yl


def _pack_w_in(w_in):
    off = {}
    o = 0
    for name, n in (('lru_x', 512), ('lru_gate', 512), ('mla_cq', 384), ('mla_ckv', 256), ('mla_kr', 64),
                    ('gdn_q', 512), ('gdn_k', 512), ('gdn_v', 512), ('gdn_gate', 512),
                    ('gdn_beta', 8), ('gdn_alpha', 8), ('na_q', 512), ('na_k', 512), ('na_v', 512)):
        off[name] = (o, n)
        o += n
    order = ['gdn_q', 'gdn_k', 'gdn_v', 'gdn_gate', 'lru_x', 'lru_gate', 'na_q', 'na_k', 'na_v',
             'mla_cq', 'mla_ckv', 'mla_kr', 'gdn_beta', 'gdn_alpha']
    parts = [w_in[..., off[n][0]:off[n][0] + off[n][1]] for n in order]
    used = sum(off[n][1] for n in order)
    parts.append(jnp.zeros(w_in.shape[:-1] + (P_COLS - used,), w_in.dtype))
    return jnp.concatenate(parts, axis=-1).astype(BF16)


def _pack_w_uq(w_uq):
    L, K, _ = w_uq.shape
    w = w_uq.reshape(L, K, HEADS, MLA_QK)
    w = jnp.pad(w, ((0, 0), (0, 0), (0, 0), (0, MLA_QPAD - MLA_QK)))
    return w.reshape(L, K, HEADS * MLA_QPAD).astype(BF16)


def _pad_gain(g):
    return jnp.pad(g, ((0, 0), (0, MLA_QPAD - MLA_QK)))[:, None, :]


def _rope_tables(T):
    t = jnp.arange(T)
    rowp = (t // GRID_W).astype(F32)
    colp = (t % GRID_W).astype(F32)
    n_freq = MLA_ROPE // 4
    inv = ROPE_BASE ** (-jnp.arange(n_freq, dtype=F32) / n_freq)
    ang = jnp.concatenate([rowp[:, None] * inv, colp[:, None] * inv], axis=-1)
    cos, sin = jnp.cos(ang), jnp.sin(ang)
    z = jnp.zeros((T, LANES - MLA_ROPE), F32)
    return (jnp.concatenate([cos, cos, z], axis=-1), jnp.concatenate([-sin, sin, z], axis=-1))


def _lane_vec(x):
    L = x.shape[0]
    return jnp.pad(x.reshape(L, 2 * HEADS), ((0, 0), (0, LANES - 2 * HEADS)))[:, None, :]


def kernel(x, c, ctx, c_ctx, ada_w, ada_b, norm_mix_g, norm_ffn_g, w_in, w_out, lru_conv_w, lru_conv_b, lru_wa, lru_ba, lru_wx, lru_bx, lru_lam, mla_qa_g, mla_w_uq, mla_kva_g, mla_w_ukv, mla_qn_g, mla_kn_g, gdn_conv_w, gdn_a_log, gdn_dt_bias, gdn_norm_g, na_qn_g, na_kn_g, na_rpb, ffn_w_up, ffn_conv_w, ffn_conv_b, ffn_w_down):
    B, T, D = x.shape
    TC = ctx.shape[1]
    L = ada_w.shape[0]
    rows = T // GRID_W
    assert B <= 2 and T % 512 == 0 and TC == 256 and rows >= NA_WIN

    w_in_p = _pack_w_in(w_in)
    w_out4 = w_out.reshape(L, 4, GROUP_W, D).astype(BF16)
    w_up = ffn_w_up.astype(BF16)
    w_down = ffn_w_down.astype(BF16)
    w_uq = _pack_w_uq(mla_w_uq)
    w_ukv = mla_w_ukv.astype(BF16)
    qn_g = _pad_gain(mla_qn_g)
    kn_g = _pad_gain(mla_kn_g)
    lru_wcat = jnp.concatenate([lru_wa, lru_wx], axis=-1).astype(BF16)
    alog_v = _lane_vec(gdn_a_log)
    dtb_v = _lane_vec(gdn_dt_bias)
    cosf, sinf = _rope_tables(T)

    cvec = jnp.zeros((SUBLANES, D), F32).at[:B].set(c).at[2].set(c_ctx)
    mod_all = _modulation(cvec, ada_w, ada_b).reshape(L, SUBLANES, N_MOD, 1, D)

    h_lat = x.reshape(B * T, D)
    h_ctx = ctx.reshape(B * TC, D)
    for l in range(L):
        want_ctx = l < L - 1
        mod = mod_all[l]
        gmix = norm_mix_g[l][None, :]
        Pl = _inproj(h_lat, gmix, mod, w_in_p[l], T, False)
        Pc = _inproj(h_ctx, gmix, mod, w_in_p[l], TC, True)

        lw = lambda dd: (lru_conv_w[l], lru_conv_b[l][None, :], lru_wcat[l, dd], lru_ba[l, dd][None, :],
                         lru_bx[l, dd][None, :], lru_lam[l, dd][None, :])
        ya_c, ya_l = _mixer_lru(Pc, Pl, TC, T, lw(0), lw(1))

        mw = (mla_qa_g[l][None, :], w_uq[l], mla_kva_g[l][None, :], w_ukv[l], qn_g[l], kn_g[l])
        Qtc, Kc, Vtc = _mla_prep(Pc, TC, False, mw, cosf, sinf)
        Qtl, Kl, Vtl = _mla_prep(Pl, T, True, mw, cosf, sinf)
        yb_l = _flash_attention(Qtl, Kc, Vtc, T, TC, Kl, Vtl)

        yc_c, yc_l = _mixer_gdn(Pc, Pl, TC, T, gdn_conv_w[l], alog_v[l], dtb_v[l], gdn_norm_g[l][None, :])

        nqg, nkg = na_qn_g[l][None, :], na_kn_g[l][None, :]
        NQc, NKc, NVc = _na_prep(Pc, nqg, nkg)
        NQl, NKl, NVl = _na_prep(Pl, nqg, nkg)
        bias = _na_bias(na_rpb[l], rows)
        yd_l = _na_attention(NQl, NKl, NVl, NKc, NVc, bias, T, TC)

        h_lat = _outproj((ya_l, yb_l, yc_l, yd_l), w_out4[l], h_lat, mod, T, False)
        gffn = norm_ffn_g[l][None, :]
        h_lat = _ffn(h_lat, gffn, mod, w_up[l], ffn_conv_w[l], ffn_conv_b[l][None, :], w_down[l], T, False)
        if want_ctx:
            yb_c = _flash_attention(Qtc, Kc, Vtc, TC, TC)
            yd_c = _ctx_attention(NQc, NKc, NVc, TC)
            h_ctx = _outproj((ya_c, yb_c, yc_c, yd_c), w_out4[l], h_ctx, mod, TC, True)
            h_ctx = _ffn(h_ctx, gffn, mod, w_up[l], ffn_conv_w[l], ffn_conv_b[l][None, :], w_down[l], TC, True)
    return h_lat.reshape(B, T, D)
```

```python
import functools
import math

import jax
import jax.numpy as jnp
from jax import lax
from jax.experimental import pallas as pl
from jax.experimental.pallas import tpu as pltpu

F32 = jnp.float32
BF16 = jnp.bfloat16

GRID_W = 64
HEADS = 4
HEAD_DIM = 128
GROUP_W = HEADS * HEAD_DIM
N_MOD = 6
EPS = 1e-6
LRU_C = 8.0
MLA_Q_LORA = 384
MLA_KV_LORA = 256
MLA_NOPE = 128
MLA_ROPE = 64
MLA_QK = MLA_NOPE + MLA_ROPE
MLA_QPAD = 256
ROPE_BASE = 10000.0
GDN_CHUNK = 64
NA_KH = 8
NA_KW = 16
NA_BAND = 8
NA_WIN = 16
NEG_BIG = -1e30
LOG2E = math.log2(math.e)
FLASH_UNROLL = 8
FFN_SUB = 256

VMEM_LIMIT = 56 * 1024 * 1024
SUBLANES = 8
LANES = 128

P_GDN_QKV = 0
P_GDN_GATE = 1536
P_LRU_X = 2048
P_LRU_GATE = 2560
P_NA_QKV = 3072
P_MLA = 4608
P_COLS = 5376
P_TAIL = 5248
TAIL_BETA = 64
TAIL_ALPHA = 72


def _cparams(sem):
    return pltpu.CompilerParams(dimension_semantics=sem, vmem_limit_bytes=VMEM_LIMIT)


def _dot(a, b):
    return jnp.dot(a.astype(BF16), b.astype(BF16), preferred_element_type=F32)


def _dot_nt(a, b):
    return lax.dot_general(a.astype(BF16), b.astype(BF16), (((1,), (1,)), ((), ())),
                           preferred_element_type=F32)


def _sigmoid(x):
    return 1.0 / (1.0 + jnp.exp(-x))


def _silu(x):
    return x * _sigmoid(x)


def _softplus(x):
    return jnp.maximum(x, 0.0) + jnp.log(1.0 + jnp.exp(-jnp.abs(x)))


def _gelu_tanh(x):
    return 0.5 * x * (1.0 + jnp.tanh(math.sqrt(2.0 / math.pi) * (x + 0.044715 * x * x * x)))


def _mod_kernel(c_ref, w_ref, b_ref, o_ref):
    c = c_ref[...]
    o_ref[...] = _dot(_silu(c), w_ref[...]) + b_ref[...]


def _modulation(cvec, ada_w, ada_b):
    L, D, N = ada_w.shape
    tn = 1024
    return pl.pallas_call(
        _mod_kernel,
        grid=(L, N // tn),
        in_specs=[pl.BlockSpec((SUBLANES, D), lambda l, j: (0, 0)),
                  pl.BlockSpec((None, D, tn), lambda l, j: (l, 0, j)),
                  pl.BlockSpec((None, 1, tn), lambda l, j: (l, 0, j))],
        out_specs=pl.BlockSpec((None, SUBLANES, tn), lambda l, j: (l, 0, j)),
        out_shape=jax.ShapeDtypeStruct((L, SUBLANES, N), F32),
        compiler_params=_cparams(("parallel", "parallel")),
        name="modulation",
    )(cvec, ada_w, ada_b.reshape(L, 1, N))


def _mod_spec(l, which, rows_per_mod, D, ngrid):
    if rows_per_mod is None:
        row = lambda i: 2
    else:
        row = lambda i: i // rows_per_mod
    if ngrid == 1:
        return pl.BlockSpec((None, None, None, 1, D), lambda i: (l, row(i), which, 0, 0))
    return pl.BlockSpec((None, None, None, 1, D), lambda i, j: (l, row(i), which, 0, 0))


def _inproj_kernel(h_ref, g_ref, shift_ref, scale_ref, w_ref, o_ref, xn_ref):
    @pl.when(pl.program_id(1) == 0)
    def _():
        x = h_ref[...]
        y = x * lax.rsqrt(jnp.mean(x * x, axis=-1, keepdims=True) + EPS) * g_ref[...]
        xn_ref[...] = (y * (1.0 + scale_ref[...]) + shift_ref[...]).astype(BF16)

    o_ref[...] = jnp.dot(xn_ref[...], w_ref[...], preferred_element_type=F32)


def _inproj(l, h, gain, mod, w, seq_len, is_ctx):
    M, D = h.shape
    N = w.shape[2]
    tm = min(1024, M)
    tn = 768
    rpm = None if is_ctx else seq_len // tm
    return pl.pallas_call(
        _inproj_kernel,
        grid=(M // tm, N // tn),
        in_specs=[pl.BlockSpec((tm, D), lambda i, j: (i, 0)),
                  pl.BlockSpec((None, 1, D), lambda i, j: (l, 0, 0)),
                  _mod_spec(l, 0, rpm, D, 2),
                  _mod_spec(l, 1, rpm, D, 2),
                  pl.BlockSpec((None, D, tn), lambda i, j: (l, 0, j))],
        out_specs=pl.BlockSpec((tm, tn), lambda i, j: (i, j)),
        out_shape=jax.ShapeDtypeStruct((M, N), F32),
        scratch_shapes=[pltpu.VMEM((tm, D), BF16)],
        compiler_params=_cparams(("parallel", "arbitrary")),
        name="inproj",
    )(h, gain, mod, mod, w)


def _outproj_kernel(ya_ref, yb_ref, yc_ref, yd_ref, w_ref, h_ref, gate_ref, o_ref):
    acc = jnp.dot(ya_ref[...], w_ref[0], preferred_element_type=F32)
    acc += jnp.dot(yb_ref[...], w_ref[1], preferred_element_type=F32)
    acc += jnp.dot(yc_ref[...], w_ref[2], preferred_element_type=F32)
    acc += jnp.dot(yd_ref[...], w_ref[3], preferred_element_type=F32)
    o_ref[...] = h_ref[...] + gate_ref[...] * acc


def _outproj(l, ys, w4, h, mod, seq_len, is_ctx):
    M, D = h.shape
    tm = min(512, M)
    rpm = None if is_ctx else seq_len // tm
    yspec = pl.BlockSpec((tm, GROUP_W), lambda i: (i, 0))
    return pl.pallas_call(
        _outproj_kernel,
        grid=(M // tm,),
        in_specs=[yspec, yspec, yspec, yspec,
                  pl.BlockSpec((None, 4, GROUP_W, D), lambda i: (l, 0, 0, 0)),
                  pl.BlockSpec((tm, D), lambda i: (i, 0)),
                  _mod_spec(l, 2, rpm, D, 1)],
        out_specs=pl.BlockSpec((tm, D), lambda i: (i, 0)),
        out_shape=jax.ShapeDtypeStruct((M, D), F32),
        compiler_params=_cparams(("parallel",)),
        name="outproj",
    )(*ys, w4, h, mod)


def _halo_specs(tm, width, col_block, nrows, ngrid):
    r = tm // SUBLANES
    last = nrows // SUBLANES - 1
    if ngrid == 1:
        prev = pl.BlockSpec((SUBLANES, width), lambda i: (jnp.maximum(i * r - 1, 0), col_block))
        nxt = pl.BlockSpec((SUBLANES, width), lambda i: (jnp.minimum((i + 1) * r, last), col_block))
    else:
        prev = pl.BlockSpec((SUBLANES, width), lambda i, j: (jnp.maximum(i * r - 1, 0), col_block))
        nxt = pl.BlockSpec((SUBLANES, width), lambda i, j: (jnp.minimum((i + 1) * r, last), col_block))
    return prev, nxt


def _ffn_kernel(seq_len, h_ref, hp_ref, hn_ref, g_ref, shift_ref, scale_ref, gate_ref,
                wa_ref, wg_ref, cwa_ref, cwg_ref, cba_ref, cbg_ref, wd_ref, o_ref, xn_ref):
    tm = h_ref.shape[0]
    tf = wa_ref.shape[1]
    i = pl.program_id(0)
    j = pl.program_id(1)

    @pl.when(j == 0)
    def _():
        def norm(x):
            y = x * lax.rsqrt(jnp.mean(x * x, axis=-1, keepdims=True) + EPS) * g_ref[...]
            return (y * (1.0 + scale_ref[...]) + shift_ref[...]).astype(BF16)
        keep_prev = (i * tm) % seq_len != 0
        keep_next = ((i + 1) * tm) % seq_len != 0
        xn_ref[0:SUBLANES, :] = jnp.where(keep_prev, norm(hp_ref[...]), jnp.zeros((), BF16))
        xn_ref[SUBLANES:SUBLANES + tm, :] = norm(h_ref[...])
        xn_ref[SUBLANES + tm:, :] = jnp.where(keep_next, norm(hn_ref[...]), jnp.zeros((), BF16))

    def conv(u, cw_ref, cb_ref, cols):
        lo = u[SUBLANES - 1:SUBLANES - 1 + tm]
        mid = u[SUBLANES:SUBLANES + tm]
        hi = u[SUBLANES + 1:SUBLANES + 1 + tm]
        return lo * cw_ref[0:1, cols] + mid * cw_ref[1:2, cols] + hi * cw_ref[2:3, cols] + cb_ref[:, cols]

    xn = xn_ref[...]
    subs = [slice(s, s + FFN_SUB) for s in range(0, tf, FFN_SUB)]
    ups = [(jnp.dot(xn, wa_ref[:, c], preferred_element_type=F32),
            jnp.dot(xn, wg_ref[:, c], preferred_element_type=F32)) for c in subs]
    part = None
    for c, (ua, ug) in zip(subs, ups):
        act = conv(ua, cwa_ref, cba_ref, c) * _silu(conv(ug, cwg_ref, cbg_ref, c))
        d = jnp.dot(act.astype(BF16), wd_ref[c, :], preferred_element_type=F32)
        part = d if part is None else part + d

    @pl.when(j == 0)
    def _():
        o_ref[...] = part

    @pl.when(j > 0)
    def _():
        o_ref[...] += part

    @pl.when(j == pl.num_programs(1) - 1)
    def _():
        o_ref[...] = h_ref[...] + gate_ref[...] * o_ref[...]


def _ffn(l, h, gain, mod, w_up, conv_w, conv_b, w_down, seq_len, is_ctx):
    M, D = h.shape
    FF = w_down.shape[1]
    tm = min(512, seq_len)
    tf = 1024
    nf = FF // tf
    rpm = None if is_ctx else seq_len // tm
    prev, nxt = _halo_specs(tm, D, 0, M, 2)
    return pl.pallas_call(
        functools.partial(_ffn_kernel, seq_len),
        grid=(M // tm, nf),
        in_specs=[pl.BlockSpec((tm, D), lambda i, j: (i, 0)), prev, nxt,
                  pl.BlockSpec((None, 1, D), lambda i, j: (l, 0, 0)),
                  _mod_spec(l, 3, rpm, D, 2), _mod_spec(l, 4, rpm, D, 2), _mod_spec(l, 5, rpm, D, 2),
                  pl.BlockSpec((None, D, tf), lambda i, j: (l, 0, j)),
                  pl.BlockSpec((None, D, tf), lambda i, j: (l, 0, j + nf)),
                  pl.BlockSpec((None, 3, tf), lambda i, j: (l, 0, j)),
                  pl.BlockSpec((None, 3, tf), lambda i, j: (l, 0, j + nf)),
                  pl.BlockSpec((None, 1, tf), lambda i, j: (l, 0, j)),
                  pl.BlockSpec((None, 1, tf), lambda i, j: (l, 0, j + nf)),
                  pl.BlockSpec((None, tf, D), lambda i, j: (l, j, 0))],
        out_specs=pl.BlockSpec((tm, D), lambda i, j: (i, 0)),
        out_shape=jax.ShapeDtypeStruct((M, D), F32),
        scratch_shapes=[pltpu.VMEM((tm + 2 * SUBLANES, D), BF16)],
        compiler_params=_cparams(("parallel", "arbitrary")),
        name="conv_ffn",
    )(h, h, h, gain, mod, mod, mod, w_up, w_up, conv_w, conv_w, conv_b, conv_b, w_down)


def _conv4(xp_ref, x_ref, xn_ref, w_ref, pos, seq_len):
    tm = x_ref.shape[0]
    xe = jnp.concatenate([xp_ref[...], x_ref[...], xn_ref[...]], axis=0)
    t0 = jnp.where(pos >= 1, xe[SUBLANES - 1:SUBLANES - 1 + tm], 0.0)
    t1 = xe[SUBLANES:SUBLANES + tm]
    t2 = jnp.where(pos + 1 < seq_len, xe[SUBLANES + 1:SUBLANES + 1 + tm], 0.0)
    t3 = jnp.where(pos + 2 < seq_len, xe[SUBLANES + 2:SUBLANES + 2 + tm], 0.0)
    return t0 * w_ref[0:1, :] + t1 * w_ref[1:2, :] + t2 * w_ref[2:3, :] + t3 * w_ref[3:4, :]


def _shift_rows(x, k, fill, rev):
    n = x.shape[0]
    if k % SUBLANES == 0:
        pad = jnp.full((k, x.shape[1]), fill, x.dtype)
        return jnp.concatenate([x[k:], pad], 0) if rev else jnp.concatenate([pad, x[:n - k]], 0)
    row = lax.broadcasted_iota(jnp.int32, x.shape, 0)
    if rev:
        return jnp.where(row >= n - k, fill, pltpu.roll(x, n - k, 0))
    return jnp.where(row < k, fill, pltpu.roll(x, k, 0))


def _lru_kernel(rev, finish, seq_len, *refs):
    if finish:
        (x_ref, xp_ref, xn_ref, gate_ref, hf_ref, cw_ref, cb_ref, wcat_ref, ba_ref, bx_ref,
         lam_ref, h0_ref, out_ref, st_ref, carry_ref) = refs
    else:
        (x_ref, xp_ref, xn_ref, cw_ref, cb_ref, wcat_ref, ba_ref, bx_ref,
         lam_ref, h0_ref, out_ref, st_ref, carry_ref) = refs
    tc = x_ref.shape[0]
    j = pl.program_id(1)
    nch = pl.num_programs(1)
    c = (nch - 1 - j) if rev else j

    @pl.when(j == 0)
    def _():
        carry_ref[...] = h0_ref[...]

    pos = c * tc + lax.broadcasted_iota(jnp.int32, (tc, 1), 0)
    u = _conv4(xp_ref, x_ref, xn_ref, cw_ref, pos, seq_len) + cb_ref[...]
    rs, is_ = [], []
    for n in range(HEADS):
        z = _dot(u[:, n * HEAD_DIM:(n + 1) * HEAD_DIM], wcat_ref[n])
        rs.append(z[:, :HEAD_DIM])
        is_.append(z[:, HEAD_DIM:])
    r = _sigmoid(jnp.concatenate(rs, axis=1) + ba_ref[...])
    ig = _sigmoid(jnp.concatenate(is_, axis=1) + bx_ref[...])
    log_a = -LRU_C * r * _softplus(-lam_ref[...])
    a = jnp.exp(log_a)
    b = jnp.sqrt(1.0 - jnp.exp(2.0 * log_a)) * ig * u
    k = 1
    while k < tc:
        b = a * _shift_rows(b, k, 0.0, rev) + b
        a = a * _shift_rows(a, k, 1.0, rev)
        k *= 2
    h = b + a * carry_ref[0:1, :]
    last = h[0:1, :] if rev else h[tc - 1:tc, :]
    carry_ref[...] = jnp.broadcast_to(last, carry_ref.shape)
    st_ref[...] = jnp.broadcast_to(last, st_ref.shape)
    if finish:
        out_ref[...] = ((hf_ref[...] + h) * _gelu_tanh(gate_ref[...])).astype(out_ref.dtype)
    else:
        out_ref[...] = h


def _lru_pass(P, seq_len, rev, h0, wts, hf=None):
    cw, cb, wcat, ba, bx, lam = wts
    M = P.shape[0]
    B = M // seq_len
    tc = 256
    nch = seq_len // tc
    finish = hf is not None
    W = GROUP_W
    xb = P_LRU_X // W
    gb = P_LRU_GATE // W
    r = tc // SUBLANES
    last = M // SUBLANES - 1

    def chunk(j):
        return (nch - 1 - j) if rev else j

    row = lambda b, j: (b * nch + chunk(j), xb)
    specs = [pl.BlockSpec((tc, W), row),
             pl.BlockSpec((SUBLANES, W), lambda b, j: (jnp.maximum((b * nch + chunk(j)) * r - 1, 0), xb)),
             pl.BlockSpec((SUBLANES, W), lambda b, j: (jnp.minimum((b * nch + chunk(j) + 1) * r, last), xb))]
    args = [P, P, P]
    if finish:
        specs += [pl.BlockSpec((tc, W), lambda b, j: (b * nch + chunk(j), gb)),
                  pl.BlockSpec((tc, W), lambda b, j: (b * nch + chunk(j), 0))]
        args += [P, hf]
    full2 = lambda b, j: (0, 0)
    specs += [pl.BlockSpec((4, W), full2), pl.BlockSpec((1, W), full2),
              pl.BlockSpec((HEADS, HEAD_DIM, 2 * HEAD_DIM), lambda b, j: (0, 0, 0)),
              pl.BlockSpec((1, W), full2), pl.BlockSpec((1, W), full2), pl.BlockSpec((1, W), full2),
              pl.BlockSpec((None, SUBLANES, W), lambda b, j: (b, 0, 0))]
    args += [cw, cb, wcat, ba, bx, lam, h0]
    out_dtype = BF16 if finish else F32
    return pl.pallas_call(
        functools.partial(_lru_kernel, rev, finish, seq_len),
        grid=(B, nch),
        in_specs=specs,
        out_specs=[pl.BlockSpec((tc, W), lambda b, j: (b * nch + chunk(j), 0)),
                   pl.BlockSpec((None, SUBLANES, W), lambda b, j: (b, 0, 0))],
        out_shape=[jax.ShapeDtypeStruct((M, W), out_dtype),
                   jax.ShapeDtypeStruct((B, SUBLANES, W), F32)],
        scratch_shapes=[pltpu.VMEM((SUBLANES, W), F32)],
        compiler_params=_cparams(("parallel", "arbitrary")),
        name="rglru_bwd" if rev else "rglru_fwd",
    )(*args)


def _mixer_lru(Pc, Pl, ctx_len, seq_len, wts_f, wts_b):
    B = Pc.shape[0] // ctx_len
    zero = jnp.zeros((B, SUBLANES, GROUP_W), F32)
    hc_f, sc_f = _lru_pass(Pc, ctx_len, False, zero, wts_f)
    hl_f, _ = _lru_pass(Pl, seq_len, False, sc_f, wts_f)
    yc, sc_b = _lru_pass(Pc, ctx_len, True, zero, wts_b, hf=hc_f)
    yl, _ = _lru_pass(Pl, seq_len, True, sc_b, wts_b, hf=hl_f)
    return yc, yl


def _rope_mix(x, cos, sin):
    lane = lax.broadcasted_iota(jnp.int32, x.shape, 1)
    half = MLA_ROPE // 2
    swapped = jnp.where(lane < half, pltpu.roll(x, LANES - half, 1), pltpu.roll(x, half, 1))
    return x * cos + swapped * sin


def _mla_prep_kernel(use_rope, x_ref, qag_ref, wuq_ref, kvg_ref, wukv_ref, qg_ref, kg_ref,
                     cos_ref, sin_ref, qt_out, k_out, vt_out):
    x = x_ref[...]

    def rms(v, g):
        return v * lax.rsqrt(jnp.mean(v * v, axis=-1, keepdims=True) + EPS) * g

    q = _dot(rms(x[:, :MLA_Q_LORA], qag_ref[...]), wuq_ref[...])
    kv = _dot(rms(x[:, MLA_Q_LORA:MLA_Q_LORA + MLA_KV_LORA], kvg_ref[...]), wukv_ref[...])
    tail = x[:, MLA_Q_LORA + MLA_KV_LORA:]
    lane = lax.broadcasted_iota(jnp.int32, tail.shape, 1)
    kr = jnp.where(lane < MLA_ROPE, tail, 0.0)
    kr_ss = jnp.sum(kr * kr, axis=-1, keepdims=True)
    scale = MLA_QK ** -0.5 * LOG2E
    for h in range(HEADS):
        qh = q[:, h * MLA_QPAD:(h + 1) * MLA_QPAD]
        inv = lax.rsqrt(jnp.sum(qh * qh, axis=-1, keepdims=True) / MLA_QK + EPS) * scale
        qh = qh * inv * qg_ref[...]
        qn, qr = qh[:, :LANES], qh[:, LANES:]
        kn = kv[:, h * 2 * HEAD_DIM:h * 2 * HEAD_DIM + MLA_NOPE]
        vh = kv[:, h * 2 * HEAD_DIM + MLA_NOPE:(h + 1) * 2 * HEAD_DIM]
        kinv = lax.rsqrt((jnp.sum(kn * kn, axis=-1, keepdims=True) + kr_ss) / MLA_QK + EPS)
        kn = kn * kinv * kg_ref[:, :LANES]
        krh = kr * kinv * kg_ref[:, LANES:]
        if use_rope:
            qr = _rope_mix(qr, cos_ref[...], sin_ref[...])
            krh = _rope_mix(krh, cos_ref[...], sin_ref[...])
        qt_out[h] = jnp.concatenate([qn, qr], axis=1).T.astype(BF16)
        k_out[h, :, :LANES] = kn.astype(BF16)
        k_out[h, :, LANES:] = krh.astype(BF16)
        vt_out[h] = vh.T.astype(BF16)


def _mla_prep(P, seq_len, use_rope, wts, cosf, sinf):
    qag, wuq, kvg, wukv, qg, kg = wts
    M = P.shape[0]
    tm = 256
    nt = seq_len // tm
    full = lambda i: (0, 0)
    wmla = P_COLS - P_MLA
    tab = pl.BlockSpec((tm, LANES), (lambda i: (i % nt, 0)) if use_rope else (lambda i: (0, 0)))
    tspec = lambda w: pl.BlockSpec((HEADS, w, tm), lambda i: (0, 0, i))
    return pl.pallas_call(
        functools.partial(_mla_prep_kernel, use_rope),
        grid=(M // tm,),
        in_specs=[pl.BlockSpec((tm, wmla), lambda i: (i, P_MLA // wmla)),
                  pl.BlockSpec((1, MLA_Q_LORA), full), pl.BlockSpec(wuq.shape, full),
                  pl.BlockSpec((1, MLA_KV_LORA), full), pl.BlockSpec(wukv.shape, full),
                  pl.BlockSpec((1, MLA_QPAD), full), pl.BlockSpec((1, MLA_QPAD), full),
                  tab, tab],
        out_specs=[tspec(MLA_QPAD), pl.BlockSpec((HEADS, tm, MLA_QPAD), lambda i: (0, i, 0)),
                   tspec(HEAD_DIM)],
        out_shape=[jax.ShapeDtypeStruct((HEADS, MLA_QPAD, M), BF16),
                   jax.ShapeDtypeStruct((HEADS, M, MLA_QPAD), BF16),
                   jax.ShapeDtypeStruct((HEADS, HEAD_DIM, M), BF16)],
        compiler_params=_cparams(("parallel",)),
        name="mla_prep",
    )(P, qag, wuq, kvg, wukv, qg, kg, cosf, sinf)


def _flash_kernel(tk, has_lat, *refs):
    if has_lat:
        qt_ref, kc_ref, vct_ref, kl_ref, vlt_ref, o_ref, m_ref, l_ref, acc_ref, s_ref = refs
    else:
        qt_ref, kc_ref, vct_ref, o_ref = refs
    qt = qt_ref[...]
    s = jnp.dot(kc_ref[...], qt, preferred_element_type=F32)
    m = jnp.max(s, axis=0, keepdims=True)
    p = jnp.exp2(s - m)
    l = jnp.sum(p, axis=0, keepdims=True)
    acc = jnp.dot(vct_ref[...], p.astype(BF16), preferred_element_type=F32)
    if has_lat:
        m_ref[...] = m
        l_ref[...] = l
        acc_ref[...] = acc

        nk = kl_ref.shape[0] // tk

        def scores(c, slot):
            off = pl.multiple_of(jnp.minimum(c, nk - 1) * tk, tk)
            s_ref[slot] = jnp.dot(kl_ref[pl.ds(off, tk), :], qt, preferred_element_type=F32)

        def step(c, slot):
            scores(c + 1, 1 - slot)
            off = pl.multiple_of(c * tk, tk)
            s = s_ref[slot]
            m_old = m_ref[...]
            m_new = jnp.maximum(m_old, jnp.max(s, axis=0, keepdims=True))
            alpha = jnp.exp2(m_old - m_new)
            p = jnp.exp2(s - m_new)
            l_ref[...] = alpha * l_ref[...] + jnp.sum(p, axis=0, keepdims=True)
            acc_ref[...] = alpha * acc_ref[...] + jnp.dot(
                vlt_ref[:, pl.ds(off, tk)], p.astype(BF16), preferred_element_type=F32)
            m_ref[...] = m_new

        scores(0, 0)

        unroll = min(FLASH_UNROLL, nk)

        def body(j, carry):
            for u in range(unroll):
                step(unroll * j + u, u % 2)
            return carry

        lax.fori_loop(0, nk // unroll, body, 0)
        acc = acc_ref[...]
        l = l_ref[...]
    o_ref[...] = (acc / l).T.astype(o_ref.dtype)


def _flash_attention(Qt, Kc, Vct, seq_len, ctx_len, Kl=None, Vlt=None):
    H, dq, M = Qt.shape
    B = M // seq_len
    has_lat = Kl is not None
    tq = min(512, seq_len)
    tk = 512
    nq = seq_len // tq
    specs = [pl.BlockSpec((None, dq, tq), lambda b, h, i: (h, 0, b * nq + i)),
             pl.BlockSpec((None, ctx_len, dq), lambda b, h, i: (h, b, 0)),
             pl.BlockSpec((None, HEAD_DIM, ctx_len), lambda b, h, i: (h, 0, b))]
    args = [Qt, Kc, Vct]
    scratch = []
    if has_lat:
        specs += [pl.BlockSpec((None, seq_len, dq), lambda b, h, i: (h, b, 0)),
                  pl.BlockSpec((None, HEAD_DIM, seq_len), lambda b, h, i: (h, 0, b))]
        args += [Kl, Vlt]
        nk = seq_len // tk
        assert nk % 2 == 0 and nk % min(FLASH_UNROLL, nk) == 0
        scratch = [pltpu.VMEM((1, tq), F32), pltpu.VMEM((1, tq), F32), pltpu.VMEM((HEAD_DIM, tq), F32),
                   pltpu.VMEM((2, tk, tq), F32)]
    return pl.pallas_call(
        functools.partial(_flash_kernel, tk, has_lat),
        grid=(B, H, nq),
        in_specs=specs,
        out_specs=pl.BlockSpec((tq, HEAD_DIM), lambda b, h, i: (b * nq + i, h)),
        out_shape=jax.ShapeDtypeStruct((M, GROUP_W), BF16),
        scratch_shapes=scratch,
        compiler_params=_cparams(("parallel", "parallel", "arbitrary")),
        name="mla_attention" if has_lat else "ctx_attention",
    )(*args)


def _na_prep_kernel(x_ref, qg_ref, kg_ref, qt_out, k_out, vt_out):
    x = x_ref[...]
    scale = HEAD_DIM ** -0.5 * LOG2E
    for h in range(HEADS):
        qh = x[:, h * HEAD_DIM:(h + 1) * HEAD_DIM]
        kh = x[:, GROUP_W + h * HEAD_DIM:GROUP_W + (h + 1) * HEAD_DIM]
        vh = x[:, 2 * GROUP_W + h * HEAD_DIM:2 * GROUP_W + (h + 1) * HEAD_DIM]
        qh = qh * (lax.rsqrt(jnp.mean(qh * qh, axis=-1, keepdims=True) + EPS) * scale) * qg_ref[...]
        kh = kh * lax.rsqrt(jnp.mean(kh * kh, axis=-1, keepdims=True) + EPS) * kg_ref[...]
        qt_out[h] = qh.T.astype(BF16)
        k_out[h] = kh.astype(BF16)
        vt_out[h] = vh.T.astype(BF16)


def _na_prep(P, qg, kg):
    M = P.shape[0]
    tm = 256
    w = 3 * GROUP_W
    hspec = pl.BlockSpec((HEADS, tm, HEAD_DIM), lambda i: (0, i, 0))
    tspec = pl.BlockSpec((HEADS, HEAD_DIM, tm), lambda i: (0, 0, i))
    sds = jax.ShapeDtypeStruct((HEADS, M, HEAD_DIM), BF16)
    tds = jax.ShapeDtypeStruct((HEADS, HEAD_DIM, M), BF16)
    return pl.pallas_call(
        _na_prep_kernel,
        grid=(M // tm,),
        in_specs=[pl.BlockSpec((tm, w), lambda i: (i, P_NA_QKV // w)),
                  pl.BlockSpec((1, HEAD_DIM), lambda i: (0, 0)),
                  pl.BlockSpec((1, HEAD_DIM), lambda i: (0, 0))],
        out_specs=[tspec, hspec, tspec],
        out_shape=[tds, sds, tds],
        compiler_params=_cparams(("parallel",)),
        name="na_prep",
    )(P, qg, kg)


def _na_bias_kernel(rows, rpb_ref, o_ref):
    h = pl.program_id(0)
    kc = lax.broadcasted_iota(jnp.int32, (GRID_W, GRID_W), 0)
    qc = lax.broadcasted_iota(jnp.int32, (GRID_W, GRID_W), 1)
    dc = jnp.clip(kc - qc + (NA_KW - 1), 0, 2 * NA_KW - 2)
    c0 = jnp.clip(qc - NA_KW // 2, 0, GRID_W - NA_KW)
    col_ok = (kc >= c0) & (kc < c0 + NA_KW)
    neg = jnp.full((GRID_W, GRID_W), NEG_BIG, F32)
    tiles = []
    for dr in range(2 * NA_KH - 1):
        t = jnp.zeros((GRID_W, GRID_W), F32)
        for d in range(2 * NA_KW - 1):
            t = t + jnp.where(dc == d, rpb_ref[h, dr, d], 0.0)
        tiles.append(jnp.where(col_ok, t * LOG2E, NEG_BIG))
    for kind in range(3):
        q_base = (0, NA_KH // 2, rows - NA_BAND)[kind]
        w_base = (0, 0, rows - NA_WIN)[kind]
        for qr in range(NA_BAND):
            r = q_base + qr
            if kind == 1:
                r0 = r - NA_KH // 2
            else:
                r0 = min(max(r - NA_KH // 2, 0), rows - NA_KH)
            for kr in range(NA_WIN):
                ka = w_base + kr
                ok = r0 <= ka < r0 + NA_KH
                blk = tiles[ka - r + NA_KH - 1] if ok else neg
                o_ref[kind, kr * GRID_W:(kr + 1) * GRID_W, qr * GRID_W:(qr + 1) * GRID_W] = blk


def _na_bias(rpb, rows):
    H = rpb.shape[0]
    nq, nk = NA_BAND * GRID_W, NA_WIN * GRID_W
    return pl.pallas_call(
        functools.partial(_na_bias_kernel, rows),
        grid=(H,),
        in_specs=[pl.BlockSpec(memory_space=pltpu.SMEM)],
        out_specs=pl.BlockSpec((None, 3, nk, nq), lambda h: (h, 0, 0, 0)),
        out_shape=jax.ShapeDtypeStruct((H, 3, nk, nq), F32),
        compiler_params=_cparams(("parallel",)),
        name="na_bias",
    )(rpb)


def _na_kernel(seq_len, qt_ref, k_ref, vt_ref, kc_ref, vct_ref, bias_ref, o_ref):
    nq = qt_ref.shape[1]
    nk = bias_ref.shape[0]
    j = pl.program_id(2)
    base = jnp.clip(j * nq - (NA_KH // 2) * GRID_W, 0, seq_len - nk)
    base = pl.multiple_of(base, (NA_KH // 2) * GRID_W)
    qt = qt_ref[...]
    s_ctx = jnp.dot(kc_ref[...], qt, preferred_element_type=F32)
    s_loc = jnp.dot(k_ref[pl.ds(base, nk), :], qt, preferred_element_type=F32) + bias_ref[...]
    m = jnp.maximum(jnp.max(s_loc, axis=0, keepdims=True), jnp.max(s_ctx, axis=0, keepdims=True))
    p_loc = jnp.exp2(s_loc - m)
    p_ctx = jnp.exp2(s_ctx - m)
    l = jnp.sum(p_loc, axis=0, keepdims=True) + jnp.sum(p_ctx, axis=0, keepdims=True)
    o = (jnp.dot(vt_ref[:, pl.ds(base, nk)], p_loc.astype(BF16), preferred_element_type=F32)
         + jnp.dot(vct_ref[...], p_ctx.astype(BF16), preferred_element_type=F32))
    o_ref[...] = (o / l).T.astype(o_ref.dtype)


def _na_attention(Qtl, Kl, Vtl, Kc, Vtc, bias, seq_len, ctx_len):
    H, d, M = Qtl.shape
    B = M // seq_len
    nq, nk = NA_BAND * GRID_W, NA_WIN * GRID_W
    nb = seq_len // nq

    def kind(j):
        return jnp.where(j == 0, 0, jnp.where(j == nb - 1, 2, 1))

    return pl.pallas_call(
        functools.partial(_na_kernel, seq_len),
        grid=(B, H, nb),
        in_specs=[pl.BlockSpec((None, d, nq), lambda b, h, j: (h, 0, b * nb + j)),
                  pl.BlockSpec((None, seq_len, d), lambda b, h, j: (h, b, 0)),
                  pl.BlockSpec((None, d, seq_len), lambda b, h, j: (h, 0, b)),
                  pl.BlockSpec((None, ctx_len, d), lambda b, h, j: (h, b, 0)),
                  pl.BlockSpec((None, d, ctx_len), lambda b, h, j: (h, 0, b)),
                  pl.BlockSpec((None, None, nk, nq), lambda b, h, j: (h, kind(j), 0, 0))],
        out_specs=pl.BlockSpec((nq, d), lambda b, h, j: (b * nb + j, h)),
        out_shape=jax.ShapeDtypeStruct((M, GROUP_W), BF16),
        compiler_params=_cparams(("parallel", "parallel", "arbitrary")),
        name="na_attention",
    )(Qtl, Kl, Vtl, Kc, Vtc, bias)


def _gdn_prep_kernel(seq_len, x_ref, xp_ref, xn_ref, tail_ref, cw_ref, alog_ref, dtb_ref,
                     qkv_out, gates_out):
    tm = x_ref.shape[0]
    pos = (pl.program_id(0) * tm + lax.broadcasted_iota(jnp.int32, (tm, 1), 0)) % seq_len
    y = _silu(_conv4(xp_ref, x_ref, xn_ref, cw_ref, pos, seq_len))
    for h in range(HEADS):
        sl = slice(h * HEAD_DIM, (h + 1) * HEAD_DIM)
        qh = y[:, sl]
        qkv_out[:, sl] = qh * (lax.rsqrt(jnp.sum(qh * qh, axis=-1, keepdims=True) + EPS) * HEAD_DIM ** -0.5)
        sl = slice(GROUP_W + h * HEAD_DIM, GROUP_W + (h + 1) * HEAD_DIM)
        kh = y[:, sl]
        qkv_out[:, sl] = kh * lax.rsqrt(jnp.sum(kh * kh, axis=-1, keepdims=True) + EPS)
    qkv_out[:, 2 * GROUP_W:] = y[:, 2 * GROUP_W:]
    t = tail_ref[...]
    lane = lax.broadcasted_iota(jnp.int32, t.shape, 1)
    alpha = pltpu.roll(t, LANES - TAIL_ALPHA, 1)
    beta = pltpu.roll(t, LANES - TAIL_BETA + 2 * HEADS, 1)
    g = -jnp.exp(alog_ref[...]) * _softplus(alpha + dtb_ref[...])
    gates_out[...] = jnp.where(lane < 2 * HEADS, g, jnp.where(lane < 4 * HEADS, _sigmoid(beta), 0.0))


def _gdn_prep(P, seq_len, cw, alog, dtb):
    M = P.shape[0]
    tm = 256
    w = 3 * GROUP_W
    prev, nxt = _halo_specs(tm, w, 0, M, 1)
    full = lambda i: (0, 0)
    return pl.pallas_call(
        functools.partial(_gdn_prep_kernel, seq_len),
        grid=(M // tm,),
        in_specs=[pl.BlockSpec((tm, w), lambda i: (i, 0)), prev, nxt,
                  pl.BlockSpec((tm, LANES), lambda i: (i, P_TAIL // LANES)),
                  pl.BlockSpec((4, w), full), pl.BlockSpec((1, LANES), full), pl.BlockSpec((1, LANES), full)],
        out_specs=[pl.BlockSpec((tm, w), lambda i: (i, 0)), pl.BlockSpec((tm, LANES), lambda i: (i, 0))],
        out_shape=[jax.ShapeDtypeStruct((M, w), F32), jax.ShapeDtypeStruct((M, LANES), F32)],
        compiler_params=_cparams(("parallel",)),
        name="gdn_prep",
    )(P, P, P, P, cw, alog, dtb)


def _time_cumsum(x, rev):
    k = 1
    while k < x.shape[0]:
        x = x + _shift_rows(x, k, 0.0, rev)
        k *= 2
    return x


def _gdn_local_kernel(qkv_ref, gates_ref, a_out, qk_out):
    C = GDN_CHUNK
    ri = lax.broadcasted_iota(jnp.int32, (C, C), 0)
    ci = lax.broadcasted_iota(jnp.int32, (C, C), 1)
    for c in range(qkv_ref.shape[0] // C):
        rows = slice(c * C, (c + 1) * C)
        gt = gates_ref[rows, :]
        zs = []
        for h in range(HEADS):
            q = qkv_ref[rows, h * HEAD_DIM:(h + 1) * HEAD_DIM]
            kk = qkv_ref[rows, GROUP_W + h * HEAD_DIM:GROUP_W + (h + 1) * HEAD_DIM]
            zs.append(_dot_nt(jnp.concatenate([kk, q], axis=0), kk))
        for d in range(2):
            rev = d == 1
            incl = (ci >= ri) if rev else (ci <= ri)
            strict = (ci > ri) if rev else (ci < ri)
            gcum = _time_cumsum(gt, rev)
            grow = gcum.T
            for h in range(HEADS):
                lg = d * HEADS + h
                gc_c = gcum[:, lg:lg + 1]
                gc_r = grow[lg:lg + 1, :]
                beta = gt[:, 2 * HEADS + lg:2 * HEADS + lg + 1]
                decay = jnp.where(incl, jnp.exp(jnp.where(incl, gc_c - gc_r, 0.0)), 0.0)
                lanes = slice((h % 2) * C, (h % 2 + 1) * C)
                a_out[d, c * 2 + h // 2, :, lanes] = jnp.where(strict, zs[h][:C] * beta * decay, 0.0)
                qk_out[d, c * 2 + h // 2, :, lanes] = zs[h][C:] * decay


def _gdn_local(qkv, gates):
    M = qkv.shape[0]
    tm = 256
    npair = tm // GDN_CHUNK * (HEADS // 2)
    w = 3 * GROUP_W
    ospec = pl.BlockSpec((2, npair, GDN_CHUNK, LANES), lambda i: (0, i, 0, 0))
    sds = jax.ShapeDtypeStruct((2, M // GDN_CHUNK * (HEADS // 2), GDN_CHUNK, LANES), F32)
    return pl.pallas_call(
        _gdn_local_kernel,
        grid=(M // tm,),
        in_specs=[pl.BlockSpec((tm, w), lambda i: (i, 0)), pl.BlockSpec((tm, LANES), lambda i: (i, 0))],
        out_specs=[ospec, ospec],
        out_shape=[sds, sds],
        compiler_params=_cparams(("parallel",)),
        name="gdn_local",
    )(qkv, gates)


SOLVE_BATCH = 128


def _gdn_solve_kernel(upper, a_ref, o_ref, at_ref, m_ref):
    C = GDN_CHUNK
    ns = C // SUBLANES
    for i in range(C):
        xt = a_ref[pl.ds(i, SOLVE_BATCH, stride=C), :].T
        at_ref[i, 0] = xt[:C]
        at_ref[i, 1] = xt[C:]
    sub = lax.broadcasted_iota(jnp.int32, (SUBLANES, SOLVE_BATCH), 0)
    for t in range(C):
        i = C - 1 - t if upper else t
        si = i // SUBLANES
        slabs = list(range(si, ns)) if upper else list(range(si + 1))
        unit = jnp.where(sub == i % SUBLANES, 1.0, 0.0).astype(F32)
        zero = jnp.zeros((SUBLANES, SOLVE_BATCH), F32)
        init = tuple(unit if s == si else zero for _ in range(2) for s in slabs)

        def col_body(j, acc):
            new = []
            for hp in range(2):
                a = jnp.broadcast_to(at_ref[i, hp, pl.ds(j, 1), :], (SUBLANES, SOLVE_BATCH))
                for n, s in enumerate(slabs):
                    new.append(acc[hp * len(slabs) + n] - a * m_ref[hp, j, s * SUBLANES:(s + 1) * SUBLANES, :])
            return tuple(new)

        lo, hi = (i + 1, C) if upper else (0, i)
        if hi > lo:
            acc = lax.fori_loop(lo, hi, col_body, init, unroll=min(4, hi - lo))
        else:
            acc = init
        for hp in range(2):
            for s in range(ns):
                val = acc[hp * len(slabs) + slabs.index(s)] if s in slabs else zero
                m_ref[hp, i, s * SUBLANES:(s + 1) * SUBLANES, :] = val
    for i in range(C):
        y = jnp.concatenate([m_ref[0, i], m_ref[1, i]], axis=0)
        o_ref[pl.ds(i, SOLVE_BATCH, stride=C), :] = y.T


def _gdn_solve(a_all):
    _, NP, C, _ = a_all.shape
    rows = SOLVE_BATCH * C
    a2 = a_all.reshape(2, NP * C, LANES)
    outs = []
    for d in range(2):
        spec = pl.BlockSpec((None, rows, LANES), lambda b, d=d: (d, b, 0))
        outs.append(pl.pallas_call(
            functools.partial(_gdn_solve_kernel, d == 1),
            grid=(NP // SOLVE_BATCH,),
            in_specs=[spec],
            out_specs=pl.BlockSpec((rows, LANES), lambda b: (b, 0)),
            out_shape=jax.ShapeDtypeStruct((NP * C, LANES), F32),
            scratch_shapes=[pltpu.VMEM((C, 2, C, SOLVE_BATCH), F32), pltpu.VMEM((2, C, C, SOLVE_BATCH), F32)],
            compiler_params=_cparams(("parallel",)),
            name="gdn_solve_upper" if d else "gdn_solve_lower",
        )(a2).reshape(NP, C, LANES))
    return outs


def _gdn_kernel(rev, finish, d, *refs):
    if finish:
        (qkv_ref, gates_ref, tinv_ref, qk_ref, s0_ref, of_ref, gate_ref, ng_ref,
         out_ref, sout_ref, s_ref, o_ref) = refs
    else:
        (qkv_ref, gates_ref, tinv_ref, qk_ref, s0_ref, out_ref, sout_ref, s_ref) = refs
        o_ref = out_ref
    C = GDN_CHUNK
    nchunk = qkv_ref.shape[0] // C
    j = pl.program_id(1)

    @pl.when(j == 0)
    def _():
        s_ref[...] = s0_ref[...]

    last_row = 0 if rev else C - 1
    order = [(nchunk - 1 - t) if rev else t for t in range(nchunk)]

    def local(c):
        rows = slice(c * C, (c + 1) * C)
        gt = gates_ref[rows, :]
        gcum = _time_cumsum(gt, rev)
        out = []
        for h in range(HEADS):
            lg = d * HEADS + h
            lanes = slice((h % 2) * C, (h % 2 + 1) * C)
            gc_c = gcum[:, lg:lg + 1]
            beta = gt[:, 2 * HEADS + lg:2 * HEADS + lg + 1]
            gl = gcum[last_row:last_row + 1, lg:lg + 1]
            q = qkv_ref[rows, h * HEAD_DIM:(h + 1) * HEAD_DIM]
            kk = qkv_ref[rows, GROUP_W + h * HEAD_DIM:GROUP_W + (h + 1) * HEAD_DIM]
            v = qkv_ref[rows, 2 * GROUP_W + h * HEAD_DIM:2 * GROUP_W + (h + 1) * HEAD_DIM]
            tinv = tinv_ref[c * 2 + h // 2, :, lanes]
            qk = qk_ref[c * 2 + h // 2, :, lanes]
            kb = kk * beta
            eg = jnp.exp(gc_c)
            uw = _dot(tinv, jnp.concatenate([v * beta, kb * eg], axis=1))
            wq = jnp.concatenate([uw[:, HEAD_DIM:], q * eg], axis=0).astype(BF16)
            kdt = (kk * jnp.exp(gl - gc_c)).T
            qkk = jnp.concatenate([qk, kdt], axis=0).astype(BF16)
            out.append((uw[:, :HEAD_DIM], wq, qkk, jnp.exp(gl)))
        return out

    states = [s_ref[h] for h in range(HEADS)]
    nxt = local(order[0])
    for t, c in enumerate(order):
        cur = nxt
        ws = [jnp.dot(cur[h][1], states[h].astype(BF16), preferred_element_type=F32) for h in range(HEADS)]
        if t + 1 < nchunk:
            nxt = local(order[t + 1])
        for h in range(HEADS):
            u, _, qkk, decay = cur[h]
            v_new = u - ws[h][:C]
            res = jnp.dot(qkk, v_new.astype(BF16), preferred_element_type=F32)
            o_ref[c * C:(c + 1) * C, h * HEAD_DIM:(h + 1) * HEAD_DIM] = ws[h][C:] + res[:C]
            states[h] = states[h] * decay + res[C:]
    for h in range(HEADS):
        s_ref[h] = states[h]
    sout_ref[...] = s_ref[...]
    if finish:
        tot = of_ref[...] + o_ref[...]
        gate = gate_ref[...]
        for h in range(HEADS):
            sl = slice(h * HEAD_DIM, (h + 1) * HEAD_DIM)
            x = tot[:, sl]
            y = x * lax.rsqrt(jnp.mean(x * x, axis=-1, keepdims=True) + EPS) * ng_ref[...]
            out_ref[:, sl] = (y * _silu(gate[:, sl])).astype(out_ref.dtype)


def _gdn_pass(qkv, gates, tinv_all, tinv_off, qk, seq_len, rev, d, s0, of=None, P=None, ng=None):
    M = qkv.shape[0]
    B = M // seq_len
    rb = 256
    nblk = seq_len // rb
    finish = of is not None
    w = 3 * GROUP_W
    npair = rb // GDN_CHUNK * (HEADS // 2)
    toff = tinv_off // npair

    def blk(j):
        return (nblk - 1 - j) if rev else j

    sspec = pl.BlockSpec((None, HEADS, HEAD_DIM, HEAD_DIM), lambda b, j: (b, 0, 0, 0))
    specs = [pl.BlockSpec((rb, w), lambda b, j: (b * nblk + blk(j), 0)),
             pl.BlockSpec((rb, LANES), lambda b, j: (b * nblk + blk(j), 0)),
             pl.BlockSpec((npair, GDN_CHUNK, LANES), lambda b, j: (toff + b * nblk + blk(j), 0, 0)),
             pl.BlockSpec((None, npair, GDN_CHUNK, LANES), lambda b, j: (d, b * nblk + blk(j), 0, 0)),
             sspec]
    args = [qkv, gates, tinv_all, qk, s0]
    scratch = [pltpu.VMEM((HEADS, HEAD_DIM, HEAD_DIM), F32)]
    if finish:
        specs += [pl.BlockSpec((rb, GROUP_W), lambda b, j: (b * nblk + blk(j), 0)),
                  pl.BlockSpec((rb, GROUP_W), lambda b, j: (b * nblk + blk(j), P_GDN_GATE // GROUP_W)),
                  pl.BlockSpec((1, HEAD_DIM), lambda b, j: (0, 0))]
        args += [of, P, ng]
        scratch.append(pltpu.VMEM((rb, GROUP_W), F32))
    return pl.pallas_call(
        functools.partial(_gdn_kernel, rev, finish, d),
        grid=(B, nblk),
        in_specs=specs,
        out_specs=[pl.BlockSpec((rb, GROUP_W), lambda b, j: (b * nblk + blk(j), 0)), sspec],
        out_shape=[jax.ShapeDtypeStruct((M, GROUP_W), BF16 if finish else F32),
                   jax.ShapeDtypeStruct((B, HEADS, HEAD_DIM, HEAD_DIM), F32)],
        scratch_shapes=scratch,
        compiler_params=_cparams(("parallel", "arbitrary")),
        name="gdn_bwd" if rev else "gdn_fwd",
    )(*args)


def _mixer_gdn(Pc, Pl, ctx_len, seq_len, cw, alog, dtb, ng):
    B = Pc.shape[0] // ctx_len
    qkv_c, g_c = _gdn_prep(Pc, ctx_len, cw, alog, dtb)
    qkv_l, g_l = _gdn_prep(Pl, seq_len, cw, alog, dtb)
    a_c, qk_c = _gdn_local(qkv_c, g_c)
    a_l, qk_l = _gdn_local(qkv_l, g_l)
    n_c, n_l = a_c.shape[1], a_l.shape[1]
    n_pad = -(n_c + n_l) % SOLVE_BATCH
    pad = jnp.zeros((2, n_pad, GDN_CHUNK, LANES), F32)
    tinv = _gdn_solve(jnp.concatenate([a_c, a_l, pad], axis=1))
    zero = jnp.zeros((B, HEADS, HEAD_DIM, HEAD_DIM), F32)
    oc_f, sc_f = _gdn_pass(qkv_c, g_c, tinv[0], 0, qk_c, ctx_len, False, 0, zero)
    ol_f, _ = _gdn_pass(qkv_l, g_l, tinv[0], n_c, qk_l, seq_len, False, 0, sc_f)
    yc, sc_b = _gdn_pass(qkv_c, g_c, tinv[1], 0, qk_c, ctx_len, True, 1, zero, of=oc_f, P=Pc, ng=ng)
    yl, _ = _gdn_pass(qkv_l, g_l, tinv[1], n_c, qk_l, seq_len, True, 1, sc_b, of=ol_f, P=Pl, ng=ng)
    return yc, yl


def _pack_w_in(w_in):
    off = {}
    o = 0
    for name, n in (('lru_x', 512), ('lru_gate', 512), ('mla_cq', 384), ('mla_ckv', 256), ('mla_kr', 64),
                    ('gdn_q', 512), ('gdn_k', 512), ('gdn_v', 512), ('gdn_gate', 512),
                    ('gdn_beta', 8), ('gdn_alpha', 8), ('na_q', 512), ('na_k', 512), ('na_v', 512)):
        off[name] = (o, n)
        o += n
    order = ['gdn_q', 'gdn_k', 'gdn_v', 'gdn_gate', 'lru_x', 'lru_gate', 'na_q', 'na_k', 'na_v',
             'mla_cq', 'mla_ckv', 'mla_kr', 'gdn_beta', 'gdn_alpha']
    parts = [w_in[..., off[n][0]:off[n][0] + off[n][1]] for n in order]
    used = sum(off[n][1] for n in order)
    parts.append(jnp.zeros(w_in.shape[:-1] + (P_COLS - used,), w_in.dtype))
    return jnp.concatenate(parts, axis=-1).astype(BF16)


def _pack_w_uq(w_uq):
    L, K, _ = w_uq.shape
    w = w_uq.reshape(L, K, HEADS, MLA_QK)
    w = jnp.pad(w, ((0, 0), (0, 0), (0, 0), (0, MLA_QPAD - MLA_QK)))
    return w.reshape(L, K, HEADS * MLA_QPAD).astype(BF16)


def _pad_gain(g):
    return jnp.pad(g, ((0, 0), (0, MLA_QPAD - MLA_QK)))[:, None, :]


def _rope_tables(T):
    t = jnp.arange(T)
    rowp = (t // GRID_W).astype(F32)
    colp = (t % GRID_W).astype(F32)
    n_freq = MLA_ROPE // 4
    inv = ROPE_BASE ** (-jnp.arange(n_freq, dtype=F32) / n_freq)
    ang = jnp.concatenate([rowp[:, None] * inv, colp[:, None] * inv], axis=-1)
    cos, sin = jnp.cos(ang), jnp.sin(ang)
    z = jnp.zeros((T, LANES - MLA_ROPE), F32)
    return (jnp.concatenate([cos, cos, z], axis=-1), jnp.concatenate([-sin, sin, z], axis=-1))


def _lane_vec(x):
    L = x.shape[0]
    return jnp.pad(x.reshape(L, 2 * HEADS), ((0, 0), (0, LANES - 2 * HEADS)))[:, None, :]


def kernel(x, c, ctx, c_ctx, ada_w, ada_b, norm_mix_g, norm_ffn_g, w_in, w_out, lru_conv_w, lru_conv_b, lru_wa, lru_ba, lru_wx, lru_bx, lru_lam, mla_qa_g, mla_w_uq, mla_kva_g, mla_w_ukv, mla_qn_g, mla_kn_g, gdn_conv_w, gdn_a_log, gdn_dt_bias, gdn_norm_g, na_qn_g, na_kn_g, na_rpb, ffn_w_up, ffn_conv_w, ffn_conv_b, ffn_w_down):
    B, T, D = x.shape
    TC = ctx.shape[1]
    L = ada_w.shape[0]
    rows = T // GRID_W
    assert B <= 2 and T % 512 == 0 and TC == 256 and rows >= NA_WIN

    w_in_p = _pack_w_in(w_in)
    w_out4 = w_out.reshape(L, 4, GROUP_W, D).astype(BF16)
    w_up = ffn_w_up.astype(BF16)
    w_down = ffn_w_down.astype(BF16)
    w_uq = _pack_w_uq(mla_w_uq)
    w_ukv = mla_w_ukv.astype(BF16)
    qn_g = _pad_gain(mla_qn_g)
    kn_g = _pad_gain(mla_kn_g)
    lru_wcat = jnp.concatenate([lru_wa, lru_wx], axis=-1).astype(BF16)
    alog_v = _lane_vec(gdn_a_log)
    dtb_v = _lane_vec(gdn_dt_bias)
    cosf, sinf = _rope_tables(T)

    cvec = jnp.zeros((SUBLANES, D), F32).at[:B].set(c).at[2].set(c_ctx)
    mod_all = _modulation(cvec, ada_w, ada_b).reshape(L, SUBLANES, N_MOD, 1, D)

    gmix = norm_mix_g[:, None, :]
    gffn = norm_ffn_g[:, None, :]
    ffn_cb = ffn_conv_b[:, None, :]
    h_lat = x.reshape(B * T, D)
    h_ctx = ctx.reshape(B * TC, D)
    for l in range(L):
        want_ctx = l < L - 1
        mod = mod_all
        Pl = _inproj(l, h_lat, gmix, mod, w_in_p, T, False)
        Pc = _inproj(l, h_ctx, gmix, mod, w_in_p, TC, True)

        lw = lambda dd: (lru_conv_w[l], lru_conv_b[l][None, :], lru_wcat[l, dd], lru_ba[l, dd][None, :],
                         lru_bx[l, dd][None, :], lru_lam[l, dd][None, :])
        ya_c, ya_l = _mixer_lru(Pc, Pl, TC, T, lw(0), lw(1))

        mw = (mla_qa_g[l][None, :], w_uq[l], mla_kva_g[l][None, :], w_ukv[l], qn_g[l], kn_g[l])
        Qtc, Kc, Vtc = _mla_prep(Pc, TC, False, mw, cosf, sinf)
        Qtl, Kl, Vtl = _mla_prep(Pl, T, True, mw, cosf, sinf)
        yb_l = _flash_attention(Qtl, Kc, Vtc, T, TC, Kl, Vtl)

        yc_c, yc_l = _mixer_gdn(Pc, Pl, TC, T, gdn_conv_w[l], alog_v[l], dtb_v[l], gdn_norm_g[l][None, :])

        nqg, nkg = na_qn_g[l][None, :], na_kn_g[l][None, :]
        NQtc, NKc, NVtc = _na_prep(Pc, nqg, nkg)
        NQtl, NKl, NVtl = _na_prep(Pl, nqg, nkg)
        bias = _na_bias(na_rpb[l], rows)
        yd_l = _na_attention(NQtl, NKl, NVtl, NKc, NVtc, bias, T, TC)

        h_lat = _outproj(l, (ya_l, yb_l, yc_l, yd_l), w_out4, h_lat, mod, T, False)
        h_lat = _ffn(l, h_lat, gffn, mod, w_up, ffn_conv_w, ffn_cb, w_down, T, False)
        if want_ctx:
            yb_c = _flash_attention(Qtc, Kc, Vtc, TC, TC)
            yd_c = _flash_attention(NQtc, NKc, NVtc, TC, TC)
            h_ctx = _outproj(l, (ya_c, yb_c, yc_c, yd_c), w_out4, h_ctx, mod, TC, True)
            h_ctx = _ffn(l, h_ctx, gffn, mod, w_up, ffn_conv_w, ffn_cb, w_down, TC, True)
    return h_lat.reshape(B, T, D)
```

```python
import functools
import math

import jax
import jax.numpy as jnp
from jax import lax
from jax.experimental import pallas as pl
from jax.experimental.pallas import tpu as pltpu

F32 = jnp.float32
BF16 = jnp.bfloat16

GRID_W = 64
HEADS = 4
HEAD_DIM = 128
GROUP_W = HEADS * HEAD_DIM
N_MOD = 6
EPS = 1e-6
LRU_C = 8.0
MLA_Q_LORA = 384
MLA_KV_LORA = 256
MLA_NOPE = 128
MLA_ROPE = 64
MLA_QK = MLA_NOPE + MLA_ROPE
MLA_QPAD = 256
ROPE_BASE = 10000.0
GDN_CHUNK = 64
NA_KH = 8
NA_KW = 16
NA_BAND = 8
NA_WIN = 16
NEG_BIG = -1e30
LOG2E = math.log2(math.e)
FLASH_UNROLL = 8
FFN_SUB = 256

VMEM_LIMIT = 56 * 1024 * 1024
SUBLANES = 8
LANES = 128

P_GDN_QKV = 0
P_GDN_GATE = 1536
P_LRU_X = 2048
P_LRU_GATE = 2560
P_NA_QKV = 3072
P_MLA = 4608
P_COLS = 5376
P_TAIL = 5248
TAIL_BETA = 64
TAIL_ALPHA = 72


def _cparams(sem):
    return pltpu.CompilerParams(dimension_semantics=sem, vmem_limit_bytes=VMEM_LIMIT)


def _dot(a, b):
    return jnp.dot(a.astype(BF16), b.astype(BF16), preferred_element_type=F32)


def _dot_nt(a, b):
    return lax.dot_general(a.astype(BF16), b.astype(BF16), (((1,), (1,)), ((), ())),
                           preferred_element_type=F32)


def _sigmoid(x):
    return 1.0 / (1.0 + jnp.exp(-x))


def _silu(x):
    return x * _sigmoid(x)


def _softplus(x):
    return jnp.maximum(x, 0.0) + jnp.log(1.0 + jnp.exp(-jnp.abs(x)))


def _gelu_tanh(x):
    return 0.5 * x * (1.0 + jnp.tanh(math.sqrt(2.0 / math.pi) * (x + 0.044715 * x * x * x)))


def _mod_kernel(c_ref, w_ref, b_ref, o_ref):
    c = c_ref[...]
    o_ref[...] = _dot(_silu(c), w_ref[...]) + b_ref[...]


def _modulation(cvec, ada_w, ada_b):
    L, D, N = ada_w.shape
    tn = 1024
    return pl.pallas_call(
        _mod_kernel,
        grid=(L, N // tn),
        in_specs=[pl.BlockSpec((SUBLANES, D), lambda l, j: (0, 0)),
                  pl.BlockSpec((None, D, tn), lambda l, j: (l, 0, j)),
                  pl.BlockSpec((None, 1, tn), lambda l, j: (l, 0, j))],
        out_specs=pl.BlockSpec((None, SUBLANES, tn), lambda l, j: (l, 0, j)),
        out_shape=jax.ShapeDtypeStruct((L, SUBLANES, N), F32),
        compiler_params=_cparams(("parallel", "parallel")),
        name="modulation",
    )(cvec, ada_w, ada_b.reshape(L, 1, N))


def _mod_spec(l, which, rows_per_mod, D, ngrid):
    if rows_per_mod is None:
        row = lambda i: 2
    else:
        row = lambda i: i // rows_per_mod
    if ngrid == 1:
        return pl.BlockSpec((None, None, None, 1, D), lambda i: (l, row(i), which, 0, 0))
    return pl.BlockSpec((None, None, None, 1, D), lambda i, j: (l, row(i), which, 0, 0))


def _inproj_kernel(h_ref, g_ref, shift_ref, scale_ref, w_ref, o_ref, xn_ref):
    @pl.when(pl.program_id(1) == 0)
    def _():
        x = h_ref[...]
        y = x * lax.rsqrt(jnp.mean(x * x, axis=-1, keepdims=True) + EPS) * g_ref[...]
        xn_ref[...] = (y * (1.0 + scale_ref[...]) + shift_ref[...]).astype(BF16)

    o_ref[...] = jnp.dot(xn_ref[...], w_ref[...], preferred_element_type=F32)


def _inproj(l, h, gain, mod, w, seq_len, is_ctx):
    M, D = h.shape
    N = w.shape[2]
    tm = min(1024, M)
    tn = 768
    rpm = None if is_ctx else seq_len // tm
    return pl.pallas_call(
        _inproj_kernel,
        grid=(M // tm, N // tn),
        in_specs=[pl.BlockSpec((tm, D), lambda i, j: (i, 0)),
                  pl.BlockSpec((None, 1, D), lambda i, j: (l, 0, 0)),
                  _mod_spec(l, 0, rpm, D, 2),
                  _mod_spec(l, 1, rpm, D, 2),
                  pl.BlockSpec((None, D, tn), lambda i, j: (l, 0, j))],
        out_specs=pl.BlockSpec((tm, tn), lambda i, j: (i, j)),
        out_shape=jax.ShapeDtypeStruct((M, N), F32),
        scratch_shapes=[pltpu.VMEM((tm, D), BF16)],
        compiler_params=_cparams(("parallel", "arbitrary")),
        name="inproj",
    )(h, gain, mod, mod, w)


def _outproj_kernel(ya_ref, yb_ref, yc_ref, yd_ref, w_ref, h_ref, gate_ref, o_ref):
    acc = jnp.dot(ya_ref[...], w_ref[0], preferred_element_type=F32)
    acc += jnp.dot(yb_ref[...], w_ref[1], preferred_element_type=F32)
    acc += jnp.dot(yc_ref[...], w_ref[2], preferred_element_type=F32)
    acc += jnp.dot(yd_ref[...], w_ref[3], preferred_element_type=F32)
    o_ref[...] = h_ref[...] + gate_ref[...] * acc


def _outproj(l, ys, w4, h, mod, seq_len, is_ctx):
    M, D = h.shape
    tm = min(512, M)
    rpm = None if is_ctx else seq_len // tm
    yspec = pl.BlockSpec((tm, GROUP_W), lambda i: (i, 0))
    return pl.pallas_call(
        _outproj_kernel,
        grid=(M // tm,),
        in_specs=[yspec, yspec, yspec, yspec,
                  pl.BlockSpec((None, 4, GROUP_W, D), lambda i: (l, 0, 0, 0)),
                  pl.BlockSpec((tm, D), lambda i: (i, 0)),
                  _mod_spec(l, 2, rpm, D, 1)],
        out_specs=pl.BlockSpec((tm, D), lambda i: (i, 0)),
        out_shape=jax.ShapeDtypeStruct((M, D), F32),
        compiler_params=_cparams(("parallel",)),
        name="outproj",
    )(*ys, w4, h, mod)


def _halo_specs(tm, width, col_block, nrows, ngrid):
    r = tm // SUBLANES
    last = nrows // SUBLANES - 1
    if ngrid == 1:
        prev = pl.BlockSpec((SUBLANES, width), lambda i: (jnp.maximum(i * r - 1, 0), col_block))
        nxt = pl.BlockSpec((SUBLANES, width), lambda i: (jnp.minimum((i + 1) * r, last), col_block))
    else:
        prev = pl.BlockSpec((SUBLANES, width), lambda i, j: (jnp.maximum(i * r - 1, 0), col_block))
        nxt = pl.BlockSpec((SUBLANES, width), lambda i, j: (jnp.minimum((i + 1) * r, last), col_block))
    return prev, nxt


def _ffn_kernel(seq_len, h_ref, hp_ref, hn_ref, g_ref, shift_ref, scale_ref, gate_ref,
                wa_ref, wg_ref, cwa_ref, cwg_ref, cba_ref, cbg_ref, wd_ref, o_ref, xn_ref):
    tm = h_ref.shape[0]
    tf = wa_ref.shape[1]
    i = pl.program_id(0)
    j = pl.program_id(1)

    @pl.when(j == 0)
    def _():
        def norm(x):
            y = x * lax.rsqrt(jnp.mean(x * x, axis=-1, keepdims=True) + EPS) * g_ref[...]
            return (y * (1.0 + scale_ref[...]) + shift_ref[...]).astype(BF16)
        keep_prev = (i * tm) % seq_len != 0
        keep_next = ((i + 1) * tm) % seq_len != 0
        xn_ref[0:SUBLANES, :] = jnp.where(keep_prev, norm(hp_ref[...]), jnp.zeros((), BF16))
        xn_ref[SUBLANES:SUBLANES + tm, :] = norm(h_ref[...])
        xn_ref[SUBLANES + tm:, :] = jnp.where(keep_next, norm(hn_ref[...]), jnp.zeros((), BF16))

    def conv(u, cw_ref, cb_ref, cols):
        lo = u[SUBLANES - 1:SUBLANES - 1 + tm]
        mid = u[SUBLANES:SUBLANES + tm]
        hi = u[SUBLANES + 1:SUBLANES + 1 + tm]
        return lo * cw_ref[0:1, cols] + mid * cw_ref[1:2, cols] + hi * cw_ref[2:3, cols] + cb_ref[:, cols]

    xn = xn_ref[...]
    subs = [slice(s, s + FFN_SUB) for s in range(0, tf, FFN_SUB)]
    ups = [(jnp.dot(xn, wa_ref[:, c], preferred_element_type=F32),
            jnp.dot(xn, wg_ref[:, c], preferred_element_type=F32)) for c in subs]
    part = None
    for c, (ua, ug) in zip(subs, ups):
        act = conv(ua, cwa_ref, cba_ref, c) * _silu(conv(ug, cwg_ref, cbg_ref, c))
        d = jnp.dot(act.astype(BF16), wd_ref[c, :], preferred_element_type=F32)
        part = d if part is None else part + d

    @pl.when(j == 0)
    def _():
        o_ref[...] = part

    @pl.when(j > 0)
    def _():
        o_ref[...] += part

    @pl.when(j == pl.num_programs(1) - 1)
    def _():
        o_ref[...] = h_ref[...] + gate_ref[...] * o_ref[...]


def _ffn(l, h, gain, mod, w_up, conv_w, conv_b, w_down, seq_len, is_ctx):
    M, D = h.shape
    FF = w_down.shape[1]
    tm = min(512, seq_len)
    tf = 1024
    nf = FF // tf
    rpm = None if is_ctx else seq_len // tm
    prev, nxt = _halo_specs(tm, D, 0, M, 2)
    return pl.pallas_call(
        functools.partial(_ffn_kernel, seq_len),
        grid=(M // tm, nf),
        in_specs=[pl.BlockSpec((tm, D), lambda i, j: (i, 0)), prev, nxt,
                  pl.BlockSpec((None, 1, D), lambda i, j: (l, 0, 0)),
                  _mod_spec(l, 3, rpm, D, 2), _mod_spec(l, 4, rpm, D, 2), _mod_spec(l, 5, rpm, D, 2),
                  pl.BlockSpec((None, D, tf), lambda i, j: (l, 0, j)),
                  pl.BlockSpec((None, D, tf), lambda i, j: (l, 0, j + nf)),
                  pl.BlockSpec((None, 3, tf), lambda i, j: (l, 0, j)),
                  pl.BlockSpec((None, 3, tf), lambda i, j: (l, 0, j + nf)),
                  pl.BlockSpec((None, 1, tf), lambda i, j: (l, 0, j)),
                  pl.BlockSpec((None, 1, tf), lambda i, j: (l, 0, j + nf)),
                  pl.BlockSpec((None, tf, D), lambda i, j: (l, j, 0))],
        out_specs=pl.BlockSpec((tm, D), lambda i, j: (i, 0)),
        out_shape=jax.ShapeDtypeStruct((M, D), F32),
        scratch_shapes=[pltpu.VMEM((tm + 2 * SUBLANES, D), BF16)],
        compiler_params=_cparams(("parallel", "arbitrary")),
        name="conv_ffn",
    )(h, h, h, gain, mod, mod, mod, w_up, w_up, conv_w, conv_w, conv_b, conv_b, w_down)


def _conv4(xp_ref, x_ref, xn_ref, w_ref, pos, seq_len):
    tm = x_ref.shape[0]
    xe = jnp.concatenate([xp_ref[...], x_ref[...], xn_ref[...]], axis=0)
    t0 = jnp.where(pos >= 1, xe[SUBLANES - 1:SUBLANES - 1 + tm], 0.0)
    t1 = xe[SUBLANES:SUBLANES + tm]
    t2 = jnp.where(pos + 1 < seq_len, xe[SUBLANES + 1:SUBLANES + 1 + tm], 0.0)
    t3 = jnp.where(pos + 2 < seq_len, xe[SUBLANES + 2:SUBLANES + 2 + tm], 0.0)
    return t0 * w_ref[0:1, :] + t1 * w_ref[1:2, :] + t2 * w_ref[2:3, :] + t3 * w_ref[3:4, :]


def _shift_rows(x, k, fill, rev):
    n = x.shape[0]
    if k % SUBLANES == 0:
        pad = jnp.full((k, x.shape[1]), fill, x.dtype)
        return jnp.concatenate([x[k:], pad], 0) if rev else jnp.concatenate([pad, x[:n - k]], 0)
    row = lax.broadcasted_iota(jnp.int32, x.shape, 0)
    if rev:
        return jnp.where(row >= n - k, fill, pltpu.roll(x, n - k, 0))
    return jnp.where(row < k, fill, pltpu.roll(x, k, 0))


def _lru_kernel(rev, finish, seq_len, *refs):
    if finish:
        (x_ref, xp_ref, xn_ref, gate_ref, hf_ref, cw_ref, cb_ref, wcat_ref, ba_ref, bx_ref,
         lam_ref, h0_ref, out_ref, st_ref, carry_ref) = refs
    else:
        (x_ref, xp_ref, xn_ref, cw_ref, cb_ref, wcat_ref, ba_ref, bx_ref,
         lam_ref, h0_ref, out_ref, st_ref, carry_ref) = refs
    tc = x_ref.shape[0]
    j = pl.program_id(1)
    nch = pl.num_programs(1)
    c = (nch - 1 - j) if rev else j

    @pl.when(j == 0)
    def _():
        carry_ref[...] = h0_ref[...]

    pos = c * tc + lax.broadcasted_iota(jnp.int32, (tc, 1), 0)
    u = _conv4(xp_ref, x_ref, xn_ref, cw_ref, pos, seq_len) + cb_ref[...]
    rs, is_ = [], []
    for n in range(HEADS):
        z = _dot(u[:, n * HEAD_DIM:(n + 1) * HEAD_DIM], wcat_ref[n])
        rs.append(z[:, :HEAD_DIM])
        is_.append(z[:, HEAD_DIM:])
    r = _sigmoid(jnp.concatenate(rs, axis=1) + ba_ref[...])
    ig = _sigmoid(jnp.concatenate(is_, axis=1) + bx_ref[...])
    log_a = -LRU_C * r * _softplus(-lam_ref[...])
    a = jnp.exp(log_a)
    b = jnp.sqrt(1.0 - jnp.exp(2.0 * log_a)) * ig * u
    k = 1
    while k < tc:
        b = a * _shift_rows(b, k, 0.0, rev) + b
        a = a * _shift_rows(a, k, 1.0, rev)
        k *= 2
    h = b + a * carry_ref[0:1, :]
    last = h[0:1, :] if rev else h[tc - 1:tc, :]
    carry_ref[...] = jnp.broadcast_to(last, carry_ref.shape)
    st_ref[...] = jnp.broadcast_to(last, st_ref.shape)
    if finish:
        out_ref[...] = ((hf_ref[...] + h) * _gelu_tanh(gate_ref[...])).astype(out_ref.dtype)
    else:
        out_ref[...] = h


def _lru_pass(P, seq_len, rev, h0, wts, hf=None):
    cw, cb, wcat, ba, bx, lam = wts
    M = P.shape[0]
    B = M // seq_len
    tc = 256
    nch = seq_len // tc
    finish = hf is not None
    W = GROUP_W
    xb = P_LRU_X // W
    gb = P_LRU_GATE // W
    r = tc // SUBLANES
    last = M // SUBLANES - 1

    def chunk(j):
        return (nch - 1 - j) if rev else j

    row = lambda b, j: (b * nch + chunk(j), xb)
    specs = [pl.BlockSpec((tc, W), row),
             pl.BlockSpec((SUBLANES, W), lambda b, j: (jnp.maximum((b * nch + chunk(j)) * r - 1, 0), xb)),
             pl.BlockSpec((SUBLANES, W), lambda b, j: (jnp.minimum((b * nch + chunk(j) + 1) * r, last), xb))]
    args = [P, P, P]
    if finish:
        specs += [pl.BlockSpec((tc, W), lambda b, j: (b * nch + chunk(j), gb)),
                  pl.BlockSpec((tc, W), lambda b, j: (b * nch + chunk(j), 0))]
        args += [P, hf]
    full2 = lambda b, j: (0, 0)
    specs += [pl.BlockSpec((4, W), full2), pl.BlockSpec((1, W), full2),
              pl.BlockSpec((HEADS, HEAD_DIM, 2 * HEAD_DIM), lambda b, j: (0, 0, 0)),
              pl.BlockSpec((1, W), full2), pl.BlockSpec((1, W), full2), pl.BlockSpec((1, W), full2),
              pl.BlockSpec((None, SUBLANES, W), lambda b, j: (b, 0, 0))]
    args += [cw, cb, wcat, ba, bx, lam, h0]
    out_dtype = BF16 if finish else F32
    return pl.pallas_call(
        functools.partial(_lru_kernel, rev, finish, seq_len),
        grid=(B, nch),
        in_specs=specs,
        out_specs=[pl.BlockSpec((tc, W), lambda b, j: (b * nch + chunk(j), 0)),
                   pl.BlockSpec((None, SUBLANES, W), lambda b, j: (b, 0, 0))],
        out_shape=[jax.ShapeDtypeStruct((M, W), out_dtype),
                   jax.ShapeDtypeStruct((B, SUBLANES, W), F32)],
        scratch_shapes=[pltpu.VMEM((SUBLANES, W), F32)],
        compiler_params=_cparams(("parallel", "arbitrary")),
        name="rglru_bwd" if rev else "rglru_fwd",
    )(*args)


def _mixer_lru(Pc, Pl, ctx_len, seq_len, wts_f, wts_b):
    B = Pc.shape[0] // ctx_len
    zero = jnp.zeros((B, SUBLANES, GROUP_W), F32)
    hc_f, sc_f = _lru_pass(Pc, ctx_len, False, zero, wts_f)
    hl_f, _ = _lru_pass(Pl, seq_len, False, sc_f, wts_f)
    yc, sc_b = _lru_pass(Pc, ctx_len, True, zero, wts_b, hf=hc_f)
    yl, _ = _lru_pass(Pl, seq_len, True, sc_b, wts_b, hf=hl_f)
    return yc, yl


def _rope_mix(x, cos, sin):
    lane = lax.broadcasted_iota(jnp.int32, x.shape, 1)
    half = MLA_ROPE // 2
    swapped = jnp.where(lane < half, pltpu.roll(x, LANES - half, 1), pltpu.roll(x, half, 1))
    return x * cos + swapped * sin


def _mla_prep_kernel(use_rope, x_ref, qag_ref, wuq_ref, kvg_ref, wukv_ref, qg_ref, kg_ref,
                     cos_ref, sin_ref, qt_out, k_out, vt_out):
    x = x_ref[...]

    def rms(v, g):
        return v * lax.rsqrt(jnp.mean(v * v, axis=-1, keepdims=True) + EPS) * g

    q = _dot(rms(x[:, :MLA_Q_LORA], qag_ref[...]), wuq_ref[...])
    kv = _dot(rms(x[:, MLA_Q_LORA:MLA_Q_LORA + MLA_KV_LORA], kvg_ref[...]), wukv_ref[...])
    tail = x[:, MLA_Q_LORA + MLA_KV_LORA:]
    lane = lax.broadcasted_iota(jnp.int32, tail.shape, 1)
    kr = jnp.where(lane < MLA_ROPE, tail, 0.0)
    kr_ss = jnp.sum(kr * kr, axis=-1, keepdims=True)
    scale = MLA_QK ** -0.5 * LOG2E
    for h in range(HEADS):
        qh = q[:, h * MLA_QPAD:(h + 1) * MLA_QPAD]
        inv = lax.rsqrt(jnp.sum(qh * qh, axis=-1, keepdims=True) / MLA_QK + EPS) * scale
        qh = qh * inv * qg_ref[...]
        qn, qr = qh[:, :LANES], qh[:, LANES:]
        kn = kv[:, h * 2 * HEAD_DIM:h * 2 * HEAD_DIM + MLA_NOPE]
        vh = kv[:, h * 2 * HEAD_DIM + MLA_NOPE:(h + 1) * 2 * HEAD_DIM]
        kinv = lax.rsqrt((jnp.sum(kn * kn, axis=-1, keepdims=True) + kr_ss) / MLA_QK + EPS)
        kn = kn * kinv * kg_ref[:, :LANES]
        krh = kr * kinv * kg_ref[:, LANES:]
        if use_rope:
            qr = _rope_mix(qr, cos_ref[...], sin_ref[...])
            krh = _rope_mix(krh, cos_ref[...], sin_ref[...])
        qt_out[h] = jnp.concatenate([qn, qr], axis=1).T.astype(BF16)
        k_out[h, :, :LANES] = kn.astype(BF16)
        k_out[h, :, LANES:] = krh.astype(BF16)
        vt_out[h] = vh.T.astype(BF16)


def _mla_prep(P, seq_len, use_rope, wts, cosf, sinf):
    qag, wuq, kvg, wukv, qg, kg = wts
    M = P.shape[0]
    tm = 256
    nt = seq_len // tm
    full = lambda i: (0, 0)
    wmla = P_COLS - P_MLA
    tab = pl.BlockSpec((tm, LANES), (lambda i: (i % nt, 0)) if use_rope else (lambda i: (0, 0)))
    tspec = lambda w: pl.BlockSpec((HEADS, w, tm), lambda i: (0, 0, i))
    return pl.pallas_call(
        functools.partial(_mla_prep_kernel, use_rope),
        grid=(M // tm,),
        in_specs=[pl.BlockSpec((tm, wmla), lambda i: (i, P_MLA // wmla)),
                  pl.BlockSpec((1, MLA_Q_LORA), full), pl.BlockSpec(wuq.shape, full),
                  pl.BlockSpec((1, MLA_KV_LORA), full), pl.BlockSpec(wukv.shape, full),
                  pl.BlockSpec((1, MLA_QPAD), full), pl.BlockSpec((1, MLA_QPAD), full),
                  tab, tab],
        out_specs=[tspec(MLA_QPAD), pl.BlockSpec((HEADS, tm, MLA_QPAD), lambda i: (0, i, 0)),
                   tspec(HEAD_DIM)],
        out_shape=[jax.ShapeDtypeStruct((HEADS, MLA_QPAD, M), BF16),
                   jax.ShapeDtypeStruct((HEADS, M, MLA_QPAD), BF16),
                   jax.ShapeDtypeStruct((HEADS, HEAD_DIM, M), BF16)],
        compiler_params=_cparams(("parallel",)),
        name="mla_prep",
    )(P, qag, wuq, kvg, wukv, qg, kg, cosf, sinf)


def _flash_kernel(tk, has_lat, *refs):
    if has_lat:
        qt_ref, kc_ref, vct_ref, kl_ref, vlt_ref, o_ref, m_ref, l_ref, acc_ref, s_ref = refs
    else:
        qt_ref, kc_ref, vct_ref, o_ref = refs
    qt = qt_ref[...]
    s = jnp.dot(kc_ref[...], qt, preferred_element_type=F32)
    if has_lat:
        nk = kl_ref.shape[0] // tk

        def scores(c, slot):
            off = pl.multiple_of(jnp.minimum(c, nk - 1) * tk, tk)
            s_ref[slot] = jnp.dot(kl_ref[pl.ds(off, tk), :], qt, preferred_element_type=F32)

        scores(0, 0)
    m = jnp.max(s, axis=0, keepdims=True)
    p = jnp.exp2(s - m)
    l = jnp.sum(p, axis=0, keepdims=True)
    acc = jnp.dot(vct_ref[...], p.astype(BF16), preferred_element_type=F32)
    if has_lat:
        m_ref[...] = m
        l_ref[...] = l
        acc_ref[...] = acc

        def step(c, slot):
            scores(c + 1, 1 - slot)
            off = pl.multiple_of(c * tk, tk)
            s = s_ref[slot]
            m_old = m_ref[...]
            m_new = jnp.maximum(m_old, jnp.max(s, axis=0, keepdims=True))
            alpha = jnp.exp2(m_old - m_new)
            p = jnp.exp2(s - m_new)
            l_ref[...] = alpha * l_ref[...] + jnp.sum(p, axis=0, keepdims=True)
            acc_ref[...] = alpha * acc_ref[...] + jnp.dot(
                vlt_ref[:, pl.ds(off, tk)], p.astype(BF16), preferred_element_type=F32)
            m_ref[...] = m_new

        unroll = min(FLASH_UNROLL, nk)

        def body(j, carry):
            for u in range(unroll):
                step(unroll * j + u, u % 2)
            return carry

        lax.fori_loop(0, nk // unroll, body, 0)
        acc = acc_ref[...]
        l = l_ref[...]
    o_ref[...] = (acc / l).T.astype(o_ref.dtype)


def _flash_attention(Qt, Kc, Vct, seq_len, ctx_len, Kl=None, Vlt=None):
    H, dq, M = Qt.shape
    B = M // seq_len
    has_lat = Kl is not None
    tq = min(1024, seq_len)
    tk = 512
    nq = seq_len // tq
    specs = [pl.BlockSpec((None, dq, tq), lambda b, h, i: (h, 0, b * nq + i)),
             pl.BlockSpec((None, ctx_len, dq), lambda b, h, i: (h, b, 0)),
             pl.BlockSpec((None, HEAD_DIM, ctx_len), lambda b, h, i: (h, 0, b))]
    args = [Qt, Kc, Vct]
    scratch = []
    if has_lat:
        specs += [pl.BlockSpec((None, seq_len, dq), lambda b, h, i: (h, b, 0)),
                  pl.BlockSpec((None, HEAD_DIM, seq_len), lambda b, h, i: (h, 0, b))]
        args += [Kl, Vlt]
        nk = seq_len // tk
        assert nk % 2 == 0 and nk % min(FLASH_UNROLL, nk) == 0
        scratch = [pltpu.VMEM((1, tq), F32), pltpu.VMEM((1, tq), F32), pltpu.VMEM((HEAD_DIM, tq), F32),
                   pltpu.VMEM((2, tk, tq), F32)]
    return pl.pallas_call(
        functools.partial(_flash_kernel, tk, has_lat),
        grid=(B, H, nq),
        in_specs=specs,
        out_specs=pl.BlockSpec((tq, HEAD_DIM), lambda b, h, i: (b * nq + i, h)),
        out_shape=jax.ShapeDtypeStruct((M, GROUP_W), BF16),
        scratch_shapes=scratch,
        compiler_params=_cparams(("parallel", "parallel", "arbitrary")),
        name="mla_attention" if has_lat else "ctx_attention",
    )(*args)


def _na_prep_kernel(x_ref, qg_ref, kg_ref, qt_out, k_out, vt_out):
    x = x_ref[...]
    scale = HEAD_DIM ** -0.5 * LOG2E
    for h in range(HEADS):
        qh = x[:, h * HEAD_DIM:(h + 1) * HEAD_DIM]
        kh = x[:, GROUP_W + h * HEAD_DIM:GROUP_W + (h + 1) * HEAD_DIM]
        vh = x[:, 2 * GROUP_W + h * HEAD_DIM:2 * GROUP_W + (h + 1) * HEAD_DIM]
        qh = qh * (lax.rsqrt(jnp.mean(qh * qh, axis=-1, keepdims=True) + EPS) * scale) * qg_ref[...]
        kh = kh * lax.rsqrt(jnp.mean(kh * kh, axis=-1, keepdims=True) + EPS) * kg_ref[...]
        qt_out[h] = qh.T.astype(BF16)
        k_out[h] = kh.astype(BF16)
        vt_out[h] = vh.T.astype(BF16)


def _na_prep(P, qg, kg):
    M = P.shape[0]
    tm = 256
    w = 3 * GROUP_W
    hspec = pl.BlockSpec((HEADS, tm, HEAD_DIM), lambda i: (0, i, 0))
    tspec = pl.BlockSpec((HEADS, HEAD_DIM, tm), lambda i: (0, 0, i))
    sds = jax.ShapeDtypeStruct((HEADS, M, HEAD_DIM), BF16)
    tds = jax.ShapeDtypeStruct((HEADS, HEAD_DIM, M), BF16)
    return pl.pallas_call(
        _na_prep_kernel,
        grid=(M // tm,),
        in_specs=[pl.BlockSpec((tm, w), lambda i: (i, P_NA_QKV // w)),
                  pl.BlockSpec((1, HEAD_DIM), lambda i: (0, 0)),
                  pl.BlockSpec((1, HEAD_DIM), lambda i: (0, 0))],
        out_specs=[tspec, hspec, tspec],
        out_shape=[tds, sds, tds],
        compiler_params=_cparams(("parallel",)),
        name="na_prep",
    )(P, qg, kg)


def _na_bias_kernel(rows, rpb_ref, o_ref):
    h = pl.program_id(0)
    kc = lax.broadcasted_iota(jnp.int32, (GRID_W, GRID_W), 0)
    qc = lax.broadcasted_iota(jnp.int32, (GRID_W, GRID_W), 1)
    dc = jnp.clip(kc - qc + (NA_KW - 1), 0, 2 * NA_KW - 2)
    c0 = jnp.clip(qc - NA_KW // 2, 0, GRID_W - NA_KW)
    col_ok = (kc >= c0) & (kc < c0 + NA_KW)
    neg = jnp.full((GRID_W, GRID_W), NEG_BIG, F32)
    tiles = []
    for dr in range(2 * NA_KH - 1):
        t = jnp.zeros((GRID_W, GRID_W), F32)
        for d in range(2 * NA_KW - 1):
            t = t + jnp.where(dc == d, rpb_ref[h, dr, d], 0.0)
        tiles.append(jnp.where(col_ok, t * LOG2E, NEG_BIG))
    for kind in range(3):
        q_base = (0, NA_KH // 2, rows - NA_BAND)[kind]
        w_base = (0, 0, rows - NA_WIN)[kind]
        for qr in range(NA_BAND):
            r = q_base + qr
            if kind == 1:
                r0 = r - NA_KH // 2
            else:
                r0 = min(max(r - NA_KH // 2, 0), rows - NA_KH)
            for kr in range(NA_WIN):
                ka = w_base + kr
                ok = r0 <= ka < r0 + NA_KH
                blk = tiles[ka - r + NA_KH - 1] if ok else neg
                o_ref[kind, kr * GRID_W:(kr + 1) * GRID_W, qr * GRID_W:(qr + 1) * GRID_W] = blk


def _na_bias(rpb, rows):
    H = rpb.shape[0]
    nq, nk = NA_BAND * GRID_W, NA_WIN * GRID_W
    return pl.pallas_call(
        functools.partial(_na_bias_kernel, rows),
        grid=(H,),
        in_specs=[pl.BlockSpec(memory_space=pltpu.SMEM)],
        out_specs=pl.BlockSpec((None, 3, nk, nq), lambda h: (h, 0, 0, 0)),
        out_shape=jax.ShapeDtypeStruct((H, 3, nk, nq), F32),
        compiler_params=_cparams(("parallel",)),
        name="na_bias",
    )(rpb)


def _na_kernel(seq_len, qt_ref, k_ref, vt_ref, kc_ref, vct_ref, bias_ref, o_ref):
    nq = qt_ref.shape[1]
    nk = bias_ref.shape[0]
    j = pl.program_id(2)
    base = jnp.clip(j * nq - (NA_KH // 2) * GRID_W, 0, seq_len - nk)
    base = pl.multiple_of(base, (NA_KH // 2) * GRID_W)
    qt = qt_ref[...]
    s_ctx = jnp.dot(kc_ref[...], qt, preferred_element_type=F32)
    s_loc = jnp.dot(k_ref[pl.ds(base, nk), :], qt, preferred_element_type=F32) + bias_ref[...]
    m = jnp.maximum(jnp.max(s_loc, axis=0, keepdims=True), jnp.max(s_ctx, axis=0, keepdims=True))
    p_loc = jnp.exp2(s_loc - m)
    p_ctx = jnp.exp2(s_ctx - m)
    l = jnp.sum(p_loc, axis=0, keepdims=True) + jnp.sum(p_ctx, axis=0, keepdims=True)
    o = (jnp.dot(vt_ref[:, pl.ds(base, nk)], p_loc.astype(BF16), preferred_element_type=F32)
         + jnp.dot(vct_ref[...], p_ctx.astype(BF16), preferred_element_type=F32))
    o_ref[...] = (o / l).T.astype(o_ref.dtype)


def _na_attention(Qtl, Kl, Vtl, Kc, Vtc, bias, seq_len, ctx_len):
    H, d, M = Qtl.shape
    B = M // seq_len
    nq, nk = NA_BAND * GRID_W, NA_WIN * GRID_W
    nb = seq_len // nq

    def kind(j):
        return jnp.where(j == 0, 0, jnp.where(j == nb - 1, 2, 1))

    return pl.pallas_call(
        functools.partial(_na_kernel, seq_len),
        grid=(B, H, nb),
        in_specs=[pl.BlockSpec((None, d, nq), lambda b, h, j: (h, 0, b * nb + j)),
                  pl.BlockSpec((None, seq_len, d), lambda b, h, j: (h, b, 0)),
                  pl.BlockSpec((None, d, seq_len), lambda b, h, j: (h, 0, b)),
                  pl.BlockSpec((None, ctx_len, d), lambda b, h, j: (h, b, 0)),
                  pl.BlockSpec((None, d, ctx_len), lambda b, h, j: (h, 0, b)),
                  pl.BlockSpec((None, None, nk, nq), lambda b, h, j: (h, kind(j), 0, 0))],
        out_specs=pl.BlockSpec((nq, d), lambda b, h, j: (b * nb + j, h)),
        out_shape=jax.ShapeDtypeStruct((M, GROUP_W), BF16),
        compiler_params=_cparams(("parallel", "parallel", "arbitrary")),
        name="na_attention",
    )(Qtl, Kl, Vtl, Kc, Vtc, bias)


def _gdn_prep_kernel(seq_len, x_ref, xp_ref, xn_ref, tail_ref, cw_ref, alog_ref, dtb_ref,
                     qkv_out, gates_out):
    tm = x_ref.shape[0]
    pos = (pl.program_id(0) * tm + lax.broadcasted_iota(jnp.int32, (tm, 1), 0)) % seq_len
    y = _silu(_conv4(xp_ref, x_ref, xn_ref, cw_ref, pos, seq_len))
    for h in range(HEADS):
        sl = slice(h * HEAD_DIM, (h + 1) * HEAD_DIM)
        qh = y[:, sl]
        qkv_out[:, sl] = qh * (lax.rsqrt(jnp.sum(qh * qh, axis=-1, keepdims=True) + EPS) * HEAD_DIM ** -0.5)
        sl = slice(GROUP_W + h * HEAD_DIM, GROUP_W + (h + 1) * HEAD_DIM)
        kh = y[:, sl]
        qkv_out[:, sl] = kh * lax.rsqrt(jnp.sum(kh * kh, axis=-1, keepdims=True) + EPS)
    qkv_out[:, 2 * GROUP_W:] = y[:, 2 * GROUP_W:]
    t = tail_ref[...]
    lane = lax.broadcasted_iota(jnp.int32, t.shape, 1)
    alpha = pltpu.roll(t, LANES - TAIL_ALPHA, 1)
    beta = pltpu.roll(t, LANES - TAIL_BETA + 2 * HEADS, 1)
    g = -jnp.exp(alog_ref[...]) * _softplus(alpha + dtb_ref[...])
    gates_out[...] = jnp.where(lane < 2 * HEADS, g, jnp.where(lane < 4 * HEADS, _sigmoid(beta), 0.0))


def _gdn_prep(P, seq_len, cw, alog, dtb):
    M = P.shape[0]
    tm = 256
    w = 3 * GROUP_W
    prev, nxt = _halo_specs(tm, w, 0, M, 1)
    full = lambda i: (0, 0)
    return pl.pallas_call(
        functools.partial(_gdn_prep_kernel, seq_len),
        grid=(M // tm,),
        in_specs=[pl.BlockSpec((tm, w), lambda i: (i, 0)), prev, nxt,
                  pl.BlockSpec((tm, LANES), lambda i: (i, P_TAIL // LANES)),
                  pl.BlockSpec((4, w), full), pl.BlockSpec((1, LANES), full), pl.BlockSpec((1, LANES), full)],
        out_specs=[pl.BlockSpec((tm, w), lambda i: (i, 0)), pl.BlockSpec((tm, LANES), lambda i: (i, 0))],
        out_shape=[jax.ShapeDtypeStruct((M, w), F32), jax.ShapeDtypeStruct((M, LANES), F32)],
        compiler_params=_cparams(("parallel",)),
        name="gdn_prep",
    )(P, P, P, P, cw, alog, dtb)


def _time_cumsum(x, rev):
    k = 1
    while k < x.shape[0]:
        x = x + _shift_rows(x, k, 0.0, rev)
        k *= 2
    return x


def _gdn_local_kernel(qkv_ref, gates_ref, a_out, qk_out):
    C = GDN_CHUNK
    ri = lax.broadcasted_iota(jnp.int32, (C, C), 0)
    ci = lax.broadcasted_iota(jnp.int32, (C, C), 1)
    for c in range(qkv_ref.shape[0] // C):
        rows = slice(c * C, (c + 1) * C)
        gt = gates_ref[rows, :]
        zs = []
        for h in range(HEADS):
            q = qkv_ref[rows, h * HEAD_DIM:(h + 1) * HEAD_DIM]
            kk = qkv_ref[rows, GROUP_W + h * HEAD_DIM:GROUP_W + (h + 1) * HEAD_DIM]
            zs.append(_dot_nt(jnp.concatenate([kk, q], axis=0), kk))
        for d in range(2):
            rev = d == 1
            incl = (ci >= ri) if rev else (ci <= ri)
            strict = (ci > ri) if rev else (ci < ri)
            gcum = _time_cumsum(gt, rev)
            grow = gcum.T
            for h in range(HEADS):
                lg = d * HEADS + h
                gc_c = gcum[:, lg:lg + 1]
                gc_r = grow[lg:lg + 1, :]
                beta = gt[:, 2 * HEADS + lg:2 * HEADS + lg + 1]
                decay = jnp.where(incl, jnp.exp(jnp.where(incl, gc_c - gc_r, 0.0)), 0.0)
                lanes = slice((h % 2) * C, (h % 2 + 1) * C)
                a_out[d, c * 2 + h // 2, :, lanes] = jnp.where(strict, zs[h][:C] * beta * decay, 0.0)
                qk_out[d, c * 2 + h // 2, :, lanes] = zs[h][C:] * decay


def _gdn_local(qkv, gates):
    M = qkv.shape[0]
    tm = 256
    npair = tm // GDN_CHUNK * (HEADS // 2)
    w = 3 * GROUP_W
    ospec = pl.BlockSpec((2, npair, GDN_CHUNK, LANES), lambda i: (0, i, 0, 0))
    sds = jax.ShapeDtypeStruct((2, M // GDN_CHUNK * (HEADS // 2), GDN_CHUNK, LANES), F32)
    return pl.pallas_call(
        _gdn_local_kernel,
        grid=(M // tm,),
        in_specs=[pl.BlockSpec((tm, w), lambda i: (i, 0)), pl.BlockSpec((tm, LANES), lambda i: (i, 0))],
        out_specs=[ospec, ospec],
        out_shape=[sds, sds],
        compiler_params=_cparams(("parallel",)),
        name="gdn_local",
    )(qkv, gates)


SOLVE_BATCH = 128


def _gdn_solve_kernel(upper, a_ref, o_ref, at_ref, m_ref):
    C = GDN_CHUNK
    ns = C // SUBLANES
    for i in range(C):
        xt = a_ref[pl.ds(i, SOLVE_BATCH, stride=C), :].T
        at_ref[i, 0] = xt[:C]
        at_ref[i, 1] = xt[C:]
    sub = lax.broadcasted_iota(jnp.int32, (SUBLANES, SOLVE_BATCH), 0)
    for t in range(C):
        i = C - 1 - t if upper else t
        si = i // SUBLANES
        slabs = list(range(si, ns)) if upper else list(range(si + 1))
        unit = jnp.where(sub == i % SUBLANES, 1.0, 0.0).astype(F32)
        zero = jnp.zeros((SUBLANES, SOLVE_BATCH), F32)
        init = tuple(unit if s == si else zero for _ in range(2) for s in slabs)

        def col_body(j, acc):
            new = []
            for hp in range(2):
                a = jnp.broadcast_to(at_ref[i, hp, pl.ds(j, 1), :], (SUBLANES, SOLVE_BATCH))
                for n, s in enumerate(slabs):
                    new.append(acc[hp * len(slabs) + n] - a * m_ref[hp, j, s * SUBLANES:(s + 1) * SUBLANES, :])
            return tuple(new)

        lo, hi = (i + 1, C) if upper else (0, i)
        if hi > lo:
            acc = lax.fori_loop(lo, hi, col_body, init, unroll=min(4, hi - lo))
        else:
            acc = init
        for hp in range(2):
            for s in range(ns):
                val = acc[hp * len(slabs) + slabs.index(s)] if s in slabs else zero
                m_ref[hp, i, s * SUBLANES:(s + 1) * SUBLANES, :] = val
    for i in range(C):
        y = jnp.concatenate([m_ref[0, i], m_ref[1, i]], axis=0)
        o_ref[pl.ds(i, SOLVE_BATCH, stride=C), :] = y.T


def _gdn_solve(a_all):
    _, NP, C, _ = a_all.shape
    rows = SOLVE_BATCH * C
    a2 = a_all.reshape(2, NP * C, LANES)
    outs = []
    for d in range(2):
        spec = pl.BlockSpec((None, rows, LANES), lambda b, d=d: (d, b, 0))
        outs.append(pl.pallas_call(
            functools.partial(_gdn_solve_kernel, d == 1),
            grid=(NP // SOLVE_BATCH,),
            in_specs=[spec],
            out_specs=pl.BlockSpec((rows, LANES), lambda b: (b, 0)),
            out_shape=jax.ShapeDtypeStruct((NP * C, LANES), F32),
            scratch_shapes=[pltpu.VMEM((C, 2, C, SOLVE_BATCH), F32), pltpu.VMEM((2, C, C, SOLVE_BATCH), F32)],
            compiler_params=_cparams(("parallel",)),
            name="gdn_solve_upper" if d else "gdn_solve_lower",
        )(a2).reshape(NP, C, LANES))
    return outs


def _gdn_kernel(rev, finish, d, *refs):
    if finish:
        (qkv_ref, gates_ref, qk_ref, s0_ref, of_ref, gate_ref, ng_ref, *tinv_refs) = refs[:-4]
        out_ref, sout_ref, s_ref, o_ref = refs[-4:]
    else:
        (qkv_ref, gates_ref, qk_ref, s0_ref, *tinv_refs) = refs[:-3]
        out_ref, sout_ref, s_ref = refs[-3:]
        o_ref = out_ref
    C = GDN_CHUNK
    B = qkv_ref.shape[0]
    nchunk = qkv_ref.shape[1] // C
    j = pl.program_id(0)

    @pl.when(j == 0)
    def _():
        s_ref[...] = s0_ref[...]

    last_row = 0 if rev else C - 1
    order = [(nchunk - 1 - t) if rev else t for t in range(nchunk)]
    units = [(b, h) for b in range(B) for h in range(HEADS)]

    def local(c):
        rows = slice(c * C, (c + 1) * C)
        out = {}
        for b in range(B):
            gt = gates_ref[b, rows, :]
            gcum = _time_cumsum(gt, rev)
            for h in range(HEADS):
                lg = d * HEADS + h
                lanes = slice((h % 2) * C, (h % 2 + 1) * C)
                gc_c = gcum[:, lg:lg + 1]
                beta = gt[:, 2 * HEADS + lg:2 * HEADS + lg + 1]
                gl = gcum[last_row:last_row + 1, lg:lg + 1]
                q = qkv_ref[b, rows, h * HEAD_DIM:(h + 1) * HEAD_DIM]
                kk = qkv_ref[b, rows, GROUP_W + h * HEAD_DIM:GROUP_W + (h + 1) * HEAD_DIM]
                v = qkv_ref[b, rows, 2 * GROUP_W + h * HEAD_DIM:2 * GROUP_W + (h + 1) * HEAD_DIM]
                tinv = tinv_refs[b][c * 2 + h // 2, :, lanes]
                qk = qk_ref[b, c * 2 + h // 2, :, lanes]
                kb = kk * beta
                eg = jnp.exp(gc_c)
                uw = _dot(tinv, jnp.concatenate([v * beta, kb * eg], axis=1))
                wq = jnp.concatenate([uw[:, HEAD_DIM:], q * eg], axis=0).astype(BF16)
                kdt = (kk * jnp.exp(gl - gc_c)).T
                qkk = jnp.concatenate([qk, kdt], axis=0).astype(BF16)
                out[b, h] = (uw[:, :HEAD_DIM], wq, qkk, jnp.exp(gl))
        return out

    states = {(b, h): s_ref[b, h] for b, h in units}
    nxt = local(order[0])
    for t, c in enumerate(order):
        cur = nxt
        ws = {k: jnp.dot(cur[k][1], states[k].astype(BF16), preferred_element_type=F32) for k in units}
        if t + 1 < nchunk:
            nxt = local(order[t + 1])
        for b, h in units:
            u, _, qkk, decay = cur[b, h]
            v_new = u - ws[b, h][:C]
            res = jnp.dot(qkk, v_new.astype(BF16), preferred_element_type=F32)
            o_ref[b, c * C:(c + 1) * C, h * HEAD_DIM:(h + 1) * HEAD_DIM] = ws[b, h][C:] + res[:C]
            states[b, h] = states[b, h] * decay + res[C:]
    for b, h in units:
        s_ref[b, h] = states[b, h]
    sout_ref[...] = s_ref[...]
    if finish:
        for b in range(B):
            tot = of_ref[b] + o_ref[b]
            gate = gate_ref[b]
            for h in range(HEADS):
                sl = slice(h * HEAD_DIM, (h + 1) * HEAD_DIM)
                x = tot[:, sl]
                y = x * lax.rsqrt(jnp.mean(x * x, axis=-1, keepdims=True) + EPS) * ng_ref[...]
                out_ref[b, :, sl] = (y * _silu(gate[:, sl])).astype(out_ref.dtype)


def _gdn_pass(qkv, gates, tinv_all, tinv_off, qk, seq_len, rev, d, s0, of=None, P=None, ng=None):
    M = qkv.shape[0]
    B = M // seq_len
    rb = 256
    nblk = seq_len // rb
    finish = of is not None
    w = 3 * GROUP_W
    npair = rb // GDN_CHUNK * (HEADS // 2)
    toff = tinv_off // npair

    def blk(j):
        return (nblk - 1 - j) if rev else j

    sspec = pl.BlockSpec((B, HEADS, HEAD_DIM, HEAD_DIM), lambda j: (0, 0, 0, 0))
    specs = [pl.BlockSpec((B, rb, w), lambda j: (0, blk(j), 0)),
             pl.BlockSpec((B, rb, LANES), lambda j: (0, blk(j), 0)),
             pl.BlockSpec((None, B, npair, GDN_CHUNK, LANES), lambda j: (d, 0, blk(j), 0, 0)),
             sspec]
    args = [qkv.reshape(B, seq_len, w), gates.reshape(B, seq_len, LANES),
            qk.reshape(2, B, nblk * npair, GDN_CHUNK, LANES), s0]
    scratch = [pltpu.VMEM((B, HEADS, HEAD_DIM, HEAD_DIM), F32)]
    if finish:
        specs += [pl.BlockSpec((B, rb, GROUP_W), lambda j: (0, blk(j), 0)),
                  pl.BlockSpec((B, rb, GROUP_W), lambda j: (0, blk(j), P_GDN_GATE // GROUP_W)),
                  pl.BlockSpec((1, HEAD_DIM), lambda j: (0, 0))]
        args += [of.reshape(B, seq_len, GROUP_W), P.reshape(B, seq_len, P.shape[1]), ng]
        scratch.append(pltpu.VMEM((B, rb, GROUP_W), F32))
    for b in range(B):
        specs.append(pl.BlockSpec((npair, GDN_CHUNK, LANES), lambda j, b=b: (toff + b * nblk + blk(j), 0, 0)))
        args.append(tinv_all)
    out, s_out = pl.pallas_call(
        functools.partial(_gdn_kernel, rev, finish, d),
        grid=(nblk,),
        in_specs=specs,
        out_specs=[pl.BlockSpec((B, rb, GROUP_W), lambda j: (0, blk(j), 0)), sspec],
        out_shape=[jax.ShapeDtypeStruct((B, seq_len, GROUP_W), BF16 if finish else F32),
                   jax.ShapeDtypeStruct((B, HEADS, HEAD_DIM, HEAD_DIM), F32)],
        scratch_shapes=scratch,
        compiler_params=_cparams(("arbitrary",)),
        name="gdn_bwd" if rev else "gdn_fwd",
    )(*args)
    return out.reshape(M, GROUP_W), s_out


def _mixer_gdn(Pc, Pl, ctx_len, seq_len, cw, alog, dtb, ng):
    B = Pc.shape[0] // ctx_len
    qkv_c, g_c = _gdn_prep(Pc, ctx_len, cw, alog, dtb)
    qkv_l, g_l = _gdn_prep(Pl, seq_len, cw, alog, dtb)
    a_c, qk_c = _gdn_local(qkv_c, g_c)
    a_l, qk_l = _gdn_local(qkv_l, g_l)
    n_c, n_l = a_c.shape[1], a_l.shape[1]
    n_pad = -(n_c + n_l) % SOLVE_BATCH
    pad = jnp.zeros((2, n_pad, GDN_CHUNK, LANES), F32)
    tinv = _gdn_solve(jnp.concatenate([a_c, a_l, pad], axis=1))
    zero = jnp.zeros((B, HEADS, HEAD_DIM, HEAD_DIM), F32)
    oc_f, sc_f = _gdn_pass(qkv_c, g_c, tinv[0], 0, qk_c, ctx_len, False, 0, zero)
    ol_f, _ = _gdn_pass(qkv_l, g_l, tinv[0], n_c, qk_l, seq_len, False, 0, sc_f)
    yc, sc_b = _gdn_pass(qkv_c, g_c, tinv[1], 0, qk_c, ctx_len, True, 1, zero, of=oc_f, P=Pc, ng=ng)
    yl, _ = _gdn_pass(qkv_l, g_l, tinv[1], n_c, qk_l, seq_len, True, 1, sc_b, of=ol_f, P=Pl, ng=ng)
    return yc, yl


def _pack_w_in(w_in):
    off = {}
    o = 0
    for name, n in (('lru_x', 512), ('lru_gate', 512), ('mla_cq', 384), ('mla_ckv', 256), ('mla_kr', 64),
                    ('gdn_q', 512), ('gdn_k', 512), ('gdn_v', 512), ('gdn_gate', 512),
                    ('gdn_beta', 8), ('gdn_alpha', 8), ('na_q', 512), ('na_k', 512), ('na_v', 512)):
        off[name] = (o, n)
        o += n
    order = ['gdn_q', 'gdn_k', 'gdn_v', 'gdn_gate', 'lru_x', 'lru_gate', 'na_q', 'na_k', 'na_v',
             'mla_cq', 'mla_ckv', 'mla_kr', 'gdn_beta', 'gdn_alpha']
    parts = [w_in[..., off[n][0]:off[n][0] + off[n][1]] for n in order]
    used = sum(off[n][1] for n in order)
    parts.append(jnp.zeros(w_in.shape[:-1] + (P_COLS - used,), w_in.dtype))
    return jnp.concatenate(parts, axis=-1).astype(BF16)


def _pack_w_uq(w_uq):
    L, K, _ = w_uq.shape
    w = w_uq.reshape(L, K, HEADS, MLA_QK)
    w = jnp.pad(w, ((0, 0), (0, 0), (0, 0), (0, MLA_QPAD - MLA_QK)))
    return w.reshape(L, K, HEADS * MLA_QPAD).astype(BF16)


def _pad_gain(g):
    return jnp.pad(g, ((0, 0), (0, MLA_QPAD - MLA_QK)))[:, None, :]


def _rope_tables(T):
    t = jnp.arange(T)
    rowp = (t // GRID_W).astype(F32)
    colp = (t % GRID_W).astype(F32)
    n_freq = MLA_ROPE // 4
    inv = ROPE_BASE ** (-jnp.arange(n_freq, dtype=F32) / n_freq)
    ang = jnp.concatenate([rowp[:, None] * inv, colp[:, None] * inv], axis=-1)
    cos, sin = jnp.cos(ang), jnp.sin(ang)
    z = jnp.zeros((T, LANES - MLA_ROPE), F32)
    return (jnp.concatenate([cos, cos, z], axis=-1), jnp.concatenate([-sin, sin, z], axis=-1))


def _lane_vec(x):
    L = x.shape[0]
    return jnp.pad(x.reshape(L, 2 * HEADS), ((0, 0), (0, LANES - 2 * HEADS)))[:, None, :]


def kernel(x, c, ctx, c_ctx, ada_w, ada_b, norm_mix_g, norm_ffn_g, w_in, w_out, lru_conv_w, lru_conv_b, lru_wa, lru_ba, lru_wx, lru_bx, lru_lam, mla_qa_g, mla_w_uq, mla_kva_g, mla_w_ukv, mla_qn_g, mla_kn_g, gdn_conv_w, gdn_a_log, gdn_dt_bias, gdn_norm_g, na_qn_g, na_kn_g, na_rpb, ffn_w_up, ffn_conv_w, ffn_conv_b, ffn_w_down):
    B, T, D = x.shape
    TC = ctx.shape[1]
    L = ada_w.shape[0]
    rows = T // GRID_W
    assert B <= 2 and T % 512 == 0 and TC == 256 and rows >= NA_WIN

    w_in_p = _pack_w_in(w_in)
    w_out4 = w_out.reshape(L, 4, GROUP_W, D).astype(BF16)
    w_up = ffn_w_up.astype(BF16)
    w_down = ffn_w_down.astype(BF16)
    w_uq = _pack_w_uq(mla_w_uq)
    w_ukv = mla_w_ukv.astype(BF16)
    qn_g = _pad_gain(mla_qn_g)
    kn_g = _pad_gain(mla_kn_g)
    lru_wcat = jnp.concatenate([lru_wa, lru_wx], axis=-1).astype(BF16)
    alog_v = _lane_vec(gdn_a_log)
    dtb_v = _lane_vec(gdn_dt_bias)
    cosf, sinf = _rope_tables(T)

    cvec = jnp.zeros((SUBLANES, D), F32).at[:B].set(c).at[2].set(c_ctx)
    mod_all = _modulation(cvec, ada_w, ada_b).reshape(L, SUBLANES, N_MOD, 1, D)

    gmix = norm_mix_g[:, None, :]
    gffn = norm_ffn_g[:, None, :]
    ffn_cb = ffn_conv_b[:, None, :]
    h_lat = x.reshape(B * T, D)
    h_ctx = ctx.reshape(B * TC, D)
    for l in range(L):
        want_ctx = l < L - 1
        mod = mod_all
        Pl = _inproj(l, h_lat, gmix, mod, w_in_p, T, False)
        Pc = _inproj(l, h_ctx, gmix, mod, w_in_p, TC, True)

        lw = lambda dd: (lru_conv_w[l], lru_conv_b[l][None, :], lru_wcat[l, dd], lru_ba[l, dd][None, :],
                         lru_bx[l, dd][None, :], lru_lam[l, dd][None, :])
        ya_c, ya_l = _mixer_lru(Pc, Pl, TC, T, lw(0), lw(1))

        mw = (mla_qa_g[l][None, :], w_uq[l], mla_kva_g[l][None, :], w_ukv[l], qn_g[l], kn_g[l])
        Qtc, Kc, Vtc = _mla_prep(Pc, TC, False, mw, cosf, sinf)
        Qtl, Kl, Vtl = _mla_prep(Pl, T, True, mw, cosf, sinf)
        yb_l = _flash_attention(Qtl, Kc, Vtc, T, TC, Kl, Vtl)

        yc_c, yc_l = _mixer_gdn(Pc, Pl, TC, T, gdn_conv_w[l], alog_v[l], dtb_v[l], gdn_norm_g[l][None, :])

        nqg, nkg = na_qn_g[l][None, :], na_kn_g[l][None, :]
        NQtc, NKc, NVtc = _na_prep(Pc, nqg, nkg)
        NQtl, NKl, NVtl = _na_prep(Pl, nqg, nkg)
        bias = _na_bias(na_rpb[l], rows)
        yd_l = _na_attention(NQtl, NKl, NVtl, NKc, NVtc, bias, T, TC)

        h_lat = _outproj(l, (ya_l, yb_l, yc_l, yd_l), w_out4, h_lat, mod, T, False)
        h_lat = _ffn(l, h_lat, gffn, mod, w_up, ffn_conv_w, ffn_cb, w_down, T, False)
        if want_ctx:
            yb_c = _flash_attention(Qtc, Kc, Vtc, TC, TC)
            yd_c = _flash_attention(NQtc, NKc, NVtc, TC, TC)
            h_ctx = _outproj(l, (ya_c, yb_c, yc_c, yd_c), w_out4, h_ctx, mod, TC, True)
            h_ctx = _ffn(l, h_ctx, gffn, mod, w_up, ffn_conv_w, ffn_cb, w_down, TC, True)
    return h_lat.reshape(B, T, D)
```

```python
import functools
import math

import jax
import jax.numpy as jnp
from jax import lax
from jax.experimental import pallas as pl
from jax.experimental.pallas import tpu as pltpu

F32 = jnp.float32
BF16 = jnp.bfloat16

GRID_W = 64
HEADS = 4
HEAD_DIM = 128
GROUP_W = HEADS * HEAD_DIM
N_MOD = 6
EPS = 1e-6
LRU_C = 8.0
MLA_Q_LORA = 384
MLA_KV_LORA = 256
MLA_NOPE = 128
MLA_ROPE = 64
MLA_QK = MLA_NOPE + MLA_ROPE
MLA_QPAD = 256
ROPE_BASE = 10000.0
GDN_CHUNK = 64
NA_KH = 8
NA_KW = 16
NA_BAND = 8
NA_WIN = 16
NEG_BIG = -1e30
LOG2E = math.log2(math.e)
FLASH_UNROLL = 8
FFN_SUB = 256
INPROJ_SUB = 4

VMEM_LIMIT = 56 * 1024 * 1024
SUBLANES = 8
LANES = 128

P_GDN_QKV = 0
P_GDN_GATE = 1536
P_LRU_X = 2048
P_LRU_GATE = 2560
P_NA_QKV = 3072
P_MLA = 4608
P_COLS = 5376
P_TAIL = 5248
TAIL_BETA = 64
TAIL_ALPHA = 72


def _cparams(sem):
    return pltpu.CompilerParams(dimension_semantics=sem, vmem_limit_bytes=VMEM_LIMIT)


def _dot(a, b):
    return jnp.dot(a.astype(BF16), b.astype(BF16), preferred_element_type=F32)


def _dot_nt(a, b):
    return lax.dot_general(a.astype(BF16), b.astype(BF16), (((1,), (1,)), ((), ())),
                           preferred_element_type=F32)


def _sigmoid(x):
    return 1.0 / (1.0 + jnp.exp(-x))


def _silu(x):
    return x * _sigmoid(x)


def _softplus(x):
    return jnp.maximum(x, 0.0) + jnp.log(1.0 + jnp.exp(-jnp.abs(x)))


def _gelu_tanh(x):
    return 0.5 * x * (1.0 + jnp.tanh(math.sqrt(2.0 / math.pi) * (x + 0.044715 * x * x * x)))


def _mod_kernel(c_ref, w_ref, b_ref, o_ref):
    c = c_ref[...]
    o_ref[...] = _dot(_silu(c), w_ref[...]) + b_ref[...]


def _modulation(cvec, ada_w, ada_b):
    L, D, N = ada_w.shape
    tn = 1024
    return pl.pallas_call(
        _mod_kernel,
        grid=(L, N // tn),
        in_specs=[pl.BlockSpec((SUBLANES, D), lambda l, j: (0, 0)),
                  pl.BlockSpec((None, D, tn), lambda l, j: (l, 0, j)),
                  pl.BlockSpec((None, 1, tn), lambda l, j: (l, 0, j))],
        out_specs=pl.BlockSpec((None, SUBLANES, tn), lambda l, j: (l, 0, j)),
        out_shape=jax.ShapeDtypeStruct((L, SUBLANES, N), F32),
        compiler_params=_cparams(("parallel", "parallel")),
        name="modulation",
    )(cvec, ada_w, ada_b.reshape(L, 1, N))


def _mod_spec(l, which, rows_per_mod, D, ngrid):
    if rows_per_mod is None:
        row = lambda i: 2
    else:
        row = lambda i: i // rows_per_mod
    if ngrid == 1:
        return pl.BlockSpec((None, None, None, 1, D), lambda i: (l, row(i), which, 0, 0))
    return pl.BlockSpec((None, None, None, 1, D), lambda i, j: (l, row(i), which, 0, 0))


def _inproj_kernel(h_ref, g_ref, shift_ref, scale_ref, w_ref, o_ref, xn_ref):
    tm = h_ref.shape[0]

    @pl.when(pl.program_id(1) == 0)
    def _():
        rs = tm // min(INPROJ_SUB, tm // 256)
        for r0 in range(0, tm, rs):
            x = h_ref[r0:r0 + rs, :]
            y = x * lax.rsqrt(jnp.mean(x * x, axis=-1, keepdims=True) + EPS) * g_ref[...]
            xb = (y * (1.0 + scale_ref[...]) + shift_ref[...]).astype(BF16)
            xn_ref[r0:r0 + rs, :] = xb
            o_ref[r0:r0 + rs, :] = jnp.dot(xb, w_ref[...], preferred_element_type=F32)

    @pl.when(pl.program_id(1) > 0)
    def _():
        o_ref[...] = jnp.dot(xn_ref[...], w_ref[...], preferred_element_type=F32)


def _inproj(l, h, gain, mod, w, seq_len, is_ctx):
    M, D = h.shape
    N = w.shape[2]
    tm = min(1024, M)
    tn = 768
    rpm = None if is_ctx else seq_len // tm
    return pl.pallas_call(
        _inproj_kernel,
        grid=(M // tm, N // tn),
        in_specs=[pl.BlockSpec((tm, D), lambda i, j: (i, 0)),
                  pl.BlockSpec((None, 1, D), lambda i, j: (l, 0, 0)),
                  _mod_spec(l, 0, rpm, D, 2),
                  _mod_spec(l, 1, rpm, D, 2),
                  pl.BlockSpec((None, D, tn), lambda i, j: (l, 0, j))],
        out_specs=pl.BlockSpec((tm, tn), lambda i, j: (i, j)),
        out_shape=jax.ShapeDtypeStruct((M, N), F32),
        scratch_shapes=[pltpu.VMEM((tm, D), BF16)],
        compiler_params=_cparams(("parallel", "arbitrary")),
        name="inproj",
    )(h, gain, mod, mod, w)


def _outproj_kernel(ya_ref, yb_ref, yc_ref, yd_ref, w_ref, h_ref, gate_ref, o_ref):
    acc = jnp.dot(ya_ref[...], w_ref[0], preferred_element_type=F32)
    acc += jnp.dot(yb_ref[...], w_ref[1], preferred_element_type=F32)
    acc += jnp.dot(yc_ref[...], w_ref[2], preferred_element_type=F32)
    acc += jnp.dot(yd_ref[...], w_ref[3], preferred_element_type=F32)
    o_ref[...] = h_ref[...] + gate_ref[...] * acc


def _outproj(l, ys, w4, h, mod, seq_len, is_ctx):
    M, D = h.shape
    tm = min(512, M)
    rpm = None if is_ctx else seq_len // tm
    yspec = pl.BlockSpec((tm, GROUP_W), lambda i: (i, 0))
    return pl.pallas_call(
        _outproj_kernel,
        grid=(M // tm,),
        in_specs=[yspec, yspec, yspec, yspec,
                  pl.BlockSpec((None, 4, GROUP_W, D), lambda i: (l, 0, 0, 0)),
                  pl.BlockSpec((tm, D), lambda i: (i, 0)),
                  _mod_spec(l, 2, rpm, D, 1)],
        out_specs=pl.BlockSpec((tm, D), lambda i: (i, 0)),
        out_shape=jax.ShapeDtypeStruct((M, D), F32),
        compiler_params=_cparams(("parallel",)),
        name="outproj",
    )(*ys, w4, h, mod)


def _halo_specs(tm, width, col_block, nrows, ngrid):
    r = tm // SUBLANES
    last = nrows // SUBLANES - 1
    if ngrid == 1:
        prev = pl.BlockSpec((SUBLANES, width), lambda i: (jnp.maximum(i * r - 1, 0), col_block))
        nxt = pl.BlockSpec((SUBLANES, width), lambda i: (jnp.minimum((i + 1) * r, last), col_block))
    else:
        prev = pl.BlockSpec((SUBLANES, width), lambda i, j: (jnp.maximum(i * r - 1, 0), col_block))
        nxt = pl.BlockSpec((SUBLANES, width), lambda i, j: (jnp.minimum((i + 1) * r, last), col_block))
    return prev, nxt


def _ffn_kernel(seq_len, h_ref, hp_ref, hn_ref, g_ref, shift_ref, scale_ref, gate_ref,
                wa_ref, wg_ref, cwa_ref, cwg_ref, cba_ref, cbg_ref, wd_ref, o_ref, xn_ref):
    tm = h_ref.shape[0]
    tf = wa_ref.shape[1]
    i = pl.program_id(0)
    j = pl.program_id(1)

    @pl.when(j == 0)
    def _():
        def norm(x):
            y = x * lax.rsqrt(jnp.mean(x * x, axis=-1, keepdims=True) + EPS) * g_ref[...]
            return (y * (1.0 + scale_ref[...]) + shift_ref[...]).astype(BF16)
        keep_prev = (i * tm) % seq_len != 0
        keep_next = ((i + 1) * tm) % seq_len != 0
        xn_ref[0:SUBLANES, :] = jnp.where(keep_prev, norm(hp_ref[...]), jnp.zeros((), BF16))
        xn_ref[SUBLANES:SUBLANES + tm, :] = norm(h_ref[...])
        xn_ref[SUBLANES + tm:, :] = jnp.where(keep_next, norm(hn_ref[...]), jnp.zeros((), BF16))

    def conv(u, cw_ref, cb_ref, cols):
        lo = u[SUBLANES - 1:SUBLANES - 1 + tm]
        mid = u[SUBLANES:SUBLANES + tm]
        hi = u[SUBLANES + 1:SUBLANES + 1 + tm]
        return lo * cw_ref[0:1, cols] + mid * cw_ref[1:2, cols] + hi * cw_ref[2:3, cols] + cb_ref[:, cols]

    xn = xn_ref[...]
    subs = [slice(s, s + FFN_SUB) for s in range(0, tf, FFN_SUB)]
    ups = [(jnp.dot(xn, wa_ref[:, c], preferred_element_type=F32),
            jnp.dot(xn, wg_ref[:, c], preferred_element_type=F32)) for c in subs]
    acts = [(conv(ua, cwa_ref, cba_ref, c) * _silu(conv(ug, cwg_ref, cbg_ref, c))).astype(BF16)
            for c, (ua, ug) in zip(subs, ups)]
    part = jnp.dot(jnp.concatenate(acts, axis=1), wd_ref[...], preferred_element_type=F32)

    @pl.when(j == 0)
    def _():
        o_ref[...] = part

    @pl.when(j > 0)
    def _():
        o_ref[...] += part

    @pl.when(j == pl.num_programs(1) - 1)
    def _():
        o_ref[...] = h_ref[...] + gate_ref[...] * o_ref[...]


def _ffn(l, h, gain, mod, w_up, conv_w, conv_b, w_down, seq_len, is_ctx):
    M, D = h.shape
    FF = w_down.shape[1]
    tm = min(512, seq_len)
    tf = 1024
    nf = FF // tf
    rpm = None if is_ctx else seq_len // tm
    prev, nxt = _halo_specs(tm, D, 0, M, 2)
    return pl.pallas_call(
        functools.partial(_ffn_kernel, seq_len),
        grid=(M // tm, nf),
        in_specs=[pl.BlockSpec((tm, D), lambda i, j: (i, 0)), prev, nxt,
                  pl.BlockSpec((None, 1, D), lambda i, j: (l, 0, 0)),
                  _mod_spec(l, 3, rpm, D, 2), _mod_spec(l, 4, rpm, D, 2), _mod_spec(l, 5, rpm, D, 2),
                  pl.BlockSpec((None, D, tf), lambda i, j: (l, 0, j)),
                  pl.BlockSpec((None, D, tf), lambda i, j: (l, 0, j + nf)),
                  pl.BlockSpec((None, 3, tf), lambda i, j: (l, 0, j)),
                  pl.BlockSpec((None, 3, tf), lambda i, j: (l, 0, j + nf)),
                  pl.BlockSpec((None, 1, tf), lambda i, j: (l, 0, j)),
                  pl.BlockSpec((None, 1, tf), lambda i, j: (l, 0, j + nf)),
                  pl.BlockSpec((None, tf, D), lambda i, j: (l, j, 0))],
        out_specs=pl.BlockSpec((tm, D), lambda i, j: (i, 0)),
        out_shape=jax.ShapeDtypeStruct((M, D), F32),
        scratch_shapes=[pltpu.VMEM((tm + 2 * SUBLANES, D), BF16)],
        compiler_params=_cparams(("parallel", "arbitrary")),
        name="conv_ffn",
    )(h, h, h, gain, mod, mod, mod, w_up, w_up, conv_w, conv_w, conv_b, conv_b, w_down)


def _conv4(xp_ref, x_ref, xn_ref, w_ref, pos, seq_len):
    tm = x_ref.shape[0]
    xe = jnp.concatenate([xp_ref[...], x_ref[...], xn_ref[...]], axis=0)
    t0 = jnp.where(pos >= 1, xe[SUBLANES - 1:SUBLANES - 1 + tm], 0.0)
    t1 = xe[SUBLANES:SUBLANES + tm]
    t2 = jnp.where(pos + 1 < seq_len, xe[SUBLANES + 1:SUBLANES + 1 + tm], 0.0)
    t3 = jnp.where(pos + 2 < seq_len, xe[SUBLANES + 2:SUBLANES + 2 + tm], 0.0)
    return t0 * w_ref[0:1, :] + t1 * w_ref[1:2, :] + t2 * w_ref[2:3, :] + t3 * w_ref[3:4, :]


def _shift_rows(x, k, fill, rev):
    n = x.shape[0]
    if k % SUBLANES == 0:
        pad = jnp.full((k, x.shape[1]), fill, x.dtype)
        return jnp.concatenate([x[k:], pad], 0) if rev else jnp.concatenate([pad, x[:n - k]], 0)
    row = lax.broadcasted_iota(jnp.int32, x.shape, 0)
    if rev:
        return jnp.where(row >= n - k, fill, pltpu.roll(x, n - k, 0))
    return jnp.where(row < k, fill, pltpu.roll(x, k, 0))


def _lru_kernel(rev, finish, seq_len, *refs):
    if finish:
        (x_ref, xp_ref, xn_ref, gate_ref, hf_ref, cw_ref, cb_ref, wcat_ref, ba_ref, bx_ref,
         lam_ref, h0_ref, out_ref, st_ref, carry_ref) = refs
    else:
        (x_ref, xp_ref, xn_ref, cw_ref, cb_ref, wcat_ref, ba_ref, bx_ref,
         lam_ref, h0_ref, out_ref, st_ref, carry_ref) = refs
    tc = x_ref.shape[0]
    j = pl.program_id(1)
    nch = pl.num_programs(1)
    c = (nch - 1 - j) if rev else j

    @pl.when(j == 0)
    def _():
        carry_ref[...] = h0_ref[...]

    pos = c * tc + lax.broadcasted_iota(jnp.int32, (tc, 1), 0)
    u = _conv4(xp_ref, x_ref, xn_ref, cw_ref, pos, seq_len) + cb_ref[...]
    rs, is_ = [], []
    for n in range(HEADS):
        z = _dot(u[:, n * HEAD_DIM:(n + 1) * HEAD_DIM], wcat_ref[n])
        rs.append(z[:, :HEAD_DIM])
        is_.append(z[:, HEAD_DIM:])
    r = _sigmoid(jnp.concatenate(rs, axis=1) + ba_ref[...])
    ig = _sigmoid(jnp.concatenate(is_, axis=1) + bx_ref[...])
    log_a = -LRU_C * r * _softplus(-lam_ref[...])
    a = jnp.exp(log_a)
    b = jnp.sqrt(1.0 - jnp.exp(2.0 * log_a)) * ig * u
    k = 1
    while k < tc:
        b = a * _shift_rows(b, k, 0.0, rev) + b
        a = a * _shift_rows(a, k, 1.0, rev)
        k *= 2
    h = b + a * carry_ref[0:1, :]
    last = h[0:1, :] if rev else h[tc - 1:tc, :]
    carry_ref[...] = jnp.broadcast_to(last, carry_ref.shape)
    st_ref[...] = jnp.broadcast_to(last, st_ref.shape)
    if finish:
        out_ref[...] = ((hf_ref[...] + h) * _gelu_tanh(gate_ref[...])).astype(out_ref.dtype)
    else:
        out_ref[...] = h


def _lru_pass(P, seq_len, rev, h0, wts, hf=None):
    cw, cb, wcat, ba, bx, lam = wts
    M = P.shape[0]
    B = M // seq_len
    tc = 256
    nch = seq_len // tc
    finish = hf is not None
    W = GROUP_W
    xb = P_LRU_X // W
    gb = P_LRU_GATE // W
    r = tc // SUBLANES
    last = M // SUBLANES - 1

    def chunk(j):
        return (nch - 1 - j) if rev else j

    row = lambda b, j: (b * nch + chunk(j), xb)
    specs = [pl.BlockSpec((tc, W), row),
             pl.BlockSpec((SUBLANES, W), lambda b, j: (jnp.maximum((b * nch + chunk(j)) * r - 1, 0), xb)),
             pl.BlockSpec((SUBLANES, W), lambda b, j: (jnp.minimum((b * nch + chunk(j) + 1) * r, last), xb))]
    args = [P, P, P]
    if finish:
        specs += [pl.BlockSpec((tc, W), lambda b, j: (b * nch + chunk(j), gb)),
                  pl.BlockSpec((tc, W), lambda b, j: (b * nch + chunk(j), 0))]
        args += [P, hf]
    full2 = lambda b, j: (0, 0)
    specs += [pl.BlockSpec((4, W), full2), pl.BlockSpec((1, W), full2),
              pl.BlockSpec((HEADS, HEAD_DIM, 2 * HEAD_DIM), lambda b, j: (0, 0, 0)),
              pl.BlockSpec((1, W), full2), pl.BlockSpec((1, W), full2), pl.BlockSpec((1, W), full2),
              pl.BlockSpec((None, SUBLANES, W), lambda b, j: (b, 0, 0))]
    args += [cw, cb, wcat, ba, bx, lam, h0]
    out_dtype = BF16 if finish else F32
    return pl.pallas_call(
        functools.partial(_lru_kernel, rev, finish, seq_len),
        grid=(B, nch),
        in_specs=specs,
        out_specs=[pl.BlockSpec((tc, W), lambda b, j: (b * nch + chunk(j), 0)),
                   pl.BlockSpec((None, SUBLANES, W), lambda b, j: (b, 0, 0))],
        out_shape=[jax.ShapeDtypeStruct((M, W), out_dtype),
                   jax.ShapeDtypeStruct((B, SUBLANES, W), F32)],
        scratch_shapes=[pltpu.VMEM((SUBLANES, W), F32)],
        compiler_params=_cparams(("parallel", "arbitrary")),
        name="rglru_bwd" if rev else "rglru_fwd",
    )(*args)


def _mixer_lru(Pc, Pl, ctx_len, seq_len, wts_f, wts_b):
    B = Pc.shape[0] // ctx_len
    zero = jnp.zeros((B, SUBLANES, GROUP_W), F32)
    hc_f, sc_f = _lru_pass(Pc, ctx_len, False, zero, wts_f)
    hl_f, _ = _lru_pass(Pl, seq_len, False, sc_f, wts_f)
    yc, sc_b = _lru_pass(Pc, ctx_len, True, zero, wts_b, hf=hc_f)
    yl, _ = _lru_pass(Pl, seq_len, True, sc_b, wts_b, hf=hl_f)
    return yc, yl


def _rope_mix(x, cos, sin):
    lane = lax.broadcasted_iota(jnp.int32, x.shape, 1)
    half = MLA_ROPE // 2
    swapped = jnp.where(lane < half, pltpu.roll(x, LANES - half, 1), pltpu.roll(x, half, 1))
    return x * cos + swapped * sin


def _mla_prep_kernel(use_rope, x_ref, qag_ref, wuq_ref, kvg_ref, wukv_ref, qg_ref, kg_ref,
                     cos_ref, sin_ref, qt_out, k_out, vt_out):
    x = x_ref[...]

    def rms(v, g):
        return v * lax.rsqrt(jnp.mean(v * v, axis=-1, keepdims=True) + EPS) * g

    q = _dot(rms(x[:, :MLA_Q_LORA], qag_ref[...]), wuq_ref[...])
    kv = _dot(rms(x[:, MLA_Q_LORA:MLA_Q_LORA + MLA_KV_LORA], kvg_ref[...]), wukv_ref[...])
    tail = x[:, MLA_Q_LORA + MLA_KV_LORA:]
    lane = lax.broadcasted_iota(jnp.int32, tail.shape, 1)
    kr = jnp.where(lane < MLA_ROPE, tail, 0.0)
    kr_ss = jnp.sum(kr * kr, axis=-1, keepdims=True)
    scale = MLA_QK ** -0.5 * LOG2E
    for h in range(HEADS):
        qh = q[:, h * MLA_QPAD:(h + 1) * MLA_QPAD]
        inv = lax.rsqrt(jnp.sum(qh * qh, axis=-1, keepdims=True) / MLA_QK + EPS) * scale
        qh = qh * inv * qg_ref[...]
        qn, qr = qh[:, :LANES], qh[:, LANES:]
        kn = kv[:, h * 2 * HEAD_DIM:h * 2 * HEAD_DIM + MLA_NOPE]
        vh = kv[:, h * 2 * HEAD_DIM + MLA_NOPE:(h + 1) * 2 * HEAD_DIM]
        kinv = lax.rsqrt((jnp.sum(kn * kn, axis=-1, keepdims=True) + kr_ss) / MLA_QK + EPS)
        kn = kn * kinv * kg_ref[:, :LANES]
        krh = kr * kinv * kg_ref[:, LANES:]
        if use_rope:
            qr = _rope_mix(qr, cos_ref[...], sin_ref[...])
            krh = _rope_mix(krh, cos_ref[...], sin_ref[...])
        qt_out[h] = jnp.concatenate([qn, qr], axis=1).T.astype(BF16)
        k_out[h, :, :LANES] = kn.astype(BF16)
        k_out[h, :, LANES:] = krh.astype(BF16)
        vt_out[h] = vh.T.astype(BF16)


def _mla_prep(P, seq_len, use_rope, wts, cosf, sinf):
    qag, wuq, kvg, wukv, qg, kg = wts
    M = P.shape[0]
    tm = 256
    nt = seq_len // tm
    full = lambda i: (0, 0)
    wmla = P_COLS - P_MLA
    tab = pl.BlockSpec((tm, LANES), (lambda i: (i % nt, 0)) if use_rope else (lambda i: (0, 0)))
    tspec = lambda w: pl.BlockSpec((HEADS, w, tm), lambda i: (0, 0, i))
    return pl.pallas_call(
        functools.partial(_mla_prep_kernel, use_rope),
        grid=(M // tm,),
        in_specs=[pl.BlockSpec((tm, wmla), lambda i: (i, P_MLA // wmla)),
                  pl.BlockSpec((1, MLA_Q_LORA), full), pl.BlockSpec(wuq.shape, full),
                  pl.BlockSpec((1, MLA_KV_LORA), full), pl.BlockSpec(wukv.shape, full),
                  pl.BlockSpec((1, MLA_QPAD), full), pl.BlockSpec((1, MLA_QPAD), full),
                  tab, tab],
        out_specs=[tspec(MLA_QPAD), pl.BlockSpec((HEADS, tm, MLA_QPAD), lambda i: (0, i, 0)),
                   tspec(HEAD_DIM)],
        out_shape=[jax.ShapeDtypeStruct((HEADS, MLA_QPAD, M), BF16),
                   jax.ShapeDtypeStruct((HEADS, M, MLA_QPAD), BF16),
                   jax.ShapeDtypeStruct((HEADS, HEAD_DIM, M), BF16)],
        compiler_params=_cparams(("parallel",)),
        name="mla_prep",
    )(P, qag, wuq, kvg, wukv, qg, kg, cosf, sinf)


def _flash_kernel(tk, has_lat, *refs):
    if has_lat:
        qt_ref, kc_ref, vct_ref, kl_ref, vlt_ref, o_ref, m_ref, l_ref, acc_ref, s_ref = refs
    else:
        qt_ref, kc_ref, vct_ref, o_ref = refs
    qt = qt_ref[...]
    s = jnp.dot(kc_ref[...], qt, preferred_element_type=F32)
    if has_lat:
        nk = kl_ref.shape[0] // tk

        def scores(c, slot):
            off = pl.multiple_of(jnp.minimum(c, nk - 1) * tk, tk)
            s_ref[slot] = jnp.dot(kl_ref[pl.ds(off, tk), :], qt, preferred_element_type=F32)

        scores(0, 0)
    m = jnp.max(s, axis=0, keepdims=True)
    p = jnp.exp2(s - m)
    l = jnp.sum(p, axis=0, keepdims=True)
    acc = jnp.dot(vct_ref[...], p.astype(BF16), preferred_element_type=F32)
    if has_lat:
        m_ref[...] = m
        l_ref[...] = l
        acc_ref[...] = acc

        def step(c, slot):
            scores(c + 1, 1 - slot)
            off = pl.multiple_of(c * tk, tk)
            s = s_ref[slot]
            m_old = m_ref[...]
            m_new = jnp.maximum(m_old, jnp.max(s, axis=0, keepdims=True))
            alpha = jnp.exp2(m_old - m_new)
            p = jnp.exp2(s - m_new)
            l_ref[...] = alpha * l_ref[...] + jnp.sum(p, axis=0, keepdims=True)
            acc_ref[...] = alpha * acc_ref[...] + jnp.dot(
                vlt_ref[:, pl.ds(off, tk)], p.astype(BF16), preferred_element_type=F32)
            m_ref[...] = m_new

        unroll = min(FLASH_UNROLL, nk)

        def body(j, carry):
            for u in range(unroll):
                step(unroll * j + u, u % 2)
            return carry

        lax.fori_loop(0, nk // unroll, body, 0)
        acc = acc_ref[...]
        l = l_ref[...]
    o_ref[...] = (acc / l).T.astype(o_ref.dtype)


def _flash_attention(Qt, Kc, Vct, seq_len, ctx_len, Kl=None, Vlt=None):
    H, dq, M = Qt.shape
    B = M // seq_len
    has_lat = Kl is not None
    tq = min(1024, seq_len)
    tk = 512
    nq = seq_len // tq
    specs = [pl.BlockSpec((None, dq, tq), lambda b, h, i: (h, 0, b * nq + i)),
             pl.BlockSpec((None, ctx_len, dq), lambda b, h, i: (h, b, 0)),
             pl.BlockSpec((None, HEAD_DIM, ctx_len), lambda b, h, i: (h, 0, b))]
    args = [Qt, Kc, Vct]
    scratch = []
    if has_lat:
        specs += [pl.BlockSpec((None, seq_len, dq), lambda b, h, i: (h, b, 0)),
                  pl.BlockSpec((None, HEAD_DIM, seq_len), lambda b, h, i: (h, 0, b))]
        args += [Kl, Vlt]
        nk = seq_len // tk
        assert nk % 2 == 0 and nk % min(FLASH_UNROLL, nk) == 0
        scratch = [pltpu.VMEM((1, tq), F32), pltpu.VMEM((1, tq), F32), pltpu.VMEM((HEAD_DIM, tq), F32),
                   pltpu.VMEM((2, tk, tq), F32)]
    return pl.pallas_call(
        functools.partial(_flash_kernel, tk, has_lat),
        grid=(B, H, nq),
        in_specs=specs,
        out_specs=pl.BlockSpec((tq, HEAD_DIM), lambda b, h, i: (b * nq + i, h)),
        out_shape=jax.ShapeDtypeStruct((M, GROUP_W), BF16),
        scratch_shapes=scratch,
        compiler_params=_cparams(("parallel", "parallel", "arbitrary")),
        name="mla_attention" if has_lat else "ctx_attention",
    )(*args)


def _na_prep_kernel(x_ref, qg_ref, kg_ref, qt_out, k_out, vt_out):
    x = x_ref[...]
    scale = HEAD_DIM ** -0.5 * LOG2E
    for h in range(HEADS):
        qh = x[:, h * HEAD_DIM:(h + 1) * HEAD_DIM]
        kh = x[:, GROUP_W + h * HEAD_DIM:GROUP_W + (h + 1) * HEAD_DIM]
        vh = x[:, 2 * GROUP_W + h * HEAD_DIM:2 * GROUP_W + (h + 1) * HEAD_DIM]
        qh = qh * (lax.rsqrt(jnp.mean(qh * qh, axis=-1, keepdims=True) + EPS) * scale) * qg_ref[...]
        kh = kh * lax.rsqrt(jnp.mean(kh * kh, axis=-1, keepdims=True) + EPS) * kg_ref[...]
        qt_out[h] = qh.T.astype(BF16)
        k_out[h] = kh.astype(BF16)
        vt_out[h] = vh.T.astype(BF16)


def _na_prep(P, qg, kg):
    M = P.shape[0]
    tm = 256
    w = 3 * GROUP_W
    hspec = pl.BlockSpec((HEADS, tm, HEAD_DIM), lambda i: (0, i, 0))
    tspec = pl.BlockSpec((HEADS, HEAD_DIM, tm), lambda i: (0, 0, i))
    sds = jax.ShapeDtypeStruct((HEADS, M, HEAD_DIM), BF16)
    tds = jax.ShapeDtypeStruct((HEADS, HEAD_DIM, M), BF16)
    return pl.pallas_call(
        _na_prep_kernel,
        grid=(M // tm,),
        in_specs=[pl.BlockSpec((tm, w), lambda i: (i, P_NA_QKV // w)),
                  pl.BlockSpec((1, HEAD_DIM), lambda i: (0, 0)),
                  pl.BlockSpec((1, HEAD_DIM), lambda i: (0, 0))],
        out_specs=[tspec, hspec, tspec],
        out_shape=[tds, sds, tds],
        compiler_params=_cparams(("parallel",)),
        name="na_prep",
    )(P, qg, kg)


def _na_bias_kernel(rows, rpb_ref, o_ref):
    h = pl.program_id(0)
    kc = lax.broadcasted_iota(jnp.int32, (GRID_W, GRID_W), 0)
    qc = lax.broadcasted_iota(jnp.int32, (GRID_W, GRID_W), 1)
    dc = jnp.clip(kc - qc + (NA_KW - 1), 0, 2 * NA_KW - 2)
    c0 = jnp.clip(qc - NA_KW // 2, 0, GRID_W - NA_KW)
    col_ok = (kc >= c0) & (kc < c0 + NA_KW)
    neg = jnp.full((GRID_W, GRID_W), NEG_BIG, F32)
    tiles = []
    for dr in range(2 * NA_KH - 1):
        t = jnp.zeros((GRID_W, GRID_W), F32)
        for d in range(2 * NA_KW - 1):
            t = t + jnp.where(dc == d, rpb_ref[h, dr, d], 0.0)
        tiles.append(jnp.where(col_ok, t * LOG2E, NEG_BIG))
    for kind in range(3):
        q_base = (0, NA_KH // 2, rows - NA_BAND)[kind]
        w_base = (0, 0, rows - NA_WIN)[kind]
        for qr in range(NA_BAND):
            r = q_base + qr
            if kind == 1:
                r0 = r - NA_KH // 2
            else:
                r0 = min(max(r - NA_KH // 2, 0), rows - NA_KH)
            for kr in range(NA_WIN):
                ka = w_base + kr
                ok = r0 <= ka < r0 + NA_KH
                blk = tiles[ka - r + NA_KH - 1] if ok else neg
                o_ref[kind, kr * GRID_W:(kr + 1) * GRID_W, qr * GRID_W:(qr + 1) * GRID_W] = blk


def _na_bias(rpb, rows):
    H = rpb.shape[0]
    nq, nk = NA_BAND * GRID_W, NA_WIN * GRID_W
    return pl.pallas_call(
        functools.partial(_na_bias_kernel, rows),
        grid=(H,),
        in_specs=[pl.BlockSpec(memory_space=pltpu.SMEM)],
        out_specs=pl.BlockSpec((None, 3, nk, nq), lambda h: (h, 0, 0, 0)),
        out_shape=jax.ShapeDtypeStruct((H, 3, nk, nq), F32),
        compiler_params=_cparams(("parallel",)),
        name="na_bias",
    )(rpb)


def _na_kernel(seq_len, qt_ref, k_ref, vt_ref, kc_ref, vct_ref, bias_ref, o_ref):
    nq = qt_ref.shape[1]
    nk = bias_ref.shape[0]
    j = pl.program_id(2)
    base = jnp.clip(j * nq - (NA_KH // 2) * GRID_W, 0, seq_len - nk)
    base = pl.multiple_of(base, (NA_KH // 2) * GRID_W)
    qt = qt_ref[...]
    s_ctx = jnp.dot(kc_ref[...], qt, preferred_element_type=F32)
    s_loc = jnp.dot(k_ref[pl.ds(base, nk), :], qt, preferred_element_type=F32) + bias_ref[...]
    m = jnp.maximum(jnp.max(s_loc, axis=0, keepdims=True), jnp.max(s_ctx, axis=0, keepdims=True))
    p_loc = jnp.exp2(s_loc - m)
    p_ctx = jnp.exp2(s_ctx - m)
    l = jnp.sum(p_loc, axis=0, keepdims=True) + jnp.sum(p_ctx, axis=0, keepdims=True)
    o = (jnp.dot(vt_ref[:, pl.ds(base, nk)], p_loc.astype(BF16), preferred_element_type=F32)
         + jnp.dot(vct_ref[...], p_ctx.astype(BF16), preferred_element_type=F32))
    o_ref[...] = (o / l).T.astype(o_ref.dtype)


def _na_attention(Qtl, Kl, Vtl, Kc, Vtc, bias, seq_len, ctx_len):
    H, d, M = Qtl.shape
    B = M // seq_len
    nq, nk = NA_BAND * GRID_W, NA_WIN * GRID_W
    nb = seq_len // nq

    def kind(j):
        return jnp.where(j == 0, 0, jnp.where(j == nb - 1, 2, 1))

    return pl.pallas_call(
        functools.partial(_na_kernel, seq_len),
        grid=(B, H, nb),
        in_specs=[pl.BlockSpec((None, d, nq), lambda b, h, j: (h, 0, b * nb + j)),
                  pl.BlockSpec((None, seq_len, d), lambda b, h, j: (h, b, 0)),
                  pl.BlockSpec((None, d, seq_len), lambda b, h, j: (h, 0, b)),
                  pl.BlockSpec((None, ctx_len, d), lambda b, h, j: (h, b, 0)),
                  pl.BlockSpec((None, d, ctx_len), lambda b, h, j: (h, 0, b)),
                  pl.BlockSpec((None, None, nk, nq), lambda b, h, j: (h, kind(j), 0, 0))],
        out_specs=pl.BlockSpec((nq, d), lambda b, h, j: (b * nb + j, h)),
        out_shape=jax.ShapeDtypeStruct((M, GROUP_W), BF16),
        compiler_params=_cparams(("parallel", "parallel", "arbitrary")),
        name="na_attention",
    )(Qtl, Kl, Vtl, Kc, Vtc, bias)


def _gdn_prep_kernel(seq_len, x_ref, xp_ref, xn_ref, tail_ref, cw_ref, alog_ref, dtb_ref,
                     qkv_out, gates_out):
    tm = x_ref.shape[0]
    pos = (pl.program_id(0) * tm + lax.broadcasted_iota(jnp.int32, (tm, 1), 0)) % seq_len
    y = _silu(_conv4(xp_ref, x_ref, xn_ref, cw_ref, pos, seq_len))
    for h in range(HEADS):
        sl = slice(h * HEAD_DIM, (h + 1) * HEAD_DIM)
        qh = y[:, sl]
        qkv_out[:, sl] = qh * (lax.rsqrt(jnp.sum(qh * qh, axis=-1, keepdims=True) + EPS) * HEAD_DIM ** -0.5)
        sl = slice(GROUP_W + h * HEAD_DIM, GROUP_W + (h + 1) * HEAD_DIM)
        kh = y[:, sl]
        qkv_out[:, sl] = kh * lax.rsqrt(jnp.sum(kh * kh, axis=-1, keepdims=True) + EPS)
    qkv_out[:, 2 * GROUP_W:] = y[:, 2 * GROUP_W:]
    t = tail_ref[...]
    lane = lax.broadcasted_iota(jnp.int32, t.shape, 1)
    alpha = pltpu.roll(t, LANES - TAIL_ALPHA, 1)
    beta = pltpu.roll(t, LANES - TAIL_BETA + 2 * HEADS, 1)
    g = -jnp.exp(alog_ref[...]) * _softplus(alpha + dtb_ref[...])
    gates_out[...] = jnp.where(lane < 2 * HEADS, g, jnp.where(lane < 4 * HEADS, _sigmoid(beta), 0.0))


def _gdn_prep(P, seq_len, cw, alog, dtb):
    M = P.shape[0]
    tm = 256
    w = 3 * GROUP_W
    prev, nxt = _halo_specs(tm, w, 0, M, 1)
    full = lambda i: (0, 0)
    return pl.pallas_call(
        functools.partial(_gdn_prep_kernel, seq_len),
        grid=(M // tm,),
        in_specs=[pl.BlockSpec((tm, w), lambda i: (i, 0)), prev, nxt,
                  pl.BlockSpec((tm, LANES), lambda i: (i, P_TAIL // LANES)),
                  pl.BlockSpec((4, w), full), pl.BlockSpec((1, LANES), full), pl.BlockSpec((1, LANES), full)],
        out_specs=[pl.BlockSpec((tm, w), lambda i: (i, 0)), pl.BlockSpec((tm, LANES), lambda i: (i, 0))],
        out_shape=[jax.ShapeDtypeStruct((M, w), F32), jax.ShapeDtypeStruct((M, LANES), F32)],
        compiler_params=_cparams(("parallel",)),
        name="gdn_prep",
    )(P, P, P, P, cw, alog, dtb)


def _time_cumsum(x, rev):
    k = 1
    while k < x.shape[0]:
        x = x + _shift_rows(x, k, 0.0, rev)
        k *= 2
    return x


def _gdn_local_kernel(qkv_ref, gates_ref, a_out, qk_out):
    C = GDN_CHUNK
    ri = lax.broadcasted_iota(jnp.int32, (C, C), 0)
    ci = lax.broadcasted_iota(jnp.int32, (C, C), 1)
    for c in range(qkv_ref.shape[0] // C):
        rows = slice(c * C, (c + 1) * C)
        gt = gates_ref[rows, :]
        zs = []
        for h in range(HEADS):
            q = qkv_ref[rows, h * HEAD_DIM:(h + 1) * HEAD_DIM]
            kk = qkv_ref[rows, GROUP_W + h * HEAD_DIM:GROUP_W + (h + 1) * HEAD_DIM]
            zs.append(_dot_nt(jnp.concatenate([kk, q], axis=0), kk))
        for d in range(2):
            rev = d == 1
            incl = (ci >= ri) if rev else (ci <= ri)
            strict = (ci > ri) if rev else (ci < ri)
            gcum = _time_cumsum(gt, rev)
            grow = gcum.T
            for h in range(HEADS):
                lg = d * HEADS + h
                gc_c = gcum[:, lg:lg + 1]
                gc_r = grow[lg:lg + 1, :]
                beta = gt[:, 2 * HEADS + lg:2 * HEADS + lg + 1]
                decay = jnp.where(incl, jnp.exp(jnp.where(incl, gc_c - gc_r, 0.0)), 0.0)
                lanes = slice((h % 2) * C, (h % 2 + 1) * C)
                a_out[d, c * 2 + h // 2, :, lanes] = jnp.where(strict, zs[h][:C] * beta * decay, 0.0)
                qk_out[d, c * 2 + h // 2, :, lanes] = zs[h][C:] * decay


def _gdn_local(qkv, gates):
    M = qkv.shape[0]
    tm = 256
    npair = tm // GDN_CHUNK * (HEADS // 2)
    w = 3 * GROUP_W
    ospec = pl.BlockSpec((2, npair, GDN_CHUNK, LANES), lambda i: (0, i, 0, 0))
    sds = jax.ShapeDtypeStruct((2, M // GDN_CHUNK * (HEADS // 2), GDN_CHUNK, LANES), F32)
    return pl.pallas_call(
        _gdn_local_kernel,
        grid=(M // tm,),
        in_specs=[pl.BlockSpec((tm, w), lambda i: (i, 0)), pl.BlockSpec((tm, LANES), lambda i: (i, 0))],
        out_specs=[ospec, ospec],
        out_shape=[sds, sds],
        compiler_params=_cparams(("parallel",)),
        name="gdn_local",
    )(qkv, gates)


SOLVE_BATCH = 128


def _gdn_solve_kernel(upper, a_ref, o_ref, at_ref, m_ref):
    C = GDN_CHUNK
    ns = C // SUBLANES
    for i in range(C):
        xt = a_ref[pl.ds(i, SOLVE_BATCH, stride=C), :].T
        at_ref[i, 0] = xt[:C]
        at_ref[i, 1] = xt[C:]
    sub = lax.broadcasted_iota(jnp.int32, (SUBLANES, SOLVE_BATCH), 0)
    for t in range(C):
        i = C - 1 - t if upper else t
        si = i // SUBLANES
        slabs = list(range(si, ns)) if upper else list(range(si + 1))
        unit = jnp.where(sub == i % SUBLANES, 1.0, 0.0).astype(F32)
        zero = jnp.zeros((SUBLANES, SOLVE_BATCH), F32)
        init = tuple(unit if s == si else zero for _ in range(2) for s in slabs)

        def col_body(j, acc):
            new = []
            for hp in range(2):
                a = jnp.broadcast_to(at_ref[i, hp, pl.ds(j, 1), :], (SUBLANES, SOLVE_BATCH))
                for n, s in enumerate(slabs):
                    new.append(acc[hp * len(slabs) + n] - a * m_ref[hp, j, s * SUBLANES:(s + 1) * SUBLANES, :])
            return tuple(new)

        lo, hi = (i + 1, C) if upper else (0, i)
        if hi > lo:
            acc = lax.fori_loop(lo, hi, col_body, init, unroll=min(4, hi - lo))
        else:
            acc = init
        for hp in range(2):
            for s in range(ns):
                val = acc[hp * len(slabs) + slabs.index(s)] if s in slabs else zero
                m_ref[hp, i, s * SUBLANES:(s + 1) * SUBLANES, :] = val
    for i in range(C):
        y = jnp.concatenate([m_ref[0, i], m_ref[1, i]], axis=0)
        o_ref[pl.ds(i, SOLVE_BATCH, stride=C), :] = y.T


def _gdn_solve(a_all):
    _, NP, C, _ = a_all.shape
    rows = SOLVE_BATCH * C
    a2 = a_all.reshape(2, NP * C, LANES)
    outs = []
    for d in range(2):
        spec = pl.BlockSpec((None, rows, LANES), lambda b, d=d: (d, b, 0))
        outs.append(pl.pallas_call(
            functools.partial(_gdn_solve_kernel, d == 1),
            grid=(NP // SOLVE_BATCH,),
            in_specs=[spec],
            out_specs=pl.BlockSpec((rows, LANES), lambda b: (b, 0)),
            out_shape=jax.ShapeDtypeStruct((NP * C, LANES), F32),
            scratch_shapes=[pltpu.VMEM((C, 2, C, SOLVE_BATCH), F32), pltpu.VMEM((2, C, C, SOLVE_BATCH), F32)],
            compiler_params=_cparams(("parallel",)),
            name="gdn_solve_upper" if d else "gdn_solve_lower",
        )(a2).reshape(NP, C, LANES))
    return outs


def _gdn_kernel(rev, finish, d, *refs):
    if finish:
        (qkv_ref, gates_ref, qk_ref, s0_ref, of_ref, gate_ref, ng_ref, *tinv_refs) = refs[:-4]
        out_ref, sout_ref, s_ref, o_ref = refs[-4:]
    else:
        (qkv_ref, gates_ref, qk_ref, s0_ref, *tinv_refs) = refs[:-3]
        out_ref, sout_ref, s_ref = refs[-3:]
        o_ref = out_ref
    C = GDN_CHUNK
    B = qkv_ref.shape[0]
    nchunk = qkv_ref.shape[1] // C
    j = pl.program_id(0)

    @pl.when(j == 0)
    def _():
        s_ref[...] = s0_ref[...]

    last_row = 0 if rev else C - 1
    order = [(nchunk - 1 - t) if rev else t for t in range(nchunk)]
    units = [(b, h) for b in range(B) for h in range(HEADS)]

    def local(c):
        rows = slice(c * C, (c + 1) * C)
        out = {}
        for b in range(B):
            gt = gates_ref[b, rows, :]
            gcum = _time_cumsum(gt, rev)
            for h in range(HEADS):
                lg = d * HEADS + h
                lanes = slice((h % 2) * C, (h % 2 + 1) * C)
                gc_c = gcum[:, lg:lg + 1]
                beta = gt[:, 2 * HEADS + lg:2 * HEADS + lg + 1]
                gl = gcum[last_row:last_row + 1, lg:lg + 1]
                q = qkv_ref[b, rows, h * HEAD_DIM:(h + 1) * HEAD_DIM]
                kk = qkv_ref[b, rows, GROUP_W + h * HEAD_DIM:GROUP_W + (h + 1) * HEAD_DIM]
                v = qkv_ref[b, rows, 2 * GROUP_W + h * HEAD_DIM:2 * GROUP_W + (h + 1) * HEAD_DIM]
                tinv = tinv_refs[b][c * 2 + h // 2, :, lanes]
                qk = qk_ref[b, c * 2 + h // 2, :, lanes]
                kb = kk * beta
                eg = jnp.exp(gc_c)
                uw = _dot(tinv, jnp.concatenate([v * beta, kb * eg], axis=1))
                wq = jnp.concatenate([uw[:, HEAD_DIM:], q * eg], axis=0).astype(BF16)
                kdt = (kk * jnp.exp(gl - gc_c)).T
                qkk = jnp.concatenate([qk, kdt], axis=0).astype(BF16)
                out[b, h] = (uw[:, :HEAD_DIM], wq, qkk, jnp.exp(gl))
        return out

    states = {(b, h): s_ref[b, h] for b, h in units}
    nxt = local(order[0])
    for t, c in enumerate(order):
        cur = nxt
        ws = {k: jnp.dot(cur[k][1], states[k].astype(BF16), preferred_element_type=F32) for k in units}
        if t + 1 < nchunk:
            nxt = local(order[t + 1])
        for b, h in units:
            u, _, qkk, decay = cur[b, h]
            v_new = u - ws[b, h][:C]
            res = jnp.dot(qkk, v_new.astype(BF16), preferred_element_type=F32)
            o_ref[b, c * C:(c + 1) * C, h * HEAD_DIM:(h + 1) * HEAD_DIM] = ws[b, h][C:] + res[:C]
            states[b, h] = states[b, h] * decay + res[C:]
    for b, h in units:
        s_ref[b, h] = states[b, h]
    sout_ref[...] = s_ref[...]
    if finish:
        for b in range(B):
            tot = of_ref[b] + o_ref[b]
            gate = gate_ref[b]
            for h in range(HEADS):
                sl = slice(h * HEAD_DIM, (h + 1) * HEAD_DIM)
                x = tot[:, sl]
                y = x * lax.rsqrt(jnp.mean(x * x, axis=-1, keepdims=True) + EPS) * ng_ref[...]
                out_ref[b, :, sl] = (y * _silu(gate[:, sl])).astype(out_ref.dtype)


def _gdn_pass(qkv, gates, tinv_all, tinv_off, qk, seq_len, rev, d, s0, of=None, P=None, ng=None):
    M = qkv.shape[0]
    B = M // seq_len
    rb = 256
    nblk = seq_len // rb
    finish = of is not None
    w = 3 * GROUP_W
    npair = rb // GDN_CHUNK * (HEADS // 2)
    toff = tinv_off // npair

    def blk(j):
        return (nblk - 1 - j) if rev else j

    sspec = pl.BlockSpec((B, HEADS, HEAD_DIM, HEAD_DIM), lambda j: (0, 0, 0, 0))
    specs = [pl.BlockSpec((B, rb, w), lambda j: (0, blk(j), 0)),
             pl.BlockSpec((B, rb, LANES), lambda j: (0, blk(j), 0)),
             pl.BlockSpec((None, B, npair, GDN_CHUNK, LANES), lambda j: (d, 0, blk(j), 0, 0)),
             sspec]
    args = [qkv.reshape(B, seq_len, w), gates.reshape(B, seq_len, LANES),
            qk.reshape(2, B, nblk * npair, GDN_CHUNK, LANES), s0]
    scratch = [pltpu.VMEM((B, HEADS, HEAD_DIM, HEAD_DIM), F32)]
    if finish:
        specs += [pl.BlockSpec((B, rb, GROUP_W), lambda j: (0, blk(j), 0)),
                  pl.BlockSpec((B, rb, GROUP_W), lambda j: (0, blk(j), P_GDN_GATE // GROUP_W)),
                  pl.BlockSpec((1, HEAD_DIM), lambda j: (0, 0))]
        args += [of.reshape(B, seq_len, GROUP_W), P.reshape(B, seq_len, P.shape[1]), ng]
        scratch.append(pltpu.VMEM((B, rb, GROUP_W), F32))
    for b in range(B):
        specs.append(pl.BlockSpec((npair, GDN_CHUNK, LANES), lambda j, b=b: (toff + b * nblk + blk(j), 0, 0)))
        args.append(tinv_all)
    out, s_out = pl.pallas_call(
        functools.partial(_gdn_kernel, rev, finish, d),
        grid=(nblk,),
        in_specs=specs,
        out_specs=[pl.BlockSpec((B, rb, GROUP_W), lambda j: (0, blk(j), 0)), sspec],
        out_shape=[jax.ShapeDtypeStruct((B, seq_len, GROUP_W), BF16 if finish else F32),
                   jax.ShapeDtypeStruct((B, HEADS, HEAD_DIM, HEAD_DIM), F32)],
        scratch_shapes=scratch,
        compiler_params=_cparams(("arbitrary",)),
        name="gdn_bwd" if rev else "gdn_fwd",
    )(*args)
    return out.reshape(M, GROUP_W), s_out


def _mixer_gdn(Pc, Pl, ctx_len, seq_len, cw, alog, dtb, ng):
    B = Pc.shape[0] // ctx_len
    qkv_c, g_c = _gdn_prep(Pc, ctx_len, cw, alog, dtb)
    qkv_l, g_l = _gdn_prep(Pl, seq_len, cw, alog, dtb)
    a_c, qk_c = _gdn_local(qkv_c, g_c)
    a_l, qk_l = _gdn_local(qkv_l, g_l)
    n_c, n_l = a_c.shape[1], a_l.shape[1]
    n_pad = -(n_c + n_l) % SOLVE_BATCH
    pad = jnp.zeros((2, n_pad, GDN_CHUNK, LANES), F32)
    tinv = _gdn_solve(jnp.concatenate([a_c, a_l, pad], axis=1))
    zero = jnp.zeros((B, HEADS, HEAD_DIM, HEAD_DIM), F32)
    oc_f, sc_f = _gdn_pass(qkv_c, g_c, tinv[0], 0, qk_c, ctx_len, False, 0, zero)
    ol_f, _ = _gdn_pass(qkv_l, g_l, tinv[0], n_c, qk_l, seq_len, False, 0, sc_f)
    yc, sc_b = _gdn_pass(qkv_c, g_c, tinv[1], 0, qk_c, ctx_len, True, 1, zero, of=oc_f, P=Pc, ng=ng)
    yl, _ = _gdn_pass(qkv_l, g_l, tinv[1], n_c, qk_l, seq_len, True, 1, sc_b, of=ol_f, P=Pl, ng=ng)
    return yc, yl


def _pack_w_in(w_in):
    off = {}
    o = 0
    for name, n in (('lru_x', 512), ('lru_gate', 512), ('mla_cq', 384), ('mla_ckv', 256), ('mla_kr', 64),
                    ('gdn_q', 512), ('gdn_k', 512), ('gdn_v', 512), ('gdn_gate', 512),
                    ('gdn_beta', 8), ('gdn_alpha', 8), ('na_q', 512), ('na_k', 512), ('na_v', 512)):
        off[name] = (o, n)
        o += n
    order = ['gdn_q', 'gdn_k', 'gdn_v', 'gdn_gate', 'lru_x', 'lru_gate', 'na_q', 'na_k', 'na_v',
             'mla_cq', 'mla_ckv', 'mla_kr', 'gdn_beta', 'gdn_alpha']
    parts = [w_in[..., off[n][0]:off[n][0] + off[n][1]] for n in order]
    used = sum(off[n][1] for n in order)
    parts.append(jnp.zeros(w_in.shape[:-1] + (P_COLS - used,), w_in.dtype))
    return jnp.concatenate(parts, axis=-1).astype(BF16)


def _pack_w_uq(w_uq):
    L, K, _ = w_uq.shape
    w = w_uq.reshape(L, K, HEADS, MLA_QK)
    w = jnp.pad(w, ((0, 0), (0, 0), (0, 0), (0, MLA_QPAD - MLA_QK)))
    return w.reshape(L, K, HEADS * MLA_QPAD).astype(BF16)


def _pad_gain(g):
    return jnp.pad(g, ((0, 0), (0, MLA_QPAD - MLA_QK)))[:, None, :]


def _rope_tables(T):
    t = jnp.arange(T)
    rowp = (t // GRID_W).astype(F32)
    colp = (t % GRID_W).astype(F32)
    n_freq = MLA_ROPE // 4
    inv = ROPE_BASE ** (-jnp.arange(n_freq, dtype=F32) / n_freq)
    ang = jnp.concatenate([rowp[:, None] * inv, colp[:, None] * inv], axis=-1)
    cos, sin = jnp.cos(ang), jnp.sin(ang)
    z = jnp.zeros((T, LANES - MLA_ROPE), F32)
    return (jnp.concatenate([cos, cos, z], axis=-1), jnp.concatenate([-sin, sin, z], axis=-1))


def _lane_vec(x):
    L = x.shape[0]
    return jnp.pad(x.reshape(L, 2 * HEADS), ((0, 0), (0, LANES - 2 * HEADS)))[:, None, :]


def kernel(x, c, ctx, c_ctx, ada_w, ada_b, norm_mix_g, norm_ffn_g, w_in, w_out, lru_conv_w, lru_conv_b, lru_wa, lru_ba, lru_wx, lru_bx, lru_lam, mla_qa_g, mla_w_uq, mla_kva_g, mla_w_ukv, mla_qn_g, mla_kn_g, gdn_conv_w, gdn_a_log, gdn_dt_bias, gdn_norm_g, na_qn_g, na_kn_g, na_rpb, ffn_w_up, ffn_conv_w, ffn_conv_b, ffn_w_down):
    B, T, D = x.shape
    TC = ctx.shape[1]
    L = ada_w.shape[0]
    rows = T // GRID_W
    assert B <= 2 and T % 512 == 0 and TC == 256 and rows >= NA_WIN

    w_in_p = _pack_w_in(w_in)
    w_out4 = w_out.reshape(L, 4, GROUP_W, D).astype(BF16)
    w_up = ffn_w_up.astype(BF16)
    w_down = ffn_w_down.astype(BF16)
    w_uq = _pack_w_uq(mla_w_uq)
    w_ukv = mla_w_ukv.astype(BF16)
    qn_g = _pad_gain(mla_qn_g)
    kn_g = _pad_gain(mla_kn_g)
    lru_wcat = jnp.concatenate([lru_wa, lru_wx], axis=-1).astype(BF16)
    alog_v = _lane_vec(gdn_a_log)
    dtb_v = _lane_vec(gdn_dt_bias)
    cosf, sinf = _rope_tables(T)

    cvec = jnp.zeros((SUBLANES, D), F32).at[:B].set(c).at[2].set(c_ctx)
    mod_all = _modulation(cvec, ada_w, ada_b).reshape(L, SUBLANES, N_MOD, 1, D)

    gmix = norm_mix_g[:, None, :]
    gffn = norm_ffn_g[:, None, :]
    ffn_cb = ffn_conv_b[:, None, :]
    h_lat = x.reshape(B * T, D)
    h_ctx = ctx.reshape(B * TC, D)
    for l in range(L):
        want_ctx = l < L - 1
        mod = mod_all
        Pl = _inproj(l, h_lat, gmix, mod, w_in_p, T, False)
        Pc = _inproj(l, h_ctx, gmix, mod, w_in_p, TC, True)

        lw = lambda dd: (lru_conv_w[l], lru_conv_b[l][None, :], lru_wcat[l, dd], lru_ba[l, dd][None, :],
                         lru_bx[l, dd][None, :], lru_lam[l, dd][None, :])
        ya_c, ya_l = _mixer_lru(Pc, Pl, TC, T, lw(0), lw(1))

        mw = (mla_qa_g[l][None, :], w_uq[l], mla_kva_g[l][None, :], w_ukv[l], qn_g[l], kn_g[l])
        Qtc, Kc, Vtc = _mla_prep(Pc, TC, False, mw, cosf, sinf)
        Qtl, Kl, Vtl = _mla_prep(Pl, T, True, mw, cosf, sinf)
        yb_l = _flash_attention(Qtl, Kc, Vtc, T, TC, Kl, Vtl)

        yc_c, yc_l = _mixer_gdn(Pc, Pl, TC, T, gdn_conv_w[l], alog_v[l], dtb_v[l], gdn_norm_g[l][None, :])

        nqg, nkg = na_qn_g[l][None, :], na_kn_g[l][None, :]
        NQtc, NKc, NVtc = _na_prep(Pc, nqg, nkg)
        NQtl, NKl, NVtl = _na_prep(Pl, nqg, nkg)
        bias = _na_bias(na_rpb[l], rows)
        yd_l = _na_attention(NQtl, NKl, NVtl, NKc, NVtc, bias, T, TC)

        h_lat = _outproj(l, (ya_l, yb_l, yc_l, yd_l), w_out4, h_lat, mod, T, False)
        h_lat = _ffn(l, h_lat, gffn, mod, w_up, ffn_conv_w, ffn_cb, w_down, T, False)
        if want_ctx:
            yb_c = _flash_attention(Qtc, Kc, Vtc, TC, TC)
            yd_c = _flash_attention(NQtc, NKc, NVtc, TC, TC)
            h_ctx = _outproj(l, (ya_c, yb_c, yc_c, yd_c), w_out4, h_ctx, mod, TC, True)
            h_ctx = _ffn(l, h_ctx, gffn, mod, w_up, ffn_conv_w, ffn_cb, w_down, TC, True)
    return h_lat.reshape(B, T, D)
```

```python
import functools
import math

import jax
import jax.numpy as jnp
from jax import lax
from jax.experimental import pallas as pl
from jax.experimental.pallas import tpu as pltpu

F32 = jnp.float32
BF16 = jnp.bfloat16

GRID_W = 64
HEADS = 4
HEAD_DIM = 128
GROUP_W = HEADS * HEAD_DIM
N_MOD = 6
EPS = 1e-6
LRU_C = 8.0
MLA_Q_LORA = 384
MLA_KV_LORA = 256
MLA_NOPE = 128
MLA_ROPE = 64
MLA_QK = MLA_NOPE + MLA_ROPE
MLA_QPAD = 256
ROPE_BASE = 10000.0
GDN_CHUNK = 64
NA_KH = 8
NA_KW = 16
NA_BAND = 8
NA_WIN = 16
NEG_BIG = -1e30
LOG2E = math.log2(math.e)
TINY = 1e-30
FLASH_UNROLL = 8
FFN_SUB = 256
INPROJ_SUB = 4

VMEM_LIMIT = 56 * 1024 * 1024
SUBLANES = 8
LANES = 128

P_GDN_QKV = 0
P_GDN_GATE = 1536
P_LRU_X = 2048
P_LRU_GATE = 2560
P_NA_QKV = 3072
P_MLA = 4608
P_COLS = 5376
P_TAIL = 5248
TAIL_BETA = 64
TAIL_ALPHA = 72


def _cparams(sem):
    return pltpu.CompilerParams(dimension_semantics=sem, vmem_limit_bytes=VMEM_LIMIT)


def _dot(a, b):
    return jnp.dot(a.astype(BF16), b.astype(BF16), preferred_element_type=F32)


def _dot_nt(a, b):
    return lax.dot_general(a.astype(BF16), b.astype(BF16), (((1,), (1,)), ((), ())),
                           preferred_element_type=F32)


def _sigmoid(x):
    return 1.0 / (1.0 + jnp.exp(-x))


def _silu(x):
    return x * _sigmoid(x)


def _softplus(x):
    return jnp.maximum(x, 0.0) + jnp.log(1.0 + jnp.exp(-jnp.abs(x)))


def _gelu_tanh(x):
    return 0.5 * x * (1.0 + jnp.tanh(math.sqrt(2.0 / math.pi) * (x + 0.044715 * x * x * x)))


def _mod_kernel(c_ref, w_ref, b_ref, o_ref):
    c = c_ref[...]
    o_ref[...] = _dot(_silu(c), w_ref[...]) + b_ref[...]


def _modulation(cvec, ada_w, ada_b):
    L, D, N = ada_w.shape
    tn = 1024
    return pl.pallas_call(
        _mod_kernel,
        grid=(L, N // tn),
        in_specs=[pl.BlockSpec((SUBLANES, D), lambda l, j: (0, 0)),
                  pl.BlockSpec((None, D, tn), lambda l, j: (l, 0, j)),
                  pl.BlockSpec((None, 1, tn), lambda l, j: (l, 0, j))],
        out_specs=pl.BlockSpec((None, SUBLANES, tn), lambda l, j: (l, 0, j)),
        out_shape=jax.ShapeDtypeStruct((L, SUBLANES, N), F32),
        compiler_params=_cparams(("parallel", "parallel")),
        name="modulation",
    )(cvec, ada_w, ada_b.reshape(L, 1, N))


def _mod_spec(l, which, rows_per_mod, D, ngrid):
    if rows_per_mod is None:
        row = lambda i: 2
    else:
        row = lambda i: i // rows_per_mod
    if ngrid == 1:
        return pl.BlockSpec((None, None, None, 1, D), lambda i: (l, row(i), which, 0, 0))
    return pl.BlockSpec((None, None, None, 1, D), lambda i, j: (l, row(i), which, 0, 0))


def _inproj_kernel(h_ref, g_ref, shift_ref, scale_ref, w_ref, o_ref, xn_ref):
    tm = h_ref.shape[0]

    @pl.when(pl.program_id(1) == 0)
    def _():
        rs = tm // min(INPROJ_SUB, tm // 256)
        for r0 in range(0, tm, rs):
            x = h_ref[r0:r0 + rs, :]
            y = x * lax.rsqrt(jnp.mean(x * x, axis=-1, keepdims=True) + EPS) * g_ref[...]
            xb = (y * (1.0 + scale_ref[...]) + shift_ref[...]).astype(BF16)
            xn_ref[r0:r0 + rs, :] = xb
            o_ref[r0:r0 + rs, :] = jnp.dot(xb, w_ref[...], preferred_element_type=F32)

    @pl.when(pl.program_id(1) > 0)
    def _():
        o_ref[...] = jnp.dot(xn_ref[...], w_ref[...], preferred_element_type=F32)


def _inproj(l, h, gain, mod, w, seq_len, is_ctx):
    M, D = h.shape
    N = w.shape[2]
    tm = min(1024, M)
    tn = 768
    rpm = None if is_ctx else seq_len // tm
    return pl.pallas_call(
        _inproj_kernel,
        grid=(M // tm, N // tn),
        in_specs=[pl.BlockSpec((tm, D), lambda i, j: (i, 0)),
                  pl.BlockSpec((None, 1, D), lambda i, j: (l, 0, 0)),
                  _mod_spec(l, 0, rpm, D, 2),
                  _mod_spec(l, 1, rpm, D, 2),
                  pl.BlockSpec((None, D, tn), lambda i, j: (l, 0, j))],
        out_specs=pl.BlockSpec((tm, tn), lambda i, j: (i, j)),
        out_shape=jax.ShapeDtypeStruct((M, N), F32),
        scratch_shapes=[pltpu.VMEM((tm, D), BF16)],
        compiler_params=_cparams(("parallel", "arbitrary")),
        name="inproj",
    )(h, gain, mod, mod, w)


def _outproj_kernel(ya_ref, yb_ref, yc_ref, yd_ref, w_ref, h_ref, gate_ref, o_ref):
    acc = jnp.dot(ya_ref[...], w_ref[0], preferred_element_type=F32)
    acc += jnp.dot(yb_ref[...], w_ref[1], preferred_element_type=F32)
    acc += jnp.dot(yc_ref[...], w_ref[2], preferred_element_type=F32)
    acc += jnp.dot(yd_ref[...], w_ref[3], preferred_element_type=F32)
    o_ref[...] = h_ref[...] + gate_ref[...] * acc


def _outproj(l, ys, w4, h, mod, seq_len, is_ctx):
    M, D = h.shape
    tm = min(512, M)
    rpm = None if is_ctx else seq_len // tm
    yspec = pl.BlockSpec((tm, GROUP_W), lambda i: (i, 0))
    return pl.pallas_call(
        _outproj_kernel,
        grid=(M // tm,),
        in_specs=[yspec, yspec, yspec, yspec,
                  pl.BlockSpec((None, 4, GROUP_W, D), lambda i: (l, 0, 0, 0)),
                  pl.BlockSpec((tm, D), lambda i: (i, 0)),
                  _mod_spec(l, 2, rpm, D, 1)],
        out_specs=pl.BlockSpec((tm, D), lambda i: (i, 0)),
        out_shape=jax.ShapeDtypeStruct((M, D), F32),
        compiler_params=_cparams(("parallel",)),
        name="outproj",
    )(*ys, w4, h, mod)


def _halo_specs(tm, width, col_block, nrows, ngrid):
    r = tm // SUBLANES
    last = nrows // SUBLANES - 1
    if ngrid == 1:
        prev = pl.BlockSpec((SUBLANES, width), lambda i: (jnp.maximum(i * r - 1, 0), col_block))
        nxt = pl.BlockSpec((SUBLANES, width), lambda i: (jnp.minimum((i + 1) * r, last), col_block))
    else:
        prev = pl.BlockSpec((SUBLANES, width), lambda i, j: (jnp.maximum(i * r - 1, 0), col_block))
        nxt = pl.BlockSpec((SUBLANES, width), lambda i, j: (jnp.minimum((i + 1) * r, last), col_block))
    return prev, nxt


def _ffn_kernel(seq_len, h_ref, hp_ref, hn_ref, g_ref, shift_ref, scale_ref, gate_ref,
                wa_ref, wg_ref, cwa_ref, cwg_ref, cba_ref, cbg_ref, wd_ref, o_ref, xn_ref):
    tm = h_ref.shape[0]
    tf = wa_ref.shape[1]
    i = pl.program_id(0)
    j = pl.program_id(1)

    @pl.when(j == 0)
    def _():
        def norm(x):
            y = x * lax.rsqrt(jnp.mean(x * x, axis=-1, keepdims=True) + EPS) * g_ref[...]
            return (y * (1.0 + scale_ref[...]) + shift_ref[...]).astype(BF16)
        keep_prev = (i * tm) % seq_len != 0
        keep_next = ((i + 1) * tm) % seq_len != 0
        xn_ref[0:SUBLANES, :] = jnp.where(keep_prev, norm(hp_ref[...]), jnp.zeros((), BF16))
        xn_ref[SUBLANES:SUBLANES + tm, :] = norm(h_ref[...])
        xn_ref[SUBLANES + tm:, :] = jnp.where(keep_next, norm(hn_ref[...]), jnp.zeros((), BF16))

    def conv(u, cw_ref, cb_ref, cols):
        lo = u[SUBLANES - 1:SUBLANES - 1 + tm]
        mid = u[SUBLANES:SUBLANES + tm]
        hi = u[SUBLANES + 1:SUBLANES + 1 + tm]
        return lo * cw_ref[0:1, cols] + mid * cw_ref[1:2, cols] + hi * cw_ref[2:3, cols] + cb_ref[:, cols]

    xn = xn_ref[...]
    subs = [slice(s, s + FFN_SUB) for s in range(0, tf, FFN_SUB)]
    ups = [(jnp.dot(xn, wa_ref[:, c], preferred_element_type=F32),
            jnp.dot(xn, wg_ref[:, c], preferred_element_type=F32)) for c in subs]
    acts = [(conv(ua, cwa_ref, cba_ref, c) * _silu(conv(ug, cwg_ref, cbg_ref, c))).astype(BF16)
            for c, (ua, ug) in zip(subs, ups)]
    part = jnp.dot(jnp.concatenate(acts, axis=1), wd_ref[...], preferred_element_type=F32)

    @pl.when(j == 0)
    def _():
        o_ref[...] = part

    @pl.when(j > 0)
    def _():
        o_ref[...] += part

    @pl.when(j == pl.num_programs(1) - 1)
    def _():
        o_ref[...] = h_ref[...] + gate_ref[...] * o_ref[...]


def _ffn(l, h, gain, mod, w_up, conv_w, conv_b, w_down, seq_len, is_ctx):
    M, D = h.shape
    FF = w_down.shape[1]
    tm = min(512, seq_len)
    tf = 1024
    nf = FF // tf
    rpm = None if is_ctx else seq_len // tm
    prev, nxt = _halo_specs(tm, D, 0, M, 2)
    return pl.pallas_call(
        functools.partial(_ffn_kernel, seq_len),
        grid=(M // tm, nf),
        in_specs=[pl.BlockSpec((tm, D), lambda i, j: (i, 0)), prev, nxt,
                  pl.BlockSpec((None, 1, D), lambda i, j: (l, 0, 0)),
                  _mod_spec(l, 3, rpm, D, 2), _mod_spec(l, 4, rpm, D, 2), _mod_spec(l, 5, rpm, D, 2),
                  pl.BlockSpec((None, D, tf), lambda i, j: (l, 0, j)),
                  pl.BlockSpec((None, D, tf), lambda i, j: (l, 0, j + nf)),
                  pl.BlockSpec((None, 3, tf), lambda i, j: (l, 0, j)),
                  pl.BlockSpec((None, 3, tf), lambda i, j: (l, 0, j + nf)),
                  pl.BlockSpec((None, 1, tf), lambda i, j: (l, 0, j)),
                  pl.BlockSpec((None, 1, tf), lambda i, j: (l, 0, j + nf)),
                  pl.BlockSpec((None, tf, D), lambda i, j: (l, j, 0))],
        out_specs=pl.BlockSpec((tm, D), lambda i, j: (i, 0)),
        out_shape=jax.ShapeDtypeStruct((M, D), F32),
        scratch_shapes=[pltpu.VMEM((tm + 2 * SUBLANES, D), BF16)],
        compiler_params=_cparams(("parallel", "arbitrary")),
        name="conv_ffn",
    )(h, h, h, gain, mod, mod, mod, w_up, w_up, conv_w, conv_w, conv_b, conv_b, w_down)


def _conv4(xp_ref, x_ref, xn_ref, w_ref, start, seq_len):
    tm = x_ref.shape[0]
    xp = jnp.where(start != 0, xp_ref[...], 0.0)
    xn = jnp.where(start + tm != seq_len, xn_ref[...], 0.0)
    xe = jnp.concatenate([xp, x_ref[...], xn], axis=0)
    t0 = xe[SUBLANES - 1:SUBLANES - 1 + tm]
    t1 = xe[SUBLANES:SUBLANES + tm]
    t2 = xe[SUBLANES + 1:SUBLANES + 1 + tm]
    t3 = xe[SUBLANES + 2:SUBLANES + 2 + tm]
    return t0 * w_ref[0:1, :] + t1 * w_ref[1:2, :] + t2 * w_ref[2:3, :] + t3 * w_ref[3:4, :]


def _shift_rows(x, k, fill, rev):
    n = x.shape[0]
    if k % SUBLANES == 0:
        pad = jnp.full((k, x.shape[1]), fill, x.dtype)
        return jnp.concatenate([x[k:], pad], 0) if rev else jnp.concatenate([pad, x[:n - k]], 0)
    row = lax.broadcasted_iota(jnp.int32, x.shape, 0)
    if rev:
        return jnp.where(row >= n - k, fill, pltpu.roll(x, n - k, 0))
    return jnp.where(row < k, fill, pltpu.roll(x, k, 0))


def _lru_kernel(rev, finish, seq_len, *refs):
    if finish:
        (x_ref, xp_ref, xn_ref, gate_ref, hf_ref, cw_ref, cb_ref, wcat_ref, ba_ref, bx_ref,
         lam_ref, h0_ref, out_ref, st_ref, carry_ref) = refs
    else:
        (x_ref, xp_ref, xn_ref, cw_ref, cb_ref, wcat_ref, ba_ref, bx_ref,
         lam_ref, h0_ref, out_ref, st_ref, carry_ref) = refs
    tc = x_ref.shape[0]
    j = pl.program_id(1)
    nch = pl.num_programs(1)
    c = (nch - 1 - j) if rev else j

    @pl.when(j == 0)
    def _():
        carry_ref[...] = h0_ref[...]

    u = _conv4(xp_ref, x_ref, xn_ref, cw_ref, c * tc, seq_len) + cb_ref[...]
    rs, is_ = [], []
    for n in range(HEADS):
        z = _dot(u[:, n * HEAD_DIM:(n + 1) * HEAD_DIM], wcat_ref[n])
        rs.append(z[:, :HEAD_DIM])
        is_.append(z[:, HEAD_DIM:])
    r = _sigmoid(jnp.concatenate(rs, axis=1) + ba_ref[...])
    ig = _sigmoid(jnp.concatenate(is_, axis=1) + bx_ref[...])
    log_a = -LRU_C * r * _softplus(-lam_ref[...])
    a = jnp.exp(log_a)
    om = 1.0 - a * a
    b = om * lax.rsqrt(jnp.maximum(om, TINY)) * ig * u
    nslab = tc // SUBLANES
    a = a.reshape(nslab, SUBLANES, a.shape[1])
    b = b.reshape(nslab, SUBLANES, b.shape[1])
    sub = lax.broadcasted_iota(jnp.int32, a.shape, 1)
    k = 1
    while k < SUBLANES:
        own = (sub >= SUBLANES - k) if rev else (sub < k)
        shift = SUBLANES - k if rev else k
        b = a * jnp.where(own, 0.0, pltpu.roll(b, shift, 1)) + b
        a = a * jnp.where(own, 1.0, pltpu.roll(a, shift, 1))
        k *= 2
    a = a.reshape(tc, a.shape[2])
    b = b.reshape(tc, b.shape[2])
    state = carry_ref[0:1, :]
    hs = [None] * nslab
    for t in range(nslab):
        s = nslab - 1 - t if rev else t
        rows = slice(s * SUBLANES, (s + 1) * SUBLANES)
        hs[s] = b[rows] + a[rows] * state
        state = hs[s][0:1, :] if rev else hs[s][SUBLANES - 1:SUBLANES, :]
    h = jnp.concatenate(hs, axis=0)
    last = state
    carry_ref[...] = jnp.broadcast_to(last, carry_ref.shape)
    st_ref[...] = jnp.broadcast_to(last, st_ref.shape)
    if finish:
        out_ref[...] = ((hf_ref[...] + h) * _gelu_tanh(gate_ref[...])).astype(out_ref.dtype)
    else:
        out_ref[...] = h


def _lru_pass(P, seq_len, rev, h0, wts, hf=None):
    cw, cb, wcat, ba, bx, lam = wts
    M = P.shape[0]
    B = M // seq_len
    tc = 256
    nch = seq_len // tc
    finish = hf is not None
    W = GROUP_W
    xb = P_LRU_X // W
    gb = P_LRU_GATE // W
    r = tc // SUBLANES
    last = M // SUBLANES - 1

    def chunk(j):
        return (nch - 1 - j) if rev else j

    row = lambda b, j: (b * nch + chunk(j), xb)
    specs = [pl.BlockSpec((tc, W), row),
             pl.BlockSpec((SUBLANES, W), lambda b, j: (jnp.maximum((b * nch + chunk(j)) * r - 1, 0), xb)),
             pl.BlockSpec((SUBLANES, W), lambda b, j: (jnp.minimum((b * nch + chunk(j) + 1) * r, last), xb))]
    args = [P, P, P]
    if finish:
        specs += [pl.BlockSpec((tc, W), lambda b, j: (b * nch + chunk(j), gb)),
                  pl.BlockSpec((tc, W), lambda b, j: (b * nch + chunk(j), 0))]
        args += [P, hf]
    full2 = lambda b, j: (0, 0)
    specs += [pl.BlockSpec((4, W), full2), pl.BlockSpec((1, W), full2),
              pl.BlockSpec((HEADS, HEAD_DIM, 2 * HEAD_DIM), lambda b, j: (0, 0, 0)),
              pl.BlockSpec((1, W), full2), pl.BlockSpec((1, W), full2), pl.BlockSpec((1, W), full2),
              pl.BlockSpec((None, SUBLANES, W), lambda b, j: (b, 0, 0))]
    args += [cw, cb, wcat, ba, bx, lam, h0]
    out_dtype = BF16 if finish else F32
    return pl.pallas_call(
        functools.partial(_lru_kernel, rev, finish, seq_len),
        grid=(B, nch),
        in_specs=specs,
        out_specs=[pl.BlockSpec((tc, W), lambda b, j: (b * nch + chunk(j), 0)),
                   pl.BlockSpec((None, SUBLANES, W), lambda b, j: (b, 0, 0))],
        out_shape=[jax.ShapeDtypeStruct((M, W), out_dtype),
                   jax.ShapeDtypeStruct((B, SUBLANES, W), F32)],
        scratch_shapes=[pltpu.VMEM((SUBLANES, W), F32)],
        compiler_params=_cparams(("parallel", "arbitrary")),
        name="rglru_bwd" if rev else "rglru_fwd",
    )(*args)


def _mixer_lru(Pc, Pl, ctx_len, seq_len, wts_f, wts_b):
    B = Pc.shape[0] // ctx_len
    zero = jnp.zeros((B, SUBLANES, GROUP_W), F32)
    hc_f, sc_f = _lru_pass(Pc, ctx_len, False, zero, wts_f)
    hl_f, _ = _lru_pass(Pl, seq_len, False, sc_f, wts_f)
    yc, sc_b = _lru_pass(Pc, ctx_len, True, zero, wts_b, hf=hc_f)
    yl, _ = _lru_pass(Pl, seq_len, True, sc_b, wts_b, hf=hl_f)
    return yc, yl


def _rope_mix(x, cos, sin):
    lane = lax.broadcasted_iota(jnp.int32, x.shape, 1)
    half = MLA_ROPE // 2
    swapped = jnp.where(lane < half, pltpu.roll(x, LANES - half, 1), pltpu.roll(x, half, 1))
    return x * cos + swapped * sin


def _mla_prep_kernel(use_rope, x_ref, qag_ref, wuq_ref, kvg_ref, wukv_ref, qg_ref, kg_ref,
                     cos_ref, sin_ref, qt_out, k_out, vt_out):
    x = x_ref[...]

    def rms(v, g):
        return v * lax.rsqrt(jnp.mean(v * v, axis=-1, keepdims=True) + EPS) * g

    q = _dot(rms(x[:, :MLA_Q_LORA], qag_ref[...]), wuq_ref[...])
    kv = _dot(rms(x[:, MLA_Q_LORA:MLA_Q_LORA + MLA_KV_LORA], kvg_ref[...]), wukv_ref[...])
    tail = x[:, MLA_Q_LORA + MLA_KV_LORA:]
    lane = lax.broadcasted_iota(jnp.int32, tail.shape, 1)
    kr = jnp.where(lane < MLA_ROPE, tail, 0.0)
    kr_ss = jnp.sum(kr * kr, axis=-1, keepdims=True)
    scale = MLA_QK ** -0.5 * LOG2E
    for h in range(HEADS):
        qh = q[:, h * MLA_QPAD:(h + 1) * MLA_QPAD]
        inv = lax.rsqrt(jnp.sum(qh * qh, axis=-1, keepdims=True) / MLA_QK + EPS) * scale
        qh = qh * inv * qg_ref[...]
        qn, qr = qh[:, :LANES], qh[:, LANES:]
        kn = kv[:, h * 2 * HEAD_DIM:h * 2 * HEAD_DIM + MLA_NOPE]
        vh = kv[:, h * 2 * HEAD_DIM + MLA_NOPE:(h + 1) * 2 * HEAD_DIM]
        kinv = lax.rsqrt((jnp.sum(kn * kn, axis=-1, keepdims=True) + kr_ss) / MLA_QK + EPS)
        kn = kn * kinv * kg_ref[:, :LANES]
        krh = kr * kinv * kg_ref[:, LANES:]
        if use_rope:
            qr = _rope_mix(qr, cos_ref[...], sin_ref[...])
            krh = _rope_mix(krh, cos_ref[...], sin_ref[...])
        qt_out[h] = jnp.concatenate([qn, qr], axis=1).T.astype(BF16)
        k_out[h, :, :LANES] = kn.astype(BF16)
        k_out[h, :, LANES:] = krh.astype(BF16)
        vt_out[h] = vh.T.astype(BF16)


def _mla_prep(P, seq_len, use_rope, wts, cosf, sinf):
    qag, wuq, kvg, wukv, qg, kg = wts
    M = P.shape[0]
    tm = 256
    nt = seq_len // tm
    full = lambda i: (0, 0)
    wmla = P_COLS - P_MLA
    tab = pl.BlockSpec((tm, LANES), (lambda i: (i % nt, 0)) if use_rope else (lambda i: (0, 0)))
    tspec = lambda w: pl.BlockSpec((HEADS, w, tm), lambda i: (0, 0, i))
    return pl.pallas_call(
        functools.partial(_mla_prep_kernel, use_rope),
        grid=(M // tm,),
        in_specs=[pl.BlockSpec((tm, wmla), lambda i: (i, P_MLA // wmla)),
                  pl.BlockSpec((1, MLA_Q_LORA), full), pl.BlockSpec(wuq.shape, full),
                  pl.BlockSpec((1, MLA_KV_LORA), full), pl.BlockSpec(wukv.shape, full),
                  pl.BlockSpec((1, MLA_QPAD), full), pl.BlockSpec((1, MLA_QPAD), full),
                  tab, tab],
        out_specs=[tspec(MLA_QPAD), pl.BlockSpec((HEADS, tm, MLA_QPAD), lambda i: (0, i, 0)),
                   tspec(HEAD_DIM)],
        out_shape=[jax.ShapeDtypeStruct((HEADS, MLA_QPAD, M), BF16),
                   jax.ShapeDtypeStruct((HEADS, M, MLA_QPAD), BF16),
                   jax.ShapeDtypeStruct((HEADS, HEAD_DIM, M), BF16)],
        compiler_params=_cparams(("parallel",)),
        name="mla_prep",
    )(P, qag, wuq, kvg, wukv, qg, kg, cosf, sinf)


def _flash_kernel(tk, has_lat, *refs):
    if has_lat:
        qt_ref, kc_ref, vct_ref, kl_ref, vlt_ref, o_ref, m_ref, l_ref, acc_ref, s_ref = refs
    else:
        qt_ref, kc_ref, vct_ref, o_ref = refs
    qt = qt_ref[...]
    s = jnp.dot(kc_ref[...], qt, preferred_element_type=F32)
    if has_lat:
        nk = kl_ref.shape[0] // tk

        def scores(c, slot):
            off = pl.multiple_of(jnp.minimum(c, nk - 1) * tk, tk)
            s_ref[slot] = jnp.dot(kl_ref[pl.ds(off, tk), :], qt, preferred_element_type=F32)

        scores(0, 0)
    m = jnp.max(s, axis=0, keepdims=True)
    p = jnp.exp2(s - m)
    l = jnp.sum(p, axis=0, keepdims=True)
    acc = jnp.dot(vct_ref[...], p.astype(BF16), preferred_element_type=F32)
    if has_lat:
        m_ref[...] = m
        l_ref[...] = l
        acc_ref[...] = acc

        def step(c, slot):
            scores(c + 1, 1 - slot)
            off = pl.multiple_of(c * tk, tk)
            s = s_ref[slot]
            m_old = m_ref[...]
            m_new = jnp.maximum(m_old, jnp.max(s, axis=0, keepdims=True))
            alpha = jnp.exp2(m_old - m_new)
            p = jnp.exp2(s - m_new)
            l_ref[...] = alpha * l_ref[...] + jnp.sum(p, axis=0, keepdims=True)
            acc_ref[...] = alpha * acc_ref[...] + jnp.dot(
                vlt_ref[:, pl.ds(off, tk)], p.astype(BF16), preferred_element_type=F32)
            m_ref[...] = m_new

        unroll = min(FLASH_UNROLL, nk)

        def body(j, carry):
            for u in range(unroll):
                step(unroll * j + u, u % 2)
            return carry

        lax.fori_loop(0, nk // unroll, body, 0)
        acc = acc_ref[...]
        l = l_ref[...]
    o_ref[...] = (acc / l).T.astype(o_ref.dtype)


def _flash_attention(Qt, Kc, Vct, seq_len, ctx_len, Kl=None, Vlt=None):
    H, dq, M = Qt.shape
    B = M // seq_len
    has_lat = Kl is not None
    tq = min(1024, seq_len)
    tk = 512
    nq = seq_len // tq
    specs = [pl.BlockSpec((None, dq, tq), lambda b, h, i: (h, 0, b * nq + i)),
             pl.BlockSpec((None, ctx_len, dq), lambda b, h, i: (h, b, 0)),
             pl.BlockSpec((None, HEAD_DIM, ctx_len), lambda b, h, i: (h, 0, b))]
    args = [Qt, Kc, Vct]
    scratch = []
    if has_lat:
        specs += [pl.BlockSpec((None, seq_len, dq), lambda b, h, i: (h, b, 0)),
                  pl.BlockSpec((None, HEAD_DIM, seq_len), lambda b, h, i: (h, 0, b))]
        args += [Kl, Vlt]
        nk = seq_len // tk
        assert nk % 2 == 0 and nk % min(FLASH_UNROLL, nk) == 0
        scratch = [pltpu.VMEM((1, tq), F32), pltpu.VMEM((1, tq), F32), pltpu.VMEM((HEAD_DIM, tq), F32),
                   pltpu.VMEM((2, tk, tq), F32)]
    return pl.pallas_call(
        functools.partial(_flash_kernel, tk, has_lat),
        grid=(B, H, nq),
        in_specs=specs,
        out_specs=pl.BlockSpec((tq, HEAD_DIM), lambda b, h, i: (b * nq + i, h)),
        out_shape=jax.ShapeDtypeStruct((M, GROUP_W), BF16),
        scratch_shapes=scratch,
        compiler_params=_cparams(("parallel", "parallel", "arbitrary")),
        name="mla_attention" if has_lat else "ctx_attention",
    )(*args)


def _na_prep_kernel(x_ref, qg_ref, kg_ref, qt_out, k_out, vt_out):
    x = x_ref[...]
    scale = HEAD_DIM ** -0.5 * LOG2E
    for h in range(HEADS):
        qh = x[:, h * HEAD_DIM:(h + 1) * HEAD_DIM]
        kh = x[:, GROUP_W + h * HEAD_DIM:GROUP_W + (h + 1) * HEAD_DIM]
        vh = x[:, 2 * GROUP_W + h * HEAD_DIM:2 * GROUP_W + (h + 1) * HEAD_DIM]
        qh = qh * (lax.rsqrt(jnp.mean(qh * qh, axis=-1, keepdims=True) + EPS) * scale) * qg_ref[...]
        kh = kh * lax.rsqrt(jnp.mean(kh * kh, axis=-1, keepdims=True) + EPS) * kg_ref[...]
        qt_out[h] = qh.T.astype(BF16)
        k_out[h] = kh.astype(BF16)
        vt_out[h] = vh.T.astype(BF16)


def _na_prep(P, qg, kg):
    M = P.shape[0]
    tm = 256
    w = 3 * GROUP_W
    hspec = pl.BlockSpec((HEADS, tm, HEAD_DIM), lambda i: (0, i, 0))
    tspec = pl.BlockSpec((HEADS, HEAD_DIM, tm), lambda i: (0, 0, i))
    sds = jax.ShapeDtypeStruct((HEADS, M, HEAD_DIM), BF16)
    tds = jax.ShapeDtypeStruct((HEADS, HEAD_DIM, M), BF16)
    return pl.pallas_call(
        _na_prep_kernel,
        grid=(M // tm,),
        in_specs=[pl.BlockSpec((tm, w), lambda i: (i, P_NA_QKV // w)),
                  pl.BlockSpec((1, HEAD_DIM), lambda i: (0, 0)),
                  pl.BlockSpec((1, HEAD_DIM), lambda i: (0, 0))],
        out_specs=[tspec, hspec, tspec],
        out_shape=[tds, sds, tds],
        compiler_params=_cparams(("parallel",)),
        name="na_prep",
    )(P, qg, kg)


def _na_bias_kernel(rows, rpb_ref, o_ref):
    h = pl.program_id(0)
    kc = lax.broadcasted_iota(jnp.int32, (GRID_W, GRID_W), 0)
    qc = lax.broadcasted_iota(jnp.int32, (GRID_W, GRID_W), 1)
    dc = jnp.clip(kc - qc + (NA_KW - 1), 0, 2 * NA_KW - 2)
    c0 = jnp.clip(qc - NA_KW // 2, 0, GRID_W - NA_KW)
    col_ok = (kc >= c0) & (kc < c0 + NA_KW)
    neg = jnp.full((GRID_W, GRID_W), NEG_BIG, F32)
    tiles = []
    for dr in range(2 * NA_KH - 1):
        t = jnp.zeros((GRID_W, GRID_W), F32)
        for d in range(2 * NA_KW - 1):
            t = t + jnp.where(dc == d, rpb_ref[h, dr, d], 0.0)
        tiles.append(jnp.where(col_ok, t * LOG2E, NEG_BIG))
    for kind in range(3):
        q_base = (0, NA_KH // 2, rows - NA_BAND)[kind]
        w_base = (0, 0, rows - NA_WIN)[kind]
        for qr in range(NA_BAND):
            r = q_base + qr
            if kind == 1:
                r0 = r - NA_KH // 2
            else:
                r0 = min(max(r - NA_KH // 2, 0), rows - NA_KH)
            for kr in range(NA_WIN):
                ka = w_base + kr
                ok = r0 <= ka < r0 + NA_KH
                blk = tiles[ka - r + NA_KH - 1] if ok else neg
                o_ref[kind, kr * GRID_W:(kr + 1) * GRID_W, qr * GRID_W:(qr + 1) * GRID_W] = blk


def _na_bias(rpb, rows):
    H = rpb.shape[0]
    nq, nk = NA_BAND * GRID_W, NA_WIN * GRID_W
    return pl.pallas_call(
        functools.partial(_na_bias_kernel, rows),
        grid=(H,),
        in_specs=[pl.BlockSpec(memory_space=pltpu.SMEM)],
        out_specs=pl.BlockSpec((None, 3, nk, nq), lambda h: (h, 0, 0, 0)),
        out_shape=jax.ShapeDtypeStruct((H, 3, nk, nq), F32),
        compiler_params=_cparams(("parallel",)),
        name="na_bias",
    )(rpb)


def _na_kernel(seq_len, qt_ref, k_ref, vt_ref, kc_ref, vct_ref, bias_ref, o_ref):
    nq = qt_ref.shape[1]
    nk = bias_ref.shape[0]
    j = pl.program_id(2)
    base = jnp.clip(j * nq - (NA_KH // 2) * GRID_W, 0, seq_len - nk)
    base = pl.multiple_of(base, (NA_KH // 2) * GRID_W)
    qt = qt_ref[...]
    s_ctx = jnp.dot(kc_ref[...], qt, preferred_element_type=F32)
    s_loc = jnp.dot(k_ref[pl.ds(base, nk), :], qt, preferred_element_type=F32) + bias_ref[...]
    m = jnp.maximum(jnp.max(s_loc, axis=0, keepdims=True), jnp.max(s_ctx, axis=0, keepdims=True))
    p_loc = jnp.exp2(s_loc - m)
    p_ctx = jnp.exp2(s_ctx - m)
    l = jnp.sum(p_loc, axis=0, keepdims=True) + jnp.sum(p_ctx, axis=0, keepdims=True)
    o = (jnp.dot(vt_ref[:, pl.ds(base, nk)], p_loc.astype(BF16), preferred_element_type=F32)
         + jnp.dot(vct_ref[...], p_ctx.astype(BF16), preferred_element_type=F32))
    o_ref[...] = (o / l).T.astype(o_ref.dtype)


def _na_attention(Qtl, Kl, Vtl, Kc, Vtc, bias, seq_len, ctx_len):
    H, d, M = Qtl.shape
    B = M // seq_len
    nq, nk = NA_BAND * GRID_W, NA_WIN * GRID_W
    nb = seq_len // nq

    def kind(j):
        return jnp.where(j == 0, 0, jnp.where(j == nb - 1, 2, 1))

    return pl.pallas_call(
        functools.partial(_na_kernel, seq_len),
        grid=(B, H, nb),
        in_specs=[pl.BlockSpec((None, d, nq), lambda b, h, j: (h, 0, b * nb + j)),
                  pl.BlockSpec((None, seq_len, d), lambda b, h, j: (h, b, 0)),
                  pl.BlockSpec((None, d, seq_len), lambda b, h, j: (h, 0, b)),
                  pl.BlockSpec((None, ctx_len, d), lambda b, h, j: (h, b, 0)),
                  pl.BlockSpec((None, d, ctx_len), lambda b, h, j: (h, 0, b)),
                  pl.BlockSpec((None, None, nk, nq), lambda b, h, j: (h, kind(j), 0, 0))],
        out_specs=pl.BlockSpec((nq, d), lambda b, h, j: (b * nb + j, h)),
        out_shape=jax.ShapeDtypeStruct((M, GROUP_W), BF16),
        compiler_params=_cparams(("parallel", "parallel", "arbitrary")),
        name="na_attention",
    )(Qtl, Kl, Vtl, Kc, Vtc, bias)


def _gdn_prep_kernel(seq_len, x_ref, xp_ref, xn_ref, tail_ref, cw_ref, alog_ref, dtb_ref,
                     qkv_out, gates_out):
    tm = x_ref.shape[0]
    y = _silu(_conv4(xp_ref, x_ref, xn_ref, cw_ref, (pl.program_id(0) * tm) % seq_len, seq_len))
    for h in range(HEADS):
        sl = slice(h * HEAD_DIM, (h + 1) * HEAD_DIM)
        qh = y[:, sl]
        qkv_out[:, sl] = qh * (lax.rsqrt(jnp.sum(qh * qh, axis=-1, keepdims=True) + EPS) * HEAD_DIM ** -0.5)
        sl = slice(GROUP_W + h * HEAD_DIM, GROUP_W + (h + 1) * HEAD_DIM)
        kh = y[:, sl]
        qkv_out[:, sl] = kh * lax.rsqrt(jnp.sum(kh * kh, axis=-1, keepdims=True) + EPS)
    qkv_out[:, 2 * GROUP_W:] = y[:, 2 * GROUP_W:]
    t = tail_ref[...]
    lane = lax.broadcasted_iota(jnp.int32, t.shape, 1)
    alpha = pltpu.roll(t, LANES - TAIL_ALPHA, 1)
    beta = pltpu.roll(t, LANES - TAIL_BETA + 2 * HEADS, 1)
    g = -jnp.exp(alog_ref[...]) * _softplus(alpha + dtb_ref[...])
    gates_out[...] = jnp.where(lane < 2 * HEADS, g, jnp.where(lane < 4 * HEADS, _sigmoid(beta), 0.0))


def _gdn_prep(P, seq_len, cw, alog, dtb):
    M = P.shape[0]
    tm = 256
    w = 3 * GROUP_W
    prev, nxt = _halo_specs(tm, w, 0, M, 1)
    full = lambda i: (0, 0)
    return pl.pallas_call(
        functools.partial(_gdn_prep_kernel, seq_len),
        grid=(M // tm,),
        in_specs=[pl.BlockSpec((tm, w), lambda i: (i, 0)), prev, nxt,
                  pl.BlockSpec((tm, LANES), lambda i: (i, P_TAIL // LANES)),
                  pl.BlockSpec((4, w), full), pl.BlockSpec((1, LANES), full), pl.BlockSpec((1, LANES), full)],
        out_specs=[pl.BlockSpec((tm, w), lambda i: (i, 0)), pl.BlockSpec((tm, LANES), lambda i: (i, 0))],
        out_shape=[jax.ShapeDtypeStruct((M, w), F32), jax.ShapeDtypeStruct((M, LANES), F32)],
        compiler_params=_cparams(("parallel",)),
        name="gdn_prep",
    )(P, P, P, P, cw, alog, dtb)


def _time_cumsum(x, rev):
    k = 1
    while k < x.shape[0]:
        x = x + _shift_rows(x, k, 0.0, rev)
        k *= 2
    return x


def _gdn_local_kernel(qkv_ref, gates_ref, a_out, qk_out):
    C = GDN_CHUNK
    ri = lax.broadcasted_iota(jnp.int32, (C, C), 0)
    ci = lax.broadcasted_iota(jnp.int32, (C, C), 1)
    for c in range(qkv_ref.shape[0] // C):
        rows = slice(c * C, (c + 1) * C)
        gt = gates_ref[rows, :]
        zs = []
        for h in range(HEADS):
            q = qkv_ref[rows, h * HEAD_DIM:(h + 1) * HEAD_DIM]
            kk = qkv_ref[rows, GROUP_W + h * HEAD_DIM:GROUP_W + (h + 1) * HEAD_DIM]
            zs.append(_dot_nt(jnp.concatenate([kk, q], axis=0), kk))
        for d in range(2):
            rev = d == 1
            incl = (ci >= ri) if rev else (ci <= ri)
            strict = (ci > ri) if rev else (ci < ri)
            gcum = _time_cumsum(gt, rev)
            grow = gcum.T
            for h in range(HEADS):
                lg = d * HEADS + h
                gc_c = gcum[:, lg:lg + 1]
                gc_r = grow[lg:lg + 1, :]
                beta = gt[:, 2 * HEADS + lg:2 * HEADS + lg + 1]
                decay = jnp.where(incl, jnp.exp(jnp.where(incl, gc_c - gc_r, 0.0)), 0.0)
                lanes = slice((h % 2) * C, (h % 2 + 1) * C)
                a_out[d, c * 2 + h // 2, :, lanes] = jnp.where(strict, zs[h][:C] * beta * decay, 0.0)
                qk_out[d, c * 2 + h // 2, :, lanes] = zs[h][C:] * decay


def _gdn_local(qkv, gates):
    M = qkv.shape[0]
    tm = 256
    npair = tm // GDN_CHUNK * (HEADS // 2)
    w = 3 * GROUP_W
    ospec = pl.BlockSpec((2, npair, GDN_CHUNK, LANES), lambda i: (0, i, 0, 0))
    sds = jax.ShapeDtypeStruct((2, M // GDN_CHUNK * (HEADS // 2), GDN_CHUNK, LANES), F32)
    return pl.pallas_call(
        _gdn_local_kernel,
        grid=(M // tm,),
        in_specs=[pl.BlockSpec((tm, w), lambda i: (i, 0)), pl.BlockSpec((tm, LANES), lambda i: (i, 0))],
        out_specs=[ospec, ospec],
        out_shape=[sds, sds],
        compiler_params=_cparams(("parallel",)),
        name="gdn_local",
    )(qkv, gates)


SOLVE_BATCH = 128


def _gdn_solve_kernel(upper, a_ref, o_ref, at_ref, m_ref):
    C = GDN_CHUNK
    ns = C // SUBLANES
    for i in range(C):
        xt = a_ref[pl.ds(i, SOLVE_BATCH, stride=C), :].T
        at_ref[i, 0] = xt[:C]
        at_ref[i, 1] = xt[C:]
    sub = lax.broadcasted_iota(jnp.int32, (SUBLANES, SOLVE_BATCH), 0)
    for t in range(C):
        i = C - 1 - t if upper else t
        si = i // SUBLANES
        slabs = list(range(si, ns)) if upper else list(range(si + 1))
        unit = jnp.where(sub == i % SUBLANES, 1.0, 0.0).astype(F32)
        zero = jnp.zeros((SUBLANES, SOLVE_BATCH), F32)
        init = tuple(unit if s == si else zero for _ in range(2) for s in slabs)

        def col_body(j, acc):
            new = []
            for hp in range(2):
                a = jnp.broadcast_to(at_ref[i, hp, pl.ds(j, 1), :], (SUBLANES, SOLVE_BATCH))
                for n, s in enumerate(slabs):
                    new.append(acc[hp * len(slabs) + n] - a * m_ref[hp, j, s * SUBLANES:(s + 1) * SUBLANES, :])
            return tuple(new)

        lo, hi = (i + 1, C) if upper else (0, i)
        if hi > lo:
            acc = lax.fori_loop(lo, hi, col_body, init, unroll=min(4, hi - lo))
        else:
            acc = init
        for hp in range(2):
            for s in range(ns):
                val = acc[hp * len(slabs) + slabs.index(s)] if s in slabs else zero
                m_ref[hp, i, s * SUBLANES:(s + 1) * SUBLANES, :] = val
    for i in range(C):
        y = jnp.concatenate([m_ref[0, i], m_ref[1, i]], axis=0)
        o_ref[i] = y.T


def _gdn_solve(a_all):
    _, NP, C, _ = a_all.shape
    rows = SOLVE_BATCH * C
    a2 = a_all.reshape(2, NP * C, LANES)
    outs = []
    for d in range(2):
        spec = pl.BlockSpec((None, rows, LANES), lambda b, d=d: (d, b, 0))
        outs.append(pl.pallas_call(
            functools.partial(_gdn_solve_kernel, d == 1),
            grid=(NP // SOLVE_BATCH,),
            in_specs=[spec],
            out_specs=pl.BlockSpec((None, C, SOLVE_BATCH, LANES), lambda b: (b, 0, 0, 0)),
            out_shape=jax.ShapeDtypeStruct((NP // SOLVE_BATCH, C, SOLVE_BATCH, LANES), F32),
            scratch_shapes=[pltpu.VMEM((C, 2, C, SOLVE_BATCH), F32), pltpu.VMEM((2, C, C, SOLVE_BATCH), F32)],
            compiler_params=_cparams(("parallel",)),
            name="gdn_solve_upper" if d else "gdn_solve_lower",
        )(a2))
    return outs


def _gdn_kernel(rev, finish, d, *refs):
    if finish:
        (qkv_ref, gates_ref, qk_ref, s0_ref, of_ref, gate_ref, ng_ref, *tinv_refs) = refs[:-4]
        out_ref, sout_ref, s_ref, o_ref = refs[-4:]
    else:
        (qkv_ref, gates_ref, qk_ref, s0_ref, *tinv_refs) = refs[:-3]
        out_ref, sout_ref, s_ref = refs[-3:]
        o_ref = out_ref
    C = GDN_CHUNK
    B = qkv_ref.shape[0]
    nchunk = qkv_ref.shape[1] // C
    j = pl.program_id(0)

    @pl.when(j == 0)
    def _():
        s_ref[...] = s0_ref[...]

    last_row = 0 if rev else C - 1
    order = [(nchunk - 1 - t) if rev else t for t in range(nchunk)]
    units = [(b, h) for b in range(B) for h in range(HEADS)]

    def local(c):
        rows = slice(c * C, (c + 1) * C)
        out = {}
        for b in range(B):
            gt = gates_ref[b, rows, :]
            gcum = _time_cumsum(gt, rev)
            for h in range(HEADS):
                lg = d * HEADS + h
                lanes = slice((h % 2) * C, (h % 2 + 1) * C)
                gc_c = gcum[:, lg:lg + 1]
                beta = gt[:, 2 * HEADS + lg:2 * HEADS + lg + 1]
                gl = gcum[last_row:last_row + 1, lg:lg + 1]
                q = qkv_ref[b, rows, h * HEAD_DIM:(h + 1) * HEAD_DIM]
                kk = qkv_ref[b, rows, GROUP_W + h * HEAD_DIM:GROUP_W + (h + 1) * HEAD_DIM]
                v = qkv_ref[b, rows, 2 * GROUP_W + h * HEAD_DIM:2 * GROUP_W + (h + 1) * HEAD_DIM]
                tinv = tinv_refs[b][:, c * 2 + h // 2, lanes]
                qk = qk_ref[b, c * 2 + h // 2, :, lanes]
                kb = kk * beta
                eg = jnp.exp(gc_c)
                uw = _dot(tinv, jnp.concatenate([v * beta, kb * eg], axis=1))
                wq = jnp.concatenate([uw[:, HEAD_DIM:], q * eg], axis=0).astype(BF16)
                kdt = (kk * jnp.exp(gl - gc_c)).T
                qkk = jnp.concatenate([qk, kdt], axis=0).astype(BF16)
                out[b, h] = (uw[:, :HEAD_DIM], wq, qkk, jnp.exp(gl))
        return out

    states = {(b, h): s_ref[b, h] for b, h in units}
    nxt = local(order[0])
    for t, c in enumerate(order):
        cur = nxt
        ws = {k: jnp.dot(cur[k][1], states[k].astype(BF16), preferred_element_type=F32) for k in units}
        if t + 1 < nchunk:
            nxt = local(order[t + 1])
        for b, h in units:
            u, _, qkk, decay = cur[b, h]
            v_new = u - ws[b, h][:C]
            res = jnp.dot(qkk, v_new.astype(BF16), preferred_element_type=F32)
            o_ref[b, c * C:(c + 1) * C, h * HEAD_DIM:(h + 1) * HEAD_DIM] = ws[b, h][C:] + res[:C]
            states[b, h] = states[b, h] * decay + res[C:]
    for b, h in units:
        s_ref[b, h] = states[b, h]
    sout_ref[...] = s_ref[...]
    if finish:
        for b in range(B):
            tot = of_ref[b] + o_ref[b]
            gate = gate_ref[b]
            for h in range(HEADS):
                sl = slice(h * HEAD_DIM, (h + 1) * HEAD_DIM)
                x = tot[:, sl]
                y = x * lax.rsqrt(jnp.mean(x * x, axis=-1, keepdims=True) + EPS) * ng_ref[...]
                out_ref[b, :, sl] = (y * _silu(gate[:, sl])).astype(out_ref.dtype)


def _gdn_pass(qkv, gates, tinv_all, tinv_off, qk, seq_len, rev, d, s0, of=None, P=None, ng=None):
    M = qkv.shape[0]
    B = M // seq_len
    rb = 256
    nblk = seq_len // rb
    finish = of is not None
    w = 3 * GROUP_W
    npair = rb // GDN_CHUNK * (HEADS // 2)
    toff = tinv_off // npair
    per_batch = SOLVE_BATCH // npair

    def blk(j):
        return (nblk - 1 - j) if rev else j

    sspec = pl.BlockSpec((B, HEADS, HEAD_DIM, HEAD_DIM), lambda j: (0, 0, 0, 0))
    specs = [pl.BlockSpec((B, rb, w), lambda j: (0, blk(j), 0)),
             pl.BlockSpec((B, rb, LANES), lambda j: (0, blk(j), 0)),
             pl.BlockSpec((None, B, npair, GDN_CHUNK, LANES), lambda j: (d, 0, blk(j), 0, 0)),
             sspec]
    args = [qkv.reshape(B, seq_len, w), gates.reshape(B, seq_len, LANES),
            qk.reshape(2, B, nblk * npair, GDN_CHUNK, LANES), s0]
    scratch = [pltpu.VMEM((B, HEADS, HEAD_DIM, HEAD_DIM), F32)]
    if finish:
        specs += [pl.BlockSpec((B, rb, GROUP_W), lambda j: (0, blk(j), 0)),
                  pl.BlockSpec((B, rb, GROUP_W), lambda j: (0, blk(j), P_GDN_GATE // GROUP_W)),
                  pl.BlockSpec((1, HEAD_DIM), lambda j: (0, 0))]
        args += [of.reshape(B, seq_len, GROUP_W), P.reshape(B, seq_len, P.shape[1]), ng]
        scratch.append(pltpu.VMEM((B, rb, GROUP_W), F32))
    for b in range(B):
        def tmap(j, b=b):
            g = toff + b * nblk + blk(j)
            return (g // per_batch, 0, g % per_batch, 0)
        specs.append(pl.BlockSpec((None, GDN_CHUNK, npair, LANES), tmap))
        args.append(tinv_all)
    out, s_out = pl.pallas_call(
        functools.partial(_gdn_kernel, rev, finish, d),
        grid=(nblk,),
        in_specs=specs,
        out_specs=[pl.BlockSpec((B, rb, GROUP_W), lambda j: (0, blk(j), 0)), sspec],
        out_shape=[jax.ShapeDtypeStruct((B, seq_len, GROUP_W), BF16 if finish else F32),
                   jax.ShapeDtypeStruct((B, HEADS, HEAD_DIM, HEAD_DIM), F32)],
        scratch_shapes=scratch,
        compiler_params=_cparams(("arbitrary",)),
        name="gdn_bwd" if rev else "gdn_fwd",
    )(*args)
    return out.reshape(M, GROUP_W), s_out


def _mixer_gdn(Pc, Pl, ctx_len, seq_len, cw, alog, dtb, ng):
    B = Pc.shape[0] // ctx_len
    qkv_c, g_c = _gdn_prep(Pc, ctx_len, cw, alog, dtb)
    qkv_l, g_l = _gdn_prep(Pl, seq_len, cw, alog, dtb)
    a_c, qk_c = _gdn_local(qkv_c, g_c)
    a_l, qk_l = _gdn_local(qkv_l, g_l)
    n_c, n_l = a_c.shape[1], a_l.shape[1]
    n_pad = -(n_c + n_l) % SOLVE_BATCH
    pad = jnp.zeros((2, n_pad, GDN_CHUNK, LANES), F32)
    tinv = _gdn_solve(jnp.concatenate([a_c, a_l, pad], axis=1))
    zero = jnp.zeros((B, HEADS, HEAD_DIM, HEAD_DIM), F32)
    oc_f, sc_f = _gdn_pass(qkv_c, g_c, tinv[0], 0, qk_c, ctx_len, False, 0, zero)
    ol_f, _ = _gdn_pass(qkv_l, g_l, tinv[0], n_c, qk_l, seq_len, False, 0, sc_f)
    yc, sc_b = _gdn_pass(qkv_c, g_c, tinv[1], 0, qk_c, ctx_len, True, 1, zero, of=oc_f, P=Pc, ng=ng)
    yl, _ = _gdn_pass(qkv_l, g_l, tinv[1], n_c, qk_l, seq_len, True, 1, sc_b, of=ol_f, P=Pl, ng=ng)
    return yc, yl


def _pack_w_in(w_in):
    off = {}
    o = 0
    for name, n in (('lru_x', 512), ('lru_gate', 512), ('mla_cq', 384), ('mla_ckv', 256), ('mla_kr', 64),
                    ('gdn_q', 512), ('gdn_k', 512), ('gdn_v', 512), ('gdn_gate', 512),
                    ('gdn_beta', 8), ('gdn_alpha', 8), ('na_q', 512), ('na_k', 512), ('na_v', 512)):
        off[name] = (o, n)
        o += n
    order = ['gdn_q', 'gdn_k', 'gdn_v', 'gdn_gate', 'lru_x', 'lru_gate', 'na_q', 'na_k', 'na_v',
             'mla_cq', 'mla_ckv', 'mla_kr', 'gdn_beta', 'gdn_alpha']
    parts = [w_in[..., off[n][0]:off[n][0] + off[n][1]] for n in order]
    used = sum(off[n][1] for n in order)
    parts.append(jnp.zeros(w_in.shape[:-1] + (P_COLS - used,), w_in.dtype))
    return jnp.concatenate(parts, axis=-1).astype(BF16)


def _pack_w_uq(w_uq):
    L, K, _ = w_uq.shape
    w = w_uq.reshape(L, K, HEADS, MLA_QK)
    w = jnp.pad(w, ((0, 0), (0, 0), (0, 0), (0, MLA_QPAD - MLA_QK)))
    return w.reshape(L, K, HEADS * MLA_QPAD).astype(BF16)


def _pad_gain(g):
    return jnp.pad(g, ((0, 0), (0, MLA_QPAD - MLA_QK)))[:, None, :]


def _rope_tables(T):
    t = jnp.arange(T)
    rowp = (t // GRID_W).astype(F32)
    colp = (t % GRID_W).astype(F32)
    n_freq = MLA_ROPE // 4
    inv = ROPE_BASE ** (-jnp.arange(n_freq, dtype=F32) / n_freq)
    ang = jnp.concatenate([rowp[:, None] * inv, colp[:, None] * inv], axis=-1)
    cos, sin = jnp.cos(ang), jnp.sin(ang)
    z = jnp.zeros((T, LANES - MLA_ROPE), F32)
    return (jnp.concatenate([cos, cos, z], axis=-1), jnp.concatenate([-sin, sin, z], axis=-1))


def _lane_vec(x):
    L = x.shape[0]
    return jnp.pad(x.reshape(L, 2 * HEADS), ((0, 0), (0, LANES - 2 * HEADS)))[:, None, :]


def kernel(x, c, ctx, c_ctx, ada_w, ada_b, norm_mix_g, norm_ffn_g, w_in, w_out, lru_conv_w, lru_conv_b, lru_wa, lru_ba, lru_wx, lru_bx, lru_lam, mla_qa_g, mla_w_uq, mla_kva_g, mla_w_ukv, mla_qn_g, mla_kn_g, gdn_conv_w, gdn_a_log, gdn_dt_bias, gdn_norm_g, na_qn_g, na_kn_g, na_rpb, ffn_w_up, ffn_conv_w, ffn_conv_b, ffn_w_down):
    B, T, D = x.shape
    TC = ctx.shape[1]
    L = ada_w.shape[0]
    rows = T // GRID_W
    assert B <= 2 and T % 512 == 0 and TC == 256 and rows >= NA_WIN

    w_in_p = _pack_w_in(w_in)
    w_out4 = w_out.reshape(L, 4, GROUP_W, D).astype(BF16)
    w_up = ffn_w_up.astype(BF16)
    w_down = ffn_w_down.astype(BF16)
    w_uq = _pack_w_uq(mla_w_uq)
    w_ukv = mla_w_ukv.astype(BF16)
    qn_g = _pad_gain(mla_qn_g)
    kn_g = _pad_gain(mla_kn_g)
    lru_wcat = jnp.concatenate([lru_wa, lru_wx], axis=-1).astype(BF16)
    alog_v = _lane_vec(gdn_a_log)
    dtb_v = _lane_vec(gdn_dt_bias)
    cosf, sinf = _rope_tables(T)

    cvec = jnp.zeros((SUBLANES, D), F32).at[:B].set(c).at[2].set(c_ctx)
    mod_all = _modulation(cvec, ada_w, ada_b).reshape(L, SUBLANES, N_MOD, 1, D)

    gmix = norm_mix_g[:, None, :]
    gffn = norm_ffn_g[:, None, :]
    ffn_cb = ffn_conv_b[:, None, :]
    h_lat = x.reshape(B * T, D)
    h_ctx = ctx.reshape(B * TC, D)
    for l in range(L):
        want_ctx = l < L - 1
        mod = mod_all
        Pl = _inproj(l, h_lat, gmix, mod, w_in_p, T, False)
        Pc = _inproj(l, h_ctx, gmix, mod, w_in_p, TC, True)

        lw = lambda dd: (lru_conv_w[l], lru_conv_b[l][None, :], lru_wcat[l, dd], lru_ba[l, dd][None, :],
                         lru_bx[l, dd][None, :], lru_lam[l, dd][None, :])
        ya_c, ya_l = _mixer_lru(Pc, Pl, TC, T, lw(0), lw(1))

        mw = (mla_qa_g[l][None, :], w_uq[l], mla_kva_g[l][None, :], w_ukv[l], qn_g[l], kn_g[l])
        Qtc, Kc, Vtc = _mla_prep(Pc, TC, False, mw, cosf, sinf)
        Qtl, Kl, Vtl = _mla_prep(Pl, T, True, mw, cosf, sinf)
        yb_l = _flash_attention(Qtl, Kc, Vtc, T, TC, Kl, Vtl)

        yc_c, yc_l = _mixer_gdn(Pc, Pl, TC, T, gdn_conv_w[l], alog_v[l], dtb_v[l], gdn_norm_g[l][None, :])

        nqg, nkg = na_qn_g[l][None, :], na_kn_g[l][None, :]
        NQtc, NKc, NVtc = _na_prep(Pc, nqg, nkg)
        NQtl, NKl, NVtl = _na_prep(Pl, nqg, nkg)
        bias = _na_bias(na_rpb[l], rows)
        yd_l = _na_attention(NQtl, NKl, NVtl, NKc, NVtc, bias, T, TC)

        h_lat = _outproj(l, (ya_l, yb_l, yc_l, yd_l), w_out4, h_lat, mod, T, False)
        h_lat = _ffn(l, h_lat, gffn, mod, w_up, ffn_conv_w, ffn_cb, w_down, T, False)
        if want_ctx:
            yb_c = _flash_attention(Qtc, Kc, Vtc, TC, TC)
            yd_c = _flash_attention(NQtc, NKc, NVtc, TC, TC)
            h_ctx = _outproj(l, (ya_c, yb_c, yc_c, yd_c), w_out4, h_ctx, mod, TC, True)
            h_ctx = _ffn(l, h_ctx, gffn, mod, w_up, ffn_conv_w, ffn_cb, w_down, TC, True)
    return h_lat.reshape(B, T, D)
```

```python
import functools
import math

import jax
import jax.numpy as jnp
from jax import lax
from jax.experimental import pallas as pl
from jax.experimental.pallas import tpu as pltpu

F32 = jnp.float32
BF16 = jnp.bfloat16

GRID_W = 64
HEADS = 4
HEAD_DIM = 128
GROUP_W = HEADS * HEAD_DIM
N_MOD = 6
EPS = 1e-6
LRU_C = 8.0
MLA_Q_LORA = 384
MLA_KV_LORA = 256
MLA_NOPE = 128
MLA_ROPE = 64
MLA_QK = MLA_NOPE + MLA_ROPE
MLA_QPAD = 256
ROPE_BASE = 10000.0
GDN_CHUNK = 64
NA_KH = 8
NA_KW = 16
NA_BAND = 8
NA_WIN = 16
NEG_BIG = -1e30
LOG2E = math.log2(math.e)
TINY = 1e-30
FLASH_UNROLL = 8
FFN_SUB = 256
INPROJ_SUB = 4

VMEM_LIMIT = 56 * 1024 * 1024
SUBLANES = 8
LANES = 128

P_GDN_QKV = 0
P_GDN_GATE = 1536
P_LRU_X = 2048
P_LRU_GATE = 2560
P_NA_QKV = 3072
P_MLA = 4608
P_COLS = 5376
P_TAIL = 5248
TAIL_BETA = 64
TAIL_ALPHA = 72


def _cparams(sem):
    return pltpu.CompilerParams(dimension_semantics=sem, vmem_limit_bytes=VMEM_LIMIT)


def _dot(a, b):
    return jnp.dot(a.astype(BF16), b.astype(BF16), preferred_element_type=F32)


def _dot_nt(a, b):
    return lax.dot_general(a.astype(BF16), b.astype(BF16), (((1,), (1,)), ((), ())),
                           preferred_element_type=F32)


def _sigmoid(x):
    return 1.0 / (1.0 + jnp.exp(-x))


def _silu(x):
    return x * _sigmoid(x)


def _softplus(x):
    return jnp.maximum(x, 0.0) + jnp.log(1.0 + jnp.exp(-jnp.abs(x)))


def _gelu_tanh(x):
    return 0.5 * x * (1.0 + jnp.tanh(math.sqrt(2.0 / math.pi) * (x + 0.044715 * x * x * x)))


def _mod_kernel(c_ref, w_ref, b_ref, o_ref):
    c = c_ref[...]
    o_ref[...] = _dot(_silu(c), w_ref[...]) + b_ref[...]


def _modulation(cvec, ada_w, ada_b):
    L, D, N = ada_w.shape
    tn = 1024
    return pl.pallas_call(
        _mod_kernel,
        grid=(L, N // tn),
        in_specs=[pl.BlockSpec((SUBLANES, D), lambda l, j: (0, 0)),
                  pl.BlockSpec((None, D, tn), lambda l, j: (l, 0, j)),
                  pl.BlockSpec((None, 1, tn), lambda l, j: (l, 0, j))],
        out_specs=pl.BlockSpec((None, SUBLANES, tn), lambda l, j: (l, 0, j)),
        out_shape=jax.ShapeDtypeStruct((L, SUBLANES, N), F32),
        compiler_params=_cparams(("parallel", "parallel")),
        name="modulation",
    )(cvec, ada_w, ada_b.reshape(L, 1, N))


def _mod_spec(l, which, rows_per_mod, D, ngrid):
    if rows_per_mod is None:
        row = lambda i: 2
    else:
        row = lambda i: i // rows_per_mod
    if ngrid == 1:
        return pl.BlockSpec((None, None, None, 1, D), lambda i: (l, row(i), which, 0, 0))
    return pl.BlockSpec((None, None, None, 1, D), lambda i, j: (l, row(i), which, 0, 0))


def _inproj_kernel(h_ref, g_ref, shift_ref, scale_ref, w_ref, o_ref, xn_ref):
    tm = h_ref.shape[0]

    @pl.when(pl.program_id(1) == 0)
    def _():
        rs = tm // min(INPROJ_SUB, tm // 256)
        for r0 in range(0, tm, rs):
            x = h_ref[r0:r0 + rs, :]
            y = x * lax.rsqrt(jnp.mean(x * x, axis=-1, keepdims=True) + EPS) * g_ref[...]
            xb = (y * (1.0 + scale_ref[...]) + shift_ref[...]).astype(BF16)
            xn_ref[r0:r0 + rs, :] = xb
            o_ref[r0:r0 + rs, :] = jnp.dot(xb, w_ref[...], preferred_element_type=F32)

    @pl.when(pl.program_id(1) > 0)
    def _():
        o_ref[...] = jnp.dot(xn_ref[...], w_ref[...], preferred_element_type=F32)


def _inproj(l, h, gain, mod, w, seq_len, is_ctx):
    M, D = h.shape
    N = w.shape[2]
    tm = min(1024, M)
    tn = 768
    rpm = None if is_ctx else seq_len // tm
    return pl.pallas_call(
        _inproj_kernel,
        grid=(M // tm, N // tn),
        in_specs=[pl.BlockSpec((tm, D), lambda i, j: (i, 0)),
                  pl.BlockSpec((None, 1, D), lambda i, j: (l, 0, 0)),
                  _mod_spec(l, 0, rpm, D, 2),
                  _mod_spec(l, 1, rpm, D, 2),
                  pl.BlockSpec((None, D, tn), lambda i, j: (l, 0, j))],
        out_specs=pl.BlockSpec((tm, tn), lambda i, j: (i, j)),
        out_shape=jax.ShapeDtypeStruct((M, N), F32),
        scratch_shapes=[pltpu.VMEM((tm, D), BF16)],
        compiler_params=_cparams(("parallel", "arbitrary")),
        name="inproj",
    )(h, gain, mod, mod, w)


def _outproj_kernel(ya_ref, yb_ref, yc_ref, yd_ref, w_ref, h_ref, gate_ref, o_ref):
    acc = jnp.dot(ya_ref[...], w_ref[0], preferred_element_type=F32)
    acc += jnp.dot(yb_ref[...], w_ref[1], preferred_element_type=F32)
    acc += jnp.dot(yc_ref[...], w_ref[2], preferred_element_type=F32)
    acc += jnp.dot(yd_ref[...], w_ref[3], preferred_element_type=F32)
    o_ref[...] = h_ref[...] + gate_ref[...] * acc


def _outproj(l, ys, w4, h, mod, seq_len, is_ctx):
    M, D = h.shape
    tm = min(512, M)
    rpm = None if is_ctx else seq_len // tm
    yspec = pl.BlockSpec((tm, GROUP_W), lambda i: (i, 0))
    return pl.pallas_call(
        _outproj_kernel,
        grid=(M // tm,),
        in_specs=[yspec, yspec, yspec, yspec,
                  pl.BlockSpec((None, 4, GROUP_W, D), lambda i: (l, 0, 0, 0)),
                  pl.BlockSpec((tm, D), lambda i: (i, 0)),
                  _mod_spec(l, 2, rpm, D, 1)],
        out_specs=pl.BlockSpec((tm, D), lambda i: (i, 0)),
        out_shape=jax.ShapeDtypeStruct((M, D), F32),
        compiler_params=_cparams(("parallel",)),
        name="outproj",
    )(*ys, w4, h, mod)


def _halo_specs(tm, width, col_block, nrows, ngrid):
    r = tm // SUBLANES
    last = nrows // SUBLANES - 1
    if ngrid == 1:
        prev = pl.BlockSpec((SUBLANES, width), lambda i: (jnp.maximum(i * r - 1, 0), col_block))
        nxt = pl.BlockSpec((SUBLANES, width), lambda i: (jnp.minimum((i + 1) * r, last), col_block))
    else:
        prev = pl.BlockSpec((SUBLANES, width), lambda i, j: (jnp.maximum(i * r - 1, 0), col_block))
        nxt = pl.BlockSpec((SUBLANES, width), lambda i, j: (jnp.minimum((i + 1) * r, last), col_block))
    return prev, nxt


def _ffn_kernel(seq_len, h_ref, hp_ref, hn_ref, g_ref, shift_ref, scale_ref, gate_ref,
                wa_ref, wg_ref, cwa_ref, cwg_ref, cba_ref, cbg_ref, wd_ref, o_ref, xn_ref):
    tm = h_ref.shape[0]
    tf = wa_ref.shape[1]
    i = pl.program_id(0)
    j = pl.program_id(1)

    @pl.when(j == 0)
    def _():
        def norm(x):
            y = x * lax.rsqrt(jnp.mean(x * x, axis=-1, keepdims=True) + EPS) * g_ref[...]
            return (y * (1.0 + scale_ref[...]) + shift_ref[...]).astype(BF16)
        keep_prev = (i * tm) % seq_len != 0
        keep_next = ((i + 1) * tm) % seq_len != 0
        xn_ref[0:SUBLANES, :] = jnp.where(keep_prev, norm(hp_ref[...]), jnp.zeros((), BF16))
        xn_ref[SUBLANES:SUBLANES + tm, :] = norm(h_ref[...])
        xn_ref[SUBLANES + tm:, :] = jnp.where(keep_next, norm(hn_ref[...]), jnp.zeros((), BF16))

    def conv(u, cw_ref, cb_ref, cols):
        lo = u[SUBLANES - 1:SUBLANES - 1 + tm]
        mid = u[SUBLANES:SUBLANES + tm]
        hi = u[SUBLANES + 1:SUBLANES + 1 + tm]
        return lo * cw_ref[0:1, cols] + mid * cw_ref[1:2, cols] + hi * cw_ref[2:3, cols] + cb_ref[:, cols]

    xn = xn_ref[...]
    subs = [slice(s, s + FFN_SUB) for s in range(0, tf, FFN_SUB)]
    ups = [(jnp.dot(xn, wa_ref[:, c], preferred_element_type=F32),
            jnp.dot(xn, wg_ref[:, c], preferred_element_type=F32)) for c in subs]
    acts = [(conv(ua, cwa_ref, cba_ref, c) * _silu(conv(ug, cwg_ref, cbg_ref, c))).astype(BF16)
            for c, (ua, ug) in zip(subs, ups)]
    part = jnp.dot(jnp.concatenate(acts, axis=1), wd_ref[...], preferred_element_type=F32)

    @pl.when(j == 0)
    def _():
        o_ref[...] = part

    @pl.when(j > 0)
    def _():
        o_ref[...] += part

    @pl.when(j == pl.num_programs(1) - 1)
    def _():
        o_ref[...] = h_ref[...] + gate_ref[...] * o_ref[...]


def _ffn(l, h, gain, mod, w_up, conv_w, conv_b, w_down, seq_len, is_ctx):
    M, D = h.shape
    FF = w_down.shape[1]
    tm = min(512, seq_len)
    tf = 1024
    nf = FF // tf
    rpm = None if is_ctx else seq_len // tm
    prev, nxt = _halo_specs(tm, D, 0, M, 2)
    return pl.pallas_call(
        functools.partial(_ffn_kernel, seq_len),
        grid=(M // tm, nf),
        in_specs=[pl.BlockSpec((tm, D), lambda i, j: (i, 0)), prev, nxt,
                  pl.BlockSpec((None, 1, D), lambda i, j: (l, 0, 0)),
                  _mod_spec(l, 3, rpm, D, 2), _mod_spec(l, 4, rpm, D, 2), _mod_spec(l, 5, rpm, D, 2),
                  pl.BlockSpec((None, D, tf), lambda i, j: (l, 0, j)),
                  pl.BlockSpec((None, D, tf), lambda i, j: (l, 0, j + nf)),
                  pl.BlockSpec((None, 3, tf), lambda i, j: (l, 0, j)),
                  pl.BlockSpec((None, 3, tf), lambda i, j: (l, 0, j + nf)),
                  pl.BlockSpec((None, 1, tf), lambda i, j: (l, 0, j)),
                  pl.BlockSpec((None, 1, tf), lambda i, j: (l, 0, j + nf)),
                  pl.BlockSpec((None, tf, D), lambda i, j: (l, j, 0))],
        out_specs=pl.BlockSpec((tm, D), lambda i, j: (i, 0)),
        out_shape=jax.ShapeDtypeStruct((M, D), F32),
        scratch_shapes=[pltpu.VMEM((tm + 2 * SUBLANES, D), BF16)],
        compiler_params=_cparams(("parallel", "arbitrary")),
        name="conv_ffn",
    )(h, h, h, gain, mod, mod, mod, w_up, w_up, conv_w, conv_w, conv_b, conv_b, w_down)


def _conv4(xp_ref, x_ref, xn_ref, w_ref, start, seq_len):
    tm = x_ref.shape[0]
    xp = jnp.where(start != 0, xp_ref[...], 0.0)
    xn = jnp.where(start + tm != seq_len, xn_ref[...], 0.0)
    xe = jnp.concatenate([xp, x_ref[...], xn], axis=0)
    t0 = xe[SUBLANES - 1:SUBLANES - 1 + tm]
    t1 = xe[SUBLANES:SUBLANES + tm]
    t2 = xe[SUBLANES + 1:SUBLANES + 1 + tm]
    t3 = xe[SUBLANES + 2:SUBLANES + 2 + tm]
    return t0 * w_ref[0:1, :] + t1 * w_ref[1:2, :] + t2 * w_ref[2:3, :] + t3 * w_ref[3:4, :]


def _shift_rows(x, k, fill, rev):
    n = x.shape[0]
    if k % SUBLANES == 0:
        pad = jnp.full((k, x.shape[1]), fill, x.dtype)
        return jnp.concatenate([x[k:], pad], 0) if rev else jnp.concatenate([pad, x[:n - k]], 0)
    row = lax.broadcasted_iota(jnp.int32, x.shape, 0)
    if rev:
        return jnp.where(row >= n - k, fill, pltpu.roll(x, n - k, 0))
    return jnp.where(row < k, fill, pltpu.roll(x, k, 0))


def _lru_kernel(rev, finish, seq_len, *refs):
    if finish:
        (x_ref, xp_ref, xn_ref, gate_ref, hf_ref, cw_ref, cb_ref, wcat_ref, ba_ref, bx_ref,
         lam_ref, h0_ref, out_ref, st_ref, carry_ref) = refs
    else:
        (x_ref, xp_ref, xn_ref, cw_ref, cb_ref, wcat_ref, ba_ref, bx_ref,
         lam_ref, h0_ref, out_ref, st_ref, carry_ref) = refs
    tc = x_ref.shape[0]
    j = pl.program_id(1)
    nch = pl.num_programs(1)
    c = (nch - 1 - j) if rev else j

    @pl.when(j == 0)
    def _():
        carry_ref[...] = h0_ref[...]

    u = _conv4(xp_ref, x_ref, xn_ref, cw_ref, c * tc, seq_len) + cb_ref[...]
    rs, is_ = [], []
    for n in range(HEADS):
        z = _dot(u[:, n * HEAD_DIM:(n + 1) * HEAD_DIM], wcat_ref[n])
        rs.append(z[:, :HEAD_DIM])
        is_.append(z[:, HEAD_DIM:])
    r = _sigmoid(jnp.concatenate(rs, axis=1) + ba_ref[...])
    ig = _sigmoid(jnp.concatenate(is_, axis=1) + bx_ref[...])
    log_a = -LRU_C * r * _softplus(-lam_ref[...])
    a = jnp.exp(log_a)
    om = 1.0 - a * a
    b = om * lax.rsqrt(jnp.maximum(om, TINY)) * ig * u
    nslab = tc // SUBLANES
    a = a.reshape(nslab, SUBLANES, a.shape[1])
    b = b.reshape(nslab, SUBLANES, b.shape[1])
    sub = lax.broadcasted_iota(jnp.int32, a.shape, 1)
    k = 1
    while k < SUBLANES:
        own = (sub >= SUBLANES - k) if rev else (sub < k)
        shift = SUBLANES - k if rev else k
        b = a * jnp.where(own, 0.0, pltpu.roll(b, shift, 1)) + b
        a = a * jnp.where(own, 1.0, pltpu.roll(a, shift, 1))
        k *= 2
    a = a.reshape(tc, a.shape[2])
    b = b.reshape(tc, b.shape[2])
    state = carry_ref[0:1, :]
    hs = [None] * nslab
    for t in range(nslab):
        s = nslab - 1 - t if rev else t
        rows = slice(s * SUBLANES, (s + 1) * SUBLANES)
        hs[s] = b[rows] + a[rows] * state
        state = hs[s][0:1, :] if rev else hs[s][SUBLANES - 1:SUBLANES, :]
    h = jnp.concatenate(hs, axis=0)
    last = state
    carry_ref[...] = jnp.broadcast_to(last, carry_ref.shape)
    st_ref[...] = jnp.broadcast_to(last, st_ref.shape)
    if finish:
        out_ref[...] = ((hf_ref[...] + h) * _gelu_tanh(gate_ref[...])).astype(out_ref.dtype)
    else:
        out_ref[...] = h


def _lru_pass(P, seq_len, rev, h0, wts, hf=None):
    cw, cb, wcat, ba, bx, lam = wts
    M = P.shape[0]
    B = M // seq_len
    tc = 256
    nch = seq_len // tc
    finish = hf is not None
    W = GROUP_W
    xb = P_LRU_X // W
    gb = P_LRU_GATE // W
    r = tc // SUBLANES
    last = M // SUBLANES - 1

    def chunk(j):
        return (nch - 1 - j) if rev else j

    row = lambda b, j: (b * nch + chunk(j), xb)
    specs = [pl.BlockSpec((tc, W), row),
             pl.BlockSpec((SUBLANES, W), lambda b, j: (jnp.maximum((b * nch + chunk(j)) * r - 1, 0), xb)),
             pl.BlockSpec((SUBLANES, W), lambda b, j: (jnp.minimum((b * nch + chunk(j) + 1) * r, last), xb))]
    args = [P, P, P]
    if finish:
        specs += [pl.BlockSpec((tc, W), lambda b, j: (b * nch + chunk(j), gb)),
                  pl.BlockSpec((tc, W), lambda b, j: (b * nch + chunk(j), 0))]
        args += [P, hf]
    full2 = lambda b, j: (0, 0)
    specs += [pl.BlockSpec((4, W), full2), pl.BlockSpec((1, W), full2),
              pl.BlockSpec((HEADS, HEAD_DIM, 2 * HEAD_DIM), lambda b, j: (0, 0, 0)),
              pl.BlockSpec((1, W), full2), pl.BlockSpec((1, W), full2), pl.BlockSpec((1, W), full2),
              pl.BlockSpec((None, SUBLANES, W), lambda b, j: (b, 0, 0))]
    args += [cw, cb, wcat, ba, bx, lam, h0]
    out_dtype = BF16 if finish else F32
    return pl.pallas_call(
        functools.partial(_lru_kernel, rev, finish, seq_len),
        grid=(B, nch),
        in_specs=specs,
        out_specs=[pl.BlockSpec((tc, W), lambda b, j: (b * nch + chunk(j), 0)),
                   pl.BlockSpec((None, SUBLANES, W), lambda b, j: (b, 0, 0))],
        out_shape=[jax.ShapeDtypeStruct((M, W), out_dtype),
                   jax.ShapeDtypeStruct((B, SUBLANES, W), F32)],
        scratch_shapes=[pltpu.VMEM((SUBLANES, W), F32)],
        compiler_params=_cparams(("parallel", "arbitrary")),
        name="rglru_bwd" if rev else "rglru_fwd",
    )(*args)


def _mixer_lru(Pc, Pl, ctx_len, seq_len, wts_f, wts_b):
    B = Pc.shape[0] // ctx_len
    zero = jnp.zeros((B, SUBLANES, GROUP_W), F32)
    hc_f, sc_f = _lru_pass(Pc, ctx_len, False, zero, wts_f)
    hl_f, _ = _lru_pass(Pl, seq_len, False, sc_f, wts_f)
    yc, sc_b = _lru_pass(Pc, ctx_len, True, zero, wts_b, hf=hc_f)
    yl, _ = _lru_pass(Pl, seq_len, True, sc_b, wts_b, hf=hl_f)
    return yc, yl


def _rope_mix(x, cos, sin):
    lane = lax.broadcasted_iota(jnp.int32, x.shape, 1)
    half = MLA_ROPE // 2
    swapped = jnp.where(lane < half, pltpu.roll(x, LANES - half, 1), pltpu.roll(x, half, 1))
    return x * cos + swapped * sin


def _mla_prep_kernel(use_rope, x_ref, qag_ref, wuq_ref, kvg_ref, wukv_ref, qg_ref, kg_ref,
                     cos_ref, sin_ref, qt_out, k_out, vt_out):
    x = x_ref[...]

    def rms(v, g):
        return v * lax.rsqrt(jnp.mean(v * v, axis=-1, keepdims=True) + EPS) * g

    q = _dot(rms(x[:, :MLA_Q_LORA], qag_ref[...]), wuq_ref[...])
    kv = _dot(rms(x[:, MLA_Q_LORA:MLA_Q_LORA + MLA_KV_LORA], kvg_ref[...]), wukv_ref[...])
    tail = x[:, MLA_Q_LORA + MLA_KV_LORA:]
    lane = lax.broadcasted_iota(jnp.int32, tail.shape, 1)
    kr = jnp.where(lane < MLA_ROPE, tail, 0.0)
    kr_ss = jnp.sum(kr * kr, axis=-1, keepdims=True)
    scale = MLA_QK ** -0.5 * LOG2E
    for h in range(HEADS):
        qh = q[:, h * MLA_QPAD:(h + 1) * MLA_QPAD]
        inv = lax.rsqrt(jnp.sum(qh * qh, axis=-1, keepdims=True) / MLA_QK + EPS) * scale
        qh = qh * inv * qg_ref[...]
        qn, qr = qh[:, :LANES], qh[:, LANES:]
        kn = kv[:, h * 2 * HEAD_DIM:h * 2 * HEAD_DIM + MLA_NOPE]
        vh = kv[:, h * 2 * HEAD_DIM + MLA_NOPE:(h + 1) * 2 * HEAD_DIM]
        kinv = lax.rsqrt((jnp.sum(kn * kn, axis=-1, keepdims=True) + kr_ss) / MLA_QK + EPS)
        kn = kn * kinv * kg_ref[:, :LANES]
        krh = kr * kinv * kg_ref[:, LANES:]
        if use_rope:
            qr = _rope_mix(qr, cos_ref[...], sin_ref[...])
            krh = _rope_mix(krh, cos_ref[...], sin_ref[...])
        qt_out[h] = jnp.concatenate([qn, qr], axis=1).T.astype(BF16)
        k_out[h, :, :LANES] = kn.astype(BF16)
        k_out[h, :, LANES:] = krh.astype(BF16)
        vt_out[h] = vh.T.astype(BF16)


def _mla_prep(P, seq_len, use_rope, wts, cosf, sinf):
    qag, wuq, kvg, wukv, qg, kg = wts
    M = P.shape[0]
    tm = min(512, seq_len)
    nt = seq_len // tm
    full = lambda i: (0, 0)
    wmla = P_COLS - P_MLA
    tab = pl.BlockSpec((tm, LANES), (lambda i: (i % nt, 0)) if use_rope else (lambda i: (0, 0)))
    tspec = lambda w: pl.BlockSpec((HEADS, w, tm), lambda i: (0, 0, i))
    return pl.pallas_call(
        functools.partial(_mla_prep_kernel, use_rope),
        grid=(M // tm,),
        in_specs=[pl.BlockSpec((tm, wmla), lambda i: (i, P_MLA // wmla)),
                  pl.BlockSpec((1, MLA_Q_LORA), full), pl.BlockSpec(wuq.shape, full),
                  pl.BlockSpec((1, MLA_KV_LORA), full), pl.BlockSpec(wukv.shape, full),
                  pl.BlockSpec((1, MLA_QPAD), full), pl.BlockSpec((1, MLA_QPAD), full),
                  tab, tab],
        out_specs=[tspec(MLA_QPAD), pl.BlockSpec((HEADS, tm, MLA_QPAD), lambda i: (0, i, 0)),
                   tspec(HEAD_DIM)],
        out_shape=[jax.ShapeDtypeStruct((HEADS, MLA_QPAD, M), BF16),
                   jax.ShapeDtypeStruct((HEADS, M, MLA_QPAD), BF16),
                   jax.ShapeDtypeStruct((HEADS, HEAD_DIM, M), BF16)],
        compiler_params=_cparams(("parallel",)),
        name="mla_prep",
    )(P, qag, wuq, kvg, wukv, qg, kg, cosf, sinf)


def _flash_kernel(tk, has_lat, *refs):
    if has_lat:
        qt_ref, kc_ref, vct_ref, kl_ref, vlt_ref, o_ref, m_ref, l_ref, acc_ref, s_ref = refs
    else:
        qt_ref, kc_ref, vct_ref, o_ref = refs
    qt = qt_ref[...]
    s = jnp.dot(kc_ref[...], qt, preferred_element_type=F32)
    if has_lat:
        nk = kl_ref.shape[0] // tk

        def scores(c, slot):
            off = pl.multiple_of(jnp.minimum(c, nk - 1) * tk, tk)
            s_ref[slot] = jnp.dot(kl_ref[pl.ds(off, tk), :], qt, preferred_element_type=F32)

        scores(0, 0)
    m = jnp.max(s, axis=0, keepdims=True)
    p = jnp.exp2(s - m)
    l = jnp.sum(p, axis=0, keepdims=True)
    acc = jnp.dot(vct_ref[...], p.astype(BF16), preferred_element_type=F32)
    if has_lat:
        m_ref[...] = m
        l_ref[...] = l
        acc_ref[...] = acc

        def step(c, slot):
            scores(c + 1, 1 - slot)
            off = pl.multiple_of(c * tk, tk)
            s = s_ref[slot]
            m_old = m_ref[...]
            m_new = jnp.maximum(m_old, jnp.max(s, axis=0, keepdims=True))
            alpha = jnp.exp2(m_old - m_new)
            p = jnp.exp2(s - m_new)
            l_ref[...] = alpha * l_ref[...] + jnp.sum(p, axis=0, keepdims=True)
            acc_ref[...] = alpha * acc_ref[...] + jnp.dot(
                vlt_ref[:, pl.ds(off, tk)], p.astype(BF16), preferred_element_type=F32)
            m_ref[...] = m_new

        unroll = min(FLASH_UNROLL, nk)

        def body(j, carry):
            for u in range(unroll):
                step(unroll * j + u, u % 2)
            return carry

        lax.fori_loop(0, nk // unroll, body, 0)
        acc = acc_ref[...]
        l = l_ref[...]
    o_ref[...] = (acc / l).T.astype(o_ref.dtype)


def _flash_attention(Qt, Kc, Vct, seq_len, ctx_len, Kl=None, Vlt=None):
    H, dq, M = Qt.shape
    B = M // seq_len
    has_lat = Kl is not None
    tq = min(1024, seq_len)
    tk = 512
    nq = seq_len // tq
    specs = [pl.BlockSpec((None, dq, tq), lambda b, h, i: (h, 0, b * nq + i)),
             pl.BlockSpec((None, ctx_len, dq), lambda b, h, i: (h, b, 0)),
             pl.BlockSpec((None, HEAD_DIM, ctx_len), lambda b, h, i: (h, 0, b))]
    args = [Qt, Kc, Vct]
    scratch = []
    if has_lat:
        specs += [pl.BlockSpec((None, seq_len, dq), lambda b, h, i: (h, b, 0)),
                  pl.BlockSpec((None, HEAD_DIM, seq_len), lambda b, h, i: (h, 0, b))]
        args += [Kl, Vlt]
        nk = seq_len // tk
        assert nk % 2 == 0 and nk % min(FLASH_UNROLL, nk) == 0
        scratch = [pltpu.VMEM((1, tq), F32), pltpu.VMEM((1, tq), F32), pltpu.VMEM((HEAD_DIM, tq), F32),
                   pltpu.VMEM((2, tk, tq), F32)]
    return pl.pallas_call(
        functools.partial(_flash_kernel, tk, has_lat),
        grid=(B, H, nq),
        in_specs=specs,
        out_specs=pl.BlockSpec((tq, HEAD_DIM), lambda b, h, i: (b * nq + i, h)),
        out_shape=jax.ShapeDtypeStruct((M, GROUP_W), BF16),
        scratch_shapes=scratch,
        compiler_params=_cparams(("parallel", "parallel", "arbitrary")),
        name="mla_attention" if has_lat else "ctx_attention",
    )(*args)


def _na_prep_kernel(x_ref, qg_ref, kg_ref, qt_out, k_out, vt_out):
    x = x_ref[...]
    scale = HEAD_DIM ** -0.5 * LOG2E
    for h in range(HEADS):
        qh = x[:, h * HEAD_DIM:(h + 1) * HEAD_DIM]
        kh = x[:, GROUP_W + h * HEAD_DIM:GROUP_W + (h + 1) * HEAD_DIM]
        vh = x[:, 2 * GROUP_W + h * HEAD_DIM:2 * GROUP_W + (h + 1) * HEAD_DIM]
        qh = qh * (lax.rsqrt(jnp.mean(qh * qh, axis=-1, keepdims=True) + EPS) * scale) * qg_ref[...]
        kh = kh * lax.rsqrt(jnp.mean(kh * kh, axis=-1, keepdims=True) + EPS) * kg_ref[...]
        qt_out[h] = qh.T.astype(BF16)
        k_out[h] = kh.astype(BF16)
        vt_out[h] = vh.T.astype(BF16)


def _na_prep(P, qg, kg):
    M = P.shape[0]
    tm = 512
    w = 3 * GROUP_W
    hspec = pl.BlockSpec((HEADS, tm, HEAD_DIM), lambda i: (0, i, 0))
    tspec = pl.BlockSpec((HEADS, HEAD_DIM, tm), lambda i: (0, 0, i))
    sds = jax.ShapeDtypeStruct((HEADS, M, HEAD_DIM), BF16)
    tds = jax.ShapeDtypeStruct((HEADS, HEAD_DIM, M), BF16)
    return pl.pallas_call(
        _na_prep_kernel,
        grid=(M // tm,),
        in_specs=[pl.BlockSpec((tm, w), lambda i: (i, P_NA_QKV // w)),
                  pl.BlockSpec((1, HEAD_DIM), lambda i: (0, 0)),
                  pl.BlockSpec((1, HEAD_DIM), lambda i: (0, 0))],
        out_specs=[tspec, hspec, tspec],
        out_shape=[tds, sds, tds],
        compiler_params=_cparams(("parallel",)),
        name="na_prep",
    )(P, qg, kg)


def _na_bias_kernel(rows, rpb_ref, o_ref):
    h = pl.program_id(0)
    kc = lax.broadcasted_iota(jnp.int32, (GRID_W, GRID_W), 0)
    qc = lax.broadcasted_iota(jnp.int32, (GRID_W, GRID_W), 1)
    dc = jnp.clip(kc - qc + (NA_KW - 1), 0, 2 * NA_KW - 2)
    c0 = jnp.clip(qc - NA_KW // 2, 0, GRID_W - NA_KW)
    col_ok = (kc >= c0) & (kc < c0 + NA_KW)
    neg = jnp.full((GRID_W, GRID_W), NEG_BIG, F32)
    tiles = []
    for dr in range(2 * NA_KH - 1):
        t = jnp.zeros((GRID_W, GRID_W), F32)
        for d in range(2 * NA_KW - 1):
            t = t + jnp.where(dc == d, rpb_ref[h, dr, d], 0.0)
        tiles.append(jnp.where(col_ok, t * LOG2E, NEG_BIG))
    for kind in range(3):
        q_base = (0, NA_KH // 2, rows - NA_BAND)[kind]
        w_base = (0, 0, rows - NA_WIN)[kind]
        for qr in range(NA_BAND):
            r = q_base + qr
            if kind == 1:
                r0 = r - NA_KH // 2
            else:
                r0 = min(max(r - NA_KH // 2, 0), rows - NA_KH)
            for kr in range(NA_WIN):
                ka = w_base + kr
                ok = r0 <= ka < r0 + NA_KH
                blk = tiles[ka - r + NA_KH - 1] if ok else neg
                o_ref[kind, kr * GRID_W:(kr + 1) * GRID_W, qr * GRID_W:(qr + 1) * GRID_W] = blk


def _na_bias(rpb, rows):
    H = rpb.shape[0]
    nq, nk = NA_BAND * GRID_W, NA_WIN * GRID_W
    return pl.pallas_call(
        functools.partial(_na_bias_kernel, rows),
        grid=(H,),
        in_specs=[pl.BlockSpec(memory_space=pltpu.SMEM)],
        out_specs=pl.BlockSpec((None, 3, nk, nq), lambda h: (h, 0, 0, 0)),
        out_shape=jax.ShapeDtypeStruct((H, 3, nk, nq), F32),
        compiler_params=_cparams(("parallel",)),
        name="na_bias",
    )(rpb)


def _na_kernel(seq_len, qt_ref, k_ref, vt_ref, kc_ref, vct_ref, bias0_ref, bias1_ref, o_ref):
    nq = bias0_ref.shape[1]
    nk = bias0_ref.shape[0]
    j = pl.program_id(2)
    kc = kc_ref[...]
    scores = []
    for t, bias_ref in enumerate((bias0_ref, bias1_ref)):
        base = jnp.clip((2 * j + t) * nq - (NA_KH // 2) * GRID_W, 0, seq_len - nk)
        base = pl.multiple_of(base, (NA_KH // 2) * GRID_W)
        qt = qt_ref[:, t * nq:(t + 1) * nq]
        s_ctx = jnp.dot(kc, qt, preferred_element_type=F32)
        s_loc = jnp.dot(k_ref[pl.ds(base, nk), :], qt, preferred_element_type=F32) + bias_ref[...]
        scores.append((base, s_loc, s_ctx))
    for t, (base, s_loc, s_ctx) in enumerate(scores):
        m = jnp.maximum(jnp.max(s_loc, axis=0, keepdims=True), jnp.max(s_ctx, axis=0, keepdims=True))
        p_loc = jnp.exp2(s_loc - m)
        p_ctx = jnp.exp2(s_ctx - m)
        l = jnp.sum(p_loc, axis=0, keepdims=True) + jnp.sum(p_ctx, axis=0, keepdims=True)
        o = (jnp.dot(vt_ref[:, pl.ds(base, nk)], p_loc.astype(BF16), preferred_element_type=F32)
             + jnp.dot(vct_ref[...], p_ctx.astype(BF16), preferred_element_type=F32))
        o_ref[t * nq:(t + 1) * nq, :] = (o / l).T.astype(o_ref.dtype)


def _na_attention(Qtl, Kl, Vtl, Kc, Vtc, bias, seq_len, ctx_len):
    H, d, M = Qtl.shape
    B = M // seq_len
    nq, nk = NA_BAND * GRID_W, NA_WIN * GRID_W
    nb = seq_len // nq
    assert nb % 2 == 0
    nb2 = nb // 2

    def kind(band):
        return jnp.where(band == 0, 0, jnp.where(band == nb - 1, 2, 1))

    return pl.pallas_call(
        functools.partial(_na_kernel, seq_len),
        grid=(B, H, nb2),
        in_specs=[pl.BlockSpec((None, d, 2 * nq), lambda b, h, j: (h, 0, b * nb2 + j)),
                  pl.BlockSpec((None, seq_len, d), lambda b, h, j: (h, b, 0)),
                  pl.BlockSpec((None, d, seq_len), lambda b, h, j: (h, 0, b)),
                  pl.BlockSpec((None, ctx_len, d), lambda b, h, j: (h, b, 0)),
                  pl.BlockSpec((None, d, ctx_len), lambda b, h, j: (h, 0, b)),
                  pl.BlockSpec((None, None, nk, nq), lambda b, h, j: (h, kind(2 * j), 0, 0)),
                  pl.BlockSpec((None, None, nk, nq), lambda b, h, j: (h, kind(2 * j + 1), 0, 0))],
        out_specs=pl.BlockSpec((2 * nq, d), lambda b, h, j: (b * nb2 + j, h)),
        out_shape=jax.ShapeDtypeStruct((M, GROUP_W), BF16),
        compiler_params=_cparams(("parallel", "parallel", "arbitrary")),
        name="na_attention",
    )(Qtl, Kl, Vtl, Kc, Vtc, bias, bias)


def _gdn_prep_kernel(seq_len, x_ref, xp_ref, xn_ref, tail_ref, cw_ref, alog_ref, dtb_ref,
                     qkv_out, gates_out):
    tm = x_ref.shape[0]
    y = _silu(_conv4(xp_ref, x_ref, xn_ref, cw_ref, (pl.program_id(0) * tm) % seq_len, seq_len))
    for h in range(HEADS):
        sl = slice(h * HEAD_DIM, (h + 1) * HEAD_DIM)
        qh = y[:, sl]
        qkv_out[:, sl] = qh * (lax.rsqrt(jnp.sum(qh * qh, axis=-1, keepdims=True) + EPS) * HEAD_DIM ** -0.5)
        sl = slice(GROUP_W + h * HEAD_DIM, GROUP_W + (h + 1) * HEAD_DIM)
        kh = y[:, sl]
        qkv_out[:, sl] = kh * lax.rsqrt(jnp.sum(kh * kh, axis=-1, keepdims=True) + EPS)
    qkv_out[:, 2 * GROUP_W:] = y[:, 2 * GROUP_W:]
    t = tail_ref[...]
    lane = lax.broadcasted_iota(jnp.int32, t.shape, 1)
    alpha = pltpu.roll(t, LANES - TAIL_ALPHA, 1)
    beta = pltpu.roll(t, LANES - TAIL_BETA + 2 * HEADS, 1)
    g = -jnp.exp(alog_ref[...]) * _softplus(alpha + dtb_ref[...])
    gates_out[...] = jnp.where(lane < 2 * HEADS, g, jnp.where(lane < 4 * HEADS, _sigmoid(beta), 0.0))


def _gdn_prep(P, seq_len, cw, alog, dtb):
    M = P.shape[0]
    tm = min(512, seq_len)
    w = 3 * GROUP_W
    prev, nxt = _halo_specs(tm, w, 0, M, 1)
    full = lambda i: (0, 0)
    return pl.pallas_call(
        functools.partial(_gdn_prep_kernel, seq_len),
        grid=(M // tm,),
        in_specs=[pl.BlockSpec((tm, w), lambda i: (i, 0)), prev, nxt,
                  pl.BlockSpec((tm, LANES), lambda i: (i, P_TAIL // LANES)),
                  pl.BlockSpec((4, w), full), pl.BlockSpec((1, LANES), full), pl.BlockSpec((1, LANES), full)],
        out_specs=[pl.BlockSpec((tm, w), lambda i: (i, 0)), pl.BlockSpec((tm, LANES), lambda i: (i, 0))],
        out_shape=[jax.ShapeDtypeStruct((M, w), F32), jax.ShapeDtypeStruct((M, LANES), F32)],
        compiler_params=_cparams(("parallel",)),
        name="gdn_prep",
    )(P, P, P, P, cw, alog, dtb)


def _time_cumsum(x, rev):
    k = 1
    while k < x.shape[0]:
        x = x + _shift_rows(x, k, 0.0, rev)
        k *= 2
    return x


def _gdn_local_kernel(qkv_ref, gates_ref, a_out, qk_out):
    C = GDN_CHUNK
    ri = lax.broadcasted_iota(jnp.int32, (C, C), 0)
    ci = lax.broadcasted_iota(jnp.int32, (C, C), 1)
    for c in range(qkv_ref.shape[0] // C):
        rows = slice(c * C, (c + 1) * C)
        gt = gates_ref[rows, :]
        zs = []
        for h in range(HEADS):
            q = qkv_ref[rows, h * HEAD_DIM:(h + 1) * HEAD_DIM]
            kk = qkv_ref[rows, GROUP_W + h * HEAD_DIM:GROUP_W + (h + 1) * HEAD_DIM]
            zs.append(_dot_nt(jnp.concatenate([kk, q], axis=0), kk))
        for d in range(2):
            rev = d == 1
            incl = (ci >= ri) if rev else (ci <= ri)
            strict = (ci > ri) if rev else (ci < ri)
            gcum = _time_cumsum(gt, rev)
            grow = gcum.T
            for h in range(HEADS):
                lg = d * HEADS + h
                gc_c = gcum[:, lg:lg + 1]
                gc_r = grow[lg:lg + 1, :]
                beta = gt[:, 2 * HEADS + lg:2 * HEADS + lg + 1]
                decay = jnp.where(incl, jnp.exp(jnp.where(incl, gc_c - gc_r, 0.0)), 0.0)
                lanes = slice((h % 2) * C, (h % 2 + 1) * C)
                a_out[d, c * 2 + h // 2, :, lanes] = jnp.where(strict, zs[h][:C] * beta * decay, 0.0)
                qk_out[d, c * 2 + h // 2, :, lanes] = zs[h][C:] * decay


def _gdn_local(qkv, gates):
    M = qkv.shape[0]
    tm = 256
    npair = tm // GDN_CHUNK * (HEADS // 2)
    w = 3 * GROUP_W
    ospec = pl.BlockSpec((2, npair, GDN_CHUNK, LANES), lambda i: (0, i, 0, 0))
    sds = jax.ShapeDtypeStruct((2, M // GDN_CHUNK * (HEADS // 2), GDN_CHUNK, LANES), F32)
    return pl.pallas_call(
        _gdn_local_kernel,
        grid=(M // tm,),
        in_specs=[pl.BlockSpec((tm, w), lambda i: (i, 0)), pl.BlockSpec((tm, LANES), lambda i: (i, 0))],
        out_specs=[ospec, ospec],
        out_shape=[sds, sds],
        compiler_params=_cparams(("parallel",)),
        name="gdn_local",
    )(qkv, gates)


SOLVE_BATCH = 128


def _gdn_solve_kernel(upper, a_ref, o_ref, at_ref, m_ref):
    C = GDN_CHUNK
    ns = C // SUBLANES
    for i in range(C):
        xt = a_ref[pl.ds(i, SOLVE_BATCH, stride=C), :].T
        at_ref[i, 0] = xt[:C]
        at_ref[i, 1] = xt[C:]
    sub = lax.broadcasted_iota(jnp.int32, (SUBLANES, SOLVE_BATCH), 0)
    for t in range(C):
        i = C - 1 - t if upper else t
        si = i // SUBLANES
        slabs = list(range(si, ns)) if upper else list(range(si + 1))
        unit = jnp.where(sub == i % SUBLANES, 1.0, 0.0).astype(F32)
        zero = jnp.zeros((SUBLANES, SOLVE_BATCH), F32)
        init = tuple(unit if s == si else zero for _ in range(2) for s in slabs)

        def col_body(j, acc):
            new = []
            for hp in range(2):
                a = jnp.broadcast_to(at_ref[i, hp, pl.ds(j, 1), :], (SUBLANES, SOLVE_BATCH))
                for n, s in enumerate(slabs):
                    new.append(acc[hp * len(slabs) + n] - a * m_ref[hp, j, s * SUBLANES:(s + 1) * SUBLANES, :])
            return tuple(new)

        lo, hi = (i + 1, C) if upper else (0, i)
        if hi > lo:
            acc = lax.fori_loop(lo, hi, col_body, init, unroll=min(4, hi - lo))
        else:
            acc = init
        for hp in range(2):
            for s in range(ns):
                val = acc[hp * len(slabs) + slabs.index(s)] if s in slabs else zero
                m_ref[hp, i, s * SUBLANES:(s + 1) * SUBLANES, :] = val
    for i in range(C):
        y = jnp.concatenate([m_ref[0, i], m_ref[1, i]], axis=0)
        o_ref[i] = y.T


def _gdn_solve(a_all):
    _, NP, C, _ = a_all.shape
    rows = SOLVE_BATCH * C
    a2 = a_all.reshape(2, NP * C, LANES)
    outs = []
    for d in range(2):
        spec = pl.BlockSpec((None, rows, LANES), lambda b, d=d: (d, b, 0))
        outs.append(pl.pallas_call(
            functools.partial(_gdn_solve_kernel, d == 1),
            grid=(NP // SOLVE_BATCH,),
            in_specs=[spec],
            out_specs=pl.BlockSpec((None, C, SOLVE_BATCH, LANES), lambda b: (b, 0, 0, 0)),
            out_shape=jax.ShapeDtypeStruct((NP // SOLVE_BATCH, C, SOLVE_BATCH, LANES), F32),
            scratch_shapes=[pltpu.VMEM((C, 2, C, SOLVE_BATCH), F32), pltpu.VMEM((2, C, C, SOLVE_BATCH), F32)],
            compiler_params=_cparams(("parallel",)),
            name="gdn_solve_upper" if d else "gdn_solve_lower",
        )(a2))
    return outs


def _gdn_kernel(rev, finish, d, *refs):
    if finish:
        (qkv_ref, gates_ref, qk_ref, s0_ref, of_ref, gate_ref, ng_ref, *tinv_refs) = refs[:-4]
        out_ref, sout_ref, s_ref, o_ref = refs[-4:]
    else:
        (qkv_ref, gates_ref, qk_ref, s0_ref, *tinv_refs) = refs[:-3]
        out_ref, sout_ref, s_ref = refs[-3:]
        o_ref = out_ref
    C = GDN_CHUNK
    B = qkv_ref.shape[0]
    nchunk = qkv_ref.shape[1] // C
    j = pl.program_id(0)

    @pl.when(j == 0)
    def _():
        s_ref[...] = s0_ref[...]

    last_row = 0 if rev else C - 1
    order = [(nchunk - 1 - t) if rev else t for t in range(nchunk)]
    units = [(b, h) for b in range(B) for h in range(HEADS)]

    def local(c):
        rows = slice(c * C, (c + 1) * C)
        out = {}
        for b in range(B):
            gt = gates_ref[b, rows, :]
            gcum = _time_cumsum(gt, rev)
            for h in range(HEADS):
                lg = d * HEADS + h
                lanes = slice((h % 2) * C, (h % 2 + 1) * C)
                gc_c = gcum[:, lg:lg + 1]
                beta = gt[:, 2 * HEADS + lg:2 * HEADS + lg + 1]
                gl = gcum[last_row:last_row + 1, lg:lg + 1]
                q = qkv_ref[b, rows, h * HEAD_DIM:(h + 1) * HEAD_DIM]
                kk = qkv_ref[b, rows, GROUP_W + h * HEAD_DIM:GROUP_W + (h + 1) * HEAD_DIM]
                v = qkv_ref[b, rows, 2 * GROUP_W + h * HEAD_DIM:2 * GROUP_W + (h + 1) * HEAD_DIM]
                tinv = tinv_refs[b][:, c * 2 + h // 2, lanes]
                qk = qk_ref[b, c * 2 + h // 2, :, lanes]
                kb = kk * beta
                eg = jnp.exp(gc_c)
                uw = _dot(tinv, jnp.concatenate([v * beta, kb * eg], axis=1))
                wq = jnp.concatenate([uw[:, HEAD_DIM:], q * eg], axis=0).astype(BF16)
                kdt = (kk * jnp.exp(gl - gc_c)).T
                qkk = jnp.concatenate([qk, kdt], axis=0).astype(BF16)
                out[b, h] = (uw[:, :HEAD_DIM], wq, qkk, jnp.exp(gl))
        return out

    states = {(b, h): s_ref[b, h] for b, h in units}
    nxt = local(order[0])
    for t, c in enumerate(order):
        cur = nxt
        ws = {k: jnp.dot(cur[k][1], states[k].astype(BF16), preferred_element_type=F32) for k in units}
        if t + 1 < nchunk:
            nxt = local(order[t + 1])
        for b, h in units:
            u, _, qkk, decay = cur[b, h]
            v_new = u - ws[b, h][:C]
            res = jnp.dot(qkk, v_new.astype(BF16), preferred_element_type=F32)
            o_ref[b, c * C:(c + 1) * C, h * HEAD_DIM:(h + 1) * HEAD_DIM] = ws[b, h][C:] + res[:C]
            states[b, h] = states[b, h] * decay + res[C:]
    for b, h in units:
        s_ref[b, h] = states[b, h]
    sout_ref[...] = s_ref[...]
    if finish:
        for b in range(B):
            tot = of_ref[b] + o_ref[b]
            gate = gate_ref[b]
            for h in range(HEADS):
                sl = slice(h * HEAD_DIM, (h + 1) * HEAD_DIM)
                x = tot[:, sl]
                y = x * lax.rsqrt(jnp.mean(x * x, axis=-1, keepdims=True) + EPS) * ng_ref[...]
                out_ref[b, :, sl] = (y * _silu(gate[:, sl])).astype(out_ref.dtype)


def _gdn_pass(qkv, gates, tinv_all, tinv_off, qk, seq_len, rev, d, s0, of=None, P=None, ng=None):
    M = qkv.shape[0]
    B = M // seq_len
    rb = 256
    nblk = seq_len // rb
    finish = of is not None
    w = 3 * GROUP_W
    npair = rb // GDN_CHUNK * (HEADS // 2)
    toff = tinv_off // npair
    per_batch = SOLVE_BATCH // npair

    def blk(j):
        return (nblk - 1 - j) if rev else j

    sspec = pl.BlockSpec((B, HEADS, HEAD_DIM, HEAD_DIM), lambda j: (0, 0, 0, 0))
    specs = [pl.BlockSpec((B, rb, w), lambda j: (0, blk(j), 0)),
             pl.BlockSpec((B, rb, LANES), lambda j: (0, blk(j), 0)),
             pl.BlockSpec((None, B, npair, GDN_CHUNK, LANES), lambda j: (d, 0, blk(j), 0, 0)),
             sspec]
    args = [qkv.reshape(B, seq_len, w), gates.reshape(B, seq_len, LANES),
            qk.reshape(2, B, nblk * npair, GDN_CHUNK, LANES), s0]
    scratch = [pltpu.VMEM((B, HEADS, HEAD_DIM, HEAD_DIM), F32)]
    if finish:
        specs += [pl.BlockSpec((B, rb, GROUP_W), lambda j: (0, blk(j), 0)),
                  pl.BlockSpec((B, rb, GROUP_W), lambda j: (0, blk(j), P_GDN_GATE // GROUP_W)),
                  pl.BlockSpec((1, HEAD_DIM), lambda j: (0, 0))]
        args += [of.reshape(B, seq_len, GROUP_W), P.reshape(B, seq_len, P.shape[1]), ng]
        scratch.append(pltpu.VMEM((B, rb, GROUP_W), F32))
    for b in range(B):
        def tmap(j, b=b):
            g = toff + b * nblk + blk(j)
            return (g // per_batch, 0, g % per_batch, 0)
        specs.append(pl.BlockSpec((None, GDN_CHUNK, npair, LANES), tmap))
        args.append(tinv_all)
    out, s_out = pl.pallas_call(
        functools.partial(_gdn_kernel, rev, finish, d),
        grid=(nblk,),
        in_specs=specs,
        out_specs=[pl.BlockSpec((B, rb, GROUP_W), lambda j: (0, blk(j), 0)), sspec],
        out_shape=[jax.ShapeDtypeStruct((B, seq_len, GROUP_W), BF16 if finish else F32),
                   jax.ShapeDtypeStruct((B, HEADS, HEAD_DIM, HEAD_DIM), F32)],
        scratch_shapes=scratch,
        compiler_params=_cparams(("arbitrary",)),
        name="gdn_bwd" if rev else "gdn_fwd",
    )(*args)
    return out.reshape(M, GROUP_W), s_out


def _mixer_gdn(Pc, Pl, ctx_len, seq_len, cw, alog, dtb, ng):
    B = Pc.shape[0] // ctx_len
    qkv_c, g_c = _gdn_prep(Pc, ctx_len, cw, alog, dtb)
    qkv_l, g_l = _gdn_prep(Pl, seq_len, cw, alog, dtb)
    a_c, qk_c = _gdn_local(qkv_c, g_c)
    a_l, qk_l = _gdn_local(qkv_l, g_l)
    n_c, n_l = a_c.shape[1], a_l.shape[1]
    n_pad = -(n_c + n_l) % SOLVE_BATCH
    pad = jnp.zeros((2, n_pad, GDN_CHUNK, LANES), F32)
    tinv = _gdn_solve(jnp.concatenate([a_c, a_l, pad], axis=1))
    zero = jnp.zeros((B, HEADS, HEAD_DIM, HEAD_DIM), F32)
    oc_f, sc_f = _gdn_pass(qkv_c, g_c, tinv[0], 0, qk_c, ctx_len, False, 0, zero)
    ol_f, _ = _gdn_pass(qkv_l, g_l, tinv[0], n_c, qk_l, seq_len, False, 0, sc_f)
    yc, sc_b = _gdn_pass(qkv_c, g_c, tinv[1], 0, qk_c, ctx_len, True, 1, zero, of=oc_f, P=Pc, ng=ng)
    yl, _ = _gdn_pass(qkv_l, g_l, tinv[1], n_c, qk_l, seq_len, True, 1, sc_b, of=ol_f, P=Pl, ng=ng)
    return yc, yl


def _pack_w_in(w_in):
    off = {}
    o = 0
    for name, n in (('lru_x', 512), ('lru_gate', 512), ('mla_cq', 384), ('mla_ckv', 256), ('mla_kr', 64),
                    ('gdn_q', 512), ('gdn_k', 512), ('gdn_v', 512), ('gdn_gate', 512),
                    ('gdn_beta', 8), ('gdn_alpha', 8), ('na_q', 512), ('na_k', 512), ('na_v', 512)):
        off[name] = (o, n)
        o += n
    order = ['gdn_q', 'gdn_k', 'gdn_v', 'gdn_gate', 'lru_x', 'lru_gate', 'na_q', 'na_k', 'na_v',
             'mla_cq', 'mla_ckv', 'mla_kr', 'gdn_beta', 'gdn_alpha']
    parts = [w_in[..., off[n][0]:off[n][0] + off[n][1]] for n in order]
    used = sum(off[n][1] for n in order)
    parts.append(jnp.zeros(w_in.shape[:-1] + (P_COLS - used,), w_in.dtype))
    return jnp.concatenate(parts, axis=-1).astype(BF16)


def _pack_w_uq(w_uq):
    L, K, _ = w_uq.shape
    w = w_uq.reshape(L, K, HEADS, MLA_QK)
    w = jnp.pad(w, ((0, 0), (0, 0), (0, 0), (0, MLA_QPAD - MLA_QK)))
    return w.reshape(L, K, HEADS * MLA_QPAD).astype(BF16)


def _pad_gain(g):
    return jnp.pad(g, ((0, 0), (0, MLA_QPAD - MLA_QK)))[:, None, :]


def _rope_tables(T):
    t = jnp.arange(T)
    rowp = (t // GRID_W).astype(F32)
    colp = (t % GRID_W).astype(F32)
    n_freq = MLA_ROPE // 4
    inv = ROPE_BASE ** (-jnp.arange(n_freq, dtype=F32) / n_freq)
    ang = jnp.concatenate([rowp[:, None] * inv, colp[:, None] * inv], axis=-1)
    cos, sin = jnp.cos(ang), jnp.sin(ang)
    z = jnp.zeros((T, LANES - MLA_ROPE), F32)
    return (jnp.concatenate([cos, cos, z], axis=-1), jnp.concatenate([-sin, sin, z], axis=-1))


def _lane_vec(x):
    L = x.shape[0]
    return jnp.pad(x.reshape(L, 2 * HEADS), ((0, 0), (0, LANES - 2 * HEADS)))[:, None, :]


def kernel(x, c, ctx, c_ctx, ada_w, ada_b, norm_mix_g, norm_ffn_g, w_in, w_out, lru_conv_w, lru_conv_b, lru_wa, lru_ba, lru_wx, lru_bx, lru_lam, mla_qa_g, mla_w_uq, mla_kva_g, mla_w_ukv, mla_qn_g, mla_kn_g, gdn_conv_w, gdn_a_log, gdn_dt_bias, gdn_norm_g, na_qn_g, na_kn_g, na_rpb, ffn_w_up, ffn_conv_w, ffn_conv_b, ffn_w_down):
    B, T, D = x.shape
    TC = ctx.shape[1]
    L = ada_w.shape[0]
    rows = T // GRID_W
    assert B <= 2 and T % 512 == 0 and TC == 256 and rows >= NA_WIN

    w_in_p = _pack_w_in(w_in)
    w_out4 = w_out.reshape(L, 4, GROUP_W, D).astype(BF16)
    w_up = ffn_w_up.astype(BF16)
    w_down = ffn_w_down.astype(BF16)
    w_uq = _pack_w_uq(mla_w_uq)
    w_ukv = mla_w_ukv.astype(BF16)
    qn_g = _pad_gain(mla_qn_g)
    kn_g = _pad_gain(mla_kn_g)
    lru_wcat = jnp.concatenate([lru_wa, lru_wx], axis=-1).astype(BF16)
    alog_v = _lane_vec(gdn_a_log)
    dtb_v = _lane_vec(gdn_dt_bias)
    cosf, sinf = _rope_tables(T)

    cvec = jnp.zeros((SUBLANES, D), F32).at[:B].set(c).at[2].set(c_ctx)
    mod_all = _modulation(cvec, ada_w, ada_b).reshape(L, SUBLANES, N_MOD, 1, D)

    gmix = norm_mix_g[:, None, :]
    gffn = norm_ffn_g[:, None, :]
    ffn_cb = ffn_conv_b[:, None, :]
    h_lat = x.reshape(B * T, D)
    h_ctx = ctx.reshape(B * TC, D)
    for l in range(L):
        want_ctx = l < L - 1
        mod = mod_all
        Pl = _inproj(l, h_lat, gmix, mod, w_in_p, T, False)
        Pc = _inproj(l, h_ctx, gmix, mod, w_in_p, TC, True)

        lw = lambda dd: (lru_conv_w[l], lru_conv_b[l][None, :], lru_wcat[l, dd], lru_ba[l, dd][None, :],
                         lru_bx[l, dd][None, :], lru_lam[l, dd][None, :])
        ya_c, ya_l = _mixer_lru(Pc, Pl, TC, T, lw(0), lw(1))

        mw = (mla_qa_g[l][None, :], w_uq[l], mla_kva_g[l][None, :], w_ukv[l], qn_g[l], kn_g[l])
        Qtc, Kc, Vtc = _mla_prep(Pc, TC, False, mw, cosf, sinf)
        Qtl, Kl, Vtl = _mla_prep(Pl, T, True, mw, cosf, sinf)
        yb_l = _flash_attention(Qtl, Kc, Vtc, T, TC, Kl, Vtl)

        yc_c, yc_l = _mixer_gdn(Pc, Pl, TC, T, gdn_conv_w[l], alog_v[l], dtb_v[l], gdn_norm_g[l][None, :])

        nqg, nkg = na_qn_g[l][None, :], na_kn_g[l][None, :]
        NQtc, NKc, NVtc = _na_prep(Pc, nqg, nkg)
        NQtl, NKl, NVtl = _na_prep(Pl, nqg, nkg)
        bias = _na_bias(na_rpb[l], rows)
        yd_l = _na_attention(NQtl, NKl, NVtl, NKc, NVtc, bias, T, TC)

        h_lat = _outproj(l, (ya_l, yb_l, yc_l, yd_l), w_out4, h_lat, mod, T, False)
        h_lat = _ffn(l, h_lat, gffn, mod, w_up, ffn_conv_w, ffn_cb, w_down, T, False)
        if want_ctx:
            yb_c = _flash_attention(Qtc, Kc, Vtc, TC, TC)
            yd_c = _flash_attention(NQtc, NKc, NVtc, TC, TC)
            h_ctx = _outproj(l, (ya_c, yb_c, yc_c, yd_c), w_out4, h_ctx, mod, TC, True)
            h_ctx = _ffn(l, h_ctx, gffn, mod, w_up, ffn_conv_w, ffn_cb, w_down, TC, True)
    return h_lat.reshape(B, T, D)
```

```python
import functools
import math

import jax
import jax.numpy as jnp
from jax import lax
from jax.experimental import pallas as pl
from jax.experimental.pallas import tpu as pltpu

F32 = jnp.float32
BF16 = jnp.bfloat16

GRID_W = 64
HEADS = 4
HEAD_DIM = 128
GROUP_W = HEADS * HEAD_DIM
N_MOD = 6
EPS = 1e-6
LRU_C = 8.0
MLA_Q_LORA = 384
MLA_KV_LORA = 256
MLA_NOPE = 128
MLA_ROPE = 64
MLA_QK = MLA_NOPE + MLA_ROPE
MLA_QPAD = 256
ROPE_BASE = 10000.0
GDN_CHUNK = 64
NA_KH = 8
NA_KW = 16
NA_BAND = 8
NA_WIN = 16
NEG_BIG = -1e30
LOG2E = math.log2(math.e)
TINY = 1e-30
FLASH_UNROLL = 8
FFN_SUB = 256
INPROJ_SUB = 4

VMEM_LIMIT = 56 * 1024 * 1024
SUBLANES = 8
LANES = 128

P_GDN_QKV = 0
P_GDN_GATE = 1536
P_LRU_X = 2048
P_LRU_GATE = 2560
P_NA_QKV = 3072
P_MLA = 4608
P_COLS = 5376
P_TAIL = 5248
TAIL_BETA = 64
TAIL_ALPHA = 72


def _cparams(sem):
    return pltpu.CompilerParams(dimension_semantics=sem, vmem_limit_bytes=VMEM_LIMIT)


def _dot(a, b):
    return jnp.dot(a.astype(BF16), b.astype(BF16), preferred_element_type=F32)


def _dot_nt(a, b):
    return lax.dot_general(a.astype(BF16), b.astype(BF16), (((1,), (1,)), ((), ())),
                           preferred_element_type=F32)


def _sigmoid(x):
    return 1.0 / (1.0 + jnp.exp(-x))


def _silu(x):
    return x * _sigmoid(x)


def _softplus(x):
    return jnp.maximum(x, 0.0) + jnp.log(1.0 + jnp.exp(-jnp.abs(x)))


def _gelu_tanh(x):
    return 0.5 * x * (1.0 + jnp.tanh(math.sqrt(2.0 / math.pi) * (x + 0.044715 * x * x * x)))


def _mod_kernel(c_ref, w_ref, b_ref, o_ref):
    c = c_ref[...]
    o_ref[...] = _dot(_silu(c), w_ref[...]) + b_ref[...]


def _modulation(cvec, ada_w, ada_b):
    L, D, N = ada_w.shape
    tn = 1024
    return pl.pallas_call(
        _mod_kernel,
        grid=(L, N // tn),
        in_specs=[pl.BlockSpec((SUBLANES, D), lambda l, j: (0, 0)),
                  pl.BlockSpec((None, D, tn), lambda l, j: (l, 0, j)),
                  pl.BlockSpec((None, 1, tn), lambda l, j: (l, 0, j))],
        out_specs=pl.BlockSpec((None, SUBLANES, tn), lambda l, j: (l, 0, j)),
        out_shape=jax.ShapeDtypeStruct((L, SUBLANES, N), F32),
        compiler_params=_cparams(("parallel", "parallel")),
        name="modulation",
    )(cvec, ada_w, ada_b.reshape(L, 1, N))


def _mod_spec(l, which, rows_per_mod, D, ngrid):
    if rows_per_mod is None:
        row = lambda i: 2
    else:
        row = lambda i: i // rows_per_mod
    if ngrid == 1:
        return pl.BlockSpec((None, None, None, 1, D), lambda i: (l, row(i), which, 0, 0))
    return pl.BlockSpec((None, None, None, 1, D), lambda i, j: (l, row(i), which, 0, 0))


def _inproj_kernel(h_ref, g_ref, shift_ref, scale_ref, w_ref, o_ref, xn_ref):
    tm = h_ref.shape[0]

    @pl.when(pl.program_id(1) == 0)
    def _():
        rs = tm // min(INPROJ_SUB, tm // 256)
        for r0 in range(0, tm, rs):
            x = h_ref[r0:r0 + rs, :]
            y = x * lax.rsqrt(jnp.mean(x * x, axis=-1, keepdims=True) + EPS) * g_ref[...]
            xb = (y * (1.0 + scale_ref[...]) + shift_ref[...]).astype(BF16)
            xn_ref[r0:r0 + rs, :] = xb
            o_ref[r0:r0 + rs, :] = jnp.dot(xb, w_ref[...], preferred_element_type=F32)

    @pl.when(pl.program_id(1) > 0)
    def _():
        o_ref[...] = jnp.dot(xn_ref[...], w_ref[...], preferred_element_type=F32)


def _inproj(l, h, gain, mod, w, seq_len, is_ctx):
    M, D = h.shape
    N = w.shape[2]
    tm = min(1024, M)
    tn = 768
    rpm = None if is_ctx else seq_len // tm
    return pl.pallas_call(
        _inproj_kernel,
        grid=(M // tm, N // tn),
        in_specs=[pl.BlockSpec((tm, D), lambda i, j: (i, 0)),
                  pl.BlockSpec((None, 1, D), lambda i, j: (l, 0, 0)),
                  _mod_spec(l, 0, rpm, D, 2),
                  _mod_spec(l, 1, rpm, D, 2),
                  pl.BlockSpec((None, D, tn), lambda i, j: (l, 0, j))],
        out_specs=pl.BlockSpec((tm, tn), lambda i, j: (i, j)),
        out_shape=jax.ShapeDtypeStruct((M, N), F32),
        scratch_shapes=[pltpu.VMEM((tm, D), BF16)],
        compiler_params=_cparams(("parallel", "arbitrary")),
        name="inproj",
    )(h, gain, mod, mod, w)


def _outproj_kernel(ya_ref, yb_ref, yc_ref, yd_ref, w_ref, h_ref, gate_ref, o_ref):
    acc = jnp.dot(ya_ref[...], w_ref[0], preferred_element_type=F32)
    acc += jnp.dot(yb_ref[...], w_ref[1], preferred_element_type=F32)
    acc += jnp.dot(yc_ref[...], w_ref[2], preferred_element_type=F32)
    acc += jnp.dot(yd_ref[...], w_ref[3], preferred_element_type=F32)
    o_ref[...] = h_ref[...] + gate_ref[...] * acc


def _outproj(l, ys, w4, h, mod, seq_len, is_ctx):
    M, D = h.shape
    tm = min(512, M)
    rpm = None if is_ctx else seq_len // tm
    yspec = pl.BlockSpec((tm, GROUP_W), lambda i: (i, 0))
    return pl.pallas_call(
        _outproj_kernel,
        grid=(M // tm,),
        in_specs=[yspec, yspec, yspec, yspec,
                  pl.BlockSpec((None, 4, GROUP_W, D), lambda i: (l, 0, 0, 0)),
                  pl.BlockSpec((tm, D), lambda i: (i, 0)),
                  _mod_spec(l, 2, rpm, D, 1)],
        out_specs=pl.BlockSpec((tm, D), lambda i: (i, 0)),
        out_shape=jax.ShapeDtypeStruct((M, D), F32),
        compiler_params=_cparams(("parallel",)),
        name="outproj",
    )(*ys, w4, h, mod)


def _halo_specs(tm, width, col_block, nrows, ngrid):
    r = tm // SUBLANES
    last = nrows // SUBLANES - 1
    if ngrid == 1:
        prev = pl.BlockSpec((SUBLANES, width), lambda i: (jnp.maximum(i * r - 1, 0), col_block))
        nxt = pl.BlockSpec((SUBLANES, width), lambda i: (jnp.minimum((i + 1) * r, last), col_block))
    else:
        prev = pl.BlockSpec((SUBLANES, width), lambda i, j: (jnp.maximum(i * r - 1, 0), col_block))
        nxt = pl.BlockSpec((SUBLANES, width), lambda i, j: (jnp.minimum((i + 1) * r, last), col_block))
    return prev, nxt


def _ffn_kernel(seq_len, h_ref, hp_ref, hn_ref, g_ref, shift_ref, scale_ref, gate_ref,
                wa_ref, wg_ref, cwa_ref, cwg_ref, cba_ref, cbg_ref, wd_ref, o_ref, xn_ref):
    tm = h_ref.shape[0]
    tf = wa_ref.shape[1]
    i = pl.program_id(0)
    j = pl.program_id(1)

    @pl.when(j == 0)
    def _():
        def norm(x):
            y = x * lax.rsqrt(jnp.mean(x * x, axis=-1, keepdims=True) + EPS) * g_ref[...]
            return (y * (1.0 + scale_ref[...]) + shift_ref[...]).astype(BF16)
        keep_prev = (i * tm) % seq_len != 0
        keep_next = ((i + 1) * tm) % seq_len != 0
        xn_ref[0:SUBLANES, :] = jnp.where(keep_prev, norm(hp_ref[...]), jnp.zeros((), BF16))
        xn_ref[SUBLANES:SUBLANES + tm, :] = norm(h_ref[...])
        xn_ref[SUBLANES + tm:, :] = jnp.where(keep_next, norm(hn_ref[...]), jnp.zeros((), BF16))

    def conv(u, cw_ref, cb_ref, cols):
        lo = u[SUBLANES - 1:SUBLANES - 1 + tm]
        mid = u[SUBLANES:SUBLANES + tm]
        hi = u[SUBLANES + 1:SUBLANES + 1 + tm]
        return lo * cw_ref[0:1, cols] + mid * cw_ref[1:2, cols] + hi * cw_ref[2:3, cols] + cb_ref[:, cols]

    xn = xn_ref[...]
    subs = [slice(s, s + FFN_SUB) for s in range(0, tf, FFN_SUB)]
    ups = [(jnp.dot(xn, wa_ref[:, c], preferred_element_type=F32),
            jnp.dot(xn, wg_ref[:, c], preferred_element_type=F32)) for c in subs]
    acts = [(conv(ua, cwa_ref, cba_ref, c) * _silu(conv(ug, cwg_ref, cbg_ref, c))).astype(BF16)
            for c, (ua, ug) in zip(subs, ups)]
    part = jnp.dot(jnp.concatenate(acts, axis=1), wd_ref[...], preferred_element_type=F32)

    @pl.when(j == 0)
    def _():
        o_ref[...] = part

    @pl.when(j > 0)
    def _():
        o_ref[...] += part

    @pl.when(j == pl.num_programs(1) - 1)
    def _():
        o_ref[...] = h_ref[...] + gate_ref[...] * o_ref[...]


def _ffn(l, h, gain, mod, w_up, conv_w, conv_b, w_down, seq_len, is_ctx):
    M, D = h.shape
    FF = w_down.shape[1]
    tm = min(512, seq_len)
    tf = 1024
    nf = FF // tf
    rpm = None if is_ctx else seq_len // tm
    prev, nxt = _halo_specs(tm, D, 0, M, 2)
    return pl.pallas_call(
        functools.partial(_ffn_kernel, seq_len),
        grid=(M // tm, nf),
        in_specs=[pl.BlockSpec((tm, D), lambda i, j: (i, 0)), prev, nxt,
                  pl.BlockSpec((None, 1, D), lambda i, j: (l, 0, 0)),
                  _mod_spec(l, 3, rpm, D, 2), _mod_spec(l, 4, rpm, D, 2), _mod_spec(l, 5, rpm, D, 2),
                  pl.BlockSpec((None, D, tf), lambda i, j: (l, 0, j)),
                  pl.BlockSpec((None, D, tf), lambda i, j: (l, 0, j + nf)),
                  pl.BlockSpec((None, 3, tf), lambda i, j: (l, 0, j)),
                  pl.BlockSpec((None, 3, tf), lambda i, j: (l, 0, j + nf)),
                  pl.BlockSpec((None, 1, tf), lambda i, j: (l, 0, j)),
                  pl.BlockSpec((None, 1, tf), lambda i, j: (l, 0, j + nf)),
                  pl.BlockSpec((None, tf, D), lambda i, j: (l, j, 0))],
        out_specs=pl.BlockSpec((tm, D), lambda i, j: (i, 0)),
        out_shape=jax.ShapeDtypeStruct((M, D), F32),
        scratch_shapes=[pltpu.VMEM((tm + 2 * SUBLANES, D), BF16)],
        compiler_params=_cparams(("parallel", "arbitrary")),
        name="conv_ffn",
    )(h, h, h, gain, mod, mod, mod, w_up, w_up, conv_w, conv_w, conv_b, conv_b, w_down)


def _conv4(xp_ref, x_ref, xn_ref, w_ref, start, seq_len):
    tm = x_ref.shape[0]
    xp = jnp.where(start != 0, xp_ref[...], 0.0)
    xn = jnp.where(start + tm != seq_len, xn_ref[...], 0.0)
    xe = jnp.concatenate([xp, x_ref[...], xn], axis=0)
    t0 = xe[SUBLANES - 1:SUBLANES - 1 + tm]
    t1 = xe[SUBLANES:SUBLANES + tm]
    t2 = xe[SUBLANES + 1:SUBLANES + 1 + tm]
    t3 = xe[SUBLANES + 2:SUBLANES + 2 + tm]
    return t0 * w_ref[0:1, :] + t1 * w_ref[1:2, :] + t2 * w_ref[2:3, :] + t3 * w_ref[3:4, :]


def _shift_rows(x, k, fill, rev):
    n = x.shape[0]
    if k % SUBLANES == 0:
        pad = jnp.full((k, x.shape[1]), fill, x.dtype)
        return jnp.concatenate([x[k:], pad], 0) if rev else jnp.concatenate([pad, x[:n - k]], 0)
    row = lax.broadcasted_iota(jnp.int32, x.shape, 0)
    if rev:
        return jnp.where(row >= n - k, fill, pltpu.roll(x, n - k, 0))
    return jnp.where(row < k, fill, pltpu.roll(x, k, 0))


def _lru_kernel(rev, finish, seq_len, *refs):
    if finish:
        (x_ref, xp_ref, xn_ref, gate_ref, hf_ref, cw_ref, cb_ref, wcat_ref, ba_ref, bx_ref,
         lam_ref, h0_ref, out_ref, st_ref, carry_ref) = refs
    else:
        (x_ref, xp_ref, xn_ref, cw_ref, cb_ref, wcat_ref, ba_ref, bx_ref,
         lam_ref, h0_ref, out_ref, st_ref, carry_ref) = refs
    tc = x_ref.shape[0]
    j = pl.program_id(1)
    nch = pl.num_programs(1)
    c = (nch - 1 - j) if rev else j

    @pl.when(j == 0)
    def _():
        carry_ref[...] = h0_ref[...]

    u = _conv4(xp_ref, x_ref, xn_ref, cw_ref, c * tc, seq_len) + cb_ref[...]
    rs, is_ = [], []
    for n in range(HEADS):
        z = _dot(u[:, n * HEAD_DIM:(n + 1) * HEAD_DIM], wcat_ref[n])
        rs.append(z[:, :HEAD_DIM])
        is_.append(z[:, HEAD_DIM:])
    r = _sigmoid(jnp.concatenate(rs, axis=1) + ba_ref[...])
    ig = _sigmoid(jnp.concatenate(is_, axis=1) + bx_ref[...])
    log_a = -LRU_C * r * _softplus(-lam_ref[...])
    a = jnp.exp(log_a)
    om = 1.0 - a * a
    b = om * lax.rsqrt(jnp.maximum(om, TINY)) * ig * u
    nslab = tc // SUBLANES
    a = a.reshape(nslab, SUBLANES, a.shape[1])
    b = b.reshape(nslab, SUBLANES, b.shape[1])
    sub = lax.broadcasted_iota(jnp.int32, a.shape, 1)
    k = 1
    while k < SUBLANES:
        own = (sub >= SUBLANES - k) if rev else (sub < k)
        shift = SUBLANES - k if rev else k
        b = a * jnp.where(own, 0.0, pltpu.roll(b, shift, 1)) + b
        a = a * jnp.where(own, 1.0, pltpu.roll(a, shift, 1))
        k *= 2
    a = a.reshape(tc, a.shape[2])
    b = b.reshape(tc, b.shape[2])
    state = carry_ref[0:1, :]
    hs = [None] * nslab
    for t in range(nslab):
        s = nslab - 1 - t if rev else t
        rows = slice(s * SUBLANES, (s + 1) * SUBLANES)
        hs[s] = b[rows] + a[rows] * state
        state = hs[s][0:1, :] if rev else hs[s][SUBLANES - 1:SUBLANES, :]
    h = jnp.concatenate(hs, axis=0)
    last = state
    carry_ref[...] = jnp.broadcast_to(last, carry_ref.shape)
    st_ref[...] = jnp.broadcast_to(last, st_ref.shape)
    if finish:
        out_ref[...] = ((hf_ref[...] + h) * _gelu_tanh(gate_ref[...])).astype(out_ref.dtype)
    else:
        out_ref[...] = h


def _lru_pass(P, seq_len, rev, h0, wts, hf=None):
    cw, cb, wcat, ba, bx, lam = wts
    M = P.shape[0]
    B = M // seq_len
    tc = min(512, seq_len)
    nch = seq_len // tc
    finish = hf is not None
    W = GROUP_W
    xb = P_LRU_X // W
    gb = P_LRU_GATE // W
    r = tc // SUBLANES
    last = M // SUBLANES - 1

    def chunk(j):
        return (nch - 1 - j) if rev else j

    row = lambda b, j: (b * nch + chunk(j), xb)
    specs = [pl.BlockSpec((tc, W), row),
             pl.BlockSpec((SUBLANES, W), lambda b, j: (jnp.maximum((b * nch + chunk(j)) * r - 1, 0), xb)),
             pl.BlockSpec((SUBLANES, W), lambda b, j: (jnp.minimum((b * nch + chunk(j) + 1) * r, last), xb))]
    args = [P, P, P]
    if finish:
        specs += [pl.BlockSpec((tc, W), lambda b, j: (b * nch + chunk(j), gb)),
                  pl.BlockSpec((tc, W), lambda b, j: (b * nch + chunk(j), 0))]
        args += [P, hf]
    full2 = lambda b, j: (0, 0)
    specs += [pl.BlockSpec((4, W), full2), pl.BlockSpec((1, W), full2),
              pl.BlockSpec((HEADS, HEAD_DIM, 2 * HEAD_DIM), lambda b, j: (0, 0, 0)),
              pl.BlockSpec((1, W), full2), pl.BlockSpec((1, W), full2), pl.BlockSpec((1, W), full2),
              pl.BlockSpec((None, SUBLANES, W), lambda b, j: (b, 0, 0))]
    args += [cw, cb, wcat, ba, bx, lam, h0]
    out_dtype = BF16 if finish else F32
    return pl.pallas_call(
        functools.partial(_lru_kernel, rev, finish, seq_len),
        grid=(B, nch),
        in_specs=specs,
        out_specs=[pl.BlockSpec((tc, W), lambda b, j: (b * nch + chunk(j), 0)),
                   pl.BlockSpec((None, SUBLANES, W), lambda b, j: (b, 0, 0))],
        out_shape=[jax.ShapeDtypeStruct((M, W), out_dtype),
                   jax.ShapeDtypeStruct((B, SUBLANES, W), F32)],
        scratch_shapes=[pltpu.VMEM((SUBLANES, W), F32)],
        compiler_params=_cparams(("parallel", "arbitrary")),
        name="rglru_bwd" if rev else "rglru_fwd",
    )(*args)


def _mixer_lru(Pc, Pl, ctx_len, seq_len, wts_f, wts_b):
    B = Pc.shape[0] // ctx_len
    zero = jnp.zeros((B, SUBLANES, GROUP_W), F32)
    hc_f, sc_f = _lru_pass(Pc, ctx_len, False, zero, wts_f)
    hl_f, _ = _lru_pass(Pl, seq_len, False, sc_f, wts_f)
    yc, sc_b = _lru_pass(Pc, ctx_len, True, zero, wts_b, hf=hc_f)
    yl, _ = _lru_pass(Pl, seq_len, True, sc_b, wts_b, hf=hl_f)
    return yc, yl


def _rope_mix(x, cos, sin):
    lane = lax.broadcasted_iota(jnp.int32, x.shape, 1)
    half = MLA_ROPE // 2
    swapped = jnp.where(lane < half, pltpu.roll(x, LANES - half, 1), pltpu.roll(x, half, 1))
    return x * cos + swapped * sin


def _mla_prep_kernel(use_rope, x_ref, qag_ref, wuq_ref, kvg_ref, wukv_ref, qg_ref, kg_ref,
                     cos_ref, sin_ref, qt_out, k_out, vt_out):
    x = x_ref[...]

    def rms(v, g):
        return v * lax.rsqrt(jnp.mean(v * v, axis=-1, keepdims=True) + EPS) * g

    q = _dot(rms(x[:, :MLA_Q_LORA], qag_ref[...]), wuq_ref[...])
    kv = _dot(rms(x[:, MLA_Q_LORA:MLA_Q_LORA + MLA_KV_LORA], kvg_ref[...]), wukv_ref[...])
    tail = x[:, MLA_Q_LORA + MLA_KV_LORA:]
    lane = lax.broadcasted_iota(jnp.int32, tail.shape, 1)
    kr = jnp.where(lane < MLA_ROPE, tail, 0.0)
    kr_ss = jnp.sum(kr * kr, axis=-1, keepdims=True)
    scale = MLA_QK ** -0.5 * LOG2E
    for h in range(HEADS):
        qh = q[:, h * MLA_QPAD:(h + 1) * MLA_QPAD]
        inv = lax.rsqrt(jnp.sum(qh * qh, axis=-1, keepdims=True) / MLA_QK + EPS) * scale
        qh = qh * inv * qg_ref[...]
        qn, qr = qh[:, :LANES], qh[:, LANES:]
        kn = kv[:, h * 2 * HEAD_DIM:h * 2 * HEAD_DIM + MLA_NOPE]
        vh = kv[:, h * 2 * HEAD_DIM + MLA_NOPE:(h + 1) * 2 * HEAD_DIM]
        kinv = lax.rsqrt((jnp.sum(kn * kn, axis=-1, keepdims=True) + kr_ss) / MLA_QK + EPS)
        kn = kn * kinv * kg_ref[:, :LANES]
        krh = kr * kinv * kg_ref[:, LANES:]
        if use_rope:
            qr = _rope_mix(qr, cos_ref[...], sin_ref[...])
            krh = _rope_mix(krh, cos_ref[...], sin_ref[...])
        qt_out[h] = jnp.concatenate([qn, qr], axis=1).T.astype(BF16)
        k_out[h, :, :LANES] = kn.astype(BF16)
        k_out[h, :, LANES:] = krh.astype(BF16)
        vt_out[h] = vh.T.astype(BF16)


def _mla_prep(P, seq_len, use_rope, wts, cosf, sinf):
    qag, wuq, kvg, wukv, qg, kg = wts
    M = P.shape[0]
    tm = min(512, seq_len)
    nt = seq_len // tm
    full = lambda i: (0, 0)
    wmla = P_COLS - P_MLA
    tab = pl.BlockSpec((tm, LANES), (lambda i: (i % nt, 0)) if use_rope else (lambda i: (0, 0)))
    tspec = lambda w: pl.BlockSpec((HEADS, w, tm), lambda i: (0, 0, i))
    return pl.pallas_call(
        functools.partial(_mla_prep_kernel, use_rope),
        grid=(M // tm,),
        in_specs=[pl.BlockSpec((tm, wmla), lambda i: (i, P_MLA // wmla)),
                  pl.BlockSpec((1, MLA_Q_LORA), full), pl.BlockSpec(wuq.shape, full),
                  pl.BlockSpec((1, MLA_KV_LORA), full), pl.BlockSpec(wukv.shape, full),
                  pl.BlockSpec((1, MLA_QPAD), full), pl.BlockSpec((1, MLA_QPAD), full),
                  tab, tab],
        out_specs=[tspec(MLA_QPAD), pl.BlockSpec((HEADS, tm, MLA_QPAD), lambda i: (0, i, 0)),
                   tspec(HEAD_DIM)],
        out_shape=[jax.ShapeDtypeStruct((HEADS, MLA_QPAD, M), BF16),
                   jax.ShapeDtypeStruct((HEADS, M, MLA_QPAD), BF16),
                   jax.ShapeDtypeStruct((HEADS, HEAD_DIM, M), BF16)],
        compiler_params=_cparams(("parallel",)),
        name="mla_prep",
    )(P, qag, wuq, kvg, wukv, qg, kg, cosf, sinf)


def _flash_kernel(tk, has_lat, *refs):
    if has_lat:
        qt_ref, kc_ref, vct_ref, kl_ref, vlt_ref, o_ref, m_ref, l_ref, acc_ref, s_ref = refs
    else:
        qt_ref, kc_ref, vct_ref, o_ref = refs
    qt = qt_ref[...]
    s = jnp.dot(kc_ref[...], qt, preferred_element_type=F32)
    if has_lat:
        nk = kl_ref.shape[0] // tk

        def scores(c, slot):
            off = pl.multiple_of(jnp.minimum(c, nk - 1) * tk, tk)
            s_ref[slot] = jnp.dot(kl_ref[pl.ds(off, tk), :], qt, preferred_element_type=F32)

        scores(0, 0)
    m = jnp.max(s, axis=0, keepdims=True)
    p = jnp.exp2(s - m)
    l = jnp.sum(p, axis=0, keepdims=True)
    acc = jnp.dot(vct_ref[...], p.astype(BF16), preferred_element_type=F32)
    if has_lat:
        m_ref[...] = m
        l_ref[...] = l
        acc_ref[...] = acc

        def step(c, slot):
            scores(c + 1, 1 - slot)
            off = pl.multiple_of(c * tk, tk)
            s = s_ref[slot]
            m_old = m_ref[...]
            m_new = jnp.maximum(m_old, jnp.max(s, axis=0, keepdims=True))
            alpha = jnp.exp2(m_old - m_new)
            p = jnp.exp2(s - m_new)
            l_ref[...] = alpha * l_ref[...] + jnp.sum(p, axis=0, keepdims=True)
            acc_ref[...] = alpha * acc_ref[...] + jnp.dot(
                vlt_ref[:, pl.ds(off, tk)], p.astype(BF16), preferred_element_type=F32)
            m_ref[...] = m_new

        unroll = min(FLASH_UNROLL, nk)

        def body(j, carry):
            for u in range(unroll):
                step(unroll * j + u, u % 2)
            return carry

        lax.fori_loop(0, nk // unroll, body, 0)
        acc = acc_ref[...]
        l = l_ref[...]
    o_ref[...] = (acc / l).T.astype(o_ref.dtype)


def _flash_attention(Qt, Kc, Vct, seq_len, ctx_len, Kl=None, Vlt=None):
    H, dq, M = Qt.shape
    B = M // seq_len
    has_lat = Kl is not None
    tq = min(1024, seq_len)
    tk = 512
    nq = seq_len // tq
    specs = [pl.BlockSpec((None, dq, tq), lambda b, h, i: (h, 0, b * nq + i)),
             pl.BlockSpec((None, ctx_len, dq), lambda b, h, i: (h, b, 0)),
             pl.BlockSpec((None, HEAD_DIM, ctx_len), lambda b, h, i: (h, 0, b))]
    args = [Qt, Kc, Vct]
    scratch = []
    if has_lat:
        specs += [pl.BlockSpec((None, seq_len, dq), lambda b, h, i: (h, b, 0)),
                  pl.BlockSpec((None, HEAD_DIM, seq_len), lambda b, h, i: (h, 0, b))]
        args += [Kl, Vlt]
        nk = seq_len // tk
        assert nk % 2 == 0 and nk % min(FLASH_UNROLL, nk) == 0
        scratch = [pltpu.VMEM((1, tq), F32), pltpu.VMEM((1, tq), F32), pltpu.VMEM((HEAD_DIM, tq), F32),
                   pltpu.VMEM((2, tk, tq), F32)]
    return pl.pallas_call(
        functools.partial(_flash_kernel, tk, has_lat),
        grid=(B, H, nq),
        in_specs=specs,
        out_specs=pl.BlockSpec((tq, HEAD_DIM), lambda b, h, i: (b * nq + i, h)),
        out_shape=jax.ShapeDtypeStruct((M, GROUP_W), BF16),
        scratch_shapes=scratch,
        compiler_params=_cparams(("parallel", "parallel", "arbitrary")),
        name="mla_attention" if has_lat else "ctx_attention",
    )(*args)


def _na_prep_kernel(x_ref, qg_ref, kg_ref, qt_out, k_out, vt_out):
    x = x_ref[...]
    scale = HEAD_DIM ** -0.5 * LOG2E
    for h in range(HEADS):
        qh = x[:, h * HEAD_DIM:(h + 1) * HEAD_DIM]
        kh = x[:, GROUP_W + h * HEAD_DIM:GROUP_W + (h + 1) * HEAD_DIM]
        vh = x[:, 2 * GROUP_W + h * HEAD_DIM:2 * GROUP_W + (h + 1) * HEAD_DIM]
        qh = qh * (lax.rsqrt(jnp.mean(qh * qh, axis=-1, keepdims=True) + EPS) * scale) * qg_ref[...]
        kh = kh * lax.rsqrt(jnp.mean(kh * kh, axis=-1, keepdims=True) + EPS) * kg_ref[...]
        qt_out[h] = qh.T.astype(BF16)
        k_out[h] = kh.astype(BF16)
        vt_out[h] = vh.T.astype(BF16)


def _na_prep(P, qg, kg):
    M = P.shape[0]
    tm = 512
    w = 3 * GROUP_W
    hspec = pl.BlockSpec((HEADS, tm, HEAD_DIM), lambda i: (0, i, 0))
    tspec = pl.BlockSpec((HEADS, HEAD_DIM, tm), lambda i: (0, 0, i))
    sds = jax.ShapeDtypeStruct((HEADS, M, HEAD_DIM), BF16)
    tds = jax.ShapeDtypeStruct((HEADS, HEAD_DIM, M), BF16)
    return pl.pallas_call(
        _na_prep_kernel,
        grid=(M // tm,),
        in_specs=[pl.BlockSpec((tm, w), lambda i: (i, P_NA_QKV // w)),
                  pl.BlockSpec((1, HEAD_DIM), lambda i: (0, 0)),
                  pl.BlockSpec((1, HEAD_DIM), lambda i: (0, 0))],
        out_specs=[tspec, hspec, tspec],
        out_shape=[tds, sds, tds],
        compiler_params=_cparams(("parallel",)),
        name="na_prep",
    )(P, qg, kg)


def _na_bias_kernel(rows, rpb_ref, o_ref):
    h = pl.program_id(0)
    kc = lax.broadcasted_iota(jnp.int32, (GRID_W, GRID_W), 0)
    qc = lax.broadcasted_iota(jnp.int32, (GRID_W, GRID_W), 1)
    dc = jnp.clip(kc - qc + (NA_KW - 1), 0, 2 * NA_KW - 2)
    c0 = jnp.clip(qc - NA_KW // 2, 0, GRID_W - NA_KW)
    col_ok = (kc >= c0) & (kc < c0 + NA_KW)
    neg = jnp.full((GRID_W, GRID_W), NEG_BIG, F32)
    tiles = []
    for dr in range(2 * NA_KH - 1):
        t = jnp.zeros((GRID_W, GRID_W), F32)
        for d in range(2 * NA_KW - 1):
            t = t + jnp.where(dc == d, rpb_ref[h, dr, d], 0.0)
        tiles.append(jnp.where(col_ok, t * LOG2E, NEG_BIG))
    for kind in range(3):
        q_base = (0, NA_KH // 2, rows - NA_BAND)[kind]
        w_base = (0, 0, rows - NA_WIN)[kind]
        for qr in range(NA_BAND):
            r = q_base + qr
            if kind == 1:
                r0 = r - NA_KH // 2
            else:
                r0 = min(max(r - NA_KH // 2, 0), rows - NA_KH)
            for kr in range(NA_WIN):
                ka = w_base + kr
                ok = r0 <= ka < r0 + NA_KH
                blk = tiles[ka - r + NA_KH - 1] if ok else neg
                o_ref[kind, kr * GRID_W:(kr + 1) * GRID_W, qr * GRID_W:(qr + 1) * GRID_W] = blk


def _na_bias(rpb, rows):
    H = rpb.shape[0]
    nq, nk = NA_BAND * GRID_W, NA_WIN * GRID_W
    return pl.pallas_call(
        functools.partial(_na_bias_kernel, rows),
        grid=(H,),
        in_specs=[pl.BlockSpec(memory_space=pltpu.SMEM)],
        out_specs=pl.BlockSpec((None, 3, nk, nq), lambda h: (h, 0, 0, 0)),
        out_shape=jax.ShapeDtypeStruct((H, 3, nk, nq), F32),
        compiler_params=_cparams(("parallel",)),
        name="na_bias",
    )(rpb)


def _na_kernel(seq_len, qt_ref, k_ref, vt_ref, kc_ref, vct_ref, bias0_ref, bias1_ref, o_ref):
    nq = bias0_ref.shape[1]
    nk = bias0_ref.shape[0]
    j = pl.program_id(2)
    kc = kc_ref[...]
    scores = []
    for t, bias_ref in enumerate((bias0_ref, bias1_ref)):
        base = jnp.clip((2 * j + t) * nq - (NA_KH // 2) * GRID_W, 0, seq_len - nk)
        base = pl.multiple_of(base, (NA_KH // 2) * GRID_W)
        qt = qt_ref[:, t * nq:(t + 1) * nq]
        s_ctx = jnp.dot(kc, qt, preferred_element_type=F32)
        s_loc = jnp.dot(k_ref[pl.ds(base, nk), :], qt, preferred_element_type=F32) + bias_ref[...]
        scores.append((base, s_loc, s_ctx))
    for t, (base, s_loc, s_ctx) in enumerate(scores):
        m = jnp.maximum(jnp.max(s_loc, axis=0, keepdims=True), jnp.max(s_ctx, axis=0, keepdims=True))
        p_loc = jnp.exp2(s_loc - m)
        p_ctx = jnp.exp2(s_ctx - m)
        l = jnp.sum(p_loc, axis=0, keepdims=True) + jnp.sum(p_ctx, axis=0, keepdims=True)
        o = (jnp.dot(vt_ref[:, pl.ds(base, nk)], p_loc.astype(BF16), preferred_element_type=F32)
             + jnp.dot(vct_ref[...], p_ctx.astype(BF16), preferred_element_type=F32))
        o_ref[t * nq:(t + 1) * nq, :] = (o / l).T.astype(o_ref.dtype)


def _na_attention(Qtl, Kl, Vtl, Kc, Vtc, bias, seq_len, ctx_len):
    H, d, M = Qtl.shape
    B = M // seq_len
    nq, nk = NA_BAND * GRID_W, NA_WIN * GRID_W
    nb = seq_len // nq
    assert nb % 2 == 0
    nb2 = nb // 2

    def kind(band):
        return jnp.where(band == 0, 0, jnp.where(band == nb - 1, 2, 1))

    return pl.pallas_call(
        functools.partial(_na_kernel, seq_len),
        grid=(B, H, nb2),
        in_specs=[pl.BlockSpec((None, d, 2 * nq), lambda b, h, j: (h, 0, b * nb2 + j)),
                  pl.BlockSpec((None, seq_len, d), lambda b, h, j: (h, b, 0)),
                  pl.BlockSpec((None, d, seq_len), lambda b, h, j: (h, 0, b)),
                  pl.BlockSpec((None, ctx_len, d), lambda b, h, j: (h, b, 0)),
                  pl.BlockSpec((None, d, ctx_len), lambda b, h, j: (h, 0, b)),
                  pl.BlockSpec((None, None, nk, nq), lambda b, h, j: (h, kind(2 * j), 0, 0)),
                  pl.BlockSpec((None, None, nk, nq), lambda b, h, j: (h, kind(2 * j + 1), 0, 0))],
        out_specs=pl.BlockSpec((2 * nq, d), lambda b, h, j: (b * nb2 + j, h)),
        out_shape=jax.ShapeDtypeStruct((M, GROUP_W), BF16),
        compiler_params=_cparams(("parallel", "parallel", "arbitrary")),
        name="na_attention",
    )(Qtl, Kl, Vtl, Kc, Vtc, bias, bias)


def _gdn_prep_kernel(seq_len, x_ref, xp_ref, xn_ref, tail_ref, cw_ref, alog_ref, dtb_ref,
                     qkv_out, gates_out):
    tm = x_ref.shape[0]
    y = _silu(_conv4(xp_ref, x_ref, xn_ref, cw_ref, (pl.program_id(0) * tm) % seq_len, seq_len))
    for h in range(HEADS):
        sl = slice(h * HEAD_DIM, (h + 1) * HEAD_DIM)
        qh = y[:, sl]
        qkv_out[:, sl] = qh * (lax.rsqrt(jnp.sum(qh * qh, axis=-1, keepdims=True) + EPS) * HEAD_DIM ** -0.5)
        sl = slice(GROUP_W + h * HEAD_DIM, GROUP_W + (h + 1) * HEAD_DIM)
        kh = y[:, sl]
        qkv_out[:, sl] = kh * lax.rsqrt(jnp.sum(kh * kh, axis=-1, keepdims=True) + EPS)
    qkv_out[:, 2 * GROUP_W:] = y[:, 2 * GROUP_W:]
    t = tail_ref[...]
    lane = lax.broadcasted_iota(jnp.int32, t.shape, 1)
    alpha = pltpu.roll(t, LANES - TAIL_ALPHA, 1)
    beta = pltpu.roll(t, LANES - TAIL_BETA + 2 * HEADS, 1)
    g = -jnp.exp(alog_ref[...]) * _softplus(alpha + dtb_ref[...])
    gates_out[...] = jnp.where(lane < 2 * HEADS, g, jnp.where(lane < 4 * HEADS, _sigmoid(beta), 0.0))


def _gdn_prep(P, seq_len, cw, alog, dtb):
    M = P.shape[0]
    tm = min(512, seq_len)
    w = 3 * GROUP_W
    prev, nxt = _halo_specs(tm, w, 0, M, 1)
    full = lambda i: (0, 0)
    return pl.pallas_call(
        functools.partial(_gdn_prep_kernel, seq_len),
        grid=(M // tm,),
        in_specs=[pl.BlockSpec((tm, w), lambda i: (i, 0)), prev, nxt,
                  pl.BlockSpec((tm, LANES), lambda i: (i, P_TAIL // LANES)),
                  pl.BlockSpec((4, w), full), pl.BlockSpec((1, LANES), full), pl.BlockSpec((1, LANES), full)],
        out_specs=[pl.BlockSpec((tm, w), lambda i: (i, 0)), pl.BlockSpec((tm, LANES), lambda i: (i, 0))],
        out_shape=[jax.ShapeDtypeStruct((M, w), F32), jax.ShapeDtypeStruct((M, LANES), F32)],
        compiler_params=_cparams(("parallel",)),
        name="gdn_prep",
    )(P, P, P, P, cw, alog, dtb)


def _time_cumsum(x, rev):
    k = 1
    while k < x.shape[0]:
        x = x + _shift_rows(x, k, 0.0, rev)
        k *= 2
    return x


def _gdn_local_kernel(qkv_ref, gates_ref, a_out, qk_out):
    C = GDN_CHUNK
    ri = lax.broadcasted_iota(jnp.int32, (C, C), 0)
    ci = lax.broadcasted_iota(jnp.int32, (C, C), 1)
    for c in range(qkv_ref.shape[0] // C):
        rows = slice(c * C, (c + 1) * C)
        gt = gates_ref[rows, :]
        zs = []
        for h in range(HEADS):
            q = qkv_ref[rows, h * HEAD_DIM:(h + 1) * HEAD_DIM]
            kk = qkv_ref[rows, GROUP_W + h * HEAD_DIM:GROUP_W + (h + 1) * HEAD_DIM]
            zs.append(_dot_nt(jnp.concatenate([kk, q], axis=0), kk))
        for d in range(2):
            rev = d == 1
            incl = (ci >= ri) if rev else (ci <= ri)
            strict = (ci > ri) if rev else (ci < ri)
            gcum = _time_cumsum(gt, rev)
            grow = gcum.T
            for h in range(HEADS):
                lg = d * HEADS + h
                gc_c = gcum[:, lg:lg + 1]
                gc_r = grow[lg:lg + 1, :]
                beta = gt[:, 2 * HEADS + lg:2 * HEADS + lg + 1]
                decay = jnp.where(incl, jnp.exp(jnp.where(incl, gc_c - gc_r, 0.0)), 0.0)
                lanes = slice((h % 2) * C, (h % 2 + 1) * C)
                a_out[d, c * 2 + h // 2, :, lanes] = jnp.where(strict, zs[h][:C] * beta * decay, 0.0)
                qk_out[d, c * 2 + h // 2, :, lanes] = zs[h][C:] * decay


def _gdn_local(qkv, gates):
    M = qkv.shape[0]
    tm = 512
    npair = tm // GDN_CHUNK * (HEADS // 2)
    w = 3 * GROUP_W
    ospec = pl.BlockSpec((2, npair, GDN_CHUNK, LANES), lambda i: (0, i, 0, 0))
    sds = jax.ShapeDtypeStruct((2, M // GDN_CHUNK * (HEADS // 2), GDN_CHUNK, LANES), F32)
    return pl.pallas_call(
        _gdn_local_kernel,
        grid=(M // tm,),
        in_specs=[pl.BlockSpec((tm, w), lambda i: (i, 0)), pl.BlockSpec((tm, LANES), lambda i: (i, 0))],
        out_specs=[ospec, ospec],
        out_shape=[sds, sds],
        compiler_params=_cparams(("parallel",)),
        name="gdn_local",
    )(qkv, gates)


SOLVE_BATCH = 128


def _gdn_solve_kernel(upper, a_ref, o_ref, at_ref, m_ref):
    C = GDN_CHUNK
    ns = C // SUBLANES
    for i in range(C):
        xt = a_ref[pl.ds(i, SOLVE_BATCH, stride=C), :].T
        at_ref[i, 0] = xt[:C]
        at_ref[i, 1] = xt[C:]
    sub = lax.broadcasted_iota(jnp.int32, (SUBLANES, SOLVE_BATCH), 0)
    for t in range(C):
        i = C - 1 - t if upper else t
        si = i // SUBLANES
        slabs = list(range(si, ns)) if upper else list(range(si + 1))
        unit = jnp.where(sub == i % SUBLANES, 1.0, 0.0).astype(F32)
        zero = jnp.zeros((SUBLANES, SOLVE_BATCH), F32)
        init = tuple(unit if s == si else zero for _ in range(2) for s in slabs)

        def col_body(j, acc):
            new = []
            for hp in range(2):
                a = jnp.broadcast_to(at_ref[i, hp, pl.ds(j, 1), :], (SUBLANES, SOLVE_BATCH))
                for n, s in enumerate(slabs):
                    new.append(acc[hp * len(slabs) + n] - a * m_ref[hp, j, s * SUBLANES:(s + 1) * SUBLANES, :])
            return tuple(new)

        lo, hi = (i + 1, C) if upper else (0, i)
        if hi > lo:
            acc = lax.fori_loop(lo, hi, col_body, init, unroll=min(4, hi - lo))
        else:
            acc = init
        for hp in range(2):
            for s in range(ns):
                val = acc[hp * len(slabs) + slabs.index(s)] if s in slabs else zero
                m_ref[hp, i, s * SUBLANES:(s + 1) * SUBLANES, :] = val
    for i in range(C):
        y = jnp.concatenate([m_ref[0, i], m_ref[1, i]], axis=0)
        o_ref[i] = y.T


def _gdn_solve(a_all):
    _, NP, C, _ = a_all.shape
    rows = SOLVE_BATCH * C
    a2 = a_all.reshape(2, NP * C, LANES)
    outs = []
    for d in range(2):
        spec = pl.BlockSpec((None, rows, LANES), lambda b, d=d: (d, b, 0))
        outs.append(pl.pallas_call(
            functools.partial(_gdn_solve_kernel, d == 1),
            grid=(NP // SOLVE_BATCH,),
            in_specs=[spec],
            out_specs=pl.BlockSpec((None, C, SOLVE_BATCH, LANES), lambda b: (b, 0, 0, 0)),
            out_shape=jax.ShapeDtypeStruct((NP // SOLVE_BATCH, C, SOLVE_BATCH, LANES), F32),
            scratch_shapes=[pltpu.VMEM((C, 2, C, SOLVE_BATCH), F32), pltpu.VMEM((2, C, C, SOLVE_BATCH), F32)],
            compiler_params=_cparams(("parallel",)),
            name="gdn_solve_upper" if d else "gdn_solve_lower",
        )(a2))
    return outs


def _gdn_kernel(rev, finish, d, *refs):
    if finish:
        (qkv_ref, gates_ref, qk_ref, s0_ref, of_ref, gate_ref, ng_ref, *tinv_refs) = refs[:-4]
        out_ref, sout_ref, s_ref, o_ref = refs[-4:]
    else:
        (qkv_ref, gates_ref, qk_ref, s0_ref, *tinv_refs) = refs[:-3]
        out_ref, sout_ref, s_ref = refs[-3:]
        o_ref = out_ref
    C = GDN_CHUNK
    B = qkv_ref.shape[0]
    nchunk = qkv_ref.shape[1] // C
    j = pl.program_id(0)

    @pl.when(j == 0)
    def _():
        s_ref[...] = s0_ref[...]

    last_row = 0 if rev else C - 1
    order = [(nchunk - 1 - t) if rev else t for t in range(nchunk)]
    units = [(b, h) for b in range(B) for h in range(HEADS)]

    def local(c):
        rows = slice(c * C, (c + 1) * C)
        out = {}
        for b in range(B):
            gt = gates_ref[b, rows, :]
            gcum = _time_cumsum(gt, rev)
            for h in range(HEADS):
                lg = d * HEADS + h
                lanes = slice((h % 2) * C, (h % 2 + 1) * C)
                gc_c = gcum[:, lg:lg + 1]
                beta = gt[:, 2 * HEADS + lg:2 * HEADS + lg + 1]
                gl = gcum[last_row:last_row + 1, lg:lg + 1]
                q = qkv_ref[b, rows, h * HEAD_DIM:(h + 1) * HEAD_DIM]
                kk = qkv_ref[b, rows, GROUP_W + h * HEAD_DIM:GROUP_W + (h + 1) * HEAD_DIM]
                v = qkv_ref[b, rows, 2 * GROUP_W + h * HEAD_DIM:2 * GROUP_W + (h + 1) * HEAD_DIM]
                tinv = tinv_refs[b][:, c * 2 + h // 2, lanes]
                qk = qk_ref[b, c * 2 + h // 2, :, lanes]
                kb = kk * beta
                eg = jnp.exp(gc_c)
                uw = _dot(tinv, jnp.concatenate([v * beta, kb * eg], axis=1))
                wq = jnp.concatenate([uw[:, HEAD_DIM:], q * eg], axis=0).astype(BF16)
                kdt = (kk * jnp.exp(gl - gc_c)).T
                qkk = jnp.concatenate([qk, kdt], axis=0).astype(BF16)
                out[b, h] = (uw[:, :HEAD_DIM], wq, qkk, jnp.exp(gl))
        return out

    states = {(b, h): s_ref[b, h] for b, h in units}
    nxt = local(order[0])
    for t, c in enumerate(order):
        cur = nxt
        ws = {k: jnp.dot(cur[k][1], states[k].astype(BF16), preferred_element_type=F32) for k in units}
        if t + 1 < nchunk:
            nxt = local(order[t + 1])
        for b, h in units:
            u, _, qkk, decay = cur[b, h]
            v_new = u - ws[b, h][:C]
            res = jnp.dot(qkk, v_new.astype(BF16), preferred_element_type=F32)
            o_ref[b, c * C:(c + 1) * C, h * HEAD_DIM:(h + 1) * HEAD_DIM] = ws[b, h][C:] + res[:C]
            states[b, h] = states[b, h] * decay + res[C:]
    for b, h in units:
        s_ref[b, h] = states[b, h]
    sout_ref[...] = s_ref[...]
    if finish:
        for b in range(B):
            tot = of_ref[b] + o_ref[b]
            gate = gate_ref[b]
            for h in range(HEADS):
                sl = slice(h * HEAD_DIM, (h + 1) * HEAD_DIM)
                x = tot[:, sl]
                y = x * lax.rsqrt(jnp.mean(x * x, axis=-1, keepdims=True) + EPS) * ng_ref[...]
                out_ref[b, :, sl] = (y * _silu(gate[:, sl])).astype(out_ref.dtype)


def _gdn_pass(qkv, gates, tinv, qk, seq_len, rev, d, s0, of=None, P=None, ng=None):
    M = qkv.shape[0]
    B = M // seq_len
    rb = 256
    nblk = seq_len // rb
    finish = of is not None
    w = 3 * GROUP_W
    npair = rb // GDN_CHUNK * (HEADS // 2)
    per_batch = SOLVE_BATCH // npair

    def blk(j):
        return (nblk - 1 - j) if rev else j

    sspec = pl.BlockSpec((B, HEADS, HEAD_DIM, HEAD_DIM), lambda j: (0, 0, 0, 0))
    specs = [pl.BlockSpec((B, rb, w), lambda j: (0, blk(j), 0)),
             pl.BlockSpec((B, rb, LANES), lambda j: (0, blk(j), 0)),
             pl.BlockSpec((None, B, npair, GDN_CHUNK, LANES), lambda j: (d, 0, blk(j), 0, 0)),
             sspec]
    args = [qkv.reshape(B, seq_len, w), gates.reshape(B, seq_len, LANES),
            qk.reshape(2, B, nblk * npair, GDN_CHUNK, LANES), s0]
    scratch = [pltpu.VMEM((B, HEADS, HEAD_DIM, HEAD_DIM), F32)]
    if finish:
        specs += [pl.BlockSpec((B, rb, GROUP_W), lambda j: (0, blk(j), 0)),
                  pl.BlockSpec((B, rb, GROUP_W), lambda j: (0, blk(j), P_GDN_GATE // GROUP_W)),
                  pl.BlockSpec((1, HEAD_DIM), lambda j: (0, 0))]
        args += [of.reshape(B, seq_len, GROUP_W), P.reshape(B, seq_len, P.shape[1]), ng]
        scratch.append(pltpu.VMEM((B, rb, GROUP_W), F32))
    for b in range(B):
        def tmap(j, b=b):
            g = b * nblk + blk(j)
            return (g // per_batch, 0, g % per_batch, 0)
        specs.append(pl.BlockSpec((None, GDN_CHUNK, npair, LANES), tmap))
        args.append(tinv)
    out, s_out = pl.pallas_call(
        functools.partial(_gdn_kernel, rev, finish, d),
        grid=(nblk,),
        in_specs=specs,
        out_specs=[pl.BlockSpec((B, rb, GROUP_W), lambda j: (0, blk(j), 0)), sspec],
        out_shape=[jax.ShapeDtypeStruct((B, seq_len, GROUP_W), BF16 if finish else F32),
                   jax.ShapeDtypeStruct((B, HEADS, HEAD_DIM, HEAD_DIM), F32)],
        scratch_shapes=scratch,
        compiler_params=_cparams(("arbitrary",)),
        name="gdn_bwd" if rev else "gdn_fwd",
    )(*args)
    return out.reshape(M, GROUP_W), s_out


def _mixer_gdn(Pc, Pl, ctx_len, seq_len, cw, alog, dtb, ng):
    B = Pc.shape[0] // ctx_len
    qkv_c, g_c = _gdn_prep(Pc, ctx_len, cw, alog, dtb)
    qkv_l, g_l = _gdn_prep(Pl, seq_len, cw, alog, dtb)
    a_c, qk_c = _gdn_local(qkv_c, g_c)
    a_l, qk_l = _gdn_local(qkv_l, g_l)
    def solve(a):
        n_pad = -a.shape[1] % SOLVE_BATCH
        if n_pad:
            a = jnp.concatenate([a, jnp.zeros((2, n_pad, GDN_CHUNK, LANES), F32)], axis=1)
        return _gdn_solve(a)

    tinv_c, tinv_l = solve(a_c), solve(a_l)
    zero = jnp.zeros((B, HEADS, HEAD_DIM, HEAD_DIM), F32)
    oc_f, sc_f = _gdn_pass(qkv_c, g_c, tinv_c[0], qk_c, ctx_len, False, 0, zero)
    ol_f, _ = _gdn_pass(qkv_l, g_l, tinv_l[0], qk_l, seq_len, False, 0, sc_f)
    yc, sc_b = _gdn_pass(qkv_c, g_c, tinv_c[1], qk_c, ctx_len, True, 1, zero, of=oc_f, P=Pc, ng=ng)
    yl, _ = _gdn_pass(qkv_l, g_l, tinv_l[1], qk_l, seq_len, True, 1, sc_b, of=ol_f, P=Pl, ng=ng)
    return yc, yl


def _pack_w_in(w_in):
    off = {}
    o = 0
    for name, n in (('lru_x', 512), ('lru_gate', 512), ('mla_cq', 384), ('mla_ckv', 256), ('mla_kr', 64),
                    ('gdn_q', 512), ('gdn_k', 512), ('gdn_v', 512), ('gdn_gate', 512),
                    ('gdn_beta', 8), ('gdn_alpha', 8), ('na_q', 512), ('na_k', 512), ('na_v', 512)):
        off[name] = (o, n)
        o += n
    order = ['gdn_q', 'gdn_k', 'gdn_v', 'gdn_gate', 'lru_x', 'lru_gate', 'na_q', 'na_k', 'na_v',
             'mla_cq', 'mla_ckv', 'mla_kr', 'gdn_beta', 'gdn_alpha']
    w_in = w_in.astype(BF16)
    parts = [w_in[..., off[n][0]:off[n][0] + off[n][1]] for n in order]
    used = sum(off[n][1] for n in order)
    parts.append(jnp.zeros(w_in.shape[:-1] + (P_COLS - used,), w_in.dtype))
    return jnp.concatenate(parts, axis=-1)


def _pack_w_uq(w_uq):
    L, K, _ = w_uq.shape
    w = w_uq.reshape(L, K, HEADS, MLA_QK)
    w = jnp.pad(w, ((0, 0), (0, 0), (0, 0), (0, MLA_QPAD - MLA_QK)))
    return w.reshape(L, K, HEADS * MLA_QPAD).astype(BF16)


def _pad_gain(g):
    return jnp.pad(g, ((0, 0), (0, MLA_QPAD - MLA_QK)))[:, None, :]


def _rope_tables(T):
    t = jnp.arange(T)
    rowp = (t // GRID_W).astype(F32)
    colp = (t % GRID_W).astype(F32)
    n_freq = MLA_ROPE // 4
    inv = ROPE_BASE ** (-jnp.arange(n_freq, dtype=F32) / n_freq)
    ang = jnp.concatenate([rowp[:, None] * inv, colp[:, None] * inv], axis=-1)
    cos, sin = jnp.cos(ang), jnp.sin(ang)
    z = jnp.zeros((T, LANES - MLA_ROPE), F32)
    return (jnp.concatenate([cos, cos, z], axis=-1), jnp.concatenate([-sin, sin, z], axis=-1))


def _lane_vec(x):
    L = x.shape[0]
    return jnp.pad(x.reshape(L, 2 * HEADS), ((0, 0), (0, LANES - 2 * HEADS)))[:, None, :]


def kernel(x, c, ctx, c_ctx, ada_w, ada_b, norm_mix_g, norm_ffn_g, w_in, w_out, lru_conv_w, lru_conv_b, lru_wa, lru_ba, lru_wx, lru_bx, lru_lam, mla_qa_g, mla_w_uq, mla_kva_g, mla_w_ukv, mla_qn_g, mla_kn_g, gdn_conv_w, gdn_a_log, gdn_dt_bias, gdn_norm_g, na_qn_g, na_kn_g, na_rpb, ffn_w_up, ffn_conv_w, ffn_conv_b, ffn_w_down):
    B, T, D = x.shape
    TC = ctx.shape[1]
    L = ada_w.shape[0]
    rows = T // GRID_W
    assert B <= 2 and T % 512 == 0 and TC == 256 and rows >= NA_WIN

    w_in_p = _pack_w_in(w_in)
    w_out4 = w_out.reshape(L, 4, GROUP_W, D).astype(BF16)
    w_up = ffn_w_up.astype(BF16)
    w_down = ffn_w_down.astype(BF16)
    w_uq = _pack_w_uq(mla_w_uq)
    w_ukv = mla_w_ukv.astype(BF16)
    qn_g = _pad_gain(mla_qn_g)
    kn_g = _pad_gain(mla_kn_g)
    lru_wcat = jnp.concatenate([lru_wa, lru_wx], axis=-1).astype(BF16)
    alog_v = _lane_vec(gdn_a_log)
    dtb_v = _lane_vec(gdn_dt_bias)
    cosf, sinf = _rope_tables(T)

    cvec = jnp.zeros((SUBLANES, D), F32).at[:B].set(c).at[2].set(c_ctx)
    mod_all = _modulation(cvec, ada_w, ada_b).reshape(L, SUBLANES, N_MOD, 1, D)

    gmix = norm_mix_g[:, None, :]
    gffn = norm_ffn_g[:, None, :]
    ffn_cb = ffn_conv_b[:, None, :]
    h_lat = x.reshape(B * T, D)
    h_ctx = ctx.reshape(B * TC, D)
    for l in range(L):
        want_ctx = l < L - 1
        mod = mod_all
        Pl = _inproj(l, h_lat, gmix, mod, w_in_p, T, False)
        Pc = _inproj(l, h_ctx, gmix, mod, w_in_p, TC, True)

        lw = lambda dd: (lru_conv_w[l], lru_conv_b[l][None, :], lru_wcat[l, dd], lru_ba[l, dd][None, :],
                         lru_bx[l, dd][None, :], lru_lam[l, dd][None, :])
        ya_c, ya_l = _mixer_lru(Pc, Pl, TC, T, lw(0), lw(1))

        mw = (mla_qa_g[l][None, :], w_uq[l], mla_kva_g[l][None, :], w_ukv[l], qn_g[l], kn_g[l])
        Qtc, Kc, Vtc = _mla_prep(Pc, TC, False, mw, cosf, sinf)
        Qtl, Kl, Vtl = _mla_prep(Pl, T, True, mw, cosf, sinf)
        yb_l = _flash_attention(Qtl, Kc, Vtc, T, TC, Kl, Vtl)

        yc_c, yc_l = _mixer_gdn(Pc, Pl, TC, T, gdn_conv_w[l], alog_v[l], dtb_v[l], gdn_norm_g[l][None, :])

        nqg, nkg = na_qn_g[l][None, :], na_kn_g[l][None, :]
        NQtc, NKc, NVtc = _na_prep(Pc, nqg, nkg)
        NQtl, NKl, NVtl = _na_prep(Pl, nqg, nkg)
        bias = _na_bias(na_rpb[l], rows)
        yd_l = _na_attention(NQtl, NKl, NVtl, NKc, NVtc, bias, T, TC)

        h_lat = _outproj(l, (ya_l, yb_l, yc_l, yd_l), w_out4, h_lat, mod, T, False)
        h_lat = _ffn(l, h_lat, gffn, mod, w_up, ffn_conv_w, ffn_cb, w_down, T, False)
        if want_ctx:
            yb_c = _flash_attention(Qtc, Kc, Vtc, TC, TC)
            yd_c = _flash_attention(NQtc, NKc, NVtc, TC, TC)
            h_ctx = _outproj(l, (ya_c, yb_c, yc_c, yd_c), w_out4, h_ctx, mod, TC, True)
            h_ctx = _ffn(l, h_ctx, gffn, mod, w_up, ffn_conv_w, ffn_cb, w_down, TC, True)
    return h_lat.reshape(B, T, D)
```

```python
import functools
import math

import jax
import jax.numpy as jnp
from jax import lax
from jax.experimental import pallas as pl
from jax.experimental.pallas import tpu as pltpu

F32 = jnp.float32
BF16 = jnp.bfloat16

GRID_W = 64
HEADS = 4
HEAD_DIM = 128
GROUP_W = HEADS * HEAD_DIM
N_MOD = 6
EPS = 1e-6
LRU_C = 8.0
MLA_Q_LORA = 384
MLA_KV_LORA = 256
MLA_NOPE = 128
MLA_ROPE = 64
MLA_QK = MLA_NOPE + MLA_ROPE
MLA_QPAD = 256
ROPE_BASE = 10000.0
GDN_CHUNK = 64
NA_KH = 8
NA_KW = 16
NA_BAND = 8
NA_WIN = 16
NEG_BIG = -1e30
LOG2E = math.log2(math.e)
TINY = 1e-30
FLASH_UNROLL = 8
FFN_SUB = 256
INPROJ_SUB = 4

VMEM_LIMIT = 56 * 1024 * 1024
SUBLANES = 8
LANES = 128

P_GDN_QKV = 0
P_GDN_GATE = 1536
P_LRU_X = 2048
P_LRU_GATE = 2560
P_NA_QKV = 3072
P_MLA = 4608
P_COLS = 5376
P_TAIL = 5248
TAIL_BETA = 64
TAIL_ALPHA = 72


def _cparams(sem):
    return pltpu.CompilerParams(dimension_semantics=sem, vmem_limit_bytes=VMEM_LIMIT)


def _dot(a, b):
    return jnp.dot(a.astype(BF16), b.astype(BF16), preferred_element_type=F32)


def _dot_nt(a, b):
    return lax.dot_general(a.astype(BF16), b.astype(BF16), (((1,), (1,)), ((), ())),
                           preferred_element_type=F32)


def _sigmoid(x):
    return 1.0 / (1.0 + jnp.exp(-x))


def _silu(x):
    return x * _sigmoid(x)


def _softplus(x):
    return jnp.maximum(x, 0.0) + jnp.log(1.0 + jnp.exp(-jnp.abs(x)))


def _gelu_tanh(x):
    return 0.5 * x * (1.0 + jnp.tanh(math.sqrt(2.0 / math.pi) * (x + 0.044715 * x * x * x)))


def _mod_kernel(c_ref, w_ref, b_ref, o_ref):
    c = c_ref[...]
    o_ref[...] = _dot(_silu(c), w_ref[...]) + b_ref[...]


def _modulation(cvec, ada_w, ada_b):
    L, D, N = ada_w.shape
    tn = 1024
    return pl.pallas_call(
        _mod_kernel,
        grid=(L, N // tn),
        in_specs=[pl.BlockSpec((SUBLANES, D), lambda l, j: (0, 0)),
                  pl.BlockSpec((None, D, tn), lambda l, j: (l, 0, j)),
                  pl.BlockSpec((None, 1, tn), lambda l, j: (l, 0, j))],
        out_specs=pl.BlockSpec((None, SUBLANES, tn), lambda l, j: (l, 0, j)),
        out_shape=jax.ShapeDtypeStruct((L, SUBLANES, N), F32),
        compiler_params=_cparams(("parallel", "parallel")),
        name="modulation",
    )(cvec, ada_w, ada_b.reshape(L, 1, N))


def _mod_spec(l, which, rows_per_mod, D, ngrid):
    if rows_per_mod is None:
        row = lambda i: 2
    else:
        row = lambda i: i // rows_per_mod
    if ngrid == 1:
        return pl.BlockSpec((None, None, None, 1, D), lambda i: (l, row(i), which, 0, 0))
    return pl.BlockSpec((None, None, None, 1, D), lambda i, j: (l, row(i), which, 0, 0))


def _inproj_kernel(h_ref, g_ref, shift_ref, scale_ref, w_ref, o_ref, xn_ref):
    tm = h_ref.shape[0]

    @pl.when(pl.program_id(1) == 0)
    def _():
        rs = tm // min(INPROJ_SUB, tm // 256)
        for r0 in range(0, tm, rs):
            x = h_ref[r0:r0 + rs, :]
            y = x * lax.rsqrt(jnp.mean(x * x, axis=-1, keepdims=True) + EPS) * g_ref[...]
            xb = (y * (1.0 + scale_ref[...]) + shift_ref[...]).astype(BF16)
            xn_ref[r0:r0 + rs, :] = xb
            o_ref[r0:r0 + rs, :] = jnp.dot(xb, w_ref[...], preferred_element_type=F32)

    @pl.when(pl.program_id(1) > 0)
    def _():
        o_ref[...] = jnp.dot(xn_ref[...], w_ref[...], preferred_element_type=F32)


def _inproj(l, h, gain, mod, w, seq_len, is_ctx):
    M, D = h.shape
    N = w.shape[2]
    tm = min(1024, M)
    tn = 768
    rpm = None if is_ctx else seq_len // tm
    return pl.pallas_call(
        _inproj_kernel,
        grid=(M // tm, N // tn),
        in_specs=[pl.BlockSpec((tm, D), lambda i, j: (i, 0)),
                  pl.BlockSpec((None, 1, D), lambda i, j: (l, 0, 0)),
                  _mod_spec(l, 0, rpm, D, 2),
                  _mod_spec(l, 1, rpm, D, 2),
                  pl.BlockSpec((None, D, tn), lambda i, j: (l, 0, j))],
        out_specs=pl.BlockSpec((tm, tn), lambda i, j: (i, j)),
        out_shape=jax.ShapeDtypeStruct((M, N), F32),
        scratch_shapes=[pltpu.VMEM((tm, D), BF16)],
        compiler_params=_cparams(("parallel", "arbitrary")),
        name="inproj",
    )(h, gain, mod, mod, w)


def _outproj_kernel(ya_ref, yb_ref, yc_ref, yd_ref, w_ref, h_ref, gate_ref, o_ref):
    acc = jnp.dot(ya_ref[...], w_ref[0], preferred_element_type=F32)
    acc += jnp.dot(yb_ref[...], w_ref[1], preferred_element_type=F32)
    acc += jnp.dot(yc_ref[...], w_ref[2], preferred_element_type=F32)
    acc += jnp.dot(yd_ref[...], w_ref[3], preferred_element_type=F32)
    o_ref[...] = h_ref[...] + gate_ref[...] * acc


def _outproj(l, ys, w4, h, mod, seq_len, is_ctx):
    M, D = h.shape
    tm = min(512, M)
    rpm = None if is_ctx else seq_len // tm
    yspec = pl.BlockSpec((tm, GROUP_W), lambda i: (i, 0))
    return pl.pallas_call(
        _outproj_kernel,
        grid=(M // tm,),
        in_specs=[yspec, yspec, yspec, yspec,
                  pl.BlockSpec((None, 4, GROUP_W, D), lambda i: (l, 0, 0, 0)),
                  pl.BlockSpec((tm, D), lambda i: (i, 0)),
                  _mod_spec(l, 2, rpm, D, 1)],
        out_specs=pl.BlockSpec((tm, D), lambda i: (i, 0)),
        out_shape=jax.ShapeDtypeStruct((M, D), F32),
        compiler_params=_cparams(("parallel",)),
        name="outproj",
    )(*ys, w4, h, mod)


def _halo_specs(tm, width, col_block, nrows, ngrid):
    r = tm // SUBLANES
    last = nrows // SUBLANES - 1
    if ngrid == 1:
        prev = pl.BlockSpec((SUBLANES, width), lambda i: (jnp.maximum(i * r - 1, 0), col_block))
        nxt = pl.BlockSpec((SUBLANES, width), lambda i: (jnp.minimum((i + 1) * r, last), col_block))
    else:
        prev = pl.BlockSpec((SUBLANES, width), lambda i, j: (jnp.maximum(i * r - 1, 0), col_block))
        nxt = pl.BlockSpec((SUBLANES, width), lambda i, j: (jnp.minimum((i + 1) * r, last), col_block))
    return prev, nxt


def _ffn_kernel(seq_len, h_ref, hp_ref, hn_ref, g_ref, shift_ref, scale_ref, gate_ref,
                wa_ref, wg_ref, cwa_ref, cwg_ref, cba_ref, cbg_ref, wd_ref, o_ref, xn_ref):
    tm = h_ref.shape[0]
    tf = wa_ref.shape[1]
    i = pl.program_id(0)
    j = pl.program_id(1)

    @pl.when(j == 0)
    def _():
        def norm(x):
            y = x * lax.rsqrt(jnp.mean(x * x, axis=-1, keepdims=True) + EPS) * g_ref[...]
            return (y * (1.0 + scale_ref[...]) + shift_ref[...]).astype(BF16)
        keep_prev = (i * tm) % seq_len != 0
        keep_next = ((i + 1) * tm) % seq_len != 0
        xn_ref[0:SUBLANES, :] = jnp.where(keep_prev, norm(hp_ref[...]), jnp.zeros((), BF16))
        xn_ref[SUBLANES:SUBLANES + tm, :] = norm(h_ref[...])
        xn_ref[SUBLANES + tm:, :] = jnp.where(keep_next, norm(hn_ref[...]), jnp.zeros((), BF16))

    def conv(u, cw_ref, cb_ref, cols):
        lo = u[SUBLANES - 1:SUBLANES - 1 + tm]
        mid = u[SUBLANES:SUBLANES + tm]
        hi = u[SUBLANES + 1:SUBLANES + 1 + tm]
        return lo * cw_ref[0:1, cols] + mid * cw_ref[1:2, cols] + hi * cw_ref[2:3, cols] + cb_ref[:, cols]

    xn = xn_ref[...]
    subs = [slice(s, s + FFN_SUB) for s in range(0, tf, FFN_SUB)]
    ups = [(jnp.dot(xn, wa_ref[:, c], preferred_element_type=F32),
            jnp.dot(xn, wg_ref[:, c], preferred_element_type=F32)) for c in subs]
    acts = [(conv(ua, cwa_ref, cba_ref, c) * _silu(conv(ug, cwg_ref, cbg_ref, c))).astype(BF16)
            for c, (ua, ug) in zip(subs, ups)]
    part = jnp.dot(jnp.concatenate(acts, axis=1), wd_ref[...], preferred_element_type=F32)

    @pl.when(j == 0)
    def _():
        o_ref[...] = part

    @pl.when(j > 0)
    def _():
        o_ref[...] += part

    @pl.when(j == pl.num_programs(1) - 1)
    def _():
        o_ref[...] = h_ref[...] + gate_ref[...] * o_ref[...]


def _ffn(l, h, gain, mod, w_up, conv_w, conv_b, w_down, seq_len, is_ctx):
    M, D = h.shape
    FF = w_down.shape[1]
    tm = min(512, seq_len)
    tf = 1280
    nf = FF // tf
    rpm = None if is_ctx else seq_len // tm
    prev, nxt = _halo_specs(tm, D, 0, M, 2)
    return pl.pallas_call(
        functools.partial(_ffn_kernel, seq_len),
        grid=(M // tm, nf),
        in_specs=[pl.BlockSpec((tm, D), lambda i, j: (i, 0)), prev, nxt,
                  pl.BlockSpec((None, 1, D), lambda i, j: (l, 0, 0)),
                  _mod_spec(l, 3, rpm, D, 2), _mod_spec(l, 4, rpm, D, 2), _mod_spec(l, 5, rpm, D, 2),
                  pl.BlockSpec((None, D, tf), lambda i, j: (l, 0, j)),
                  pl.BlockSpec((None, D, tf), lambda i, j: (l, 0, j + nf)),
                  pl.BlockSpec((None, 3, tf), lambda i, j: (l, 0, j)),
                  pl.BlockSpec((None, 3, tf), lambda i, j: (l, 0, j + nf)),
                  pl.BlockSpec((None, 1, tf), lambda i, j: (l, 0, j)),
                  pl.BlockSpec((None, 1, tf), lambda i, j: (l, 0, j + nf)),
                  pl.BlockSpec((None, tf, D), lambda i, j: (l, j, 0))],
        out_specs=pl.BlockSpec((tm, D), lambda i, j: (i, 0), pipeline_mode=pl.Buffered(1)),
        out_shape=jax.ShapeDtypeStruct((M, D), F32),
        scratch_shapes=[pltpu.VMEM((tm + 2 * SUBLANES, D), BF16)],
        compiler_params=_cparams(("parallel", "arbitrary")),
        name="conv_ffn",
    )(h, h, h, gain, mod, mod, mod, w_up, w_up, conv_w, conv_w, conv_b, conv_b, w_down)


def _conv4(xp_ref, x_ref, xn_ref, w_ref, start, seq_len):
    tm = x_ref.shape[0]
    xp = jnp.where(start != 0, xp_ref[...], 0.0)
    xn = jnp.where(start + tm != seq_len, xn_ref[...], 0.0)
    xe = jnp.concatenate([xp, x_ref[...], xn], axis=0)
    t0 = xe[SUBLANES - 1:SUBLANES - 1 + tm]
    t1 = xe[SUBLANES:SUBLANES + tm]
    t2 = xe[SUBLANES + 1:SUBLANES + 1 + tm]
    t3 = xe[SUBLANES + 2:SUBLANES + 2 + tm]
    return t0 * w_ref[0:1, :] + t1 * w_ref[1:2, :] + t2 * w_ref[2:3, :] + t3 * w_ref[3:4, :]


def _shift_rows(x, k, fill, rev):
    n = x.shape[0]
    if k % SUBLANES == 0:
        pad = jnp.full((k, x.shape[1]), fill, x.dtype)
        return jnp.concatenate([x[k:], pad], 0) if rev else jnp.concatenate([pad, x[:n - k]], 0)
    row = lax.broadcasted_iota(jnp.int32, x.shape, 0)
    if rev:
        return jnp.where(row >= n - k, fill, pltpu.roll(x, n - k, 0))
    return jnp.where(row < k, fill, pltpu.roll(x, k, 0))


def _lru_kernel(rev, finish, seq_len, *refs):
    if finish:
        (x_ref, xp_ref, xn_ref, gate_ref, hf_ref, cw_ref, cb_ref, wcat_ref, ba_ref, bx_ref,
         lam_ref, h0_ref, out_ref, st_ref, carry_ref) = refs
    else:
        (x_ref, xp_ref, xn_ref, cw_ref, cb_ref, wcat_ref, ba_ref, bx_ref,
         lam_ref, h0_ref, out_ref, st_ref, carry_ref) = refs
    tc = x_ref.shape[0]
    j = pl.program_id(1)
    nch = pl.num_programs(1)
    c = (nch - 1 - j) if rev else j

    @pl.when(j == 0)
    def _():
        carry_ref[...] = h0_ref[...]

    u = _conv4(xp_ref, x_ref, xn_ref, cw_ref, c * tc, seq_len) + cb_ref[...]
    rs, is_ = [], []
    for n in range(HEADS):
        z = _dot(u[:, n * HEAD_DIM:(n + 1) * HEAD_DIM], wcat_ref[n])
        rs.append(z[:, :HEAD_DIM])
        is_.append(z[:, HEAD_DIM:])
    r = _sigmoid(jnp.concatenate(rs, axis=1) + ba_ref[...])
    ig = _sigmoid(jnp.concatenate(is_, axis=1) + bx_ref[...])
    log_a = -LRU_C * r * _softplus(-lam_ref[...])
    a = jnp.exp(log_a)
    om = 1.0 - a * a
    b = om * lax.rsqrt(jnp.maximum(om, TINY)) * ig * u
    nslab = tc // SUBLANES
    a = a.reshape(nslab, SUBLANES, a.shape[1])
    b = b.reshape(nslab, SUBLANES, b.shape[1])
    sub = lax.broadcasted_iota(jnp.int32, a.shape, 1)
    k = 1
    while k < SUBLANES:
        own = (sub >= SUBLANES - k) if rev else (sub < k)
        shift = SUBLANES - k if rev else k
        b = a * jnp.where(own, 0.0, pltpu.roll(b, shift, 1)) + b
        a = a * jnp.where(own, 1.0, pltpu.roll(a, shift, 1))
        k *= 2
    a = a.reshape(tc, a.shape[2])
    b = b.reshape(tc, b.shape[2])
    state = carry_ref[0:1, :]
    hs = [None] * nslab
    for t in range(nslab):
        s = nslab - 1 - t if rev else t
        rows = slice(s * SUBLANES, (s + 1) * SUBLANES)
        hs[s] = b[rows] + a[rows] * state
        state = hs[s][0:1, :] if rev else hs[s][SUBLANES - 1:SUBLANES, :]
    h = jnp.concatenate(hs, axis=0)
    last = state
    carry_ref[...] = jnp.broadcast_to(last, carry_ref.shape)
    st_ref[...] = jnp.broadcast_to(last, st_ref.shape)
    if finish:
        out_ref[...] = ((hf_ref[...] + h) * _gelu_tanh(gate_ref[...])).astype(out_ref.dtype)
    else:
        out_ref[...] = h


def _lru_pass(P, seq_len, rev, h0, wts, hf=None):
    cw, cb, wcat, ba, bx, lam = wts
    M = P.shape[0]
    B = M // seq_len
    tc = min(512, seq_len)
    nch = seq_len // tc
    finish = hf is not None
    W = GROUP_W
    xb = P_LRU_X // W
    gb = P_LRU_GATE // W
    r = tc // SUBLANES
    last = M // SUBLANES - 1

    def chunk(j):
        return (nch - 1 - j) if rev else j

    row = lambda b, j: (b * nch + chunk(j), xb)
    specs = [pl.BlockSpec((tc, W), row),
             pl.BlockSpec((SUBLANES, W), lambda b, j: (jnp.maximum((b * nch + chunk(j)) * r - 1, 0), xb)),
             pl.BlockSpec((SUBLANES, W), lambda b, j: (jnp.minimum((b * nch + chunk(j) + 1) * r, last), xb))]
    args = [P, P, P]
    if finish:
        specs += [pl.BlockSpec((tc, W), lambda b, j: (b * nch + chunk(j), gb)),
                  pl.BlockSpec((tc, W), lambda b, j: (b * nch + chunk(j), 0))]
        args += [P, hf]
    full2 = lambda b, j: (0, 0)
    specs += [pl.BlockSpec((4, W), full2), pl.BlockSpec((1, W), full2),
              pl.BlockSpec((HEADS, HEAD_DIM, 2 * HEAD_DIM), lambda b, j: (0, 0, 0)),
              pl.BlockSpec((1, W), full2), pl.BlockSpec((1, W), full2), pl.BlockSpec((1, W), full2),
              pl.BlockSpec((None, SUBLANES, W), lambda b, j: (b, 0, 0))]
    args += [cw, cb, wcat, ba, bx, lam, h0]
    out_dtype = BF16 if finish else F32
    return pl.pallas_call(
        functools.partial(_lru_kernel, rev, finish, seq_len),
        grid=(B, nch),
        in_specs=specs,
        out_specs=[pl.BlockSpec((tc, W), lambda b, j: (b * nch + chunk(j), 0)),
                   pl.BlockSpec((None, SUBLANES, W), lambda b, j: (b, 0, 0))],
        out_shape=[jax.ShapeDtypeStruct((M, W), out_dtype),
                   jax.ShapeDtypeStruct((B, SUBLANES, W), F32)],
        scratch_shapes=[pltpu.VMEM((SUBLANES, W), F32)],
        compiler_params=_cparams(("parallel", "arbitrary")),
        name="rglru_bwd" if rev else "rglru_fwd",
    )(*args)


def _mixer_lru(Pc, Pl, ctx_len, seq_len, wts_f, wts_b):
    B = Pc.shape[0] // ctx_len
    zero = jnp.zeros((B, SUBLANES, GROUP_W), F32)
    hc_f, sc_f = _lru_pass(Pc, ctx_len, False, zero, wts_f)
    hl_f, _ = _lru_pass(Pl, seq_len, False, sc_f, wts_f)
    yc, sc_b = _lru_pass(Pc, ctx_len, True, zero, wts_b, hf=hc_f)
    yl, _ = _lru_pass(Pl, seq_len, True, sc_b, wts_b, hf=hl_f)
    return yc, yl


def _rope_mix(x, cos, sin):
    lane = lax.broadcasted_iota(jnp.int32, x.shape, 1)
    half = MLA_ROPE // 2
    swapped = jnp.where(lane < half, pltpu.roll(x, LANES - half, 1), pltpu.roll(x, half, 1))
    return x * cos + swapped * sin


def _mla_prep_kernel(use_rope, x_ref, qag_ref, wuq_ref, kvg_ref, wukv_ref, qg_ref, kg_ref,
                     cos_ref, sin_ref, qt_out, k_out, vt_out):
    x = x_ref[...]

    def rms(v, g):
        return v * lax.rsqrt(jnp.mean(v * v, axis=-1, keepdims=True) + EPS) * g

    q = _dot(rms(x[:, :MLA_Q_LORA], qag_ref[...]), wuq_ref[...])
    kv = _dot(rms(x[:, MLA_Q_LORA:MLA_Q_LORA + MLA_KV_LORA], kvg_ref[...]), wukv_ref[...])
    tail = x[:, MLA_Q_LORA + MLA_KV_LORA:]
    lane = lax.broadcasted_iota(jnp.int32, tail.shape, 1)
    kr = jnp.where(lane < MLA_ROPE, tail, 0.0)
    kr_ss = jnp.sum(kr * kr, axis=-1, keepdims=True)
    scale = MLA_QK ** -0.5 * LOG2E
    for h in range(HEADS):
        qh = q[:, h * MLA_QPAD:(h + 1) * MLA_QPAD]
        inv = lax.rsqrt(jnp.sum(qh * qh, axis=-1, keepdims=True) / MLA_QK + EPS) * scale
        qh = qh * inv * qg_ref[...]
        qn, qr = qh[:, :LANES], qh[:, LANES:]
        kn = kv[:, h * 2 * HEAD_DIM:h * 2 * HEAD_DIM + MLA_NOPE]
        vh = kv[:, h * 2 * HEAD_DIM + MLA_NOPE:(h + 1) * 2 * HEAD_DIM]
        kinv = lax.rsqrt((jnp.sum(kn * kn, axis=-1, keepdims=True) + kr_ss) / MLA_QK + EPS)
        kn = kn * kinv * kg_ref[:, :LANES]
        krh = kr * kinv * kg_ref[:, LANES:]
        if use_rope:
            qr = _rope_mix(qr, cos_ref[...], sin_ref[...])
            krh = _rope_mix(krh, cos_ref[...], sin_ref[...])
        qt_out[h] = jnp.concatenate([qn, qr], axis=1).T.astype(BF16)
        k_out[h, :, :LANES] = kn.astype(BF16)
        k_out[h, :, LANES:] = krh.astype(BF16)
        vt_out[h] = vh.T.astype(BF16)


def _mla_prep(P, seq_len, use_rope, wts, cosf, sinf):
    qag, wuq, kvg, wukv, qg, kg = wts
    M = P.shape[0]
    tm = min(512, seq_len)
    nt = seq_len // tm
    full = lambda i: (0, 0)
    wmla = P_COLS - P_MLA
    tab = pl.BlockSpec((tm, LANES), (lambda i: (i % nt, 0)) if use_rope else (lambda i: (0, 0)))
    tspec = lambda w: pl.BlockSpec((HEADS, w, tm), lambda i: (0, 0, i))
    return pl.pallas_call(
        functools.partial(_mla_prep_kernel, use_rope),
        grid=(M // tm,),
        in_specs=[pl.BlockSpec((tm, wmla), lambda i: (i, P_MLA // wmla)),
                  pl.BlockSpec((1, MLA_Q_LORA), full), pl.BlockSpec(wuq.shape, full),
                  pl.BlockSpec((1, MLA_KV_LORA), full), pl.BlockSpec(wukv.shape, full),
                  pl.BlockSpec((1, MLA_QPAD), full), pl.BlockSpec((1, MLA_QPAD), full),
                  tab, tab],
        out_specs=[tspec(MLA_QPAD), pl.BlockSpec((HEADS, tm, MLA_QPAD), lambda i: (0, i, 0)),
                   tspec(HEAD_DIM)],
        out_shape=[jax.ShapeDtypeStruct((HEADS, MLA_QPAD, M), BF16),
                   jax.ShapeDtypeStruct((HEADS, M, MLA_QPAD), BF16),
                   jax.ShapeDtypeStruct((HEADS, HEAD_DIM, M), BF16)],
        compiler_params=_cparams(("parallel",)),
        name="mla_prep",
    )(P, qag, wuq, kvg, wukv, qg, kg, cosf, sinf)


def _flash_kernel(tk, has_lat, *refs):
    if has_lat:
        qt_ref, kc_ref, vct_ref, kl_ref, vlt_ref, o_ref, m_ref, l_ref, acc_ref, s_ref = refs
    else:
        qt_ref, kc_ref, vct_ref, o_ref = refs
    qt = qt_ref[...]
    s = jnp.dot(kc_ref[...], qt, preferred_element_type=F32)
    if has_lat:
        nk = kl_ref.shape[0] // tk

        def scores(c, slot):
            off = pl.multiple_of(jnp.minimum(c, nk - 1) * tk, tk)
            s_ref[slot] = jnp.dot(kl_ref[pl.ds(off, tk), :], qt, preferred_element_type=F32)

        scores(0, 0)
    m = jnp.max(s, axis=0, keepdims=True)
    p = jnp.exp2(s - m)
    l = jnp.sum(p, axis=0, keepdims=True)
    acc = jnp.dot(vct_ref[...], p.astype(BF16), preferred_element_type=F32)
    if has_lat:
        m_ref[...] = m
        l_ref[...] = l
        acc_ref[...] = acc

        def step(c, slot):
            scores(c + 1, 1 - slot)
            off = pl.multiple_of(c * tk, tk)
            s = s_ref[slot]
            m_old = m_ref[...]
            m_new = jnp.maximum(m_old, jnp.max(s, axis=0, keepdims=True))
            alpha = jnp.exp2(m_old - m_new)
            p = jnp.exp2(s - m_new)
            l_ref[...] = alpha * l_ref[...] + jnp.sum(p, axis=0, keepdims=True)
            acc_ref[...] = alpha * acc_ref[...] + jnp.dot(
                vlt_ref[:, pl.ds(off, tk)], p.astype(BF16), preferred_element_type=F32)
            m_ref[...] = m_new

        unroll = min(FLASH_UNROLL, nk)

        def body(j, carry):
            for u in range(unroll):
                step(unroll * j + u, u % 2)
            return carry

        lax.fori_loop(0, nk // unroll, body, 0)
        acc = acc_ref[...]
        l = l_ref[...]
    o_ref[...] = (acc / l).T.astype(o_ref.dtype)


def _flash_attention(Qt, Kc, Vct, seq_len, ctx_len, Kl=None, Vlt=None):
    H, dq, M = Qt.shape
    B = M // seq_len
    has_lat = Kl is not None
    tq = min(1024, seq_len)
    tk = 512
    nq = seq_len // tq
    specs = [pl.BlockSpec((None, dq, tq), lambda b, h, i: (h, 0, b * nq + i)),
             pl.BlockSpec((None, ctx_len, dq), lambda b, h, i: (h, b, 0)),
             pl.BlockSpec((None, HEAD_DIM, ctx_len), lambda b, h, i: (h, 0, b))]
    args = [Qt, Kc, Vct]
    scratch = []
    if has_lat:
        specs += [pl.BlockSpec((None, seq_len, dq), lambda b, h, i: (h, b, 0)),
                  pl.BlockSpec((None, HEAD_DIM, seq_len), lambda b, h, i: (h, 0, b))]
        args += [Kl, Vlt]
        nk = seq_len // tk
        assert nk % 2 == 0 and nk % min(FLASH_UNROLL, nk) == 0
        scratch = [pltpu.VMEM((1, tq), F32), pltpu.VMEM((1, tq), F32), pltpu.VMEM((HEAD_DIM, tq), F32),
                   pltpu.VMEM((2, tk, tq), F32)]
    return pl.pallas_call(
        functools.partial(_flash_kernel, tk, has_lat),
        grid=(B, H, nq),
        in_specs=specs,
        out_specs=pl.BlockSpec((tq, HEAD_DIM), lambda b, h, i: (b * nq + i, h)),
        out_shape=jax.ShapeDtypeStruct((M, GROUP_W), BF16),
        scratch_shapes=scratch,
        compiler_params=_cparams(("parallel", "parallel", "arbitrary")),
        name="mla_attention" if has_lat else "ctx_attention",
    )(*args)


def _na_prep_kernel(x_ref, qg_ref, kg_ref, qt_out, k_out, vt_out):
    x = x_ref[...]
    scale = HEAD_DIM ** -0.5 * LOG2E
    for h in range(HEADS):
        qh = x[:, h * HEAD_DIM:(h + 1) * HEAD_DIM]
        kh = x[:, GROUP_W + h * HEAD_DIM:GROUP_W + (h + 1) * HEAD_DIM]
        vh = x[:, 2 * GROUP_W + h * HEAD_DIM:2 * GROUP_W + (h + 1) * HEAD_DIM]
        qh = qh * (lax.rsqrt(jnp.mean(qh * qh, axis=-1, keepdims=True) + EPS) * scale) * qg_ref[...]
        kh = kh * lax.rsqrt(jnp.mean(kh * kh, axis=-1, keepdims=True) + EPS) * kg_ref[...]
        qt_out[h] = qh.T.astype(BF16)
        k_out[h] = kh.astype(BF16)
        vt_out[h] = vh.T.astype(BF16)


def _na_prep(P, qg, kg):
    M = P.shape[0]
    tm = 512
    w = 3 * GROUP_W
    hspec = pl.BlockSpec((HEADS, tm, HEAD_DIM), lambda i: (0, i, 0))
    tspec = pl.BlockSpec((HEADS, HEAD_DIM, tm), lambda i: (0, 0, i))
    sds = jax.ShapeDtypeStruct((HEADS, M, HEAD_DIM), BF16)
    tds = jax.ShapeDtypeStruct((HEADS, HEAD_DIM, M), BF16)
    return pl.pallas_call(
        _na_prep_kernel,
        grid=(M // tm,),
        in_specs=[pl.BlockSpec((tm, w), lambda i: (i, P_NA_QKV // w)),
                  pl.BlockSpec((1, HEAD_DIM), lambda i: (0, 0)),
                  pl.BlockSpec((1, HEAD_DIM), lambda i: (0, 0))],
        out_specs=[tspec, hspec, tspec],
        out_shape=[tds, sds, tds],
        compiler_params=_cparams(("parallel",)),
        name="na_prep",
    )(P, qg, kg)


def _na_bias_kernel(rows, rpb_ref, o_ref):
    h = pl.program_id(0)
    kc = lax.broadcasted_iota(jnp.int32, (GRID_W, GRID_W), 0)
    qc = lax.broadcasted_iota(jnp.int32, (GRID_W, GRID_W), 1)
    dc = jnp.clip(kc - qc + (NA_KW - 1), 0, 2 * NA_KW - 2)
    c0 = jnp.clip(qc - NA_KW // 2, 0, GRID_W - NA_KW)
    col_ok = (kc >= c0) & (kc < c0 + NA_KW)
    neg = jnp.full((GRID_W, GRID_W), NEG_BIG, F32)
    tiles = []
    for dr in range(2 * NA_KH - 1):
        t = jnp.zeros((GRID_W, GRID_W), F32)
        for d in range(2 * NA_KW - 1):
            t = t + jnp.where(dc == d, rpb_ref[h, dr, d], 0.0)
        tiles.append(jnp.where(col_ok, t * LOG2E, NEG_BIG))
    for kind in range(3):
        q_base = (0, NA_KH // 2, rows - NA_BAND)[kind]
        w_base = (0, 0, rows - NA_WIN)[kind]
        for qr in range(NA_BAND):
            r = q_base + qr
            if kind == 1:
                r0 = r - NA_KH // 2
            else:
                r0 = min(max(r - NA_KH // 2, 0), rows - NA_KH)
            for kr in range(NA_WIN):
                ka = w_base + kr
                ok = r0 <= ka < r0 + NA_KH
                blk = tiles[ka - r + NA_KH - 1] if ok else neg
                o_ref[kind, kr * GRID_W:(kr + 1) * GRID_W, qr * GRID_W:(qr + 1) * GRID_W] = blk


def _na_bias(rpb, rows):
    H = rpb.shape[0]
    nq, nk = NA_BAND * GRID_W, NA_WIN * GRID_W
    return pl.pallas_call(
        functools.partial(_na_bias_kernel, rows),
        grid=(H,),
        in_specs=[pl.BlockSpec(memory_space=pltpu.SMEM)],
        out_specs=pl.BlockSpec((None, 3, nk, nq), lambda h: (h, 0, 0, 0)),
        out_shape=jax.ShapeDtypeStruct((H, 3, nk, nq), F32),
        compiler_params=_cparams(("parallel",)),
        name="na_bias",
    )(rpb)


def _na_kernel(seq_len, qt_ref, k_ref, vt_ref, kc_ref, vct_ref, bias0_ref, bias1_ref, o_ref):
    nq = bias0_ref.shape[1]
    nk = bias0_ref.shape[0]
    j = pl.program_id(2)
    kc = kc_ref[...]
    scores = []
    for t, bias_ref in enumerate((bias0_ref, bias1_ref)):
        base = jnp.clip((2 * j + t) * nq - (NA_KH // 2) * GRID_W, 0, seq_len - nk)
        base = pl.multiple_of(base, (NA_KH // 2) * GRID_W)
        qt = qt_ref[:, t * nq:(t + 1) * nq]
        s_ctx = jnp.dot(kc, qt, preferred_element_type=F32)
        s_loc = jnp.dot(k_ref[pl.ds(base, nk), :], qt, preferred_element_type=F32) + bias_ref[...]
        scores.append((base, s_loc, s_ctx))
    for t, (base, s_loc, s_ctx) in enumerate(scores):
        m = jnp.maximum(jnp.max(s_loc, axis=0, keepdims=True), jnp.max(s_ctx, axis=0, keepdims=True))
        p_loc = jnp.exp2(s_loc - m)
        p_ctx = jnp.exp2(s_ctx - m)
        l = jnp.sum(p_loc, axis=0, keepdims=True) + jnp.sum(p_ctx, axis=0, keepdims=True)
        o = (jnp.dot(vt_ref[:, pl.ds(base, nk)], p_loc.astype(BF16), preferred_element_type=F32)
             + jnp.dot(vct_ref[...], p_ctx.astype(BF16), preferred_element_type=F32))
        o_ref[t * nq:(t + 1) * nq, :] = (o / l).T.astype(o_ref.dtype)


def _na_attention(Qtl, Kl, Vtl, Kc, Vtc, bias, seq_len, ctx_len):
    H, d, M = Qtl.shape
    B = M // seq_len
    nq, nk = NA_BAND * GRID_W, NA_WIN * GRID_W
    nb = seq_len // nq
    assert nb % 2 == 0
    nb2 = nb // 2

    def kind(band):
        return jnp.where(band == 0, 0, jnp.where(band == nb - 1, 2, 1))

    return pl.pallas_call(
        functools.partial(_na_kernel, seq_len),
        grid=(B, H, nb2),
        in_specs=[pl.BlockSpec((None, d, 2 * nq), lambda b, h, j: (h, 0, b * nb2 + j)),
                  pl.BlockSpec((None, seq_len, d), lambda b, h, j: (h, b, 0)),
                  pl.BlockSpec((None, d, seq_len), lambda b, h, j: (h, 0, b)),
                  pl.BlockSpec((None, ctx_len, d), lambda b, h, j: (h, b, 0)),
                  pl.BlockSpec((None, d, ctx_len), lambda b, h, j: (h, 0, b)),
                  pl.BlockSpec((None, None, nk, nq), lambda b, h, j: (h, kind(2 * j), 0, 0)),
                  pl.BlockSpec((None, None, nk, nq), lambda b, h, j: (h, kind(2 * j + 1), 0, 0))],
        out_specs=pl.BlockSpec((2 * nq, d), lambda b, h, j: (b * nb2 + j, h)),
        out_shape=jax.ShapeDtypeStruct((M, GROUP_W), BF16),
        compiler_params=_cparams(("parallel", "parallel", "arbitrary")),
        name="na_attention",
    )(Qtl, Kl, Vtl, Kc, Vtc, bias, bias)


def _gdn_prep_kernel(seq_len, x_ref, xp_ref, xn_ref, tail_ref, cw_ref, alog_ref, dtb_ref,
                     qkv_out, gates_out):
    tm = x_ref.shape[0]
    y = _silu(_conv4(xp_ref, x_ref, xn_ref, cw_ref, (pl.program_id(0) * tm) % seq_len, seq_len))
    for h in range(HEADS):
        sl = slice(h * HEAD_DIM, (h + 1) * HEAD_DIM)
        qh = y[:, sl]
        qkv_out[:, sl] = qh * (lax.rsqrt(jnp.sum(qh * qh, axis=-1, keepdims=True) + EPS) * HEAD_DIM ** -0.5)
        sl = slice(GROUP_W + h * HEAD_DIM, GROUP_W + (h + 1) * HEAD_DIM)
        kh = y[:, sl]
        qkv_out[:, sl] = kh * lax.rsqrt(jnp.sum(kh * kh, axis=-1, keepdims=True) + EPS)
    qkv_out[:, 2 * GROUP_W:] = y[:, 2 * GROUP_W:]
    t = tail_ref[...]
    lane = lax.broadcasted_iota(jnp.int32, t.shape, 1)
    alpha = pltpu.roll(t, LANES - TAIL_ALPHA, 1)
    beta = pltpu.roll(t, LANES - TAIL_BETA + 2 * HEADS, 1)
    g = -jnp.exp(alog_ref[...]) * _softplus(alpha + dtb_ref[...])
    gates_out[...] = jnp.where(lane < 2 * HEADS, g, jnp.where(lane < 4 * HEADS, _sigmoid(beta), 0.0))


def _gdn_prep(P, seq_len, cw, alog, dtb):
    M = P.shape[0]
    tm = min(512, seq_len)
    w = 3 * GROUP_W
    prev, nxt = _halo_specs(tm, w, 0, M, 1)
    full = lambda i: (0, 0)
    return pl.pallas_call(
        functools.partial(_gdn_prep_kernel, seq_len),
        grid=(M // tm,),
        in_specs=[pl.BlockSpec((tm, w), lambda i: (i, 0)), prev, nxt,
                  pl.BlockSpec((tm, LANES), lambda i: (i, P_TAIL // LANES)),
                  pl.BlockSpec((4, w), full), pl.BlockSpec((1, LANES), full), pl.BlockSpec((1, LANES), full)],
        out_specs=[pl.BlockSpec((tm, w), lambda i: (i, 0)), pl.BlockSpec((tm, LANES), lambda i: (i, 0))],
        out_shape=[jax.ShapeDtypeStruct((M, w), F32), jax.ShapeDtypeStruct((M, LANES), F32)],
        compiler_params=_cparams(("parallel",)),
        name="gdn_prep",
    )(P, P, P, P, cw, alog, dtb)


def _time_cumsum(x, rev):
    k = 1
    while k < x.shape[0]:
        x = x + _shift_rows(x, k, 0.0, rev)
        k *= 2
    return x


def _gdn_local_kernel(qkv_ref, gates_ref, a_out, qk_out):
    C = GDN_CHUNK
    ri = lax.broadcasted_iota(jnp.int32, (C, C), 0)
    ci = lax.broadcasted_iota(jnp.int32, (C, C), 1)
    for c in range(qkv_ref.shape[0] // C):
        rows = slice(c * C, (c + 1) * C)
        gt = gates_ref[rows, :]
        zs = []
        for h in range(HEADS):
            q = qkv_ref[rows, h * HEAD_DIM:(h + 1) * HEAD_DIM]
            kk = qkv_ref[rows, GROUP_W + h * HEAD_DIM:GROUP_W + (h + 1) * HEAD_DIM]
            zs.append(_dot_nt(jnp.concatenate([kk, q], axis=0), kk))
        for d in range(2):
            rev = d == 1
            incl = (ci >= ri) if rev else (ci <= ri)
            strict = (ci > ri) if rev else (ci < ri)
            gcum = _time_cumsum(gt, rev)
            grow = gcum.T
            for h in range(HEADS):
                lg = d * HEADS + h
                gc_c = gcum[:, lg:lg + 1]
                gc_r = grow[lg:lg + 1, :]
                beta = gt[:, 2 * HEADS + lg:2 * HEADS + lg + 1]
                decay = jnp.where(incl, jnp.exp(jnp.where(incl, gc_c - gc_r, 0.0)), 0.0)
                lanes = slice((h % 2) * C, (h % 2 + 1) * C)
                a_out[d, c * 2 + h // 2, :, lanes] = jnp.where(strict, zs[h][:C] * beta * decay, 0.0)
                qk_out[d, c * 2 + h // 2, :, lanes] = zs[h][C:] * decay


def _gdn_local(qkv, gates):
    M = qkv.shape[0]
    tm = 512
    npair = tm // GDN_CHUNK * (HEADS // 2)
    w = 3 * GROUP_W
    ospec = pl.BlockSpec((2, npair, GDN_CHUNK, LANES), lambda i: (0, i, 0, 0))
    sds = jax.ShapeDtypeStruct((2, M // GDN_CHUNK * (HEADS // 2), GDN_CHUNK, LANES), F32)
    return pl.pallas_call(
        _gdn_local_kernel,
        grid=(M // tm,),
        in_specs=[pl.BlockSpec((tm, w), lambda i: (i, 0)), pl.BlockSpec((tm, LANES), lambda i: (i, 0))],
        out_specs=[ospec, ospec],
        out_shape=[sds, sds],
        compiler_params=_cparams(("parallel",)),
        name="gdn_local",
    )(qkv, gates)


SOLVE_BATCH = 128


def _gdn_solve_kernel(upper, a_ref, o_ref, at_ref, m_ref):
    C = GDN_CHUNK
    ns = C // SUBLANES
    for i in range(C):
        xt = a_ref[pl.ds(i, SOLVE_BATCH, stride=C), :].T
        at_ref[i, 0] = xt[:C]
        at_ref[i, 1] = xt[C:]
    sub = lax.broadcasted_iota(jnp.int32, (SUBLANES, SOLVE_BATCH), 0)
    for t in range(C):
        i = C - 1 - t if upper else t
        si = i // SUBLANES
        slabs = list(range(si, ns)) if upper else list(range(si + 1))
        unit = jnp.where(sub == i % SUBLANES, 1.0, 0.0).astype(F32)
        zero = jnp.zeros((SUBLANES, SOLVE_BATCH), F32)
        init = tuple(unit if s == si else zero for _ in range(2) for s in slabs)

        def col_body(j, acc):
            new = []
            for hp in range(2):
                a = jnp.broadcast_to(at_ref[i, hp, pl.ds(j, 1), :], (SUBLANES, SOLVE_BATCH))
                for n, s in enumerate(slabs):
                    new.append(acc[hp * len(slabs) + n] - a * m_ref[hp, j, s * SUBLANES:(s + 1) * SUBLANES, :])
            return tuple(new)

        lo, hi = (i + 1, C) if upper else (0, i)
        if hi > lo:
            acc = lax.fori_loop(lo, hi, col_body, init, unroll=min(4, hi - lo))
        else:
            acc = init
        for hp in range(2):
            for s in range(ns):
                val = acc[hp * len(slabs) + slabs.index(s)] if s in slabs else zero
                m_ref[hp, i, s * SUBLANES:(s + 1) * SUBLANES, :] = val
    for i in range(C):
        y = jnp.concatenate([m_ref[0, i], m_ref[1, i]], axis=0)
        o_ref[i] = y.T


def _gdn_solve(a_all):
    _, NP, C, _ = a_all.shape
    rows = SOLVE_BATCH * C
    a2 = a_all.reshape(2, NP * C, LANES)
    outs = []
    for d in range(2):
        spec = pl.BlockSpec((None, rows, LANES), lambda b, d=d: (d, b, 0))
        outs.append(pl.pallas_call(
            functools.partial(_gdn_solve_kernel, d == 1),
            grid=(NP // SOLVE_BATCH,),
            in_specs=[spec],
            out_specs=pl.BlockSpec((None, C, SOLVE_BATCH, LANES), lambda b: (b, 0, 0, 0)),
            out_shape=jax.ShapeDtypeStruct((NP // SOLVE_BATCH, C, SOLVE_BATCH, LANES), F32),
            scratch_shapes=[pltpu.VMEM((C, 2, C, SOLVE_BATCH), F32), pltpu.VMEM((2, C, C, SOLVE_BATCH), F32)],
            compiler_params=_cparams(("parallel",)),
            name="gdn_solve_upper" if d else "gdn_solve_lower",
        )(a2))
    return outs


def _gdn_kernel(rev, finish, d, *refs):
    if finish:
        (qkv_ref, gates_ref, qk_ref, s0_ref, of_ref, gate_ref, ng_ref, *tinv_refs) = refs[:-4]
        out_ref, sout_ref, s_ref, o_ref = refs[-4:]
    else:
        (qkv_ref, gates_ref, qk_ref, s0_ref, *tinv_refs) = refs[:-3]
        out_ref, sout_ref, s_ref = refs[-3:]
        o_ref = out_ref
    C = GDN_CHUNK
    B = qkv_ref.shape[0]
    nchunk = qkv_ref.shape[1] // C
    j = pl.program_id(0)

    @pl.when(j == 0)
    def _():
        s_ref[...] = s0_ref[...]

    last_row = 0 if rev else C - 1
    order = [(nchunk - 1 - t) if rev else t for t in range(nchunk)]
    units = [(b, h) for b in range(B) for h in range(HEADS)]

    def local(c):
        rows = slice(c * C, (c + 1) * C)
        out = {}
        for b in range(B):
            gt = gates_ref[b, rows, :]
            gcum = _time_cumsum(gt, rev)
            for h in range(HEADS):
                lg = d * HEADS + h
                lanes = slice((h % 2) * C, (h % 2 + 1) * C)
                gc_c = gcum[:, lg:lg + 1]
                beta = gt[:, 2 * HEADS + lg:2 * HEADS + lg + 1]
                gl = gcum[last_row:last_row + 1, lg:lg + 1]
                q = qkv_ref[b, rows, h * HEAD_DIM:(h + 1) * HEAD_DIM]
                kk = qkv_ref[b, rows, GROUP_W + h * HEAD_DIM:GROUP_W + (h + 1) * HEAD_DIM]
                v = qkv_ref[b, rows, 2 * GROUP_W + h * HEAD_DIM:2 * GROUP_W + (h + 1) * HEAD_DIM]
                tinv = tinv_refs[b][:, c * 2 + h // 2, lanes]
                qk = qk_ref[b, c * 2 + h // 2, :, lanes]
                kb = kk * beta
                eg = jnp.exp(gc_c)
                uw = _dot(tinv, jnp.concatenate([v * beta, kb * eg], axis=1))
                wq = jnp.concatenate([uw[:, HEAD_DIM:], q * eg], axis=0).astype(BF16)
                kdt = (kk * jnp.exp(gl - gc_c)).T
                qkk = jnp.concatenate([qk, kdt], axis=0).astype(BF16)
                out[b, h] = (uw[:, :HEAD_DIM], wq, qkk, jnp.exp(gl))
        return out

    states = {(b, h): s_ref[b, h] for b, h in units}
    nxt = local(order[0])
    for t, c in enumerate(order):
        cur = nxt
        ws = {k: jnp.dot(cur[k][1], states[k].astype(BF16), preferred_element_type=F32) for k in units}
        if t + 1 < nchunk:
            nxt = local(order[t + 1])
        for b, h in units:
            u, _, qkk, decay = cur[b, h]
            v_new = u - ws[b, h][:C]
            res = jnp.dot(qkk, v_new.astype(BF16), preferred_element_type=F32)
            o_ref[b, c * C:(c + 1) * C, h * HEAD_DIM:(h + 1) * HEAD_DIM] = ws[b, h][C:] + res[:C]
            states[b, h] = states[b, h] * decay + res[C:]
    for b, h in units:
        s_ref[b, h] = states[b, h]
    sout_ref[...] = s_ref[...]
    if finish:
        for b in range(B):
            tot = of_ref[b] + o_ref[b]
            gate = gate_ref[b]
            for h in range(HEADS):
                sl = slice(h * HEAD_DIM, (h + 1) * HEAD_DIM)
                x = tot[:, sl]
                y = x * lax.rsqrt(jnp.mean(x * x, axis=-1, keepdims=True) + EPS) * ng_ref[...]
                out_ref[b, :, sl] = (y * _silu(gate[:, sl])).astype(out_ref.dtype)


def _gdn_pass(qkv, gates, tinv, qk, seq_len, rev, d, s0, of=None, P=None, ng=None):
    M = qkv.shape[0]
    B = M // seq_len
    rb = 256
    nblk = seq_len // rb
    finish = of is not None
    w = 3 * GROUP_W
    npair = rb // GDN_CHUNK * (HEADS // 2)
    per_batch = SOLVE_BATCH // npair

    def blk(j):
        return (nblk - 1 - j) if rev else j

    sspec = pl.BlockSpec((B, HEADS, HEAD_DIM, HEAD_DIM), lambda j: (0, 0, 0, 0))
    specs = [pl.BlockSpec((B, rb, w), lambda j: (0, blk(j), 0)),
             pl.BlockSpec((B, rb, LANES), lambda j: (0, blk(j), 0)),
             pl.BlockSpec((None, B, npair, GDN_CHUNK, LANES), lambda j: (d, 0, blk(j), 0, 0)),
             sspec]
    args = [qkv.reshape(B, seq_len, w), gates.reshape(B, seq_len, LANES),
            qk.reshape(2, B, nblk * npair, GDN_CHUNK, LANES), s0]
    scratch = [pltpu.VMEM((B, HEADS, HEAD_DIM, HEAD_DIM), F32)]
    if finish:
        specs += [pl.BlockSpec((B, rb, GROUP_W), lambda j: (0, blk(j), 0)),
                  pl.BlockSpec((B, rb, GROUP_W), lambda j: (0, blk(j), P_GDN_GATE // GROUP_W)),
                  pl.BlockSpec((1, HEAD_DIM), lambda j: (0, 0))]
        args += [of.reshape(B, seq_len, GROUP_W), P.reshape(B, seq_len, P.shape[1]), ng]
        scratch.append(pltpu.VMEM((B, rb, GROUP_W), F32))
    for b in range(B):
        def tmap(j, b=b):
            g = b * nblk + blk(j)
            return (g // per_batch, 0, g % per_batch, 0)
        specs.append(pl.BlockSpec((None, GDN_CHUNK, npair, LANES), tmap))
        args.append(tinv)
    out, s_out = pl.pallas_call(
        functools.partial(_gdn_kernel, rev, finish, d),
        grid=(nblk,),
        in_specs=specs,
        out_specs=[pl.BlockSpec((B, rb, GROUP_W), lambda j: (0, blk(j), 0)), sspec],
        out_shape=[jax.ShapeDtypeStruct((B, seq_len, GROUP_W), BF16 if finish else F32),
                   jax.ShapeDtypeStruct((B, HEADS, HEAD_DIM, HEAD_DIM), F32)],
        scratch_shapes=scratch,
        compiler_params=_cparams(("arbitrary",)),
        name="gdn_bwd" if rev else "gdn_fwd",
    )(*args)
    return out.reshape(M, GROUP_W), s_out


def _mixer_gdn(Pc, Pl, ctx_len, seq_len, cw, alog, dtb, ng):
    B = Pc.shape[0] // ctx_len
    qkv_c, g_c = _gdn_prep(Pc, ctx_len, cw, alog, dtb)
    qkv_l, g_l = _gdn_prep(Pl, seq_len, cw, alog, dtb)
    a_c, qk_c = _gdn_local(qkv_c, g_c)
    a_l, qk_l = _gdn_local(qkv_l, g_l)
    def solve(a):
        n_pad = -a.shape[1] % SOLVE_BATCH
        if n_pad:
            a = jnp.concatenate([a, jnp.zeros((2, n_pad, GDN_CHUNK, LANES), F32)], axis=1)
        return _gdn_solve(a)

    tinv_c, tinv_l = solve(a_c), solve(a_l)
    zero = jnp.zeros((B, HEADS, HEAD_DIM, HEAD_DIM), F32)
    oc_f, sc_f = _gdn_pass(qkv_c, g_c, tinv_c[0], qk_c, ctx_len, False, 0, zero)
    ol_f, _ = _gdn_pass(qkv_l, g_l, tinv_l[0], qk_l, seq_len, False, 0, sc_f)
    yc, sc_b = _gdn_pass(qkv_c, g_c, tinv_c[1], qk_c, ctx_len, True, 1, zero, of=oc_f, P=Pc, ng=ng)
    yl, _ = _gdn_pass(qkv_l, g_l, tinv_l[1], qk_l, seq_len, True, 1, sc_b, of=ol_f, P=Pl, ng=ng)
    return yc, yl


def _pack_w_in(w_in):
    off = {}
    o = 0
    for name, n in (('lru_x', 512), ('lru_gate', 512), ('mla_cq', 384), ('mla_ckv', 256), ('mla_kr', 64),
                    ('gdn_q', 512), ('gdn_k', 512), ('gdn_v', 512), ('gdn_gate', 512),
                    ('gdn_beta', 8), ('gdn_alpha', 8), ('na_q', 512), ('na_k', 512), ('na_v', 512)):
        off[name] = (o, n)
        o += n
    order = ['gdn_q', 'gdn_k', 'gdn_v', 'gdn_gate', 'lru_x', 'lru_gate', 'na_q', 'na_k', 'na_v',
             'mla_cq', 'mla_ckv', 'mla_kr', 'gdn_beta', 'gdn_alpha']
    w_in = w_in.astype(BF16)
    parts = [w_in[..., off[n][0]:off[n][0] + off[n][1]] for n in order]
    used = sum(off[n][1] for n in order)
    parts.append(jnp.zeros(w_in.shape[:-1] + (P_COLS - used,), w_in.dtype))
    return jnp.concatenate(parts, axis=-1)


def _pack_w_uq(w_uq):
    L, K, _ = w_uq.shape
    w = w_uq.reshape(L, K, HEADS, MLA_QK)
    w = jnp.pad(w, ((0, 0), (0, 0), (0, 0), (0, MLA_QPAD - MLA_QK)))
    return w.reshape(L, K, HEADS * MLA_QPAD).astype(BF16)


def _pad_gain(g):
    return jnp.pad(g, ((0, 0), (0, MLA_QPAD - MLA_QK)))[:, None, :]


def _rope_tables(T):
    t = jnp.arange(T)
    rowp = (t // GRID_W).astype(F32)
    colp = (t % GRID_W).astype(F32)
    n_freq = MLA_ROPE // 4
    inv = ROPE_BASE ** (-jnp.arange(n_freq, dtype=F32) / n_freq)
    ang = jnp.concatenate([rowp[:, None] * inv, colp[:, None] * inv], axis=-1)
    cos, sin = jnp.cos(ang), jnp.sin(ang)
    z = jnp.zeros((T, LANES - MLA_ROPE), F32)
    return (jnp.concatenate([cos, cos, z], axis=-1), jnp.concatenate([-sin, sin, z], axis=-1))


def _lane_vec(x):
    L = x.shape[0]
    return jnp.pad(x.reshape(L, 2 * HEADS), ((0, 0), (0, LANES - 2 * HEADS)))[:, None, :]


def kernel(x, c, ctx, c_ctx, ada_w, ada_b, norm_mix_g, norm_ffn_g, w_in, w_out, lru_conv_w, lru_conv_b, lru_wa, lru_ba, lru_wx, lru_bx, lru_lam, mla_qa_g, mla_w_uq, mla_kva_g, mla_w_ukv, mla_qn_g, mla_kn_g, gdn_conv_w, gdn_a_log, gdn_dt_bias, gdn_norm_g, na_qn_g, na_kn_g, na_rpb, ffn_w_up, ffn_conv_w, ffn_conv_b, ffn_w_down):
    B, T, D = x.shape
    TC = ctx.shape[1]
    L = ada_w.shape[0]
    rows = T // GRID_W
    assert B <= 2 and T % 512 == 0 and TC == 256 and rows >= NA_WIN

    w_in_p = _pack_w_in(w_in)
    w_out4 = w_out.reshape(L, 4, GROUP_W, D).astype(BF16)
    w_up = ffn_w_up.astype(BF16)
    w_down = ffn_w_down.astype(BF16)
    w_uq = _pack_w_uq(mla_w_uq)
    w_ukv = mla_w_ukv.astype(BF16)
    qn_g = _pad_gain(mla_qn_g)
    kn_g = _pad_gain(mla_kn_g)
    lru_wcat = jnp.concatenate([lru_wa, lru_wx], axis=-1).astype(BF16)
    alog_v = _lane_vec(gdn_a_log)
    dtb_v = _lane_vec(gdn_dt_bias)
    cosf, sinf = _rope_tables(T)

    cvec = jnp.zeros((SUBLANES, D), F32).at[:B].set(c).at[2].set(c_ctx)
    mod_all = _modulation(cvec, ada_w, ada_b).reshape(L, SUBLANES, N_MOD, 1, D)

    gmix = norm_mix_g[:, None, :]
    gffn = norm_ffn_g[:, None, :]
    ffn_cb = ffn_conv_b[:, None, :]
    h_lat = x.reshape(B * T, D)
    h_ctx = ctx.reshape(B * TC, D)
    for l in range(L):
        want_ctx = l < L - 1
        mod = mod_all
        Pl = _inproj(l, h_lat, gmix, mod, w_in_p, T, False)
        Pc = _inproj(l, h_ctx, gmix, mod, w_in_p, TC, True)

        lw = lambda dd: (lru_conv_w[l], lru_conv_b[l][None, :], lru_wcat[l, dd], lru_ba[l, dd][None, :],
                         lru_bx[l, dd][None, :], lru_lam[l, dd][None, :])
        ya_c, ya_l = _mixer_lru(Pc, Pl, TC, T, lw(0), lw(1))

        mw = (mla_qa_g[l][None, :], w_uq[l], mla_kva_g[l][None, :], w_ukv[l], qn_g[l], kn_g[l])
        Qtc, Kc, Vtc = _mla_prep(Pc, TC, False, mw, cosf, sinf)
        Qtl, Kl, Vtl = _mla_prep(Pl, T, True, mw, cosf, sinf)
        yb_l = _flash_attention(Qtl, Kc, Vtc, T, TC, Kl, Vtl)

        yc_c, yc_l = _mixer_gdn(Pc, Pl, TC, T, gdn_conv_w[l], alog_v[l], dtb_v[l], gdn_norm_g[l][None, :])

        nqg, nkg = na_qn_g[l][None, :], na_kn_g[l][None, :]
        NQtc, NKc, NVtc = _na_prep(Pc, nqg, nkg)
        NQtl, NKl, NVtl = _na_prep(Pl, nqg, nkg)
        bias = _na_bias(na_rpb[l], rows)
        yd_l = _na_attention(NQtl, NKl, NVtl, NKc, NVtc, bias, T, TC)

        h_lat = _outproj(l, (ya_l, yb_l, yc_l, yd_l), w_out4, h_lat, mod, T, False)
        h_lat = _ffn(l, h_lat, gffn, mod, w_up, ffn_conv_w, ffn_cb, w_down, T, False)
        if want_ctx:
            yb_c = _flash_attention(Qtc, Kc, Vtc, TC, TC)
            yd_c = _flash_attention(NQtc, NKc, NVtc, TC, TC)
            h_ctx = _outproj(l, (ya_c, yb_c, yc_c, yd_c), w_out4, h_ctx, mod, TC, True)
            h_ctx = _ffn(l, h_ctx, gffn, mod, w_up, ffn_conv_w, ffn_cb, w_down, TC, True)
    return h_lat.reshape(B, T, D)
```
